```python
import math
import jax, jax.numpy as jnp
from jax import lax
import numpy as np

D_MODEL = 1024
BATCH = 8
SEQ = 4096
DEPTH = 1
DEC_BATCH = 32
DEC_SEQ = 16
PAST_LEN = 2048

CHUNK = 64
Q_BLOCK = 128
HD = 64
FOX_HEADS = 8
FOX_WIDTH = FOX_HEADS * HD
DIFF_HEADS = 4
DIFF_WIDTH = DIFF_HEADS * 2 * HD
N_BUCKETS = 32
MAX_DISTANCE = 128
NORM_EPS = 1e-6
SCALE = HD ** -0.5
NEG_INF = -1e30

_SIZES = (FOX_WIDTH, FOX_WIDTH, FOX_WIDTH, FOX_WIDTH, FOX_HEADS,
          DIFF_WIDTH, DIFF_WIDTH, DIFF_WIDTH, DIFF_WIDTH, D_MODEL, D_MODEL)
IN_WIDTH = sum(_SIZES)
SPLIT_POINTS = tuple(sum(_SIZES[:i + 1]) for i in range(len(_SIZES) - 1))

kernel_name = "fox_diffattn_gated_hybrid_stream_step"


def rmsnorm(x, g):
    xf = x.astype(jnp.float32)
    y = xf * lax.rsqrt(jnp.mean(xf * xf, axis=-1, keepdims=True) + NORM_EPS) * g.astype(jnp.float32)
    return y.astype(x.dtype)


def t5_bucket(rel):
    nb = N_BUCKETS // 2
    side = jnp.where(rel > 0, nb, 0)
    n = jnp.abs(rel)
    max_exact = nb // 2
    large = max_exact + (jnp.log(jnp.maximum(n, 1).astype(jnp.float32) / max_exact)
                         / math.log(MAX_DISTANCE / max_exact) * (nb - max_exact)).astype(jnp.int32)
    large = jnp.minimum(large, nb - 1)
    return side + jnp.where(n < max_exact, n, large)


def sweep_queries(fn, qpos, *q_args):
    T = qpos.shape[0]
    if T <= Q_BLOCK or T % Q_BLOCK != 0:
        return fn(qpos, *q_args)
    nb = T // Q_BLOCK
    blk = lambda a: jnp.moveaxis(a.reshape(a.shape[0], nb, Q_BLOCK, *a.shape[2:]), 1, 0)
    out = lax.map(lambda args: fn(*args), (qpos.reshape(nb, Q_BLOCK),) + tuple(blk(a) for a in q_args))
    out = jnp.moveaxis(out, 0, 1)
    return out.reshape(out.shape[0], T, *out.shape[3:])


def _layer(x, past_fk, past_fv, past_fl, past_dk, past_dv, norm_g, w_in, b_f,
           lq1, lk1, lq2, lk2, subln_g, w_up_f, w_up_d, w_o, rel_bias, lambda_init):
    B, T, _ = x.shape
    P = past_fk.shape[1]
    dt = x.dtype
    f32 = jnp.float32
    h = rmsnorm(x, norm_g)
    proj = h @ w_in
    fq, fk, fv, fz, fl, dq, dk, dv, dz, ga, gb = jnp.split(proj, SPLIT_POINTS, axis=-1)
    fq = fq.reshape(B, T, FOX_HEADS, HD)
    fk = fk.reshape(B, T, FOX_HEADS, HD)
    fv = fv.reshape(B, T, FOX_HEADS, HD)
    logf = jax.nn.log_sigmoid((fl + b_f).astype(f32))
    dq = dq.reshape(B, T, DIFF_HEADS, 2, HD)
    dk = dk.reshape(B, T, DIFF_HEADS, 2, HD)
    dv = dv.reshape(B, T, DIFF_HEADS, 2 * HD)
    qpos = P + jnp.arange(T, dtype=jnp.int32)
    kpos = jnp.arange(P + T, dtype=jnp.int32)

    kf = jnp.concatenate([past_fk, fk], axis=1).astype(f32)
    vf = jnp.concatenate([past_fv, fv], axis=1).astype(f32)
    cum = lax.cumsum(jnp.concatenate([past_fl.astype(f32), logf], axis=1), axis=1)
    ck = jnp.swapaxes(cum, 1, 2)
    cq = cum[:, P:]

    def fox_block(qp, q, c):
        s = jnp.einsum('bqhd,bkhd->bhqk', q.astype(f32), kf) * SCALE
        s = s + jnp.swapaxes(c, 1, 2)[..., :, None] - ck[:, :, None, :]
        s = jnp.where(kpos[None, :] <= qp[:, None], s, NEG_INF)
        p = jax.nn.softmax(s, axis=-1)
        return jnp.einsum('bhqk,bkhd->bqhd', p, vf)

    of = sweep_queries(fox_block, qpos, fq, cq)

    kd = jnp.concatenate([past_dk, dk], axis=1).astype(f32)
    vd = jnp.concatenate([past_dv, dv], axis=1).astype(f32)
    lam = (jnp.exp(jnp.sum(lq1.astype(f32) * lk1.astype(f32)))
           - jnp.exp(jnp.sum(lq2.astype(f32) * lk2.astype(f32))) + lambda_init)
    table = rel_bias.astype(f32)

    def diff_block(qp, q):
        s = jnp.einsum('bqhmd,bkhmd->bhmqk', q.astype(f32), kd) * SCALE
        bias = table[t5_bucket(kpos[None, :] - qp[:, None])]
        s = s + jnp.transpose(bias, (2, 0, 1))[None, :, None]
        mask = (kpos[None, :] // CHUNK) <= (qp[:, None] // CHUNK)
        s = jnp.where(mask, s, NEG_INF)
        p = jax.nn.softmax(s, axis=-1)
        a = p[:, :, 0] - lam * p[:, :, 1]
        return jnp.einsum('bhqk,bkhe->bqhe', a, vd)

    od = sweep_queries(diff_block, qpos, dq)
    od = rmsnorm(od, subln_g) * (1.0 - lambda_init)

    yf = of.reshape(B, T, FOX_WIDTH).astype(dt) * jax.nn.silu(fz)
    yd = od.reshape(B, T, DIFF_WIDTH).astype(dt) * jax.nn.silu(dz)
    merged = jax.nn.sigmoid(ga) * (yf @ w_up_f) + jax.nn.sigmoid(gb) * (yd @ w_up_d)
    out = x + merged @ w_o
    return out, (fk, fv, logf.astype(dt), dk, dv)


def setup_inputs(seed: int = 0) -> dict:
    key = jax.random.key(seed)
    ks = jax.random.split(key, 24)
    nrm = lambda k, shape, s=1.0: jax.random.normal(k, shape, jnp.float32) * s
    return {
        "x_prompt": nrm(ks[0], (BATCH, SEQ, D_MODEL)),
        "x_sample": nrm(ks[1], (DEC_BATCH, DEC_SEQ, D_MODEL)),
        "cache_fox_k": nrm(ks[2], (DEPTH, DEC_BATCH, PAST_LEN, FOX_HEADS, HD)),
        "cache_fox_v": nrm(ks[3], (DEPTH, DEC_BATCH, PAST_LEN, FOX_HEADS, HD)),
        "cache_fox_logf": jax.nn.log_sigmoid(3.0 + nrm(ks[4], (DEPTH, DEC_BATCH, PAST_LEN, FOX_HEADS))),
        "cache_diff_k": nrm(ks[5], (DEPTH, DEC_BATCH, PAST_LEN, DIFF_HEADS, 2, HD)),
        "cache_diff_v": nrm(ks[6], (DEPTH, DEC_BATCH, PAST_LEN, DIFF_HEADS, 2 * HD)),
        "norm_in_g": 1.0 + nrm(ks[7], (DEPTH, D_MODEL), 0.05),
        "w_in": nrm(ks[8], (DEPTH, D_MODEL, IN_WIDTH), D_MODEL ** -0.5),
        "b_forget": 3.0 + nrm(ks[9], (DEPTH, FOX_HEADS), 0.5),
        "lambda_q1": nrm(ks[10], (DEPTH, HD), 0.1),
        "lambda_k1": nrm(ks[11], (DEPTH, HD), 0.1),
        "lambda_q2": nrm(ks[12], (DEPTH, HD), 0.1),
        "lambda_k2": nrm(ks[13], (DEPTH, HD), 0.1),
        "subln_g": 1.0 + nrm(ks[14], (DEPTH, 2 * HD), 0.05),
        "w_up_fox": nrm(ks[15], (DEPTH, FOX_WIDTH, D_MODEL), FOX_WIDTH ** -0.5),
        "w_up_diff": nrm(ks[16], (DEPTH, DIFF_WIDTH, D_MODEL), DIFF_WIDTH ** -0.5),
        "w_o": nrm(ks[17], (DEPTH, D_MODEL, D_MODEL), D_MODEL ** -0.5),
        "rel_bias": nrm(ks[18], (N_BUCKETS, DIFF_HEADS), 0.5),
        "final_norm_g": 1.0 + nrm(ks[19], (D_MODEL,), 0.05),
    }


def reference(x_prompt, x_sample, cache_fox_k, cache_fox_v, cache_fox_logf, cache_diff_k, cache_diff_v,
              norm_in_g, w_in, b_forget, lambda_q1, lambda_k1, lambda_q2, lambda_k2, subln_g,
              w_up_fox, w_up_diff, w_o, rel_bias, final_norm_g):
    hp, hs = x_prompt, x_sample
    Bp = x_prompt.shape[0]
    dt = x_prompt.dtype
    new_p, new_s = [], []
    for l in range(DEPTH):
        lambda_init = 0.8 - 0.6 * math.exp(-0.3 * l)
        w = (norm_in_g[l], w_in[l], b_forget[l], lambda_q1[l], lambda_k1[l], lambda_q2[l], lambda_k2[l],
             subln_g[l], w_up_fox[l], w_up_diff[l], w_o[l], rel_bias, lambda_init)
        hp, st_p = _layer(hp,
                          jnp.zeros((Bp, 0, FOX_HEADS, HD), dt), jnp.zeros((Bp, 0, FOX_HEADS, HD), dt),
                          jnp.zeros((Bp, 0, FOX_HEADS), dt),
                          jnp.zeros((Bp, 0, DIFF_HEADS, 2, HD), dt), jnp.zeros((Bp, 0, DIFF_HEADS, 2 * HD), dt),
                          *w)
        hs, st_s = _layer(hs, cache_fox_k[l], cache_fox_v[l], cache_fox_logf[l], cache_diff_k[l], cache_diff_v[l], *w)
        new_p.append(st_p)
        new_s.append(st_s)
    y_prompt = rmsnorm(hp, final_norm_g)
    y_sample = rmsnorm(hs, final_norm_g)
    stk = lambda states, i: jnp.stack([s[i] for s in states], axis=0)
    fox_k_p, fox_v_p, fox_logf_p, diff_k_p, diff_v_p = (stk(new_p, 0), stk(new_p, 1), stk(new_p, 2), stk(new_p, 3), stk(new_p, 4))
    fox_k_s, fox_v_s, fox_logf_s, diff_k_s, diff_v_s = (stk(new_s, 0), stk(new_s, 1), stk(new_s, 2), stk(new_s, 3), stk(new_s, 4))
    return (y_prompt, y_sample, fox_k_p, fox_v_p, fox_logf_p, diff_k_p, diff_v_p,
            fox_k_s, fox_v_s, fox_logf_s, diff_k_s, diff_v_s)
```

```python
import functools
import math

import numpy as np
import jax
import jax.numpy as jnp
from jax import lax
from jax.experimental import pallas as pl
from jax.experimental.pallas import tpu as pltpu

F32 = jnp.float32
BF16 = jnp.bfloat16

D_MODEL = 1024
HD = 64
FOX_HEADS = 8
DIFF_HEADS = 4
WIDTH = 512
CHUNK = 64
N_BUCKETS = 32
MAX_DISTANCE = 128
NORM_EPS = 1e-6
SCALE = HD ** -0.5
NEG = -1e30
LAMBDA_INIT = 0.8 - 0.6 * math.exp(-0.3 * 0)

LANES = 128
PAIRS = WIDTH // LANES

_O_FQ, _O_FK, _O_FV, _O_FZ = 0, 512, 1024, 1536
_O_DQ, _O_DK, _O_DV, _O_DZ = 2048, 2560, 3072, 3584
_O_GA, _O_GB = 4096, 5120
_O_FLW = 6144
_O_FL8 = 6656
_N_ALL = 6784

VMEM_LIMIT = 56 * 1024 * 1024


def _t5_thresholds():
    nb = N_BUCKETS // 2
    max_exact = nb // 2
    n = np.arange(0, 4 * MAX_DISTANCE)
    large = max_exact + (np.log(np.maximum(n, 1).astype(np.float32) / max_exact)
                         / math.log(MAX_DISTANCE / max_exact) * (nb - max_exact)).astype(np.int32)
    large = np.minimum(large, nb - 1)
    thr = [int(np.argmax(large >= b)) for b in range(max_exact + 1, nb)]
    return max_exact, tuple(thr)


_MAX_EXACT, _T5_THR = _t5_thresholds()
FAR_DIST = _T5_THR[-1]


def _mm(a, b):
    return jnp.dot(a, b, preferred_element_type=F32)


def _mm_nt(a, b):
    return lax.dot_general(a, b, (((1,), (1,)), ((), ())), preferred_element_type=F32)


def _log_sigmoid(x):
    return -(jnp.maximum(-x, 0.0) + jnp.log1p(jnp.exp(-jnp.abs(x))))


def _sigmoid(x):
    return 1.0 / (1.0 + jnp.exp(-x))


def _silu(x):
    return x * _sigmoid(x)


def _split3(x):
    p1 = x.astype(BF16).astype(F32)
    r1 = x - p1
    p2 = r1.astype(BF16).astype(F32)
    p3 = r1 - p2
    return p1, p2, p3


def _proj_kernel(x_ref, g_ref, w_ref, bw_ref, b8_ref, *refs, tm, seq_len, aug):
    if aug:
        (fk_ref, fv_ref, lf_ref, dk_ref, dv_ref, qa_ref, ka_ref, va_ref,
         fz_ref, dq_ref, dkb_ref, dvb_ref, dz_ref, ga_ref, gb_ref, carry_ref) = refs
    else:
        (fk_ref, fv_ref, lf_ref, dk_ref, dv_ref, fq_ref,
         fz_ref, dq_ref, dz_ref, ga_ref, gb_ref) = refs

    x = x_ref[...]
    ms = jnp.mean(x * x, axis=-1, keepdims=True)
    h = (x * lax.rsqrt(ms + NORM_EPS) * g_ref[...]).astype(BF16)

    def grp(off, n=WIDTH):
        return _mm(h, w_ref[:, off:off + n])

    fq = grp(_O_FQ) * SCALE
    fk = grp(_O_FK)
    fv = grp(_O_FV)
    fk_ref[...] = fk
    fv_ref[...] = fv
    fz_ref[...] = grp(_O_FZ).astype(BF16)
    dq_ref[...] = (grp(_O_DQ) * SCALE).astype(BF16)
    dk = grp(_O_DK)
    dv = grp(_O_DV)
    dk_ref[...] = dk
    dv_ref[...] = dv
    dz_ref[...] = grp(_O_DZ).astype(BF16)
    ga_ref[...] = _sigmoid(grp(_O_GA, D_MODEL)).astype(BF16)
    gb_ref[...] = _sigmoid(grp(_O_GB, D_MODEL)).astype(BF16)
    lf8 = _log_sigmoid(grp(_O_FL8, LANES)[:, :FOX_HEADS] + b8_ref[...])
    lf_ref[...] = lf8

    if not aug:
        fq_ref[...] = fq.astype(BF16)
        return

    dkb_ref[...] = dk.astype(BF16)
    dvb_ref[...] = dv.astype(BF16)

    lfw = _log_sigmoid(grp(_O_FLW) + bw_ref[...])
    row = lax.broadcasted_iota(jnp.int32, (tm, tm), 0)
    col = lax.broadcasted_iota(jnp.int32, (tm, tm), 1)
    tri = jnp.where(col <= row, 1.0, 0.0).astype(BF16)
    c = sum(_mm(tri, p.astype(BF16)) for p in _split3(lfw))
    tiles_per_seq = seq_len // tm

    @pl.when(pl.program_id(0) % tiles_per_seq == 0)
    def _():
        carry_ref[...] = jnp.zeros_like(carry_ref)

    c = c + carry_ref[...]
    carry_ref[...] = c[tm - 1:tm, :]
    c_hi, c_mid, c_lo = _split3(c)

    lane = lax.broadcasted_iota(jnp.int32, (tm, LANES), 1)
    for hd in range(FOX_HEADS):
        blk, par = divmod(hd, 2)
        sl = slice(blk * LANES, (blk + 1) * LANES)
        data = (lane < HD) if par == 0 else (lane >= HD)
        a = lane - (1 - par) * HD
        hi, mid, lo = c_hi[:, sl], c_mid[:, sl], c_lo[:, sl]
        q_aug = jnp.where(a == 0, hi, jnp.where(a == 1, mid, jnp.where(
            a == 2, lo, jnp.where((a >= 3) & (a < 6), 1.0, 0.0))))
        k_aug = jnp.where((a >= 0) & (a < 3), 1.0, jnp.where(a == 3, -hi, jnp.where(
            a == 4, -mid, jnp.where(a == 5, -lo, 0.0))))
        v_aug = jnp.where(a == 0, 1.0, 0.0)
        osl = slice(hd * LANES, (hd + 1) * LANES)
        qa_ref[:, osl] = jnp.where(data, fq[:, sl], q_aug).astype(BF16)
        ka_ref[:, osl] = jnp.where(data, fk[:, sl], k_aug).astype(BF16)
        va_ref[:, osl] = jnp.where(data, fv[:, sl], v_aug).astype(BF16)


def _project(x2d, g, w_all, bw, b8, *, seq_len, aug, tm):
    rows = x2d.shape[0]
    assert rows % tm == 0
    if aug:
        assert seq_len % tm == 0
    row_spec = lambda n: pl.BlockSpec((tm, n), lambda i: (i, 0))
    const = lambda shape: pl.BlockSpec(shape, lambda i: (0, 0))
    f32o = lambda n: jax.ShapeDtypeStruct((rows, n), F32)
    b16o = lambda n: jax.ShapeDtypeStruct((rows, n), BF16)
    out_shape = [f32o(WIDTH), f32o(WIDTH), f32o(FOX_HEADS), f32o(WIDTH), f32o(WIDTH)]
    out_specs = [row_spec(WIDTH), row_spec(WIDTH), row_spec(FOX_HEADS), row_spec(WIDTH), row_spec(WIDTH)]
    if aug:
        wide = FOX_HEADS * LANES
        out_shape += [b16o(wide)] * 3 + [b16o(WIDTH)] * 5 + [b16o(D_MODEL)] * 2
        out_specs += [row_spec(wide)] * 3 + [row_spec(WIDTH)] * 5 + [row_spec(D_MODEL)] * 2
        scratch = [pltpu.VMEM((1, WIDTH), F32)]
    else:
        out_shape += [b16o(WIDTH)] * 4 + [b16o(D_MODEL)] * 2
        out_specs += [row_spec(WIDTH)] * 4 + [row_spec(D_MODEL)] * 2
        scratch = []
    return pl.pallas_call(
        functools.partial(_proj_kernel, tm=tm, seq_len=seq_len, aug=aug),
        grid=(rows // tm,),
        in_specs=[row_spec(D_MODEL), const((1, D_MODEL)),
                  pl.BlockSpec((D_MODEL, _N_ALL), lambda i: (0, 0), pipeline_mode=pl.Buffered(1)),
                  const((1, WIDTH)), const((1, FOX_HEADS))],
        out_specs=out_specs,
        out_shape=out_shape,
        scratch_shapes=scratch,
        compiler_params=pltpu.CompilerParams(
            dimension_semantics=("arbitrary",), vmem_limit_bytes=VMEM_LIMIT),
        name="proj_aug" if aug else "proj_plain",
    )(x2d, g, w_all, bw, b8)


def _bias_kernel(tbl_ref, o_ref, *, q_start, nq, segments):
    hd = pl.program_id(0)
    far = tbl_ref[N_BUCKETS // 2 - 1, hd]
    off = 0
    for k_start, nk, valid in segments:
        qpos = q_start + lax.broadcasted_iota(jnp.int32, (nq, nk), 0)
        kidx = lax.broadcasted_iota(jnp.int32, (nq, nk), 1)
        kpos = k_start + kidx
        rel = kpos - qpos
        n = jnp.abs(rel)
        large = jnp.full((nq, nk), _MAX_EXACT, jnp.int32)
        for thr in _T5_THR:
            large = large + jnp.where(n >= thr, 1, 0)
        bucket = jnp.where(rel > 0, N_BUCKETS // 2, 0) + jnp.where(n < _MAX_EXACT, n, large)
        val = jnp.zeros((nq, nk), F32)
        for b in range(N_BUCKETS):
            val = jnp.where(bucket == b, tbl_ref[b, hd], val)
        val = val - far
        visible = ((kpos >> 6) <= (qpos >> 6)) & (kidx < valid)
        o_ref[0, :, off:off + nk] = jnp.where(visible, val, NEG)
        off += nk


def _bias_tiles(rel_bias, *, q_start, nq, segments):
    total = sum(s[1] for s in segments)
    return pl.pallas_call(
        functools.partial(_bias_kernel, q_start=q_start, nq=nq, segments=segments),
        grid=(DIFF_HEADS,),
        in_specs=[pl.BlockSpec(memory_space=pltpu.SMEM)],
        out_specs=pl.BlockSpec((1, nq, total), lambda h: (h, 0, 0)),
        out_shape=jax.ShapeDtypeStruct((DIFF_HEADS, nq, total), F32),
        name="t5_bias",
    )(rel_bias)


def _fox_kernel(qa_ref, ka_ref, va_ref, z_ref, o_ref, m_ref, acc_ref, *, blk):
    qi = pl.program_id(2)
    row = lax.broadcasted_iota(jnp.int32, (blk, blk), 0)
    col = lax.broadcasted_iota(jnp.int32, (blk, blk), 1)
    causal = col <= row
    lane = lax.broadcasted_iota(jnp.int32, (blk, LANES), 1)
    outs = []
    for par in range(2):
        sl = slice(par * LANES, (par + 1) * LANES)
        q = qa_ref[0, :, sl]
        m_ref[...] = jnp.full_like(m_ref, NEG)
        acc_ref[...] = jnp.zeros_like(acc_ref)

        def step(j, mask):
            start = pl.multiple_of(j * blk, blk)
            k = ka_ref[0, pl.ds(start, blk), sl]
            v = va_ref[0, pl.ds(start, blk), sl]
            s = _mm_nt(q, k)
            if mask is not None:
                s = jnp.where(mask, s, NEG)
            m_prev = m_ref[...]
            m_new = jnp.maximum(m_prev, jnp.max(s, axis=-1, keepdims=True))
            p = jnp.exp(s - m_new)
            acc_ref[...] = jnp.exp(m_prev - m_new) * acc_ref[...] + _mm(p.astype(BF16), v)
            m_ref[...] = m_new

        def body(j, carry):
            step(j, None)
            return carry

        lax.fori_loop(0, qi, body, 0)
        step(qi, causal)
        acc = acc_ref[...]
        ones_lane = (1 - par) * HD
        outs.append(acc / acc[:, ones_lane:ones_lane + 1])
    o = jnp.where(lane < HD, outs[0], outs[1])
    o_ref[0] = (o * _silu(z_ref[0].astype(F32))).astype(BF16)


def _fox_attention(qa, ka, va, fz, *, blk):
    b, t, _ = qa.shape
    return pl.pallas_call(
        functools.partial(_fox_kernel, blk=blk),
        grid=(b, PAIRS, t // blk),
        in_specs=[pl.BlockSpec((1, blk, 2 * LANES), lambda b, p, i: (b, i, p)),
                  pl.BlockSpec((1, t, 2 * LANES), lambda b, p, i: (b, 0, p)),
                  pl.BlockSpec((1, t, 2 * LANES), lambda b, p, i: (b, 0, p)),
                  pl.BlockSpec((1, blk, LANES), lambda b, p, i: (b, i, p))],
        out_specs=pl.BlockSpec((1, blk, LANES), lambda b, p, i: (b, i, p)),
        out_shape=jax.ShapeDtypeStruct((b, t, WIDTH), BF16),
        scratch_shapes=[pltpu.VMEM((blk, 1), F32), pltpu.VMEM((blk, LANES), F32)],
        compiler_params=pltpu.CompilerParams(
            dimension_semantics=("parallel", "parallel", "arbitrary"), vmem_limit_bytes=VMEM_LIMIT),
        name="fox_prompt",
    )(qa, ka, va, fz)


def _lambda(lq1_ref, lk1_ref, lq2_ref, lk2_ref):
    s1 = jnp.sum(lq1_ref[...] * lk1_ref[...], axis=-1, keepdims=True)
    s2 = jnp.sum(lq2_ref[...] * lk2_ref[...], axis=-1, keepdims=True)
    return jnp.exp(s1) - jnp.exp(s2) + LAMBDA_INIT


def _diff_finish(o0, o1, lam, subg, z):
    o = o0 - lam * o1
    ms = jnp.mean(o * o, axis=-1, keepdims=True)
    od = o * lax.rsqrt(ms + NORM_EPS) * subg * (1.0 - LAMBDA_INIT)
    return (od * _silu(z)).astype(BF16)


def _diff_kernel(q_ref, k_ref, v_ref, bias_ref, z_ref, lq1_ref, lk1_ref, lq2_ref, lk2_ref, subg_ref,
                 o_ref, m_ref, l_ref, acc_ref, *, blk):
    qi = pl.program_id(2)
    lane = lax.broadcasted_iota(jnp.int32, (blk, LANES), 1)
    q = q_ref[0]
    zero = jnp.zeros_like(q)
    outs = []
    for mp in range(2):
        qm = jnp.where((lane < HD) if mp == 0 else (lane >= HD), q, zero)
        m_ref[...] = jnp.full_like(m_ref, NEG)
        l_ref[...] = jnp.zeros_like(l_ref)
        acc_ref[...] = jnp.zeros_like(acc_ref)

        def step(j, bias):
            start = pl.multiple_of(j * blk, blk)
            k = k_ref[0, pl.ds(start, blk), :]
            v = v_ref[0, pl.ds(start, blk), :]
            s = _mm_nt(qm, k)
            if bias is not None:
                s = s + bias
            m_prev = m_ref[...]
            m_new = jnp.maximum(m_prev, jnp.max(s, axis=-1, keepdims=True))
            p = jnp.exp(s - m_new)
            alpha = jnp.exp(m_prev - m_new)
            l_ref[...] = alpha * l_ref[...] + jnp.sum(p, axis=-1, keepdims=True)
            acc_ref[...] = alpha * acc_ref[...] + _mm(p.astype(BF16), v)
            m_ref[...] = m_new

        def body(j, carry):
            step(j, None)
            return carry

        lax.fori_loop(0, jnp.maximum(qi - 1, 0), body, 0)

        @pl.when(qi >= 1)
        def _():
            step(qi - 1, bias_ref[0, :, 0:blk])

        step(qi, bias_ref[0, :, blk:2 * blk])
        outs.append(acc_ref[...] / l_ref[...])
    lam = _lambda(lq1_ref, lk1_ref, lq2_ref, lk2_ref)
    o_ref[0] = _diff_finish(outs[0], outs[1], lam, subg_ref[...], z_ref[0].astype(F32))


def _diff_attention(dq, dk, dv, bias, dz, lq1, lk1, lq2, lk2, subg, *, blk):
    b, t, _ = dq.shape
    small = lambda n: pl.BlockSpec((1, n), lambda b, h, i: (0, 0))
    return pl.pallas_call(
        functools.partial(_diff_kernel, blk=blk),
        grid=(b, DIFF_HEADS, t // blk),
        in_specs=[pl.BlockSpec((1, blk, LANES), lambda b, h, i: (b, i, h)),
                  pl.BlockSpec((1, t, LANES), lambda b, h, i: (b, 0, h)),
                  pl.BlockSpec((1, t, LANES), lambda b, h, i: (b, 0, h)),
                  pl.BlockSpec((1, blk, 2 * blk), lambda b, h, i: (h, 0, 0)),
                  pl.BlockSpec((1, blk, LANES), lambda b, h, i: (b, i, h)),
                  small(HD), small(HD), small(HD), small(HD), small(2 * HD)],
        out_specs=pl.BlockSpec((1, blk, LANES), lambda b, h, i: (b, i, h)),
        out_shape=jax.ShapeDtypeStruct((b, t, WIDTH), BF16),
        scratch_shapes=[pltpu.VMEM((blk, 1), F32), pltpu.VMEM((blk, 1), F32),
                        pltpu.VMEM((blk, LANES), F32)],
        compiler_params=pltpu.CompilerParams(
            dimension_semantics=("parallel", "parallel", "arbitrary"), vmem_limit_bytes=VMEM_LIMIT),
        name="diff_prompt",
    )(dq, dk, dv, bias, dz, lq1, lk1, lq2, lk2, subg)


def _suffix_sums(x):
    n = x.shape[1] // LANES
    lane = lax.broadcasted_iota(jnp.int32, (x.shape[0], LANES), 1)
    after = jnp.zeros((x.shape[0], 1), F32)
    pieces = [None] * n
    for blk in reversed(range(n)):
        piece = x[:, blk * LANES:(blk + 1) * LANES]
        y = piece
        shift = 1
        while shift < LANES:
            y = y + jnp.where(lane + shift < LANES, pltpu.roll(y, LANES - shift, 1), 0.0)
            shift *= 2
        pieces[blk] = y - piece + after
        after = after + y[:, 0:1]
    return jnp.concatenate(pieces, axis=1), after


def _pad_rows(x, rows):
    return jnp.concatenate([x, jnp.zeros((rows - x.shape[0], x.shape[1]), x.dtype)], axis=0)


def _joint_softmax_pv(s_p, s_n, vp, vn):
    m = jnp.maximum(jnp.max(s_p, axis=-1, keepdims=True), jnp.max(s_n, axis=-1, keepdims=True))
    p_p = jnp.exp(s_p - m)
    p_n = jnp.exp(s_n - m)
    l = jnp.sum(p_p, axis=-1, keepdims=True) + jnp.sum(p_n, axis=-1, keepdims=True)
    return (_mm(p_p.astype(BF16), vp) + _mm(p_n.astype(BF16), vn)) / l


def _fox_sample_kernel(q_ref, kp_ref, vp_ref, kn_ref, vn_ref, lfp_ref, lfn_ref, z_ref, o_ref, *, nq):
    r_new, total_new = _suffix_sums(lfn_ref[0])
    r_past, _ = _suffix_sums(lfp_ref[0])
    r_past = r_past + total_new
    lane = lax.broadcasted_iota(jnp.int32, (nq, LANES), 1)
    causal = lane <= lax.broadcasted_iota(jnp.int32, (nq, LANES), 0)
    for blk in range(PAIRS):
        sl = slice(blk * LANES, (blk + 1) * LANES)
        q = q_ref[:, sl]
        kp = kp_ref[0, :, sl].astype(BF16)
        vp = vp_ref[0, :, sl].astype(BF16)
        kn = _pad_rows(kn_ref[:, sl], LANES).astype(BF16)
        vn = _pad_rows(vn_ref[:, sl], LANES).astype(BF16)
        outs = []
        for par in range(2):
            hd = 2 * blk + par
            qm = jnp.where((lane < HD) if par == 0 else (lane >= HD), q, jnp.zeros_like(q))
            s_p = _mm_nt(qm, kp) + r_past[hd:hd + 1, :]
            s_n = jnp.where(causal, _mm_nt(qm, kn) + r_new[hd:hd + 1, :], NEG)
            outs.append(_joint_softmax_pv(s_p, s_n, vp, vn))
        o = jnp.where(lane < HD, outs[0], outs[1])
        o_ref[:, sl] = (o * _silu(z_ref[:, sl].astype(F32))).astype(BF16)


def _fox_sample(fq, cache_k, cache_v, fk, fv, lf_past_t, lf_new_t, fz, *, nq):
    b, past, _ = cache_k.shape
    new = lambda: pl.BlockSpec((nq, WIDTH), lambda i: (i, 0))
    cache = lambda: pl.BlockSpec((1, past, WIDTH), lambda i: (i, 0, 0))
    return pl.pallas_call(
        functools.partial(_fox_sample_kernel, nq=nq),
        grid=(b,),
        in_specs=[new(), cache(), cache(), new(), new(),
                  pl.BlockSpec((1, FOX_HEADS, past), lambda i: (i, 0, 0)),
                  pl.BlockSpec((1, FOX_HEADS, LANES), lambda i: (i, 0, 0)),
                  new()],
        out_specs=new(),
        out_shape=jax.ShapeDtypeStruct((b * nq, WIDTH), BF16),
        compiler_params=pltpu.CompilerParams(
            dimension_semantics=("parallel",), vmem_limit_bytes=VMEM_LIMIT),
        name="fox_sample",
    )(fq, cache_k, cache_v, fk, fv, lf_past_t, lf_new_t, fz)


def _diff_sample_kernel(q_ref, kp_ref, vp_ref, kn_ref, vn_ref, bias_ref, z_ref,
                        lq1_ref, lk1_ref, lq2_ref, lk2_ref, subg_ref, o_ref, *, nq, past):
    lane = lax.broadcasted_iota(jnp.int32, (nq, LANES), 1)
    lam = _lambda(lq1_ref, lk1_ref, lq2_ref, lk2_ref)
    for hd in range(DIFF_HEADS):
        sl = slice(hd * LANES, (hd + 1) * LANES)
        q = q_ref[:, sl]
        kp = kp_ref[0, :, sl].astype(BF16)
        vp = vp_ref[0, :, sl].astype(BF16)
        kn = _pad_rows(kn_ref[:, sl], LANES).astype(BF16)
        vn = _pad_rows(vn_ref[:, sl], LANES).astype(BF16)
        outs = []
        for mp in range(2):
            qm = jnp.where((lane < HD) if mp == 0 else (lane >= HD), q, jnp.zeros_like(q))
            s_p = _mm_nt(qm, kp) + bias_ref[hd, :, 0:past]
            s_n = _mm_nt(qm, kn) + bias_ref[hd, :, past:past + LANES]
            outs.append(_joint_softmax_pv(s_p, s_n, vp, vn))
        o_ref[:, sl] = _diff_finish(outs[0], outs[1], lam, subg_ref[...], z_ref[:, sl].astype(F32))


def _diff_sample(dq, cache_k, cache_v, dk, dv, bias, dz, lq1, lk1, lq2, lk2, subg, *, nq):
    b, past, _ = cache_k.shape
    new = lambda: pl.BlockSpec((nq, WIDTH), lambda i: (i, 0))
    cache = lambda: pl.BlockSpec((1, past, WIDTH), lambda i: (i, 0, 0))
    small = lambda n: pl.BlockSpec((1, n), lambda i: (0, 0))
    return pl.pallas_call(
        functools.partial(_diff_sample_kernel, nq=nq, past=past),
        grid=(b,),
        in_specs=[new(), cache(), cache(), new(), new(),
                  pl.BlockSpec((DIFF_HEADS, nq, past + LANES), lambda i: (0, 0, 0)),
                  new(), small(HD), small(HD), small(HD), small(HD), small(2 * HD)],
        out_specs=new(),
        out_shape=jax.ShapeDtypeStruct((b * nq, WIDTH), BF16),
        compiler_params=pltpu.CompilerParams(
            dimension_semantics=("parallel",), vmem_limit_bytes=VMEM_LIMIT),
        name="diff_sample",
    )(dq, cache_k, cache_v, dk, dv, bias, dz, lq1, lk1, lq2, lk2, subg)


def _out_kernel(yf_ref, yd_ref, ga_ref, gb_ref, x_ref, wuf_ref, wud_ref, wo_ref, g_ref, o_ref):
    merged = (ga_ref[...].astype(F32) * _mm(yf_ref[...], wuf_ref[...])
              + gb_ref[...].astype(F32) * _mm(yd_ref[...], wud_ref[...]))
    out = x_ref[...] + _mm(merged.astype(BF16), wo_ref[...])
    ms = jnp.mean(out * out, axis=-1, keepdims=True)
    o_ref[...] = out * lax.rsqrt(ms + NORM_EPS) * g_ref[...]


def _output(yf, yd, ga, gb, x2d, wuf, wud, wo, g, *, tm):
    rows = x2d.shape[0]
    assert rows % tm == 0
    row_spec = lambda n: pl.BlockSpec((tm, n), lambda i: (i, 0))
    const = lambda shape: pl.BlockSpec(shape, lambda i: (0, 0))
    return pl.pallas_call(
        _out_kernel,
        grid=(rows // tm,),
        in_specs=[row_spec(WIDTH), row_spec(WIDTH), row_spec(D_MODEL), row_spec(D_MODEL), row_spec(D_MODEL),
                  const((WIDTH, D_MODEL)), const((WIDTH, D_MODEL)), const((D_MODEL, D_MODEL)),
                  const((1, D_MODEL))],
        out_specs=row_spec(D_MODEL),
        out_shape=jax.ShapeDtypeStruct((rows, D_MODEL), F32),
        compiler_params=pltpu.CompilerParams(
            dimension_semantics=("parallel",), vmem_limit_bytes=VMEM_LIMIT),
        name="out_proj",
    )(yf, yd, ga, gb, x2d, wuf, wud, wo, g)


def _pack_w_in(w_in, b_forget):
    sizes = (WIDTH, WIDTH, WIDTH, WIDTH, FOX_HEADS, WIDTH, WIDTH, WIDTH, WIDTH, D_MODEL, D_MODEL)
    offs = np.cumsum((0,) + sizes)
    fq, fk, fv, fz, fl, dq, dk, dv, dz, ga, gb = (w_in[:, offs[i]:offs[i + 1]] for i in range(len(sizes)))
    swap = np.arange(FOX_HEADS) ^ 1
    flw = jnp.repeat(fl[:, swap], HD, axis=1)
    fl8 = jnp.pad(fl, ((0, 0), (0, LANES - FOX_HEADS)))
    w_all = jnp.concatenate([fq, fk, fv, fz, dq, dk, dv, dz, ga, gb, flw, fl8], axis=1).astype(BF16)
    bw = jnp.repeat(b_forget[swap], HD)[None, :]
    return w_all, bw, b_forget[None, :]


def kernel(x_prompt, x_sample, cache_fox_k, cache_fox_v, cache_fox_logf, cache_diff_k, cache_diff_v,
           norm_in_g, w_in, b_forget, lambda_q1, lambda_k1, lambda_q2, lambda_k2, subln_g,
           w_up_fox, w_up_diff, w_o, rel_bias, final_norm_g):
    bp, tp, _ = x_prompt.shape
    bs, ts, _ = x_sample.shape
    past = cache_fox_k.shape[2]
    blk = 256
    tm = 512

    w_all, bw, b8 = _pack_w_in(w_in[0], b_forget[0])
    g_in = norm_in_g[0][None, :]
    g_out = final_norm_g[None, :]
    wuf, wud, wo = w_up_fox[0].astype(BF16), w_up_diff[0].astype(BF16), w_o[0].astype(BF16)
    lq1, lk1, lq2, lk2 = (a[0][None, :] for a in (lambda_q1, lambda_k1, lambda_q2, lambda_k2))
    subg = subln_g[0][None, :]

    xp = x_prompt.reshape(bp * tp, D_MODEL)
    (fk_p, fv_p, lf_p, dk_p, dv_p, qa, ka, va, fz, dq, dkb, dvb, dz, ga, gb) = _project(
        xp, g_in, w_all, bw, b8, seq_len=tp, aug=True, tm=tm)
    r3 = lambda a: a.reshape(bp, tp, a.shape[-1])
    yf = _fox_attention(r3(qa), r3(ka), r3(va), r3(fz), blk=blk)
    bias_p = _bias_tiles(rel_bias, q_start=blk, nq=blk, segments=((0, blk, blk), (blk, blk, blk)))
    yd = _diff_attention(r3(dq), r3(dkb), r3(dvb), bias_p, r3(dz), lq1, lk1, lq2, lk2, subg, blk=blk)
    y_p = _output(yf.reshape(bp * tp, WIDTH), yd.reshape(bp * tp, WIDTH), ga, gb, xp, wuf, wud, wo, g_out, tm=tm)

    xs = x_sample.reshape(bs * ts, D_MODEL)
    (fk_s, fv_s, lf_s, dk_s, dv_s, fq_s, fz_s, dq_s, dz_s, ga_s, gb_s) = _project(
        xs, g_in, w_all, bw, b8, seq_len=ts, aug=False, tm=tm)
    lf_past_t = jnp.swapaxes(cache_fox_logf[0], 1, 2)
    lf_new_t = jnp.pad(jnp.swapaxes(lf_s.reshape(bs, ts, FOX_HEADS), 1, 2), ((0, 0), (0, 0), (0, LANES - ts)))
    yf_s = _fox_sample(fq_s, cache_fox_k[0].reshape(bs, past, WIDTH), cache_fox_v[0].reshape(bs, past, WIDTH),
                       fk_s, fv_s, lf_past_t, lf_new_t, fz_s, nq=ts)
    bias_s = _bias_tiles(rel_bias, q_start=past, nq=ts, segments=((0, past, past), (past, LANES, ts)))
    yd_s = _diff_sample(dq_s, cache_diff_k[0].reshape(bs, past, WIDTH), cache_diff_v[0].reshape(bs, past, WIDTH),
                        dk_s, dv_s, bias_s, dz_s, lq1, lk1, lq2, lk2, subg, nq=ts)
    y_s = _output(yf_s, yd_s, ga_s, gb_s, xs, wuf, wud, wo, g_out, tm=tm)

    return (y_p.reshape(bp, tp, D_MODEL), y_s.reshape(bs, ts, D_MODEL),
            fk_p.reshape(1, bp, tp, FOX_HEADS, HD), fv_p.reshape(1, bp, tp, FOX_HEADS, HD),
            lf_p.reshape(1, bp, tp, FOX_HEADS),
            dk_p.reshape(1, bp, tp, DIFF_HEADS, 2, HD), dv_p.reshape(1, bp, tp, DIFF_HEADS, 2 * HD),
            fk_s.reshape(1, bs, ts, FOX_HEADS, HD), fv_s.reshape(1, bs, ts, FOX_HEADS, HD),
            lf_s.reshape(1, bs, ts, FOX_HEADS),
            dk_s.reshape(1, bs, ts, DIFF_HEADS, 2, HD), dv_s.reshape(1, bs, ts, DIFF_HEADS, 2 * HD))
```

```python
import functools
import math

import numpy as np
import jax
import jax.numpy as jnp
from jax import lax
from jax.experimental import pallas as pl
from jax.experimental.pallas import tpu as pltpu

F32 = jnp.float32
BF16 = jnp.bfloat16

D_MODEL = 1024
HD = 64
FOX_HEADS = 8
DIFF_HEADS = 4
WIDTH = 512
CHUNK = 64
CHUNK_SHIFT = 6
N_BUCKETS = 32
MAX_DISTANCE = 128
NORM_EPS = 1e-6
SCALE = HD ** -0.5
NEG = -1e30
LAMBDA_INIT = 0.8 - 0.6 * math.exp(-0.3 * 0)

LANES = 128
PAIRS = WIDTH // LANES
VT_PAD = 16
VT_ROWS = LANES + VT_PAD

_O_FQ, _O_FK, _O_FV, _O_FZ = 0, 512, 1024, 1536
_O_DQ, _O_DK, _O_DV, _O_DZ = 2048, 2560, 3072, 3584
_O_GA, _O_GB = 4096, 5120
_O_FLW = 6144
_O_FL8 = 6656
_N_ALL = 6784

VMEM_LIMIT = 56 * 1024 * 1024

FOX_STEP_HEADS = 8
DIFF_STEP_HEADS = 4


def _t5_thresholds():
    nb = N_BUCKETS // 2
    max_exact = nb // 2
    n = np.arange(0, 4 * MAX_DISTANCE)
    large = max_exact + (np.log(np.maximum(n, 1).astype(np.float32) / max_exact)
                         / math.log(MAX_DISTANCE / max_exact) * (nb - max_exact)).astype(np.int32)
    large = np.minimum(large, nb - 1)
    thr = [int(np.argmax(large >= b)) for b in range(max_exact + 1, nb)]
    return max_exact, tuple(thr)


_MAX_EXACT, _T5_THR = _t5_thresholds()
FAR_DIST = _T5_THR[-1]


def _mm(a, b):
    return jnp.dot(a, b, preferred_element_type=F32)


def _mm_nt(a, b):
    return lax.dot_general(a, b, (((1,), (1,)), ((), ())), preferred_element_type=F32)


def _log_sigmoid(x):
    return -(jnp.maximum(-x, 0.0) + jnp.log1p(jnp.exp(-jnp.abs(x))))


def _sigmoid(x):
    return 1.0 / (1.0 + jnp.exp(-x))


def _silu(x):
    return x * _sigmoid(x)


def _split3(x):
    p1 = x.astype(BF16).astype(F32)
    r1 = x - p1
    p2 = r1.astype(BF16).astype(F32)
    p3 = r1 - p2
    return p1, p2, p3


def _proj_kernel(x_ref, g_ref, w_ref, bw_ref, b8_ref, *refs, tm, seq_len, aug):
    if aug:
        (fk_ref, fv_ref, lf_ref, dk_ref, dv_ref, qa_ref, ka_ref, vtf_ref,
         fz_ref, dq_ref, dkb_ref, vtd_ref, dz_ref, ga_ref, gb_ref, carry_ref) = refs
    else:
        (fk_ref, fv_ref, lf_ref, dk_ref, dv_ref, fq_ref,
         fz_ref, dq_ref, dz_ref, ga_ref, gb_ref) = refs

    x = x_ref[...]
    ms = jnp.mean(x * x, axis=-1, keepdims=True)
    h = (x * lax.rsqrt(ms + NORM_EPS) * g_ref[...]).astype(BF16)

    def grp(off, n=WIDTH):
        return _mm(h, w_ref[:, off:off + n])

    fq = grp(_O_FQ) * SCALE
    fk = grp(_O_FK)
    fv = grp(_O_FV)
    fk_ref[...] = fk
    fv_ref[...] = fv
    fz_ref[...] = grp(_O_FZ).astype(BF16)
    dq_ref[...] = (grp(_O_DQ) * SCALE).astype(BF16)
    dk = grp(_O_DK)
    dv = grp(_O_DV)
    dk_ref[...] = dk
    dv_ref[...] = dv
    dz_ref[...] = grp(_O_DZ).astype(BF16)
    ga_ref[...] = _sigmoid(grp(_O_GA, D_MODEL)).astype(BF16)
    gb_ref[...] = _sigmoid(grp(_O_GB, D_MODEL)).astype(BF16)
    lf8 = _log_sigmoid(grp(_O_FL8, LANES)[:, :FOX_HEADS] + b8_ref[...])
    lf_ref[...] = lf8

    if not aug:
        fq_ref[...] = fq.astype(BF16)
        return

    dkb_ref[...] = dk.astype(BF16)
    tail = jnp.where(lax.broadcasted_iota(jnp.int32, (VT_PAD, tm), 0) == 0, 1.0, 0.0).astype(BF16)
    for hd in range(DIFF_HEADS):
        base = hd * VT_ROWS
        vtd_ref[0, base:base + LANES, :] = dv[:, hd * LANES:(hd + 1) * LANES].T.astype(BF16)
        vtd_ref[0, base + LANES:base + VT_ROWS, :] = tail

    lfw = _log_sigmoid(grp(_O_FLW) + bw_ref[...])
    row = lax.broadcasted_iota(jnp.int32, (tm, tm), 0)
    col = lax.broadcasted_iota(jnp.int32, (tm, tm), 1)
    tri = jnp.where(col <= row, 1.0, 0.0).astype(BF16)
    c = sum(_mm(tri, p.astype(BF16)) for p in _split3(lfw))
    tiles_per_seq = seq_len // tm

    @pl.when(pl.program_id(0) % tiles_per_seq == 0)
    def _():
        carry_ref[...] = jnp.zeros_like(carry_ref)

    c = c + carry_ref[...]
    carry_ref[...] = c[tm - 1:tm, :]
    c_hi, c_mid, c_lo = _split3(c)

    lane = lax.broadcasted_iota(jnp.int32, (tm, LANES), 1)
    for hd in range(FOX_HEADS):
        blk, par = divmod(hd, 2)
        sl = slice(blk * LANES, (blk + 1) * LANES)
        data = (lane < HD) if par == 0 else (lane >= HD)
        a = lane - (1 - par) * HD
        hi, mid, lo = c_hi[:, sl], c_mid[:, sl], c_lo[:, sl]
        q_aug = jnp.where(a == 0, hi, jnp.where(a == 1, mid, jnp.where(
            a == 2, lo, jnp.where((a >= 3) & (a < 6), 1.0, 0.0))))
        k_aug = jnp.where((a >= 0) & (a < 3), 1.0, jnp.where(a == 3, -hi, jnp.where(
            a == 4, -mid, jnp.where(a == 5, -lo, 0.0))))
        v_aug = jnp.where(a == 0, 1.0, 0.0)
        osl = slice(hd * LANES, (hd + 1) * LANES)
        qa_ref[:, osl] = jnp.where(data, fq[:, sl], q_aug).astype(BF16)
        ka_ref[:, osl] = jnp.where(data, fk[:, sl], k_aug).astype(BF16)
        vtf_ref[0, osl, :] = jnp.where(data, fv[:, sl], v_aug).T.astype(BF16)


def _project(x2d, g, w_all, bw, b8, *, seq_len, aug, tm):
    rows = x2d.shape[0]
    assert rows % tm == 0
    if aug:
        assert seq_len % tm == 0
    row_spec = lambda n: pl.BlockSpec((tm, n), lambda i: (i, 0))
    const = lambda shape: pl.BlockSpec(shape, lambda i: (0, 0))
    f32o = lambda n: jax.ShapeDtypeStruct((rows, n), F32)
    b16o = lambda n: jax.ShapeDtypeStruct((rows, n), BF16)
    out_shape = [f32o(WIDTH), f32o(WIDTH), f32o(FOX_HEADS), f32o(WIDTH), f32o(WIDTH)]
    out_specs = [row_spec(WIDTH), row_spec(WIDTH), row_spec(FOX_HEADS), row_spec(WIDTH), row_spec(WIDTH)]
    if aug:
        wide = FOX_HEADS * LANES
        tps = seq_len // tm
        vt_shape = lambda n: jax.ShapeDtypeStruct((rows // seq_len, n, seq_len), BF16)
        vt_spec = lambda n: pl.BlockSpec((1, n, tm), lambda i: (i // tps, 0, i % tps))
        out_shape += ([b16o(wide)] * 2 + [vt_shape(wide)] + [b16o(WIDTH)] * 3
                      + [vt_shape(DIFF_HEADS * VT_ROWS)] + [b16o(WIDTH)] + [b16o(D_MODEL)] * 2)
        out_specs += ([row_spec(wide)] * 2 + [vt_spec(wide)] + [row_spec(WIDTH)] * 3
                      + [vt_spec(DIFF_HEADS * VT_ROWS)] + [row_spec(WIDTH)] + [row_spec(D_MODEL)] * 2)
        scratch = [pltpu.VMEM((1, WIDTH), F32)]
    else:
        out_shape += [b16o(WIDTH)] * 4 + [b16o(D_MODEL)] * 2
        out_specs += [row_spec(WIDTH)] * 4 + [row_spec(D_MODEL)] * 2
        scratch = []
    return pl.pallas_call(
        functools.partial(_proj_kernel, tm=tm, seq_len=seq_len, aug=aug),
        grid=(rows // tm,),
        in_specs=[row_spec(D_MODEL), const((1, D_MODEL)),
                  pl.BlockSpec((D_MODEL, _N_ALL), lambda i: (0, 0), pipeline_mode=pl.Buffered(1)),
                  const((1, WIDTH)), const((1, FOX_HEADS))],
        out_specs=out_specs,
        out_shape=out_shape,
        scratch_shapes=scratch,
        compiler_params=pltpu.CompilerParams(
            dimension_semantics=("arbitrary",), vmem_limit_bytes=VMEM_LIMIT),
        name="proj_aug" if aug else "proj_plain",
    )(x2d, g, w_all, bw, b8)


def _bias_kernel(tbl_ref, o_ref, *, q_start, nq, segments, transposed):
    hd = pl.program_id(0)
    far = tbl_ref[N_BUCKETS // 2 - 1, hd]
    q_axis, k_axis = (1, 0) if transposed else (0, 1)
    off = 0
    for k_start, nk, valid in segments:
        shape = (nk, nq) if transposed else (nq, nk)
        qpos = q_start + lax.broadcasted_iota(jnp.int32, shape, q_axis)
        kidx = lax.broadcasted_iota(jnp.int32, shape, k_axis)
        kpos = k_start + kidx
        rel = kpos - qpos
        n = jnp.abs(rel)
        large = jnp.full(shape, _MAX_EXACT, jnp.int32)
        for thr in _T5_THR:
            large = large + jnp.where(n >= thr, 1, 0)
        bucket = jnp.where(rel > 0, N_BUCKETS // 2, 0) + jnp.where(n < _MAX_EXACT, n, large)
        val = jnp.zeros(shape, F32)
        for b in range(N_BUCKETS):
            val = jnp.where(bucket == b, tbl_ref[b, hd], val)
        val = val - far
        visible = ((kpos >> CHUNK_SHIFT) <= (qpos >> CHUNK_SHIFT)) & (kidx < valid)
        val = jnp.where(visible, val, NEG)
        if transposed:
            o_ref[0, off:off + nk, :] = val
        else:
            o_ref[0, :, off:off + nk] = val
        off += nk


def _bias_tiles(rel_bias, *, q_start, nq, segments, transposed):
    total = sum(s[1] for s in segments)
    shape = (total, nq) if transposed else (nq, total)
    return pl.pallas_call(
        functools.partial(_bias_kernel, q_start=q_start, nq=nq, segments=segments, transposed=transposed),
        grid=(DIFF_HEADS,),
        in_specs=[pl.BlockSpec(memory_space=pltpu.SMEM)],
        out_specs=pl.BlockSpec((1,) + shape, lambda h: (h, 0, 0)),
        out_shape=jax.ShapeDtypeStruct((DIFF_HEADS,) + shape, F32),
        name="t5_bias",
    )(rel_bias)


def _flash_t(k_ref, vt_ref, acc_ref, streams, qi, blk, prev_bias, diag_bias, diag_mask):
    n = len(streams)
    acc_ref[...] = jnp.zeros_like(acc_ref)

    def step(j, ms, biases, mask):
        start = pl.multiple_of(j * blk, blk)
        sts = [_mm_nt(k_ref[0, pl.ds(start, blk), ksl], q) for q, ksl, _ in streams]
        out, ps, alphas = [], [], []
        for s, st in enumerate(sts):
            if biases is not None:
                st = st + biases[s]()
            if mask is not None:
                st = jnp.where(mask, st, NEG)
            m_new = jnp.maximum(ms[s], jnp.max(st, axis=0, keepdims=True))
            ps.append(jnp.exp(st - m_new).astype(BF16))
            alphas.append(jnp.exp(ms[s] - m_new))
            out.append(m_new)
        for s, (_, _, vsl) in enumerate(streams):
            acc_ref[s] = alphas[s] * acc_ref[s] + _mm(vt_ref[0, vsl, pl.ds(start, blk)], ps[s])
        return tuple(out)

    ms = tuple(jnp.full((1, blk), NEG, F32) for _ in range(n))
    if prev_bias is None:
        ms = lax.fori_loop(0, qi, lambda j, c: step(j, c, None, None), ms)
    else:
        ms = lax.fori_loop(0, jnp.maximum(qi - 1, 0), lambda j, c: step(j, c, None, None), ms)
        ms = lax.cond(qi >= 1, lambda c: step(qi - 1, c, prev_bias, None), lambda c: c, ms)
    step(qi, ms, diag_bias, diag_mask)


def _fox_kernel(qa_ref, ka_ref, vt_ref, z_ref, o_ref, acc_ref, *, blk, nh):
    qi = pl.program_id(2)
    key = lax.broadcasted_iota(jnp.int32, (blk, blk), 0)
    qry = lax.broadcasted_iota(jnp.int32, (blk, blk), 1)
    hsl = [slice(hd * LANES, (hd + 1) * LANES) for hd in range(nh)]
    streams = [(qa_ref[0, :, sl], sl, sl) for sl in hsl]
    _flash_t(ka_ref, vt_ref, acc_ref, streams, qi, blk, None, None, key <= qry)
    lane = lax.broadcasted_iota(jnp.int32, (blk, LANES), 1)
    for pair in range(nh // 2):
        outs = []
        for par in range(2):
            acc = acc_ref[2 * pair + par]
            ones_row = (1 - par) * HD
            outs.append((acc / acc[ones_row:ones_row + 1, :]).T)
        o = jnp.where(lane < HD, outs[0], outs[1])
        o_ref[0, :, hsl[pair]] = (o * _silu(z_ref[0, :, hsl[pair]].astype(F32))).astype(BF16)


def _fox_attention(qa, ka, vt, fz, *, blk, nh):
    b, t, _ = qa.shape
    return pl.pallas_call(
        functools.partial(_fox_kernel, blk=blk, nh=nh),
        grid=(b, FOX_HEADS // nh, t // blk),
        in_specs=[pl.BlockSpec((1, blk, nh * LANES), lambda b, p, i: (b, i, p)),
                  pl.BlockSpec((1, t, nh * LANES), lambda b, p, i: (b, 0, p)),
                  pl.BlockSpec((1, nh * LANES, t), lambda b, p, i: (b, p, 0)),
                  pl.BlockSpec((1, blk, nh * HD), lambda b, p, i: (b, i, p))],
        out_specs=pl.BlockSpec((1, blk, nh * HD), lambda b, p, i: (b, i, p)),
        out_shape=jax.ShapeDtypeStruct((b, t, WIDTH), BF16),
        scratch_shapes=[pltpu.VMEM((nh, LANES, blk), F32)],
        compiler_params=pltpu.CompilerParams(
            dimension_semantics=("parallel", "parallel", "arbitrary"), vmem_limit_bytes=VMEM_LIMIT),
        name="fox_prompt",
    )(qa, ka, vt, fz)


def _lambda(lq1_ref, lk1_ref, lq2_ref, lk2_ref):
    s1 = jnp.sum(lq1_ref[...] * lk1_ref[...], axis=-1, keepdims=True)
    s2 = jnp.sum(lq2_ref[...] * lk2_ref[...], axis=-1, keepdims=True)
    return jnp.exp(s1) - jnp.exp(s2) + LAMBDA_INIT


def _diff_finish(o0, o1, lam, subg, z):
    o = o0 - lam * o1
    ms = jnp.mean(o * o, axis=-1, keepdims=True)
    od = o * lax.rsqrt(ms + NORM_EPS) * subg * (1.0 - LAMBDA_INIT)
    return (od * _silu(z)).astype(BF16)


def _diff_kernel(q_ref, k_ref, vt_ref, bias_ref, z_ref, lq1_ref, lk1_ref, lq2_ref, lk2_ref, subg_ref,
                 o_ref, acc_ref, *, blk, nh):
    qi = pl.program_id(2)
    lane = lax.broadcasted_iota(jnp.int32, (blk, LANES), 1)
    streams, prev_bias, diag_bias = [], [], []
    for hd in range(nh):
        q = q_ref[0, :, hd * LANES:(hd + 1) * LANES]
        for mp in range(2):
            streams.append((jnp.where((lane < HD) if mp == 0 else (lane >= HD), q, jnp.zeros_like(q)),
                            slice(hd * LANES, (hd + 1) * LANES), slice(hd * VT_ROWS, (hd + 1) * VT_ROWS)))
            prev_bias.append(lambda hd=hd: bias_ref[hd, 0:blk, :])
            diag_bias.append(lambda hd=hd: bias_ref[hd, blk:2 * blk, :])
    _flash_t(k_ref, vt_ref, acc_ref, streams, qi, blk, prev_bias, diag_bias, None)
    lam = _lambda(lq1_ref, lk1_ref, lq2_ref, lk2_ref)
    for hd in range(nh):
        outs = []
        for mp in range(2):
            acc = acc_ref[2 * hd + mp]
            outs.append((acc[0:LANES, :] / acc[LANES:LANES + 1, :]).T)
        sl = slice(hd * LANES, (hd + 1) * LANES)
        o_ref[0, :, sl] = _diff_finish(outs[0], outs[1], lam, subg_ref[...], z_ref[0, :, sl].astype(F32))


def _diff_attention(dq, dk, dvt, bias, dz, lq1, lk1, lq2, lk2, subg, *, blk, nh):
    b, t, _ = dq.shape
    small = lambda n: pl.BlockSpec((1, n), lambda b, h, i: (0, 0))
    return pl.pallas_call(
        functools.partial(_diff_kernel, blk=blk, nh=nh),
        grid=(b, DIFF_HEADS // nh, t // blk),
        in_specs=[pl.BlockSpec((1, blk, nh * LANES), lambda b, h, i: (b, i, h)),
                  pl.BlockSpec((1, t, nh * LANES), lambda b, h, i: (b, 0, h)),
                  pl.BlockSpec((1, nh * VT_ROWS, t), lambda b, h, i: (b, h, 0)),
                  pl.BlockSpec((nh, 2 * blk, blk), lambda b, h, i: (h, 0, 0)),
                  pl.BlockSpec((1, blk, nh * LANES), lambda b, h, i: (b, i, h)),
                  small(HD), small(HD), small(HD), small(HD), small(2 * HD)],
        out_specs=pl.BlockSpec((1, blk, nh * LANES), lambda b, h, i: (b, i, h)),
        out_shape=jax.ShapeDtypeStruct((b, t, WIDTH), BF16),
        scratch_shapes=[pltpu.VMEM((2 * nh, VT_ROWS, blk), F32)],
        compiler_params=pltpu.CompilerParams(
            dimension_semantics=("parallel", "parallel", "arbitrary"), vmem_limit_bytes=VMEM_LIMIT),
        name="diff_prompt",
    )(dq, dk, dvt, bias, dz, lq1, lk1, lq2, lk2, subg)


def _suffix_sums(x):
    n = x.shape[1] // LANES
    lane = lax.broadcasted_iota(jnp.int32, (x.shape[0], LANES), 1)
    after = jnp.zeros((x.shape[0], 1), F32)
    pieces = [None] * n
    for blk in reversed(range(n)):
        piece = x[:, blk * LANES:(blk + 1) * LANES]
        y = piece
        shift = 1
        while shift < LANES:
            y = y + jnp.where(lane + shift < LANES, pltpu.roll(y, LANES - shift, 1), 0.0)
            shift *= 2
        pieces[blk] = y - piece + after
        after = after + y[:, 0:1]
    return jnp.concatenate(pieces, axis=1), after


def _pad_rows(x, rows):
    return jnp.concatenate([x, jnp.zeros((rows - x.shape[0], x.shape[1]), x.dtype)], axis=0)


def _joint_softmax_pv(s_p, s_n, vp, vn):
    m = jnp.maximum(jnp.max(s_p, axis=-1, keepdims=True), jnp.max(s_n, axis=-1, keepdims=True))
    p_p = jnp.exp(s_p - m)
    p_n = jnp.exp(s_n - m)
    l = jnp.sum(p_p, axis=-1, keepdims=True) + jnp.sum(p_n, axis=-1, keepdims=True)
    return (_mm(p_p.astype(BF16), vp) + _mm(p_n.astype(BF16), vn)) / l


def _fox_sample_kernel(q_ref, kp_ref, vp_ref, kn_ref, vn_ref, lfp_ref, lfn_ref, z_ref, o_ref, *, nq):
    r_new, total_new = _suffix_sums(lfn_ref[0])
    r_past, _ = _suffix_sums(lfp_ref[0])
    r_past = r_past + total_new
    lane = lax.broadcasted_iota(jnp.int32, (nq, LANES), 1)
    causal = lane <= lax.broadcasted_iota(jnp.int32, (nq, LANES), 0)
    for blk in range(PAIRS):
        sl = slice(blk * LANES, (blk + 1) * LANES)
        q = q_ref[:, sl]
        kp = kp_ref[0, :, sl].astype(BF16)
        vp = vp_ref[0, :, sl].astype(BF16)
        kn = _pad_rows(kn_ref[:, sl], LANES).astype(BF16)
        vn = _pad_rows(vn_ref[:, sl], LANES).astype(BF16)
        outs = []
        for par in range(2):
            hd = 2 * blk + par
            qm = jnp.where((lane < HD) if par == 0 else (lane >= HD), q, jnp.zeros_like(q))
            s_p = _mm_nt(qm, kp) + r_past[hd:hd + 1, :]
            s_n = jnp.where(causal, _mm_nt(qm, kn) + r_new[hd:hd + 1, :], NEG)
            outs.append(_joint_softmax_pv(s_p, s_n, vp, vn))
        o = jnp.where(lane < HD, outs[0], outs[1])
        o_ref[:, sl] = (o * _silu(z_ref[:, sl].astype(F32))).astype(BF16)


def _fox_sample(fq, cache_k, cache_v, fk, fv, lf_past_t, lf_new_t, fz, *, nq):
    b, past, _ = cache_k.shape
    new = lambda: pl.BlockSpec((nq, WIDTH), lambda i: (i, 0))
    cache = lambda: pl.BlockSpec((1, past, WIDTH), lambda i: (i, 0, 0))
    return pl.pallas_call(
        functools.partial(_fox_sample_kernel, nq=nq),
        grid=(b,),
        in_specs=[new(), cache(), cache(), new(), new(),
                  pl.BlockSpec((1, FOX_HEADS, past), lambda i: (i, 0, 0)),
                  pl.BlockSpec((1, FOX_HEADS, LANES), lambda i: (i, 0, 0)),
                  new()],
        out_specs=new(),
        out_shape=jax.ShapeDtypeStruct((b * nq, WIDTH), BF16),
        compiler_params=pltpu.CompilerParams(
            dimension_semantics=("parallel",), vmem_limit_bytes=VMEM_LIMIT),
        name="fox_sample",
    )(fq, cache_k, cache_v, fk, fv, lf_past_t, lf_new_t, fz)


def _diff_sample_kernel(q_ref, kp_ref, vp_ref, kn_ref, vn_ref, bias_ref, z_ref,
                        lq1_ref, lk1_ref, lq2_ref, lk2_ref, subg_ref, o_ref, *, nq, past):
    lane = lax.broadcasted_iota(jnp.int32, (nq, LANES), 1)
    lam = _lambda(lq1_ref, lk1_ref, lq2_ref, lk2_ref)
    for hd in range(DIFF_HEADS):
        sl = slice(hd * LANES, (hd + 1) * LANES)
        q = q_ref[:, sl]
        kp = kp_ref[0, :, sl].astype(BF16)
        vp = vp_ref[0, :, sl].astype(BF16)
        kn = _pad_rows(kn_ref[:, sl], LANES).astype(BF16)
        vn = _pad_rows(vn_ref[:, sl], LANES).astype(BF16)
        outs = []
        for mp in range(2):
            qm = jnp.where((lane < HD) if mp == 0 else (lane >= HD), q, jnp.zeros_like(q))
            s_p = _mm_nt(qm, kp) + bias_ref[hd, :, 0:past]
            s_n = _mm_nt(qm, kn) + bias_ref[hd, :, past:past + LANES]
            outs.append(_joint_softmax_pv(s_p, s_n, vp, vn))
        o_ref[:, sl] = _diff_finish(outs[0], outs[1], lam, subg_ref[...], z_ref[:, sl].astype(F32))


def _diff_sample(dq, cache_k, cache_v, dk, dv, bias, dz, lq1, lk1, lq2, lk2, subg, *, nq):
    b, past, _ = cache_k.shape
    new = lambda: pl.BlockSpec((nq, WIDTH), lambda i: (i, 0))
    cache = lambda: pl.BlockSpec((1, past, WIDTH), lambda i: (i, 0, 0))
    small = lambda n: pl.BlockSpec((1, n), lambda i: (0, 0))
    return pl.pallas_call(
        functools.partial(_diff_sample_kernel, nq=nq, past=past),
        grid=(b,),
        in_specs=[new(), cache(), cache(), new(), new(),
                  pl.BlockSpec((DIFF_HEADS, nq, past + LANES), lambda i: (0, 0, 0)),
                  new(), small(HD), small(HD), small(HD), small(HD), small(2 * HD)],
        out_specs=new(),
        out_shape=jax.ShapeDtypeStruct((b * nq, WIDTH), BF16),
        compiler_params=pltpu.CompilerParams(
            dimension_semantics=("parallel",), vmem_limit_bytes=VMEM_LIMIT),
        name="diff_sample",
    )(dq, cache_k, cache_v, dk, dv, bias, dz, lq1, lk1, lq2, lk2, subg)


def _out_kernel(yf_ref, yd_ref, ga_ref, gb_ref, x_ref, wuf_ref, wud_ref, wo_ref, g_ref, o_ref):
    merged = (ga_ref[...].astype(F32) * _mm(yf_ref[...], wuf_ref[...])
              + gb_ref[...].astype(F32) * _mm(yd_ref[...], wud_ref[...]))
    out = x_ref[...] + _mm(merged.astype(BF16), wo_ref[...])
    ms = jnp.mean(out * out, axis=-1, keepdims=True)
    o_ref[...] = out * lax.rsqrt(ms + NORM_EPS) * g_ref[...]


def _output(yf, yd, ga, gb, x2d, wuf, wud, wo, g, *, tm):
    rows = x2d.shape[0]
    assert rows % tm == 0
    row_spec = lambda n: pl.BlockSpec((tm, n), lambda i: (i, 0))
    const = lambda shape: pl.BlockSpec(shape, lambda i: (0, 0))
    return pl.pallas_call(
        _out_kernel,
        grid=(rows // tm,),
        in_specs=[row_spec(WIDTH), row_spec(WIDTH), row_spec(D_MODEL), row_spec(D_MODEL), row_spec(D_MODEL),
                  const((WIDTH, D_MODEL)), const((WIDTH, D_MODEL)), const((D_MODEL, D_MODEL)),
                  const((1, D_MODEL))],
        out_specs=row_spec(D_MODEL),
        out_shape=jax.ShapeDtypeStruct((rows, D_MODEL), F32),
        compiler_params=pltpu.CompilerParams(
            dimension_semantics=("parallel",), vmem_limit_bytes=VMEM_LIMIT),
        name="out_proj",
    )(yf, yd, ga, gb, x2d, wuf, wud, wo, g)


def _pack_w_in(w_in, b_forget):
    sizes = (WIDTH, WIDTH, WIDTH, WIDTH, FOX_HEADS, WIDTH, WIDTH, WIDTH, WIDTH, D_MODEL, D_MODEL)
    offs = np.cumsum((0,) + sizes)
    fq, fk, fv, fz, fl, dq, dk, dv, dz, ga, gb = (w_in[:, offs[i]:offs[i + 1]] for i in range(len(sizes)))
    swap = np.arange(FOX_HEADS) ^ 1
    flw = jnp.repeat(fl[:, swap], HD, axis=1)
    fl8 = jnp.pad(fl, ((0, 0), (0, LANES - FOX_HEADS)))
    w_all = jnp.concatenate([fq, fk, fv, fz, dq, dk, dv, dz, ga, gb, flw, fl8], axis=1).astype(BF16)
    bw = jnp.repeat(b_forget[swap], HD)[None, :]
    return w_all, bw, b_forget[None, :]


def kernel(x_prompt, x_sample, cache_fox_k, cache_fox_v, cache_fox_logf, cache_diff_k, cache_diff_v,
           norm_in_g, w_in, b_forget, lambda_q1, lambda_k1, lambda_q2, lambda_k2, subln_g,
           w_up_fox, w_up_diff, w_o, rel_bias, final_norm_g):
    bp, tp, _ = x_prompt.shape
    bs, ts, _ = x_sample.shape
    past = cache_fox_k.shape[2]
    blk = 256
    tm = 512

    w_all, bw, b8 = _pack_w_in(w_in[0], b_forget[0])
    g_in = norm_in_g[0][None, :]
    g_out = final_norm_g[None, :]
    wuf, wud, wo = w_up_fox[0].astype(BF16), w_up_diff[0].astype(BF16), w_o[0].astype(BF16)
    lq1, lk1, lq2, lk2 = (a[0][None, :] for a in (lambda_q1, lambda_k1, lambda_q2, lambda_k2))
    subg = subln_g[0][None, :]

    xp = x_prompt.reshape(bp * tp, D_MODEL)
    (fk_p, fv_p, lf_p, dk_p, dv_p, qa, ka, vtf, fz, dq, dkb, vtd, dz, ga, gb) = _project(
        xp, g_in, w_all, bw, b8, seq_len=tp, aug=True, tm=tm)
    r3 = lambda a: a.reshape(bp, tp, a.shape[-1])
    yf = _fox_attention(r3(qa), r3(ka), vtf, r3(fz), blk=blk, nh=FOX_STEP_HEADS)
    bias_p = _bias_tiles(rel_bias, q_start=blk, nq=blk, segments=((0, blk, blk), (blk, blk, blk)),
                         transposed=True)
    yd = _diff_attention(r3(dq), r3(dkb), vtd, bias_p, r3(dz), lq1, lk1, lq2, lk2, subg, blk=blk,
                         nh=DIFF_STEP_HEADS)
    y_p = _output(yf.reshape(bp * tp, WIDTH), yd.reshape(bp * tp, WIDTH), ga, gb, xp, wuf, wud, wo, g_out, tm=tm)

    xs = x_sample.reshape(bs * ts, D_MODEL)
    (fk_s, fv_s, lf_s, dk_s, dv_s, fq_s, fz_s, dq_s, dz_s, ga_s, gb_s) = _project(
        xs, g_in, w_all, bw, b8, seq_len=ts, aug=False, tm=tm)
    lf_past_t = jnp.swapaxes(cache_fox_logf[0], 1, 2)
    lf_new_t = jnp.pad(jnp.swapaxes(lf_s.reshape(bs, ts, FOX_HEADS), 1, 2), ((0, 0), (0, 0), (0, LANES - ts)))
    yf_s = _fox_sample(fq_s, cache_fox_k[0].reshape(bs, past, WIDTH), cache_fox_v[0].reshape(bs, past, WIDTH),
                       fk_s, fv_s, lf_past_t, lf_new_t, fz_s, nq=ts)
    bias_s = _bias_tiles(rel_bias, q_start=past, nq=ts, segments=((0, past, past), (past, LANES, ts)),
                         transposed=False)
    yd_s = _diff_sample(dq_s, cache_diff_k[0].reshape(bs, past, WIDTH), cache_diff_v[0].reshape(bs, past, WIDTH),
                        dk_s, dv_s, bias_s, dz_s, lq1, lk1, lq2, lk2, subg, nq=ts)
    y_s = _output(yf_s, yd_s, ga_s, gb_s, xs, wuf, wud, wo, g_out, tm=tm)

    return (y_p.reshape(bp, tp, D_MODEL), y_s.reshape(bs, ts, D_MODEL),
            fk_p.reshape(1, bp, tp, FOX_HEADS, HD), fv_p.reshape(1, bp, tp, FOX_HEADS, HD),
            lf_p.reshape(1, bp, tp, FOX_HEADS),
            dk_p.reshape(1, bp, tp, DIFF_HEADS, 2, HD), dv_p.reshape(1, bp, tp, DIFF_HEADS, 2 * HD),
            fk_s.reshape(1, bs, ts, FOX_HEADS, HD), fv_s.reshape(1, bs, ts, FOX_HEADS, HD),
            lf_s.reshape(1, bs, ts, FOX_HEADS),
            dk_s.reshape(1, bs, ts, DIFF_HEADS, 2, HD), dv_s.reshape(1, bs, ts, DIFF_HEADS, 2 * HD))
```

```python
import functools
import math

import numpy as np
import jax
import jax.numpy as jnp
from jax import lax
from jax.experimental import pallas as pl
from jax.experimental.pallas import tpu as pltpu

F32 = jnp.float32
BF16 = jnp.bfloat16

D_MODEL = 1024
HD = 64
FOX_HEADS = 8
DIFF_HEADS = 4
WIDTH = 512
CHUNK = 64
CHUNK_SHIFT = 6
N_BUCKETS = 32
MAX_DISTANCE = 128
NORM_EPS = 1e-6
SCALE = HD ** -0.5
NEG = -1e30
LAMBDA_INIT = 0.8 - 0.6 * math.exp(-0.3 * 0)
LOG2E = math.log2(math.e)

LANES = 128
PAIRS = WIDTH // LANES
VT_PAD = 16
VT_ROWS = LANES + VT_PAD

_O_FQ, _O_FK, _O_FV, _O_FZ = 0, 512, 1024, 1536
_O_DQ, _O_DK, _O_DV, _O_DZ = 2048, 2560, 3072, 3584
_O_GA, _O_GB = 4096, 5120
_O_FLW = 6144
_O_FL8 = 6656
_N_ALL = 6784

VMEM_LIMIT = 56 * 1024 * 1024

FOX_STEP_HEADS = 8
DIFF_STEP_HEADS = 4


def _t5_thresholds():
    nb = N_BUCKETS // 2
    max_exact = nb // 2
    n = np.arange(0, 4 * MAX_DISTANCE)
    large = max_exact + (np.log(np.maximum(n, 1).astype(np.float32) / max_exact)
                         / math.log(MAX_DISTANCE / max_exact) * (nb - max_exact)).astype(np.int32)
    large = np.minimum(large, nb - 1)
    thr = [int(np.argmax(large >= b)) for b in range(max_exact + 1, nb)]
    return max_exact, tuple(thr)


_MAX_EXACT, _T5_THR = _t5_thresholds()
FAR_DIST = _T5_THR[-1]


def _mm(a, b):
    return jnp.dot(a, b, preferred_element_type=F32)


def _mm_nt(a, b):
    return lax.dot_general(a, b, (((1,), (1,)), ((), ())), preferred_element_type=F32)


def _log_sigmoid(x):
    return -(jnp.maximum(-x, 0.0) + jnp.log1p(jnp.exp(-jnp.abs(x))))


def _sigmoid(x):
    return 1.0 / (1.0 + jnp.exp(-x))


def _silu(x):
    return x * _sigmoid(x)


def _split3(x):
    p1 = x.astype(BF16).astype(F32)
    r1 = x - p1
    p2 = r1.astype(BF16).astype(F32)
    p3 = r1 - p2
    return p1, p2, p3


def _proj_kernel(x_ref, g_ref, w_ref, bw_ref, b8_ref, *refs, tm, seq_len, aug):
    if aug:
        (fkt_ref, fvt_ref, lft_ref, dkt_ref, dv4_ref, qa_ref, ka_ref, vtf_ref,
         fz_ref, dq_ref, dkb_ref, vtd_ref, dz_ref, ga_ref, gb_ref, carry_ref) = refs
    else:
        (fk_ref, fv_ref, lf_ref, dk_ref, dv_ref, fq_ref,
         fz_ref, dq_ref, dz_ref, ga_ref, gb_ref) = refs

    x = x_ref[...]
    ms = jnp.mean(x * x, axis=-1, keepdims=True)
    h = (x * lax.rsqrt(ms + NORM_EPS) * g_ref[...]).astype(BF16)

    def grp(off, n=WIDTH):
        return _mm(h, w_ref[:, off:off + n])

    q_scale = SCALE * LOG2E if aug else SCALE
    fq = grp(_O_FQ) * q_scale
    fk = grp(_O_FK)
    fv = grp(_O_FV)
    fz_ref[...] = grp(_O_FZ).astype(BF16)
    dq_ref[...] = (grp(_O_DQ) * q_scale).astype(BF16)
    dk = grp(_O_DK)
    dv = grp(_O_DV)
    dz_ref[...] = grp(_O_DZ).astype(BF16)
    ga_ref[...] = _sigmoid(grp(_O_GA, D_MODEL)).astype(BF16)
    gb_ref[...] = _sigmoid(grp(_O_GB, D_MODEL)).astype(BF16)
    lf8 = _log_sigmoid(grp(_O_FL8, LANES) + b8_ref[...])

    if not aug:
        fk_ref[...] = fk
        fv_ref[...] = fv
        dk_ref[...] = dk
        dv_ref[...] = dv
        lf_ref[...] = lf8[:, :FOX_HEADS]
        fq_ref[...] = fq.astype(BF16)
        return

    fvt = fv.T
    fkt_ref[0] = fk.T
    fvt_ref[0] = fvt
    dkt_ref[0] = dk.T
    dv4_ref[...] = dv.reshape(tm, DIFF_HEADS, 2 * HD)
    lft_ref[0] = lf8.T[:FOX_HEADS, :]
    dkb_ref[...] = dk.astype(BF16)
    tail = jnp.where(lax.broadcasted_iota(jnp.int32, (VT_PAD, tm), 0) == 0, 1.0, 0.0).astype(BF16)
    for hd in range(DIFF_HEADS):
        base = hd * VT_ROWS
        vtd_ref[0, base:base + LANES, :] = dv[:, hd * LANES:(hd + 1) * LANES].T.astype(BF16)
        vtd_ref[0, base + LANES:base + VT_ROWS, :] = tail

    lfw = _log_sigmoid(grp(_O_FLW) + bw_ref[...])
    row = lax.broadcasted_iota(jnp.int32, (tm, tm), 0)
    col = lax.broadcasted_iota(jnp.int32, (tm, tm), 1)
    tri = jnp.where(col <= row, 1.0, 0.0).astype(BF16)
    c = sum(_mm(tri, p.astype(BF16)) for p in _split3(lfw))
    tiles_per_seq = seq_len // tm

    @pl.when(pl.program_id(0) % tiles_per_seq == 0)
    def _():
        carry_ref[...] = jnp.zeros_like(carry_ref)

    c = c + carry_ref[...]
    carry_ref[...] = c[tm - 1:tm, :]
    c_hi, c_mid, c_lo = _split3(c * LOG2E)

    lane = lax.broadcasted_iota(jnp.int32, (tm, LANES), 1)
    vrow = lax.broadcasted_iota(jnp.int32, (LANES, tm), 0)
    for hd in range(FOX_HEADS):
        blk, par = divmod(hd, 2)
        sl = slice(blk * LANES, (blk + 1) * LANES)
        data = (lane < HD) if par == 0 else (lane >= HD)
        a = lane - (1 - par) * HD
        hi, mid, lo = c_hi[:, sl], c_mid[:, sl], c_lo[:, sl]
        q_aug = jnp.where(a == 0, hi, jnp.where(a == 1, mid, jnp.where(
            a == 2, lo, jnp.where((a >= 3) & (a < 6), 1.0, 0.0))))
        k_aug = jnp.where((a >= 0) & (a < 3), 1.0, jnp.where(a == 3, -hi, jnp.where(
            a == 4, -mid, jnp.where(a == 5, -lo, 0.0))))
        osl = slice(hd * LANES, (hd + 1) * LANES)
        qa_ref[:, osl] = jnp.where(data, fq[:, sl], q_aug).astype(BF16)
        ka_ref[:, osl] = jnp.where(data, fk[:, sl], k_aug).astype(BF16)
        vdata = (vrow < HD) if par == 0 else (vrow >= HD)
        v_aug = jnp.where(vrow == (1 - par) * HD, 1.0, 0.0)
        vtf_ref[0, osl, :] = jnp.where(vdata, fvt[sl, :], v_aug).astype(BF16)


def _project(x2d, g, w_all, bw, b8, *, seq_len, aug, tm):
    rows = x2d.shape[0]
    assert rows % tm == 0
    row_spec = lambda n: pl.BlockSpec((tm, n), lambda i: (i, 0))
    const = lambda shape: pl.BlockSpec(shape, lambda i: (0, 0))
    f32o = lambda n: jax.ShapeDtypeStruct((rows, n), F32)
    b16o = lambda n: jax.ShapeDtypeStruct((rows, n), BF16)
    if aug:
        assert seq_len % tm == 0
        wide = FOX_HEADS * LANES
        tps = seq_len // tm
        t_shape = lambda n, dt: jax.ShapeDtypeStruct((rows // seq_len, n, seq_len), dt)
        t_spec = lambda n: pl.BlockSpec((1, n, tm), lambda i: (i // tps, 0, i % tps))
        out_shape = [t_shape(WIDTH, F32), t_shape(WIDTH, F32), t_shape(FOX_HEADS, F32), t_shape(WIDTH, F32),
                     jax.ShapeDtypeStruct((rows, DIFF_HEADS, 2 * HD), F32)]
        out_specs = [t_spec(WIDTH), t_spec(WIDTH), t_spec(FOX_HEADS), t_spec(WIDTH),
                     pl.BlockSpec((tm, DIFF_HEADS, 2 * HD), lambda i: (i, 0, 0))]
        out_shape += ([b16o(wide)] * 2 + [t_shape(wide, BF16)] + [b16o(WIDTH)] * 3
                      + [t_shape(DIFF_HEADS * VT_ROWS, BF16)] + [b16o(WIDTH)] + [b16o(D_MODEL)] * 2)
        out_specs += ([row_spec(wide)] * 2 + [t_spec(wide)] + [row_spec(WIDTH)] * 3
                      + [t_spec(DIFF_HEADS * VT_ROWS)] + [row_spec(WIDTH)] + [row_spec(D_MODEL)] * 2)
        scratch = [pltpu.VMEM((1, WIDTH), F32)]
    else:
        out_shape = [f32o(WIDTH), f32o(WIDTH), f32o(FOX_HEADS), f32o(WIDTH), f32o(WIDTH)]
        out_specs = [row_spec(WIDTH), row_spec(WIDTH), row_spec(FOX_HEADS), row_spec(WIDTH), row_spec(WIDTH)]
        out_shape += [b16o(WIDTH)] * 4 + [b16o(D_MODEL)] * 2
        out_specs += [row_spec(WIDTH)] * 4 + [row_spec(D_MODEL)] * 2
        scratch = []
    return pl.pallas_call(
        functools.partial(_proj_kernel, tm=tm, seq_len=seq_len, aug=aug),
        grid=(rows // tm,),
        in_specs=[row_spec(D_MODEL), const((1, D_MODEL)),
                  pl.BlockSpec((D_MODEL, _N_ALL), lambda i: (0, 0), pipeline_mode=pl.Buffered(1)),
                  const((1, WIDTH)), const((1, LANES))],
        out_specs=out_specs,
        out_shape=out_shape,
        scratch_shapes=scratch,
        compiler_params=pltpu.CompilerParams(
            dimension_semantics=("arbitrary",), vmem_limit_bytes=VMEM_LIMIT),
        name="proj_aug" if aug else "proj_plain",
    )(x2d, g, w_all, bw, b8)


def _bias_kernel(tbl_ref, o_ref, *, q_start, nq, segments, transposed, scale):
    hd = pl.program_id(0)
    far = tbl_ref[N_BUCKETS // 2 - 1, hd]
    q_axis, k_axis = (1, 0) if transposed else (0, 1)
    off = 0
    for k_start, nk, valid in segments:
        shape = (nk, nq) if transposed else (nq, nk)
        qpos = q_start + lax.broadcasted_iota(jnp.int32, shape, q_axis)
        kidx = lax.broadcasted_iota(jnp.int32, shape, k_axis)
        kpos = k_start + kidx
        rel = kpos - qpos
        n = jnp.abs(rel)
        large = jnp.full(shape, _MAX_EXACT, jnp.int32)
        for thr in _T5_THR:
            large = large + jnp.where(n >= thr, 1, 0)
        bucket = jnp.where(rel > 0, N_BUCKETS // 2, 0) + jnp.where(n < _MAX_EXACT, n, large)
        val = jnp.zeros(shape, F32)
        for b in range(N_BUCKETS):
            val = jnp.where(bucket == b, tbl_ref[b, hd], val)
        val = (val - far) * scale
        visible = ((kpos >> CHUNK_SHIFT) <= (qpos >> CHUNK_SHIFT)) & (kidx < valid)
        val = jnp.where(visible, val, NEG)
        if transposed:
            o_ref[0, off:off + nk, :] = val
        else:
            o_ref[0, :, off:off + nk] = val
        off += nk


def _bias_tiles(rel_bias, *, q_start, nq, segments, transposed, scale):
    total = sum(s[1] for s in segments)
    shape = (total, nq) if transposed else (nq, total)
    return pl.pallas_call(
        functools.partial(_bias_kernel, q_start=q_start, nq=nq, segments=segments, transposed=transposed,
                          scale=scale),
        grid=(DIFF_HEADS,),
        in_specs=[pl.BlockSpec(memory_space=pltpu.SMEM)],
        out_specs=pl.BlockSpec((1,) + shape, lambda h: (h, 0, 0)),
        out_shape=jax.ShapeDtypeStruct((DIFF_HEADS,) + shape, F32),
        name="t5_bias",
    )(rel_bias)


def _flash_t(k_ref, vt_ref, acc_ref, streams, qi, blk, prev_bias, diag_bias, diag_mask):
    n = len(streams)
    acc_ref[...] = jnp.zeros_like(acc_ref)

    def step(j, ms, biases, mask):
        start = pl.multiple_of(j * blk, blk)
        sts = [_mm_nt(k_ref[0, pl.ds(start, blk), ksl], q) for q, ksl, _ in streams]
        out, ps, alphas = [], [], []
        for s, st in enumerate(sts):
            if biases is not None:
                st = st + biases[s]()
            if mask is not None:
                st = jnp.where(mask, st, NEG)
            m_new = jnp.maximum(ms[s], jnp.max(st, axis=0, keepdims=True))
            ps.append(jnp.exp2(st - m_new).astype(BF16))
            alphas.append(jnp.exp2(ms[s] - m_new))
            out.append(m_new)
        for s, (_, _, vsl) in enumerate(streams):
            acc_ref[s] = alphas[s] * acc_ref[s] + _mm(vt_ref[0, vsl, pl.ds(start, blk)], ps[s])
        return tuple(out)

    ms = tuple(jnp.full((1, blk), NEG, F32) for _ in range(n))
    if prev_bias is None:
        ms = lax.fori_loop(0, qi, lambda j, c: step(j, c, None, None), ms)
    else:
        ms = lax.fori_loop(0, jnp.maximum(qi - 1, 0), lambda j, c: step(j, c, None, None), ms)
        ms = lax.cond(qi >= 1, lambda c: step(qi - 1, c, prev_bias, None), lambda c: c, ms)
    step(qi, ms, diag_bias, diag_mask)


def _fox_kernel(qa_ref, ka_ref, vt_ref, z_ref, o_ref, acc_ref, *, blk, nh):
    qi = pl.program_id(2)
    key = lax.broadcasted_iota(jnp.int32, (blk, blk), 0)
    qry = lax.broadcasted_iota(jnp.int32, (blk, blk), 1)
    hsl = [slice(hd * LANES, (hd + 1) * LANES) for hd in range(nh)]
    streams = [(qa_ref[0, :, sl], sl, sl) for sl in hsl]
    _flash_t(ka_ref, vt_ref, acc_ref, streams, qi, blk, None, None, key <= qry)
    lane = lax.broadcasted_iota(jnp.int32, (blk, LANES), 1)
    for pair in range(nh // 2):
        outs = []
        for par in range(2):
            acc = acc_ref[2 * pair + par]
            ones_row = (1 - par) * HD
            outs.append((acc / acc[ones_row:ones_row + 1, :]).T)
        o = jnp.where(lane < HD, outs[0], outs[1])
        o_ref[0, :, hsl[pair]] = (o * _silu(z_ref[0, :, hsl[pair]].astype(F32))).astype(BF16)


def _fox_attention(qa, ka, vt, fz, *, blk, nh):
    b, t, _ = qa.shape
    return pl.pallas_call(
        functools.partial(_fox_kernel, blk=blk, nh=nh),
        grid=(b, FOX_HEADS // nh, t // blk),
        in_specs=[pl.BlockSpec((1, blk, nh * LANES), lambda b, p, i: (b, i, p)),
                  pl.BlockSpec((1, t, nh * LANES), lambda b, p, i: (b, 0, p)),
                  pl.BlockSpec((1, nh * LANES, t), lambda b, p, i: (b, p, 0)),
                  pl.BlockSpec((1, blk, nh * HD), lambda b, p, i: (b, i, p))],
        out_specs=pl.BlockSpec((1, blk, nh * HD), lambda b, p, i: (b, i, p)),
        out_shape=jax.ShapeDtypeStruct((b, t, WIDTH), BF16),
        scratch_shapes=[pltpu.VMEM((nh, LANES, blk), F32)],
        compiler_params=pltpu.CompilerParams(
            dimension_semantics=("parallel", "parallel", "arbitrary"), vmem_limit_bytes=VMEM_LIMIT),
        name="fox_prompt",
    )(qa, ka, vt, fz)


def _lambda(lq1_ref, lk1_ref, lq2_ref, lk2_ref):
    s1 = jnp.sum(lq1_ref[...] * lk1_ref[...], axis=-1, keepdims=True)
    s2 = jnp.sum(lq2_ref[...] * lk2_ref[...], axis=-1, keepdims=True)
    return jnp.exp(s1) - jnp.exp(s2) + LAMBDA_INIT


def _diff_finish(o0, o1, lam, subg, z):
    o = o0 - lam * o1
    ms = jnp.mean(o * o, axis=-1, keepdims=True)
    od = o * lax.rsqrt(ms + NORM_EPS) * subg * (1.0 - LAMBDA_INIT)
    return (od * _silu(z)).astype(BF16)


def _diff_kernel(q_ref, k_ref, vt_ref, bias_ref, z_ref, lq1_ref, lk1_ref, lq2_ref, lk2_ref, subg_ref,
                 o_ref, acc_ref, *, blk, nh):
    qi = pl.program_id(2)
    lane = lax.broadcasted_iota(jnp.int32, (blk, LANES), 1)
    streams, prev_bias, diag_bias = [], [], []
    for hd in range(nh):
        q = q_ref[0, :, hd * LANES:(hd + 1) * LANES]
        for mp in range(2):
            streams.append((jnp.where((lane < HD) if mp == 0 else (lane >= HD), q, jnp.zeros_like(q)),
                            slice(hd * LANES, (hd + 1) * LANES), slice(hd * VT_ROWS, (hd + 1) * VT_ROWS)))
            prev_bias.append(lambda hd=hd: bias_ref[hd, 0:blk, :])
            diag_bias.append(lambda hd=hd: bias_ref[hd, blk:2 * blk, :])
    _flash_t(k_ref, vt_ref, acc_ref, streams, qi, blk, prev_bias, diag_bias, None)
    lam = _lambda(lq1_ref, lk1_ref, lq2_ref, lk2_ref)
    for hd in range(nh):
        outs = []
        for mp in range(2):
            acc = acc_ref[2 * hd + mp]
            outs.append((acc[0:LANES, :] / acc[LANES:LANES + 1, :]).T)
        sl = slice(hd * LANES, (hd + 1) * LANES)
        o_ref[0, :, sl] = _diff_finish(outs[0], outs[1], lam, subg_ref[...], z_ref[0, :, sl].astype(F32))


def _diff_attention(dq, dk, dvt, bias, dz, lq1, lk1, lq2, lk2, subg, *, blk, nh):
    b, t, _ = dq.shape
    small = lambda n: pl.BlockSpec((1, n), lambda b, h, i: (0, 0))
    return pl.pallas_call(
        functools.partial(_diff_kernel, blk=blk, nh=nh),
        grid=(b, DIFF_HEADS // nh, t // blk),
        in_specs=[pl.BlockSpec((1, blk, nh * LANES), lambda b, h, i: (b, i, h)),
                  pl.BlockSpec((1, t, nh * LANES), lambda b, h, i: (b, 0, h)),
                  pl.BlockSpec((1, nh * VT_ROWS, t), lambda b, h, i: (b, h, 0)),
                  pl.BlockSpec((nh, 2 * blk, blk), lambda b, h, i: (h, 0, 0)),
                  pl.BlockSpec((1, blk, nh * LANES), lambda b, h, i: (b, i, h)),
                  small(HD), small(HD), small(HD), small(HD), small(2 * HD)],
        out_specs=pl.BlockSpec((1, blk, nh * LANES), lambda b, h, i: (b, i, h)),
        out_shape=jax.ShapeDtypeStruct((b, t, WIDTH), BF16),
        scratch_shapes=[pltpu.VMEM((2 * nh, VT_ROWS, blk), F32)],
        compiler_params=pltpu.CompilerParams(
            dimension_semantics=("parallel", "parallel", "arbitrary"), vmem_limit_bytes=VMEM_LIMIT),
        name="diff_prompt",
    )(dq, dk, dvt, bias, dz, lq1, lk1, lq2, lk2, subg)


def _suffix_sums(x):
    n = x.shape[1] // LANES
    lane = lax.broadcasted_iota(jnp.int32, (x.shape[0], LANES), 1)
    after = jnp.zeros((x.shape[0], 1), F32)
    pieces = [None] * n
    for blk in reversed(range(n)):
        piece = x[:, blk * LANES:(blk + 1) * LANES]
        y = piece
        shift = 1
        while shift < LANES:
            y = y + jnp.where(lane + shift < LANES, pltpu.roll(y, LANES - shift, 1), 0.0)
            shift *= 2
        pieces[blk] = y - piece + after
        after = after + y[:, 0:1]
    return jnp.concatenate(pieces, axis=1), after


def _pad_rows(x, rows):
    return jnp.concatenate([x, jnp.zeros((rows - x.shape[0], x.shape[1]), x.dtype)], axis=0)


def _joint_softmax_pv(s_p, s_n, vt_p, v_n):
    m = jnp.maximum(jnp.max(s_p, axis=-1, keepdims=True), jnp.max(s_n, axis=-1, keepdims=True))
    p_p = jnp.exp(s_p - m)
    p_n = jnp.exp(s_n - m)
    l = jnp.sum(p_p, axis=-1, keepdims=True) + jnp.sum(p_n, axis=-1, keepdims=True)
    return (_mm_nt(p_p.astype(BF16), vt_p) + _mm(p_n.astype(BF16), v_n)) / l


def _fox_sample_kernel(q_ref, kt_ref, vt_ref, kn_ref, vn_ref, lfp_ref, lfn_ref, z_ref, o_ref, *, nq):
    r_new, total_new = _suffix_sums(lfn_ref[0])
    r_past, _ = _suffix_sums(lfp_ref[0])
    r_past = r_past + total_new
    lane = lax.broadcasted_iota(jnp.int32, (nq, LANES), 1)
    causal = lane <= lax.broadcasted_iota(jnp.int32, (nq, LANES), 0)
    for blk in range(PAIRS):
        sl = slice(blk * LANES, (blk + 1) * LANES)
        q = q_ref[:, sl]
        kt = kt_ref[0, sl, :].astype(BF16)
        vt = vt_ref[0, sl, :].astype(BF16)
        kn = _pad_rows(kn_ref[:, sl], LANES).astype(BF16)
        vn = _pad_rows(vn_ref[:, sl], LANES).astype(BF16)
        outs = []
        for par in range(2):
            hd = 2 * blk + par
            qm = jnp.where((lane < HD) if par == 0 else (lane >= HD), q, jnp.zeros_like(q))
            s_p = _mm(qm, kt) + r_past[hd:hd + 1, :]
            s_n = jnp.where(causal, _mm_nt(qm, kn) + r_new[hd:hd + 1, :], NEG)
            outs.append(_joint_softmax_pv(s_p, s_n, vt, vn))
        o = jnp.where(lane < HD, outs[0], outs[1])
        o_ref[:, sl] = (o * _silu(z_ref[:, sl].astype(F32))).astype(BF16)


def _fox_sample(fq, cache_kt, cache_vt, fk, fv, lf_past_t, lf_new_t, fz, *, nq):
    b, _, past = cache_kt.shape
    new = lambda: pl.BlockSpec((nq, WIDTH), lambda i: (i, 0))
    cache = lambda: pl.BlockSpec((1, WIDTH, past), lambda i: (i, 0, 0))
    return pl.pallas_call(
        functools.partial(_fox_sample_kernel, nq=nq),
        grid=(b,),
        in_specs=[new(), cache(), cache(), new(), new(),
                  pl.BlockSpec((1, FOX_HEADS, past), lambda i: (i, 0, 0)),
                  pl.BlockSpec((1, FOX_HEADS, LANES), lambda i: (i, 0, 0)),
                  new()],
        out_specs=new(),
        out_shape=jax.ShapeDtypeStruct((b * nq, WIDTH), BF16),
        compiler_params=pltpu.CompilerParams(
            dimension_semantics=("parallel",), vmem_limit_bytes=VMEM_LIMIT),
        name="fox_sample",
    )(fq, cache_kt, cache_vt, fk, fv, lf_past_t, lf_new_t, fz)


def _diff_sample_kernel(q_ref, kt_ref, v_ref, kn_ref, vn_ref, bias_ref, z_ref,
                        lq1_ref, lk1_ref, lq2_ref, lk2_ref, subg_ref, o_ref, *, nq, past):
    lane = lax.broadcasted_iota(jnp.int32, (nq, LANES), 1)
    lam = _lambda(lq1_ref, lk1_ref, lq2_ref, lk2_ref)
    for hd in range(DIFF_HEADS):
        sl = slice(hd * LANES, (hd + 1) * LANES)
        q = q_ref[:, sl]
        kt = kt_ref[0, sl, :].astype(BF16)
        vt = v_ref[0, pl.ds(hd, past, stride=DIFF_HEADS), :].T.astype(BF16)
        kn = _pad_rows(kn_ref[:, sl], LANES).astype(BF16)
        vn = _pad_rows(vn_ref[:, sl], LANES).astype(BF16)
        outs = []
        for mp in range(2):
            qm = jnp.where((lane < HD) if mp == 0 else (lane >= HD), q, jnp.zeros_like(q))
            s_p = _mm(qm, kt) + bias_ref[hd, :, 0:past]
            s_n = _mm_nt(qm, kn) + bias_ref[hd, :, past:past + LANES]
            outs.append(_joint_softmax_pv(s_p, s_n, vt, vn))
        o_ref[:, sl] = _diff_finish(outs[0], outs[1], lam, subg_ref[...], z_ref[:, sl].astype(F32))


def _diff_sample(dq, cache_kt, cache_v, dk, dv, bias, dz, lq1, lk1, lq2, lk2, subg, *, nq):
    b, _, past = cache_kt.shape
    new = lambda: pl.BlockSpec((nq, WIDTH), lambda i: (i, 0))
    small = lambda n: pl.BlockSpec((1, n), lambda i: (0, 0))
    return pl.pallas_call(
        functools.partial(_diff_sample_kernel, nq=nq, past=past),
        grid=(b,),
        in_specs=[new(), pl.BlockSpec((1, WIDTH, past), lambda i: (i, 0, 0)),
                  pl.BlockSpec((1, past * DIFF_HEADS, 2 * HD), lambda i: (i, 0, 0)), new(), new(),
                  pl.BlockSpec((DIFF_HEADS, nq, past + LANES), lambda i: (0, 0, 0)),
                  new(), small(HD), small(HD), small(HD), small(HD), small(2 * HD)],
        out_specs=new(),
        out_shape=jax.ShapeDtypeStruct((b * nq, WIDTH), BF16),
        compiler_params=pltpu.CompilerParams(
            dimension_semantics=("parallel",), vmem_limit_bytes=VMEM_LIMIT),
        name="diff_sample",
    )(dq, cache_kt, cache_v, dk, dv, bias, dz, lq1, lk1, lq2, lk2, subg)


def _out_kernel(yf_ref, yd_ref, ga_ref, gb_ref, x_ref, wuf_ref, wud_ref, wo_ref, g_ref, o_ref):
    merged = (ga_ref[...].astype(F32) * _mm(yf_ref[...], wuf_ref[...])
              + gb_ref[...].astype(F32) * _mm(yd_ref[...], wud_ref[...]))
    out = x_ref[...] + _mm(merged.astype(BF16), wo_ref[...])
    ms = jnp.mean(out * out, axis=-1, keepdims=True)
    o_ref[...] = out * lax.rsqrt(ms + NORM_EPS) * g_ref[...]


def _output(yf, yd, ga, gb, x2d, wuf, wud, wo, g, *, tm):
    rows = x2d.shape[0]
    assert rows % tm == 0
    row_spec = lambda n: pl.BlockSpec((tm, n), lambda i: (i, 0))
    const = lambda shape: pl.BlockSpec(shape, lambda i: (0, 0))
    return pl.pallas_call(
        _out_kernel,
        grid=(rows // tm,),
        in_specs=[row_spec(WIDTH), row_spec(WIDTH), row_spec(D_MODEL), row_spec(D_MODEL), row_spec(D_MODEL),
                  const((WIDTH, D_MODEL)), const((WIDTH, D_MODEL)), const((D_MODEL, D_MODEL)),
                  const((1, D_MODEL))],
        out_specs=row_spec(D_MODEL),
        out_shape=jax.ShapeDtypeStruct((rows, D_MODEL), F32),
        compiler_params=pltpu.CompilerParams(
            dimension_semantics=("parallel",), vmem_limit_bytes=VMEM_LIMIT),
        name="out_proj",
    )(yf, yd, ga, gb, x2d, wuf, wud, wo, g)


def _pack_w_in(w_in, b_forget):
    sizes = (WIDTH, WIDTH, WIDTH, WIDTH, FOX_HEADS, WIDTH, WIDTH, WIDTH, WIDTH, D_MODEL, D_MODEL)
    offs = np.cumsum((0,) + sizes)
    fq, fk, fv, fz, fl, dq, dk, dv, dz, ga, gb = (w_in[:, offs[i]:offs[i + 1]] for i in range(len(sizes)))
    swap = np.arange(FOX_HEADS) ^ 1
    flw = jnp.repeat(fl[:, swap], HD, axis=1)
    fl8 = jnp.pad(fl, ((0, 0), (0, LANES - FOX_HEADS)))
    w_all = jnp.concatenate([fq, fk, fv, fz, dq, dk, dv, dz, ga, gb, flw, fl8], axis=1).astype(BF16)
    bw = jnp.repeat(b_forget[swap], HD)[None, :]
    return w_all, bw, jnp.pad(b_forget, (0, LANES - FOX_HEADS))[None, :]


def kernel(x_prompt, x_sample, cache_fox_k, cache_fox_v, cache_fox_logf, cache_diff_k, cache_diff_v,
           norm_in_g, w_in, b_forget, lambda_q1, lambda_k1, lambda_q2, lambda_k2, subln_g,
           w_up_fox, w_up_diff, w_o, rel_bias, final_norm_g):
    bp, tp, _ = x_prompt.shape
    bs, ts, _ = x_sample.shape
    past = cache_fox_k.shape[2]
    blk = 256
    tm = 512

    w_all, bw, b8 = _pack_w_in(w_in[0], b_forget[0])
    g_in = norm_in_g[0][None, :]
    g_out = final_norm_g[None, :]
    wuf, wud, wo = w_up_fox[0].astype(BF16), w_up_diff[0].astype(BF16), w_o[0].astype(BF16)
    lq1, lk1, lq2, lk2 = (a[0][None, :] for a in (lambda_q1, lambda_k1, lambda_q2, lambda_k2))
    subg = subln_g[0][None, :]

    xp = x_prompt.reshape(bp * tp, D_MODEL)
    (fkt, fvt, lft, dkt, dv4, qa, ka, vtf, fz, dq, dkb, vtd, dz, ga, gb) = _project(
        xp, g_in, w_all, bw, b8, seq_len=tp, aug=True, tm=tm)
    r3 = lambda a: a.reshape(bp, tp, a.shape[-1])
    yf = _fox_attention(r3(qa), r3(ka), vtf, r3(fz), blk=blk, nh=FOX_STEP_HEADS)
    bias_p = _bias_tiles(rel_bias, q_start=blk, nq=blk, segments=((0, blk, blk), (blk, blk, blk)),
                         transposed=True, scale=LOG2E)
    yd = _diff_attention(r3(dq), r3(dkb), vtd, bias_p, r3(dz), lq1, lk1, lq2, lk2, subg, blk=blk,
                         nh=DIFF_STEP_HEADS)
    y_p = _output(yf.reshape(bp * tp, WIDTH), yd.reshape(bp * tp, WIDTH), ga, gb, xp, wuf, wud, wo, g_out, tm=tm)
    fox_k_p = jnp.transpose(fkt.reshape(bp, FOX_HEADS, HD, tp), (0, 3, 1, 2))[None]
    fox_v_p = jnp.transpose(fvt.reshape(bp, FOX_HEADS, HD, tp), (0, 3, 1, 2))[None]
    fox_lf_p = jnp.transpose(lft, (0, 2, 1))[None]
    diff_k_p = jnp.transpose(dkt.reshape(bp, DIFF_HEADS, 2, HD, tp), (0, 4, 1, 2, 3))[None]
    diff_v_p = dv4.reshape(1, bp, tp, DIFF_HEADS, 2 * HD)

    xs = x_sample.reshape(bs * ts, D_MODEL)
    (fk_s, fv_s, lf_s, dk_s, dv_s, fq_s, fz_s, dq_s, dz_s, ga_s, gb_s) = _project(
        xs, g_in, w_all, bw, b8, seq_len=ts, aug=False, tm=tm)
    fox_kt = jnp.transpose(cache_fox_k[0], (0, 2, 3, 1)).reshape(bs, WIDTH, past)
    fox_vt = jnp.transpose(cache_fox_v[0], (0, 2, 3, 1)).reshape(bs, WIDTH, past)
    diff_kt = jnp.transpose(cache_diff_k[0], (0, 2, 3, 4, 1)).reshape(bs, WIDTH, past)
    diff_v = cache_diff_v[0].reshape(bs, past * DIFF_HEADS, 2 * HD)
    lf_past_t = jnp.swapaxes(cache_fox_logf[0], 1, 2)
    lf_new_t = jnp.pad(jnp.swapaxes(lf_s.reshape(bs, ts, FOX_HEADS), 1, 2), ((0, 0), (0, 0), (0, LANES - ts)))
    yf_s = _fox_sample(fq_s, fox_kt, fox_vt, fk_s, fv_s, lf_past_t, lf_new_t, fz_s, nq=ts)
    bias_s = _bias_tiles(rel_bias, q_start=past, nq=ts, segments=((0, past, past), (past, LANES, ts)),
                         transposed=False, scale=1.0)
    yd_s = _diff_sample(dq_s, diff_kt, diff_v, dk_s, dv_s, bias_s, dz_s, lq1, lk1, lq2, lk2, subg, nq=ts)
    y_s = _output(yf_s, yd_s, ga_s, gb_s, xs, wuf, wud, wo, g_out, tm=tm)

    return (y_p.reshape(bp, tp, D_MODEL), y_s.reshape(bs, ts, D_MODEL),
            fox_k_p, fox_v_p, fox_lf_p, diff_k_p, diff_v_p,
            fk_s.reshape(1, bs, ts, FOX_HEADS, HD), fv_s.reshape(1, bs, ts, FOX_HEADS, HD),
            lf_s.reshape(1, bs, ts, FOX_HEADS),
            dk_s.reshape(1, bs, ts, DIFF_HEADS, 2, HD), dv_s.reshape(1, bs, ts, DIFF_HEADS, 2 * HD))
```

```python
import functools
import math

import numpy as np
import jax
import jax.numpy as jnp
from jax import lax
from jax.experimental import pallas as pl
from jax.experimental.pallas import tpu as pltpu

F32 = jnp.float32
BF16 = jnp.bfloat16

D_MODEL = 1024
HD = 64
FOX_HEADS = 8
DIFF_HEADS = 4
WIDTH = 512
CHUNK = 64
CHUNK_SHIFT = 6
N_BUCKETS = 32
MAX_DISTANCE = 128
NORM_EPS = 1e-6
SCALE = HD ** -0.5
NEG = -1e30
LAMBDA_INIT = 0.8 - 0.6 * math.exp(-0.3 * 0)
LOG2E = math.log2(math.e)

LANES = 128
PAIRS = WIDTH // LANES
VT_PAD = 16
VT_ROWS = LANES + VT_PAD

_O_FQ, _O_FK, _O_FV, _O_FZ = 0, 512, 1024, 1536
_O_DQ, _O_DK, _O_DV, _O_DZ = 2048, 2560, 3072, 3584
_O_GA, _O_GB = 4096, 5120
_O_FLW = 6144
_O_FL8 = 6656
_N_ALL = 6784

VMEM_LIMIT = 56 * 1024 * 1024

FOX_STEP_HEADS = 8
DIFF_STEP_HEADS = 4
HEADROOM = 64.0
WIDE_BLOCKS = 2


def _t5_thresholds():
    nb = N_BUCKETS // 2
    max_exact = nb // 2
    n = np.arange(0, 4 * MAX_DISTANCE)
    large = max_exact + (np.log(np.maximum(n, 1).astype(np.float32) / max_exact)
                         / math.log(MAX_DISTANCE / max_exact) * (nb - max_exact)).astype(np.int32)
    large = np.minimum(large, nb - 1)
    thr = [int(np.argmax(large >= b)) for b in range(max_exact + 1, nb)]
    return max_exact, tuple(thr)


_MAX_EXACT, _T5_THR = _t5_thresholds()
FAR_DIST = _T5_THR[-1]


def _mm(a, b):
    return jnp.dot(a, b, preferred_element_type=F32)


def _mm_nt(a, b):
    return lax.dot_general(a, b, (((1,), (1,)), ((), ())), preferred_element_type=F32)


def _log_sigmoid(x):
    return -(jnp.maximum(-x, 0.0) + jnp.log1p(jnp.exp(-jnp.abs(x))))


def _sigmoid(x):
    return 1.0 / (1.0 + jnp.exp(-x))


def _silu(x):
    return x * _sigmoid(x)


def _split3(x):
    p1 = x.astype(BF16).astype(F32)
    r1 = x - p1
    p2 = r1.astype(BF16).astype(F32)
    p3 = r1 - p2
    return p1, p2, p3


def _proj_kernel(x_ref, g_ref, w_ref, bw_ref, b8_ref, *refs, tm, seq_len, aug):
    if aug:
        (fkt_ref, fvt_ref, lft_ref, dkt_ref, dv4_ref, qa_ref, ka_ref, vtf_ref,
         fz_ref, dq_ref, dkb_ref, vtd_ref, dz_ref, ga_ref, gb_ref, carry_ref) = refs
    else:
        (fk_ref, fv_ref, lf_ref, dk_ref, dv_ref, fq_ref,
         fz_ref, dq_ref, dz_ref, ga_ref, gb_ref) = refs

    x = x_ref[...]
    ms = jnp.mean(x * x, axis=-1, keepdims=True)
    h = (x * lax.rsqrt(ms + NORM_EPS) * g_ref[...]).astype(BF16)

    def grp(off, n=WIDTH):
        return _mm(h, w_ref[:, off:off + n])

    q_scale = SCALE * LOG2E if aug else SCALE
    fq = grp(_O_FQ) * q_scale
    fk = grp(_O_FK)
    fv = grp(_O_FV)
    fz_ref[...] = grp(_O_FZ).astype(BF16)
    dq_ref[...] = (grp(_O_DQ) * q_scale).astype(BF16)
    dk = grp(_O_DK)
    dv = grp(_O_DV)
    dz_ref[...] = grp(_O_DZ).astype(BF16)
    ga_ref[...] = _sigmoid(grp(_O_GA, D_MODEL)).astype(BF16)
    gb_ref[...] = _sigmoid(grp(_O_GB, D_MODEL)).astype(BF16)
    lf8 = _log_sigmoid(grp(_O_FL8, LANES) + b8_ref[...])

    if not aug:
        fk_ref[...] = fk
        fv_ref[...] = fv
        dk_ref[...] = dk
        dv_ref[...] = dv
        lf_ref[...] = lf8[:, :FOX_HEADS]
        fq_ref[...] = fq.astype(BF16)
        return

    fvt = fv.T
    fkt_ref[0] = fk.T
    fvt_ref[0] = fvt
    dkt_ref[0] = dk.T
    dv4_ref[...] = dv.reshape(tm, DIFF_HEADS, 2 * HD)
    lft_ref[0] = lf8.T[:FOX_HEADS, :]
    dkb_ref[...] = dk.astype(BF16)
    tail = jnp.where(lax.broadcasted_iota(jnp.int32, (VT_PAD, tm), 0) == 0, 1.0, 0.0).astype(BF16)
    for hd in range(DIFF_HEADS):
        base = hd * VT_ROWS
        vtd_ref[0, base:base + LANES, :] = dv[:, hd * LANES:(hd + 1) * LANES].T.astype(BF16)
        vtd_ref[0, base + LANES:base + VT_ROWS, :] = tail

    lfw = _log_sigmoid(grp(_O_FLW) + bw_ref[...])
    row = lax.broadcasted_iota(jnp.int32, (tm, tm), 0)
    col = lax.broadcasted_iota(jnp.int32, (tm, tm), 1)
    tri = jnp.where(col <= row, 1.0, 0.0).astype(BF16)
    c = sum(_mm(tri, p.astype(BF16)) for p in _split3(lfw))
    tiles_per_seq = seq_len // tm

    @pl.when(pl.program_id(0) % tiles_per_seq == 0)
    def _():
        carry_ref[...] = jnp.zeros_like(carry_ref)

    c = c + carry_ref[...]
    carry_ref[...] = c[tm - 1:tm, :]
    c_hi, c_mid, c_lo = _split3(c * LOG2E)

    lane = lax.broadcasted_iota(jnp.int32, (tm, LANES), 1)
    vrow = lax.broadcasted_iota(jnp.int32, (LANES, tm), 0)
    for hd in range(FOX_HEADS):
        blk, par = divmod(hd, 2)
        sl = slice(blk * LANES, (blk + 1) * LANES)
        data = (lane < HD) if par == 0 else (lane >= HD)
        a = lane - (1 - par) * HD
        hi, mid, lo = c_hi[:, sl], c_mid[:, sl], c_lo[:, sl]
        q_aug = jnp.where(a == 0, hi, jnp.where(a == 1, mid, jnp.where(
            a == 2, lo, jnp.where((a >= 3) & (a < 6), 1.0, 0.0))))
        k_aug = jnp.where((a >= 0) & (a < 3), 1.0, jnp.where(a == 3, -hi, jnp.where(
            a == 4, -mid, jnp.where(a == 5, -lo, 0.0))))
        osl = slice(hd * LANES, (hd + 1) * LANES)
        qa_ref[:, osl] = jnp.where(data, fq[:, sl], q_aug).astype(BF16)
        ka_ref[:, osl] = jnp.where(data, fk[:, sl], k_aug).astype(BF16)
        vdata = (vrow < HD) if par == 0 else (vrow >= HD)
        v_aug = jnp.where(vrow == (1 - par) * HD, 1.0, 0.0)
        vtf_ref[0, osl, :] = jnp.where(vdata, fvt[sl, :], v_aug).astype(BF16)


def _project(x2d, g, w_all, bw, b8, *, seq_len, aug, tm):
    rows = x2d.shape[0]
    assert rows % tm == 0
    row_spec = lambda n: pl.BlockSpec((tm, n), lambda i: (i, 0))
    const = lambda shape: pl.BlockSpec(shape, lambda i: (0, 0))
    f32o = lambda n: jax.ShapeDtypeStruct((rows, n), F32)
    b16o = lambda n: jax.ShapeDtypeStruct((rows, n), BF16)
    if aug:
        assert seq_len % tm == 0
        wide = FOX_HEADS * LANES
        tps = seq_len // tm
        t_shape = lambda n, dt: jax.ShapeDtypeStruct((rows // seq_len, n, seq_len), dt)
        t_spec = lambda n: pl.BlockSpec((1, n, tm), lambda i: (i // tps, 0, i % tps))
        out_shape = [t_shape(WIDTH, F32), t_shape(WIDTH, F32), t_shape(FOX_HEADS, F32), t_shape(WIDTH, F32),
                     jax.ShapeDtypeStruct((rows, DIFF_HEADS, 2 * HD), F32)]
        out_specs = [t_spec(WIDTH), t_spec(WIDTH), t_spec(FOX_HEADS), t_spec(WIDTH),
                     pl.BlockSpec((tm, DIFF_HEADS, 2 * HD), lambda i: (i, 0, 0))]
        out_shape += ([b16o(wide)] * 2 + [t_shape(wide, BF16)] + [b16o(WIDTH)] * 3
                      + [t_shape(DIFF_HEADS * VT_ROWS, BF16)] + [b16o(WIDTH)] + [b16o(D_MODEL)] * 2)
        out_specs += ([row_spec(wide)] * 2 + [t_spec(wide)] + [row_spec(WIDTH)] * 3
                      + [t_spec(DIFF_HEADS * VT_ROWS)] + [row_spec(WIDTH)] + [row_spec(D_MODEL)] * 2)
        scratch = [pltpu.VMEM((1, WIDTH), F32)]
    else:
        out_shape = [f32o(WIDTH), f32o(WIDTH), f32o(FOX_HEADS), f32o(WIDTH), f32o(WIDTH)]
        out_specs = [row_spec(WIDTH), row_spec(WIDTH), row_spec(FOX_HEADS), row_spec(WIDTH), row_spec(WIDTH)]
        out_shape += [b16o(WIDTH)] * 4 + [b16o(D_MODEL)] * 2
        out_specs += [row_spec(WIDTH)] * 4 + [row_spec(D_MODEL)] * 2
        scratch = []
    return pl.pallas_call(
        functools.partial(_proj_kernel, tm=tm, seq_len=seq_len, aug=aug),
        grid=(rows // tm,),
        in_specs=[row_spec(D_MODEL), const((1, D_MODEL)),
                  pl.BlockSpec((D_MODEL, _N_ALL), lambda i: (0, 0), pipeline_mode=pl.Buffered(1)),
                  const((1, WIDTH)), const((1, LANES))],
        out_specs=out_specs,
        out_shape=out_shape,
        scratch_shapes=scratch,
        compiler_params=pltpu.CompilerParams(
            dimension_semantics=("arbitrary",), vmem_limit_bytes=VMEM_LIMIT),
        name="proj_aug" if aug else "proj_plain",
    )(x2d, g, w_all, bw, b8)


def _bias_kernel(tbl_ref, o_ref, *, q_start, nq, segments, transposed, scale):
    hd = pl.program_id(0)
    far = tbl_ref[N_BUCKETS // 2 - 1, hd]
    q_axis, k_axis = (1, 0) if transposed else (0, 1)
    off = 0
    for k_start, nk, valid in segments:
        shape = (nk, nq) if transposed else (nq, nk)
        qpos = q_start + lax.broadcasted_iota(jnp.int32, shape, q_axis)
        kidx = lax.broadcasted_iota(jnp.int32, shape, k_axis)
        kpos = k_start + kidx
        rel = kpos - qpos
        n = jnp.abs(rel)
        large = jnp.full(shape, _MAX_EXACT, jnp.int32)
        for thr in _T5_THR:
            large = large + jnp.where(n >= thr, 1, 0)
        bucket = jnp.where(rel > 0, N_BUCKETS // 2, 0) + jnp.where(n < _MAX_EXACT, n, large)
        val = jnp.zeros(shape, F32)
        for b in range(N_BUCKETS):
            val = jnp.where(bucket == b, tbl_ref[b, hd], val)
        val = (val - far) * scale
        visible = ((kpos >> CHUNK_SHIFT) <= (qpos >> CHUNK_SHIFT)) & (kidx < valid)
        val = jnp.where(visible, val, NEG)
        if transposed:
            o_ref[0, off:off + nk, :] = val
        else:
            o_ref[0, :, off:off + nk] = val
        off += nk


def _bias_tiles(rel_bias, *, q_start, nq, segments, transposed, scale):
    total = sum(s[1] for s in segments)
    shape = (total, nq) if transposed else (nq, total)
    return pl.pallas_call(
        functools.partial(_bias_kernel, q_start=q_start, nq=nq, segments=segments, transposed=transposed,
                          scale=scale),
        grid=(DIFF_HEADS,),
        in_specs=[pl.BlockSpec(memory_space=pltpu.SMEM)],
        out_specs=pl.BlockSpec((1,) + shape, lambda h: (h, 0, 0)),
        out_shape=jax.ShapeDtypeStruct((DIFF_HEADS,) + shape, F32),
        name="t5_bias",
    )(rel_bias)


def _flash_t(k_ref, vt_ref, acc_ref, streams, qi, blk, prev_bias, diag_bias, diag_mask):
    n = len(streams)
    acc_ref[...] = jnp.zeros_like(acc_ref)

    def step(first_blk, width, ms, biases, mask):
        start = pl.multiple_of(first_blk * blk, blk)
        sts = [_mm_nt(k_ref[0, pl.ds(start, width), ksl], q) for q, ksl, _ in streams]
        out, ps, alphas = [], [], []
        for s, st in enumerate(sts):
            if biases is not None:
                st = st + biases[s]()
            if mask is not None:
                st = jnp.where(mask, st, NEG)
            m_new = jnp.maximum(ms[s], jnp.max(st, axis=0, keepdims=True))
            ps.append(jnp.exp2(st - m_new).astype(BF16))
            alphas.append(jnp.exp2(ms[s] - m_new))
            out.append(m_new)
        for s, (_, _, vsl) in enumerate(streams):
            acc_ref[s] = alphas[s] * acc_ref[s] + _mm(vt_ref[0, vsl, pl.ds(start, width)], ps[s])
        return tuple(out)

    def fast_step(first_blk, width, ms):
        start = pl.multiple_of(first_blk * blk, blk)
        sts = [_mm_nt(k_ref[0, pl.ds(start, width), ksl], q) for q, ksl, _ in streams]
        ps = [jnp.exp2(st - ms[s]).astype(BF16) for s, st in enumerate(sts)]
        over = None
        for p in ps:
            d = jnp.max(p, axis=0, keepdims=True)
            over = d if over is None else jnp.maximum(over, d)
        pvs = [_mm(vt_ref[0, vsl, pl.ds(start, width)], ps[s]) for s, (_, _, vsl) in enumerate(streams)]

        def commit(c):
            for s in range(n):
                acc_ref[s] = acc_ref[s] + pvs[s]
            return c

        return lax.cond(jnp.max(over.astype(F32)) > 2.0 ** HEADROOM,
                        lambda c: step(first_blk, width, c, None, None), commit, ms)

    ms = tuple(jnp.full((1, blk), NEG, F32) for _ in range(n))
    ms = step(qi, blk, ms, diag_bias, diag_mask)
    if prev_bias is not None:
        ms = lax.cond(qi >= 1, lambda c: step(qi - 1, blk, c, prev_bias, None), lambda c: c, ms)
    n_plain = qi if prev_bias is None else jnp.maximum(qi - 1, 0)
    ms = lax.fori_loop(0, n_plain // WIDE_BLOCKS,
                       lambda j, c: fast_step(j * WIDE_BLOCKS, WIDE_BLOCKS * blk, c), ms)
    n_wide = (n_plain // WIDE_BLOCKS) * WIDE_BLOCKS
    lax.fori_loop(n_wide, n_plain, lambda j, c: fast_step(j, blk, c), ms)


def _fox_kernel(qa_ref, ka_ref, vt_ref, z_ref, o_ref, acc_ref, *, blk, nh):
    qi = pl.program_id(2)
    key = lax.broadcasted_iota(jnp.int32, (blk, blk), 0)
    qry = lax.broadcasted_iota(jnp.int32, (blk, blk), 1)
    hsl = [slice(hd * LANES, (hd + 1) * LANES) for hd in range(nh)]
    streams = [(qa_ref[0, :, sl], sl, sl) for sl in hsl]
    _flash_t(ka_ref, vt_ref, acc_ref, streams, qi, blk, None, None, key <= qry)
    lane = lax.broadcasted_iota(jnp.int32, (blk, LANES), 1)
    for pair in range(nh // 2):
        outs = []
        for par in range(2):
            acc = acc_ref[2 * pair + par]
            ones_row = (1 - par) * HD
            outs.append((acc / acc[ones_row:ones_row + 1, :]).T)
        o = jnp.where(lane < HD, outs[0], outs[1])
        o_ref[0, :, hsl[pair]] = (o * _silu(z_ref[0, :, hsl[pair]].astype(F32))).astype(BF16)


def _fox_attention(qa, ka, vt, fz, *, blk, nh):
    b, t, _ = qa.shape
    return pl.pallas_call(
        functools.partial(_fox_kernel, blk=blk, nh=nh),
        grid=(b, FOX_HEADS // nh, t // blk),
        in_specs=[pl.BlockSpec((1, blk, nh * LANES), lambda b, p, i: (b, i, p)),
                  pl.BlockSpec((1, t, nh * LANES), lambda b, p, i: (b, 0, p)),
                  pl.BlockSpec((1, nh * LANES, t), lambda b, p, i: (b, p, 0)),
                  pl.BlockSpec((1, blk, nh * HD), lambda b, p, i: (b, i, p))],
        out_specs=pl.BlockSpec((1, blk, nh * HD), lambda b, p, i: (b, i, p)),
        out_shape=jax.ShapeDtypeStruct((b, t, WIDTH), BF16),
        scratch_shapes=[pltpu.VMEM((nh, LANES, blk), F32)],
        compiler_params=pltpu.CompilerParams(
            dimension_semantics=("parallel", "parallel", "arbitrary"), vmem_limit_bytes=VMEM_LIMIT),
        name="fox_prompt",
    )(qa, ka, vt, fz)


def _lambda(lq1_ref, lk1_ref, lq2_ref, lk2_ref):
    s1 = jnp.sum(lq1_ref[...] * lk1_ref[...], axis=-1, keepdims=True)
    s2 = jnp.sum(lq2_ref[...] * lk2_ref[...], axis=-1, keepdims=True)
    return jnp.exp(s1) - jnp.exp(s2) + LAMBDA_INIT


def _diff_finish(o0, o1, lam, subg, z):
    o = o0 - lam * o1
    ms = jnp.mean(o * o, axis=-1, keepdims=True)
    od = o * lax.rsqrt(ms + NORM_EPS) * subg * (1.0 - LAMBDA_INIT)
    return (od * _silu(z)).astype(BF16)


def _diff_kernel(q_ref, k_ref, vt_ref, bias_ref, z_ref, lq1_ref, lk1_ref, lq2_ref, lk2_ref, subg_ref,
                 o_ref, acc_ref, *, blk, nh):
    qi = pl.program_id(2)
    lane = lax.broadcasted_iota(jnp.int32, (blk, LANES), 1)
    streams, prev_bias, diag_bias = [], [], []
    for hd in range(nh):
        q = q_ref[0, :, hd * LANES:(hd + 1) * LANES]
        for mp in range(2):
            streams.append((jnp.where((lane < HD) if mp == 0 else (lane >= HD), q, jnp.zeros_like(q)),
                            slice(hd * LANES, (hd + 1) * LANES), slice(hd * VT_ROWS, (hd + 1) * VT_ROWS)))
            prev_bias.append(lambda hd=hd: bias_ref[hd, 0:blk, :])
            diag_bias.append(lambda hd=hd: bias_ref[hd, blk:2 * blk, :])
    _flash_t(k_ref, vt_ref, acc_ref, streams, qi, blk, prev_bias, diag_bias, None)
    lam = _lambda(lq1_ref, lk1_ref, lq2_ref, lk2_ref)
    for hd in range(nh):
        outs = []
        for mp in range(2):
            acc = acc_ref[2 * hd + mp]
            outs.append((acc[0:LANES, :] / acc[LANES:LANES + 1, :]).T)
        sl = slice(hd * LANES, (hd + 1) * LANES)
        o_ref[0, :, sl] = _diff_finish(outs[0], outs[1], lam, subg_ref[...], z_ref[0, :, sl].astype(F32))


def _diff_attention(dq, dk, dvt, bias, dz, lq1, lk1, lq2, lk2, subg, *, blk, nh):
    b, t, _ = dq.shape
    small = lambda n: pl.BlockSpec((1, n), lambda b, h, i: (0, 0))
    return pl.pallas_call(
        functools.partial(_diff_kernel, blk=blk, nh=nh),
        grid=(b, DIFF_HEADS // nh, t // blk),
        in_specs=[pl.BlockSpec((1, blk, nh * LANES), lambda b, h, i: (b, i, h)),
                  pl.BlockSpec((1, t, nh * LANES), lambda b, h, i: (b, 0, h)),
                  pl.BlockSpec((1, nh * VT_ROWS, t), lambda b, h, i: (b, h, 0)),
                  pl.BlockSpec((nh, 2 * blk, blk), lambda b, h, i: (h, 0, 0)),
                  pl.BlockSpec((1, blk, nh * LANES), lambda b, h, i: (b, i, h)),
                  small(HD), small(HD), small(HD), small(HD), small(2 * HD)],
        out_specs=pl.BlockSpec((1, blk, nh * LANES), lambda b, h, i: (b, i, h)),
        out_shape=jax.ShapeDtypeStruct((b, t, WIDTH), BF16),
        scratch_shapes=[pltpu.VMEM((2 * nh, VT_ROWS, blk), F32)],
        compiler_params=pltpu.CompilerParams(
            dimension_semantics=("parallel", "parallel", "arbitrary"), vmem_limit_bytes=VMEM_LIMIT),
        name="diff_prompt",
    )(dq, dk, dvt, bias, dz, lq1, lk1, lq2, lk2, subg)


def _suffix_sums(x):
    n = x.shape[1] // LANES
    lane = lax.broadcasted_iota(jnp.int32, (x.shape[0], LANES), 1)
    after = jnp.zeros((x.shape[0], 1), F32)
    pieces = [None] * n
    for blk in reversed(range(n)):
        piece = x[:, blk * LANES:(blk + 1) * LANES]
        y = piece
        shift = 1
        while shift < LANES:
            y = y + jnp.where(lane + shift < LANES, pltpu.roll(y, LANES - shift, 1), 0.0)
            shift *= 2
        pieces[blk] = y - piece + after
        after = after + y[:, 0:1]
    return jnp.concatenate(pieces, axis=1), after


def _pad_rows(x, rows):
    return jnp.concatenate([x, jnp.zeros((rows - x.shape[0], x.shape[1]), x.dtype)], axis=0)


def _joint_softmax_pv(s_p, s_n, vt_p, v_n):
    m = jnp.maximum(jnp.max(s_p, axis=-1, keepdims=True), jnp.max(s_n, axis=-1, keepdims=True))
    p_p = jnp.exp(s_p - m)
    p_n = jnp.exp(s_n - m)
    l = jnp.sum(p_p, axis=-1, keepdims=True) + jnp.sum(p_n, axis=-1, keepdims=True)
    return (_mm_nt(p_p.astype(BF16), vt_p) + _mm(p_n.astype(BF16), v_n)) / l


def _fox_sample_kernel(q_ref, kt_ref, vt_ref, kn_ref, vn_ref, lfp_ref, lfn_ref, z_ref, o_ref, *, nq):
    r_new, total_new = _suffix_sums(lfn_ref[0])
    r_past, _ = _suffix_sums(lfp_ref[0])
    r_past = r_past + total_new
    lane = lax.broadcasted_iota(jnp.int32, (nq, LANES), 1)
    causal = lane <= lax.broadcasted_iota(jnp.int32, (nq, LANES), 0)
    for blk in range(PAIRS):
        sl = slice(blk * LANES, (blk + 1) * LANES)
        q = q_ref[:, sl]
        kt = kt_ref[0, sl, :].astype(BF16)
        vt = vt_ref[0, sl, :].astype(BF16)
        kn = _pad_rows(kn_ref[:, sl], LANES).astype(BF16)
        vn = _pad_rows(vn_ref[:, sl], LANES).astype(BF16)
        outs = []
        for par in range(2):
            hd = 2 * blk + par
            qm = jnp.where((lane < HD) if par == 0 else (lane >= HD), q, jnp.zeros_like(q))
            s_p = _mm(qm, kt) + r_past[hd:hd + 1, :]
            s_n = jnp.where(causal, _mm_nt(qm, kn) + r_new[hd:hd + 1, :], NEG)
            outs.append(_joint_softmax_pv(s_p, s_n, vt, vn))
        o = jnp.where(lane < HD, outs[0], outs[1])
        o_ref[:, sl] = (o * _silu(z_ref[:, sl].astype(F32))).astype(BF16)


def _fox_sample(fq, cache_kt, cache_vt, fk, fv, lf_past_t, lf_new_t, fz, *, nq):
    b, _, past = cache_kt.shape
    new = lambda: pl.BlockSpec((nq, WIDTH), lambda i: (i, 0))
    cache = lambda: pl.BlockSpec((1, WIDTH, past), lambda i: (i, 0, 0))
    return pl.pallas_call(
        functools.partial(_fox_sample_kernel, nq=nq),
        grid=(b,),
        in_specs=[new(), cache(), cache(), new(), new(),
                  pl.BlockSpec((1, FOX_HEADS, past), lambda i: (i, 0, 0)),
                  pl.BlockSpec((1, FOX_HEADS, LANES), lambda i: (i, 0, 0)),
                  new()],
        out_specs=new(),
        out_shape=jax.ShapeDtypeStruct((b * nq, WIDTH), BF16),
        compiler_params=pltpu.CompilerParams(
            dimension_semantics=("parallel",), vmem_limit_bytes=VMEM_LIMIT),
        name="fox_sample",
    )(fq, cache_kt, cache_vt, fk, fv, lf_past_t, lf_new_t, fz)


def _diff_sample_kernel(q_ref, kt_ref, v_ref, kn_ref, vn_ref, bias_ref, z_ref,
                        lq1_ref, lk1_ref, lq2_ref, lk2_ref, subg_ref, o_ref, *, nq, past):
    lane = lax.broadcasted_iota(jnp.int32, (nq, LANES), 1)
    lam = _lambda(lq1_ref, lk1_ref, lq2_ref, lk2_ref)
    for hd in range(DIFF_HEADS):
        sl = slice(hd * LANES, (hd + 1) * LANES)
        q = q_ref[:, sl]
        kt = kt_ref[0, sl, :].astype(BF16)
        vt = v_ref[0, pl.ds(hd, past, stride=DIFF_HEADS), :].T.astype(BF16)
        kn = _pad_rows(kn_ref[:, sl], LANES).astype(BF16)
        vn = _pad_rows(vn_ref[:, sl], LANES).astype(BF16)
        outs = []
        for mp in range(2):
            qm = jnp.where((lane < HD) if mp == 0 else (lane >= HD), q, jnp.zeros_like(q))
            s_p = _mm(qm, kt) + bias_ref[hd, :, 0:past]
            s_n = _mm_nt(qm, kn) + bias_ref[hd, :, past:past + LANES]
            outs.append(_joint_softmax_pv(s_p, s_n, vt, vn))
        o_ref[:, sl] = _diff_finish(outs[0], outs[1], lam, subg_ref[...], z_ref[:, sl].astype(F32))


def _diff_sample(dq, cache_kt, cache_v, dk, dv, bias, dz, lq1, lk1, lq2, lk2, subg, *, nq):
    b, _, past = cache_kt.shape
    new = lambda: pl.BlockSpec((nq, WIDTH), lambda i: (i, 0))
    small = lambda n: pl.BlockSpec((1, n), lambda i: (0, 0))
    return pl.pallas_call(
        functools.partial(_diff_sample_kernel, nq=nq, past=past),
        grid=(b,),
        in_specs=[new(), pl.BlockSpec((1, WIDTH, past), lambda i: (i, 0, 0)),
                  pl.BlockSpec((1, past * DIFF_HEADS, 2 * HD), lambda i: (i, 0, 0)), new(), new(),
                  pl.BlockSpec((DIFF_HEADS, nq, past + LANES), lambda i: (0, 0, 0)),
                  new(), small(HD), small(HD), small(HD), small(HD), small(2 * HD)],
        out_specs=new(),
        out_shape=jax.ShapeDtypeStruct((b * nq, WIDTH), BF16),
        compiler_params=pltpu.CompilerParams(
            dimension_semantics=("parallel",), vmem_limit_bytes=VMEM_LIMIT),
        name="diff_sample",
    )(dq, cache_kt, cache_v, dk, dv, bias, dz, lq1, lk1, lq2, lk2, subg)


def _out_kernel(yf_ref, yd_ref, ga_ref, gb_ref, x_ref, wuf_ref, wud_ref, wo_ref, g_ref, o_ref):
    merged = (ga_ref[...].astype(F32) * _mm(yf_ref[...], wuf_ref[...])
              + gb_ref[...].astype(F32) * _mm(yd_ref[...], wud_ref[...]))
    out = x_ref[...] + _mm(merged.astype(BF16), wo_ref[...])
    ms = jnp.mean(out * out, axis=-1, keepdims=True)
    o_ref[...] = out * lax.rsqrt(ms + NORM_EPS) * g_ref[...]


def _output(yf, yd, ga, gb, x2d, wuf, wud, wo, g, *, tm):
    rows = x2d.shape[0]
    assert rows % tm == 0
    row_spec = lambda n: pl.BlockSpec((tm, n), lambda i: (i, 0))
    const = lambda shape: pl.BlockSpec(shape, lambda i: (0, 0))
    return pl.pallas_call(
        _out_kernel,
        grid=(rows // tm,),
        in_specs=[row_spec(WIDTH), row_spec(WIDTH), row_spec(D_MODEL), row_spec(D_MODEL), row_spec(D_MODEL),
                  const((WIDTH, D_MODEL)), const((WIDTH, D_MODEL)), const((D_MODEL, D_MODEL)),
                  const((1, D_MODEL))],
        out_specs=row_spec(D_MODEL),
        out_shape=jax.ShapeDtypeStruct((rows, D_MODEL), F32),
        compiler_params=pltpu.CompilerParams(
            dimension_semantics=("parallel",), vmem_limit_bytes=VMEM_LIMIT),
        name="out_proj",
    )(yf, yd, ga, gb, x2d, wuf, wud, wo, g)


def _pack_w_in(w_in, b_forget):
    sizes = (WIDTH, WIDTH, WIDTH, WIDTH, FOX_HEADS, WIDTH, WIDTH, WIDTH, WIDTH, D_MODEL, D_MODEL)
    offs = np.cumsum((0,) + sizes)
    fq, fk, fv, fz, fl, dq, dk, dv, dz, ga, gb = (w_in[:, offs[i]:offs[i + 1]] for i in range(len(sizes)))
    swap = np.arange(FOX_HEADS) ^ 1
    flw = jnp.repeat(fl[:, swap], HD, axis=1)
    fl8 = jnp.pad(fl, ((0, 0), (0, LANES - FOX_HEADS)))
    w_all = jnp.concatenate([fq, fk, fv, fz, dq, dk, dv, dz, ga, gb, flw, fl8], axis=1).astype(BF16)
    bw = jnp.repeat(b_forget[swap], HD)[None, :]
    return w_all, bw, jnp.pad(b_forget, (0, LANES - FOX_HEADS))[None, :]


def kernel(x_prompt, x_sample, cache_fox_k, cache_fox_v, cache_fox_logf, cache_diff_k, cache_diff_v,
           norm_in_g, w_in, b_forget, lambda_q1, lambda_k1, lambda_q2, lambda_k2, subln_g,
           w_up_fox, w_up_diff, w_o, rel_bias, final_norm_g):
    bp, tp, _ = x_prompt.shape
    bs, ts, _ = x_sample.shape
    past = cache_fox_k.shape[2]
    blk = 256
    tm = 512

    w_all, bw, b8 = _pack_w_in(w_in[0], b_forget[0])
    g_in = norm_in_g[0][None, :]
    g_out = final_norm_g[None, :]
    wuf, wud, wo = w_up_fox[0].astype(BF16), w_up_diff[0].astype(BF16), w_o[0].astype(BF16)
    lq1, lk1, lq2, lk2 = (a[0][None, :] for a in (lambda_q1, lambda_k1, lambda_q2, lambda_k2))
    subg = subln_g[0][None, :]

    xp = x_prompt.reshape(bp * tp, D_MODEL)
    (fkt, fvt, lft, dkt, dv4, qa, ka, vtf, fz, dq, dkb, vtd, dz, ga, gb) = _project(
        xp, g_in, w_all, bw, b8, seq_len=tp, aug=True, tm=tm)
    r3 = lambda a: a.reshape(bp, tp, a.shape[-1])
    yf = _fox_attention(r3(qa), r3(ka), vtf, r3(fz), blk=blk, nh=FOX_STEP_HEADS)
    bias_p = _bias_tiles(rel_bias, q_start=blk, nq=blk, segments=((0, blk, blk), (blk, blk, blk)),
                         transposed=True, scale=LOG2E)
    yd = _diff_attention(r3(dq), r3(dkb), vtd, bias_p, r3(dz), lq1, lk1, lq2, lk2, subg, blk=blk,
                         nh=DIFF_STEP_HEADS)
    y_p = _output(yf.reshape(bp * tp, WIDTH), yd.reshape(bp * tp, WIDTH), ga, gb, xp, wuf, wud, wo, g_out, tm=tm)
    fox_k_p = jnp.transpose(fkt.reshape(bp, FOX_HEADS, HD, tp), (0, 3, 1, 2))[None]
    fox_v_p = jnp.transpose(fvt.reshape(bp, FOX_HEADS, HD, tp), (0, 3, 1, 2))[None]
    fox_lf_p = jnp.transpose(lft, (0, 2, 1))[None]
    diff_k_p = jnp.transpose(dkt.reshape(bp, DIFF_HEADS, 2, HD, tp), (0, 4, 1, 2, 3))[None]
    diff_v_p = dv4.reshape(1, bp, tp, DIFF_HEADS, 2 * HD)

    xs = x_sample.reshape(bs * ts, D_MODEL)
    (fk_s, fv_s, lf_s, dk_s, dv_s, fq_s, fz_s, dq_s, dz_s, ga_s, gb_s) = _project(
        xs, g_in, w_all, bw, b8, seq_len=ts, aug=False, tm=tm)
    fox_kt = jnp.transpose(cache_fox_k[0], (0, 2, 3, 1)).reshape(bs, WIDTH, past)
    fox_vt = jnp.transpose(cache_fox_v[0], (0, 2, 3, 1)).reshape(bs, WIDTH, past)
    diff_kt = jnp.transpose(cache_diff_k[0], (0, 2, 3, 4, 1)).reshape(bs, WIDTH, past)
    diff_v = cache_diff_v[0].reshape(bs, past * DIFF_HEADS, 2 * HD)
    lf_past_t = jnp.swapaxes(cache_fox_logf[0], 1, 2)
    lf_new_t = jnp.pad(jnp.swapaxes(lf_s.reshape(bs, ts, FOX_HEADS), 1, 2), ((0, 0), (0, 0), (0, LANES - ts)))
    yf_s = _fox_sample(fq_s, fox_kt, fox_vt, fk_s, fv_s, lf_past_t, lf_new_t, fz_s, nq=ts)
    bias_s = _bias_tiles(rel_bias, q_start=past, nq=ts, segments=((0, past, past), (past, LANES, ts)),
                         transposed=False, scale=1.0)
    yd_s = _diff_sample(dq_s, diff_kt, diff_v, dk_s, dv_s, bias_s, dz_s, lq1, lk1, lq2, lk2, subg, nq=ts)
    y_s = _output(yf_s, yd_s, ga_s, gb_s, xs, wuf, wud, wo, g_out, tm=tm)

    return (y_p.reshape(bp, tp, D_MODEL), y_s.reshape(bs, ts, D_MODEL),
            fox_k_p, fox_v_p, fox_lf_p, diff_k_p, diff_v_p,
            fk_s.reshape(1, bs, ts, FOX_HEADS, HD), fv_s.reshape(1, bs, ts, FOX_HEADS, HD),
            lf_s.reshape(1, bs, ts, FOX_HEADS),
            dk_s.reshape(1, bs, ts, DIFF_HEADS, 2, HD), dv_s.reshape(1, bs, ts, DIFF_HEADS, 2 * HD))
```

```python
import functools
import math

import numpy as np
import jax
import jax.numpy as jnp
from jax import lax
from jax.experimental import pallas as pl
from jax.experimental.pallas import tpu as pltpu

F32 = jnp.float32
BF16 = jnp.bfloat16

D_MODEL = 1024
HD = 64
FOX_HEADS = 8
DIFF_HEADS = 4
WIDTH = 512
CHUNK = 64
CHUNK_SHIFT = 6
N_BUCKETS = 32
MAX_DISTANCE = 128
NORM_EPS = 1e-6
SCALE = HD ** -0.5
NEG = -1e30
LAMBDA_INIT = 0.8 - 0.6 * math.exp(-0.3 * 0)
LOG2E = math.log2(math.e)

LANES = 128
PAIRS = WIDTH // LANES
VT_PAD = 16
VT_ROWS = LANES + VT_PAD
FVT_ROWS = HD + VT_PAD

_O_FQ, _O_FK, _O_FV, _O_FZ = 0, 512, 1024, 1536
_O_DQ, _O_DK, _O_DV, _O_DZ = 2048, 2560, 3072, 3584
_O_GA, _O_GB = 4096, 5120
_O_FLW = 6144
_O_FL8 = 6656
_N_ALL = 6784

VMEM_LIMIT = 56 * 1024 * 1024

FOX_STEP_HEADS = 8
DIFF_STEP_HEADS = 4
HEADROOM = 64.0
WIDE_BLOCKS = 2


def _t5_thresholds():
    nb = N_BUCKETS // 2
    max_exact = nb // 2
    n = np.arange(0, 4 * MAX_DISTANCE)
    large = max_exact + (np.log(np.maximum(n, 1).astype(np.float32) / max_exact)
                         / math.log(MAX_DISTANCE / max_exact) * (nb - max_exact)).astype(np.int32)
    large = np.minimum(large, nb - 1)
    thr = [int(np.argmax(large >= b)) for b in range(max_exact + 1, nb)]
    return max_exact, tuple(thr)


_MAX_EXACT, _T5_THR = _t5_thresholds()
FAR_DIST = _T5_THR[-1]


def _mm(a, b):
    return jnp.dot(a, b, preferred_element_type=F32)


def _mm_nt(a, b):
    return lax.dot_general(a, b, (((1,), (1,)), ((), ())), preferred_element_type=F32)


def _log_sigmoid(x):
    return -(jnp.maximum(-x, 0.0) + jnp.log1p(jnp.exp(-jnp.abs(x))))


def _sigmoid(x):
    return 1.0 / (1.0 + jnp.exp(-x))


def _silu(x):
    return x * _sigmoid(x)


def _split3(x):
    p1 = x.astype(BF16).astype(F32)
    r1 = x - p1
    p2 = r1.astype(BF16).astype(F32)
    p3 = r1 - p2
    return p1, p2, p3


def _proj_kernel(x_ref, g_ref, w_ref, bw_ref, b8_ref, *refs, tm, seq_len, aug):
    if aug:
        (fkt_ref, fvt_ref, lft_ref, dkt_ref, dv4_ref, qa_ref, ka_ref, vtf_ref,
         fz_ref, dq_ref, dkb_ref, vtd_ref, dz_ref, ga_ref, gb_ref, carry_ref) = refs
    else:
        (fk_ref, fv_ref, lf_ref, dk_ref, dv_ref, fq_ref,
         fz_ref, dq_ref, dz_ref, ga_ref, gb_ref) = refs

    x = x_ref[...]
    ms = jnp.mean(x * x, axis=-1, keepdims=True)
    h = (x * lax.rsqrt(ms + NORM_EPS) * g_ref[...]).astype(BF16)

    def grp(off, n=WIDTH):
        return _mm(h, w_ref[:, off:off + n])

    if not aug:
        fq_ref[...] = (grp(_O_FQ) * SCALE).astype(BF16)
        fk_ref[...] = grp(_O_FK)
        fv_ref[...] = grp(_O_FV)
        fz_ref[...] = grp(_O_FZ).astype(BF16)
        dq_ref[...] = (grp(_O_DQ) * SCALE).astype(BF16)
        dk_ref[...] = grp(_O_DK)
        dv_ref[...] = grp(_O_DV)
        dz_ref[...] = grp(_O_DZ).astype(BF16)
        ga_ref[...] = _sigmoid(grp(_O_GA, D_MODEL)).astype(BF16)
        gb_ref[...] = _sigmoid(grp(_O_GB, D_MODEL)).astype(BF16)
        lf_ref[...] = _log_sigmoid(grp(_O_FL8, LANES) + b8_ref[...])[:, :FOX_HEADS]
        return

    lfw = _log_sigmoid(grp(_O_FLW) + bw_ref[...])
    row = lax.broadcasted_iota(jnp.int32, (tm, tm), 0)
    col = lax.broadcasted_iota(jnp.int32, (tm, tm), 1)
    tri = jnp.where(col <= row, 1.0, 0.0).astype(BF16)
    c = sum(_mm(tri, p.astype(BF16)) for p in _split3(lfw))
    tiles_per_seq = seq_len // tm

    @pl.when(pl.program_id(0) % tiles_per_seq == 0)
    def _():
        carry_ref[...] = jnp.zeros_like(carry_ref)

    c = c + carry_ref[...]
    carry_ref[...] = c[tm - 1:tm, :]
    c_hi, c_mid, c_lo = _split3(c * LOG2E)

    fq = grp(_O_FQ) * (SCALE * LOG2E)
    fk = grp(_O_FK)
    fv = grp(_O_FV)
    fvt = fv.T
    fkt_ref[0] = fk.T
    fvt_ref[0] = fvt
    tail = jnp.where(lax.broadcasted_iota(jnp.int32, (VT_PAD, tm), 0) == 0, 1.0, 0.0).astype(BF16)
    lane = lax.broadcasted_iota(jnp.int32, (tm, LANES), 1)
    for hd in range(FOX_HEADS):
        blk, par = divmod(hd, 2)
        sl = slice(blk * LANES, (blk + 1) * LANES)
        data = (lane < HD) if par == 0 else (lane >= HD)
        a = lane - (1 - par) * HD
        hi, mid, lo = c_hi[:, sl], c_mid[:, sl], c_lo[:, sl]
        q_aug = jnp.where(a == 0, hi, jnp.where(a == 1, mid, jnp.where(
            a == 2, lo, jnp.where((a >= 3) & (a < 6), 1.0, 0.0))))
        k_aug = jnp.where((a >= 0) & (a < 3), 1.0, jnp.where(a == 3, -hi, jnp.where(
            a == 4, -mid, jnp.where(a == 5, -lo, 0.0))))
        osl = slice(hd * LANES, (hd + 1) * LANES)
        qa_ref[:, osl] = jnp.where(data, fq[:, sl], q_aug).astype(BF16)
        ka_ref[:, osl] = jnp.where(data, fk[:, sl], k_aug).astype(BF16)
        base = hd * FVT_ROWS
        vtf_ref[0, base:base + HD, :] = fvt[hd * HD:(hd + 1) * HD, :].astype(BF16)
        vtf_ref[0, base + HD:base + FVT_ROWS, :] = tail

    ga_ref[...] = _sigmoid(grp(_O_GA, D_MODEL)).astype(BF16)
    gb_ref[...] = _sigmoid(grp(_O_GB, D_MODEL)).astype(BF16)

    dk = grp(_O_DK)
    dkt_ref[0] = dk.T
    dkb_ref[...] = dk.astype(BF16)
    dv = grp(_O_DV)
    dv4_ref[...] = dv.reshape(tm, DIFF_HEADS, 2 * HD)
    for hd in range(DIFF_HEADS):
        base = hd * VT_ROWS
        vtd_ref[0, base:base + LANES, :] = dv[:, hd * LANES:(hd + 1) * LANES].T.astype(BF16)
        vtd_ref[0, base + LANES:base + VT_ROWS, :] = tail

    lft_ref[0] = _log_sigmoid(grp(_O_FL8, LANES) + b8_ref[...]).T[:FOX_HEADS, :]
    dq_ref[...] = (grp(_O_DQ) * (SCALE * LOG2E)).astype(BF16)
    fz_ref[...] = grp(_O_FZ).astype(BF16)
    dz_ref[...] = grp(_O_DZ).astype(BF16)


def _project(x2d, g, w_all, bw, b8, *, seq_len, aug, tm):
    rows = x2d.shape[0]
    assert rows % tm == 0
    row_spec = lambda n: pl.BlockSpec((tm, n), lambda i: (i, 0))
    const = lambda shape: pl.BlockSpec(shape, lambda i: (0, 0))
    f32o = lambda n: jax.ShapeDtypeStruct((rows, n), F32)
    b16o = lambda n: jax.ShapeDtypeStruct((rows, n), BF16)
    if aug:
        assert seq_len % tm == 0
        wide = FOX_HEADS * LANES
        tps = seq_len // tm
        t_shape = lambda n, dt: jax.ShapeDtypeStruct((rows // seq_len, n, seq_len), dt)
        t_spec = lambda n: pl.BlockSpec((1, n, tm), lambda i: (i // tps, 0, i % tps))
        out_shape = [t_shape(WIDTH, F32), t_shape(WIDTH, F32), t_shape(FOX_HEADS, F32), t_shape(WIDTH, F32),
                     jax.ShapeDtypeStruct((rows, DIFF_HEADS, 2 * HD), F32)]
        out_specs = [t_spec(WIDTH), t_spec(WIDTH), t_spec(FOX_HEADS), t_spec(WIDTH),
                     pl.BlockSpec((tm, DIFF_HEADS, 2 * HD), lambda i: (i, 0, 0))]
        out_shape += ([b16o(wide)] * 2 + [t_shape(FOX_HEADS * FVT_ROWS, BF16)] + [b16o(WIDTH)] * 3
                      + [t_shape(DIFF_HEADS * VT_ROWS, BF16)] + [b16o(WIDTH)] + [b16o(D_MODEL)] * 2)
        out_specs += ([row_spec(wide)] * 2 + [t_spec(FOX_HEADS * FVT_ROWS)] + [row_spec(WIDTH)] * 3
                      + [t_spec(DIFF_HEADS * VT_ROWS)] + [row_spec(WIDTH)] + [row_spec(D_MODEL)] * 2)
        scratch = [pltpu.VMEM((1, WIDTH), F32)]
    else:
        out_shape = [f32o(WIDTH), f32o(WIDTH), f32o(FOX_HEADS), f32o(WIDTH), f32o(WIDTH)]
        out_specs = [row_spec(WIDTH), row_spec(WIDTH), row_spec(FOX_HEADS), row_spec(WIDTH), row_spec(WIDTH)]
        out_shape += [b16o(WIDTH)] * 4 + [b16o(D_MODEL)] * 2
        out_specs += [row_spec(WIDTH)] * 4 + [row_spec(D_MODEL)] * 2
        scratch = []
    return pl.pallas_call(
        functools.partial(_proj_kernel, tm=tm, seq_len=seq_len, aug=aug),
        grid=(rows // tm,),
        in_specs=[row_spec(D_MODEL), const((1, D_MODEL)),
                  pl.BlockSpec((D_MODEL, _N_ALL), lambda i: (0, 0), pipeline_mode=pl.Buffered(1)),
                  const((1, WIDTH)), const((1, LANES))],
        out_specs=out_specs,
        out_shape=out_shape,
        scratch_shapes=scratch,
        compiler_params=pltpu.CompilerParams(
            dimension_semantics=("arbitrary",), vmem_limit_bytes=VMEM_LIMIT),
        name="proj_aug" if aug else "proj_plain",
    )(x2d, g, w_all, bw, b8)


def _bias_kernel(tbl_ref, o_ref, *, q_start, nq, segments, transposed, scale):
    hd = pl.program_id(0)
    far = tbl_ref[N_BUCKETS // 2 - 1, hd]
    q_axis, k_axis = (1, 0) if transposed else (0, 1)
    off = 0
    for k_start, nk, valid in segments:
        shape = (nk, nq) if transposed else (nq, nk)
        qpos = q_start + lax.broadcasted_iota(jnp.int32, shape, q_axis)
        kidx = lax.broadcasted_iota(jnp.int32, shape, k_axis)
        kpos = k_start + kidx
        rel = kpos - qpos
        n = jnp.abs(rel)
        large = jnp.full(shape, _MAX_EXACT, jnp.int32)
        for thr in _T5_THR:
            large = large + jnp.where(n >= thr, 1, 0)
        bucket = jnp.where(rel > 0, N_BUCKETS // 2, 0) + jnp.where(n < _MAX_EXACT, n, large)
        val = jnp.zeros(shape, F32)
        for b in range(N_BUCKETS):
            val = jnp.where(bucket == b, tbl_ref[b, hd], val)
        val = (val - far) * scale
        visible = ((kpos >> CHUNK_SHIFT) <= (qpos >> CHUNK_SHIFT)) & (kidx < valid)
        val = jnp.where(visible, val, NEG)
        if transposed:
            o_ref[0, off:off + nk, :] = val
        else:
            o_ref[0, :, off:off + nk] = val
        off += nk


def _bias_tiles(rel_bias, *, q_start, nq, segments, transposed, scale):
    total = sum(s[1] for s in segments)
    shape = (total, nq) if transposed else (nq, total)
    return pl.pallas_call(
        functools.partial(_bias_kernel, q_start=q_start, nq=nq, segments=segments, transposed=transposed,
                          scale=scale),
        grid=(DIFF_HEADS,),
        in_specs=[pl.BlockSpec(memory_space=pltpu.SMEM)],
        out_specs=pl.BlockSpec((1,) + shape, lambda h: (h, 0, 0)),
        out_shape=jax.ShapeDtypeStruct((DIFF_HEADS,) + shape, F32),
        name="t5_bias",
    )(rel_bias)


def _flash_t(k_ref, vt_ref, acc_ref, streams, qi, blk, prev_bias, diag_bias, diag_mask):
    n = len(streams)
    acc_ref[...] = jnp.zeros_like(acc_ref)

    def step(first_blk, width, ms, biases, mask):
        start = pl.multiple_of(first_blk * blk, blk)
        sts = [_mm_nt(k_ref[0, pl.ds(start, width), ksl], q) for q, ksl, _ in streams]
        out, ps, alphas = [], [], []
        for s, st in enumerate(sts):
            if biases is not None:
                st = st + biases[s]()
            if mask is not None:
                st = jnp.where(mask, st, NEG)
            m_new = jnp.maximum(ms[s], jnp.max(st, axis=0, keepdims=True))
            ps.append(jnp.exp2(st - m_new).astype(BF16))
            alphas.append(jnp.exp2(ms[s] - m_new))
            out.append(m_new)
        for s, (_, _, vsl) in enumerate(streams):
            acc_ref[s] = alphas[s] * acc_ref[s] + _mm(vt_ref[0, vsl, pl.ds(start, width)], ps[s])
        return tuple(out)

    def fast_step(first_blk, width, ms):
        start = pl.multiple_of(first_blk * blk, blk)
        sts = [_mm_nt(k_ref[0, pl.ds(start, width), ksl], q) for q, ksl, _ in streams]
        ps = [jnp.exp2(st - ms[s]).astype(BF16) for s, st in enumerate(sts)]
        over = None
        for p in ps:
            d = jnp.max(p, axis=0, keepdims=True)
            over = d if over is None else jnp.maximum(over, d)
        pvs = [_mm(vt_ref[0, vsl, pl.ds(start, width)], ps[s]) for s, (_, _, vsl) in enumerate(streams)]

        def commit(c):
            for s in range(n):
                acc_ref[s] = acc_ref[s] + pvs[s]
            return c

        return lax.cond(jnp.max(over.astype(F32)) > 2.0 ** HEADROOM,
                        lambda c: step(first_blk, width, c, None, None), commit, ms)

    ms = tuple(jnp.full((1, blk), NEG, F32) for _ in range(n))
    ms = step(qi, blk, ms, diag_bias, diag_mask)
    if prev_bias is not None:
        ms = lax.cond(qi >= 1, lambda c: step(qi - 1, blk, c, prev_bias, None), lambda c: c, ms)
    n_plain = qi if prev_bias is None else jnp.maximum(qi - 1, 0)
    ms = lax.fori_loop(0, n_plain // WIDE_BLOCKS,
                       lambda j, c: fast_step(j * WIDE_BLOCKS, WIDE_BLOCKS * blk, c), ms)
    n_wide = (n_plain // WIDE_BLOCKS) * WIDE_BLOCKS
    lax.fori_loop(n_wide, n_plain, lambda j, c: fast_step(j, blk, c), ms)


def _fox_kernel(qa_ref, ka_ref, vt_ref, z_ref, o_ref, acc_ref, *, blk, nh):
    qi = pl.program_id(2)
    key = lax.broadcasted_iota(jnp.int32, (blk, blk), 0)
    qry = lax.broadcasted_iota(jnp.int32, (blk, blk), 1)
    hsl = [slice(hd * LANES, (hd + 1) * LANES) for hd in range(nh)]
    streams = [(qa_ref[0, :, sl], sl, slice(hd * FVT_ROWS, (hd + 1) * FVT_ROWS)) for hd, sl in enumerate(hsl)]
    _flash_t(ka_ref, vt_ref, acc_ref, streams, qi, blk, None, None, key <= qry)
    for pair in range(nh // 2):
        halves = []
        for par in range(2):
            acc = acc_ref[2 * pair + par]
            halves.append(acc[0:HD, :] / acc[HD:HD + 1, :])
        o = jnp.concatenate(halves, axis=0).T
        o_ref[0, :, hsl[pair]] = (o * _silu(z_ref[0, :, hsl[pair]].astype(F32))).astype(BF16)


def _fox_attention(qa, ka, vt, fz, *, blk, nh):
    b, t, _ = qa.shape
    return pl.pallas_call(
        functools.partial(_fox_kernel, blk=blk, nh=nh),
        grid=(b, FOX_HEADS // nh, t // blk),
        in_specs=[pl.BlockSpec((1, blk, nh * LANES), lambda b, p, i: (b, i, p)),
                  pl.BlockSpec((1, t, nh * LANES), lambda b, p, i: (b, 0, p)),
                  pl.BlockSpec((1, nh * FVT_ROWS, t), lambda b, p, i: (b, p, 0)),
                  pl.BlockSpec((1, blk, nh * HD), lambda b, p, i: (b, i, p))],
        out_specs=pl.BlockSpec((1, blk, nh * HD), lambda b, p, i: (b, i, p)),
        out_shape=jax.ShapeDtypeStruct((b, t, WIDTH), BF16),
        scratch_shapes=[pltpu.VMEM((nh, FVT_ROWS, blk), F32)],
        compiler_params=pltpu.CompilerParams(
            dimension_semantics=("parallel", "parallel", "arbitrary"), vmem_limit_bytes=VMEM_LIMIT),
        name="fox_prompt",
    )(qa, ka, vt, fz)


def _lambda(lq1_ref, lk1_ref, lq2_ref, lk2_ref):
    s1 = jnp.sum(lq1_ref[...] * lk1_ref[...], axis=-1, keepdims=True)
    s2 = jnp.sum(lq2_ref[...] * lk2_ref[...], axis=-1, keepdims=True)
    return jnp.exp(s1) - jnp.exp(s2) + LAMBDA_INIT


def _diff_finish(o0, o1, lam, subg, z):
    o = o0 - lam * o1
    ms = jnp.mean(o * o, axis=-1, keepdims=True)
    od = o * lax.rsqrt(ms + NORM_EPS) * subg * (1.0 - LAMBDA_INIT)
    return (od * _silu(z)).astype(BF16)


def _diff_kernel(q_ref, k_ref, vt_ref, bias_ref, z_ref, lq1_ref, lk1_ref, lq2_ref, lk2_ref, subg_ref,
                 o_ref, acc_ref, *, blk, nh):
    qi = pl.program_id(2)
    lane = lax.broadcasted_iota(jnp.int32, (blk, LANES), 1)
    streams, prev_bias, diag_bias = [], [], []
    for hd in range(nh):
        q = q_ref[0, :, hd * LANES:(hd + 1) * LANES]
        for mp in range(2):
            streams.append((jnp.where((lane < HD) if mp == 0 else (lane >= HD), q, jnp.zeros_like(q)),
                            slice(hd * LANES, (hd + 1) * LANES), slice(hd * VT_ROWS, (hd + 1) * VT_ROWS)))
            prev_bias.append(lambda hd=hd: bias_ref[hd, 0:blk, :])
            diag_bias.append(lambda hd=hd: bias_ref[hd, blk:2 * blk, :])
    _flash_t(k_ref, vt_ref, acc_ref, streams, qi, blk, prev_bias, diag_bias, None)
    lam = _lambda(lq1_ref, lk1_ref, lq2_ref, lk2_ref)
    for hd in range(nh):
        outs = []
        for mp in range(2):
            acc = acc_ref[2 * hd + mp]
            outs.append((acc[0:LANES, :] / acc[LANES:LANES + 1, :]).T)
        sl = slice(hd * LANES, (hd + 1) * LANES)
        o_ref[0, :, sl] = _diff_finish(outs[0], outs[1], lam, subg_ref[...], z_ref[0, :, sl].astype(F32))


def _diff_attention(dq, dk, dvt, bias, dz, lq1, lk1, lq2, lk2, subg, *, blk, nh):
    b, t, _ = dq.shape
    small = lambda n: pl.BlockSpec((1, n), lambda b, h, i: (0, 0))
    return pl.pallas_call(
        functools.partial(_diff_kernel, blk=blk, nh=nh),
        grid=(b, DIFF_HEADS // nh, t // blk),
        in_specs=[pl.BlockSpec((1, blk, nh * LANES), lambda b, h, i: (b, i, h)),
                  pl.BlockSpec((1, t, nh * LANES), lambda b, h, i: (b, 0, h)),
                  pl.BlockSpec((1, nh * VT_ROWS, t), lambda b, h, i: (b, h, 0)),
                  pl.BlockSpec((nh, 2 * blk, blk), lambda b, h, i: (h, 0, 0)),
                  pl.BlockSpec((1, blk, nh * LANES), lambda b, h, i: (b, i, h)),
                  small(HD), small(HD), small(HD), small(HD), small(2 * HD)],
        out_specs=pl.BlockSpec((1, blk, nh * LANES), lambda b, h, i: (b, i, h)),
        out_shape=jax.ShapeDtypeStruct((b, t, WIDTH), BF16),
        scratch_shapes=[pltpu.VMEM((2 * nh, VT_ROWS, blk), F32)],
        compiler_params=pltpu.CompilerParams(
            dimension_semantics=("parallel", "parallel", "arbitrary"), vmem_limit_bytes=VMEM_LIMIT),
        name="diff_prompt",
    )(dq, dk, dvt, bias, dz, lq1, lk1, lq2, lk2, subg)


def _suffix_sums(x):
    n = x.shape[1] // LANES
    lane = lax.broadcasted_iota(jnp.int32, (x.shape[0], LANES), 1)
    after = jnp.zeros((x.shape[0], 1), F32)
    pieces = [None] * n
    for blk in reversed(range(n)):
        piece = x[:, blk * LANES:(blk + 1) * LANES]
        y = piece
        shift = 1
        while shift < LANES:
            y = y + jnp.where(lane + shift < LANES, pltpu.roll(y, LANES - shift, 1), 0.0)
            shift *= 2
        pieces[blk] = y - piece + after
        after = after + y[:, 0:1]
    return jnp.concatenate(pieces, axis=1), after


def _pad_rows(x, rows):
    return jnp.concatenate([x, jnp.zeros((rows - x.shape[0], x.shape[1]), x.dtype)], axis=0)


def _joint_softmax_pv(s_p, s_n, vt_p, v_n):
    m = jnp.maximum(jnp.max(s_p, axis=-1, keepdims=True), jnp.max(s_n, axis=-1, keepdims=True))
    p_p = jnp.exp(s_p - m)
    p_n = jnp.exp(s_n - m)
    l = jnp.sum(p_p, axis=-1, keepdims=True) + jnp.sum(p_n, axis=-1, keepdims=True)
    return (_mm_nt(p_p.astype(BF16), vt_p) + _mm(p_n.astype(BF16), v_n)) / l


def _fox_sample_kernel(q_ref, kt_ref, vt_ref, kn_ref, vn_ref, lfp_ref, lfn_ref, z_ref, o_ref, *, nq):
    r_new, total_new = _suffix_sums(lfn_ref[0])
    r_past, _ = _suffix_sums(lfp_ref[0])
    r_past = r_past + total_new
    lane = lax.broadcasted_iota(jnp.int32, (nq, LANES), 1)
    causal = lane <= lax.broadcasted_iota(jnp.int32, (nq, LANES), 0)
    for blk in range(PAIRS):
        sl = slice(blk * LANES, (blk + 1) * LANES)
        q = q_ref[:, sl]
        kt = kt_ref[0, sl, :].astype(BF16)
        vt = vt_ref[0, sl, :].astype(BF16)
        kn = _pad_rows(kn_ref[:, sl], LANES).astype(BF16)
        vn = _pad_rows(vn_ref[:, sl], LANES).astype(BF16)
        outs = []
        for par in range(2):
            hd = 2 * blk + par
            qm = jnp.where((lane < HD) if par == 0 else (lane >= HD), q, jnp.zeros_like(q))
            s_p = _mm(qm, kt) + r_past[hd:hd + 1, :]
            s_n = jnp.where(causal, _mm_nt(qm, kn) + r_new[hd:hd + 1, :], NEG)
            outs.append(_joint_softmax_pv(s_p, s_n, vt, vn))
        o = jnp.where(lane < HD, outs[0], outs[1])
        o_ref[:, sl] = (o * _silu(z_ref[:, sl].astype(F32))).astype(BF16)


def _fox_sample(fq, cache_kt, cache_vt, fk, fv, lf_past_t, lf_new_t, fz, *, nq):
    b, _, past = cache_kt.shape
    new = lambda: pl.BlockSpec((nq, WIDTH), lambda i: (i, 0))
    cache = lambda: pl.BlockSpec((1, WIDTH, past), lambda i: (i, 0, 0))
    return pl.pallas_call(
        functools.partial(_fox_sample_kernel, nq=nq),
        grid=(b,),
        in_specs=[new(), cache(), cache(), new(), new(),
                  pl.BlockSpec((1, FOX_HEADS, past), lambda i: (i, 0, 0)),
                  pl.BlockSpec((1, FOX_HEADS, LANES), lambda i: (i, 0, 0)),
                  new()],
        out_specs=new(),
        out_shape=jax.ShapeDtypeStruct((b * nq, WIDTH), BF16),
        compiler_params=pltpu.CompilerParams(
            dimension_semantics=("parallel",), vmem_limit_bytes=VMEM_LIMIT),
        name="fox_sample",
    )(fq, cache_kt, cache_vt, fk, fv, lf_past_t, lf_new_t, fz)


def _diff_sample_kernel(q_ref, kt_ref, v_ref, kn_ref, vn_ref, bias_ref, z_ref,
                        lq1_ref, lk1_ref, lq2_ref, lk2_ref, subg_ref, o_ref, *, nq, past):
    lane = lax.broadcasted_iota(jnp.int32, (nq, LANES), 1)
    lam = _lambda(lq1_ref, lk1_ref, lq2_ref, lk2_ref)
    for hd in range(DIFF_HEADS):
        sl = slice(hd * LANES, (hd + 1) * LANES)
        q = q_ref[:, sl]
        kt = kt_ref[0, sl, :].astype(BF16)
        vt = v_ref[0, pl.ds(hd, past, stride=DIFF_HEADS), :].T.astype(BF16)
        kn = _pad_rows(kn_ref[:, sl], LANES).astype(BF16)
        vn = _pad_rows(vn_ref[:, sl], LANES).astype(BF16)
        outs = []
        for mp in range(2):
            qm = jnp.where((lane < HD) if mp == 0 else (lane >= HD), q, jnp.zeros_like(q))
            s_p = _mm(qm, kt) + bias_ref[hd, :, 0:past]
            s_n = _mm_nt(qm, kn) + bias_ref[hd, :, past:past + LANES]
            outs.append(_joint_softmax_pv(s_p, s_n, vt, vn))
        o_ref[:, sl] = _diff_finish(outs[0], outs[1], lam, subg_ref[...], z_ref[:, sl].astype(F32))


def _diff_sample(dq, cache_kt, cache_v, dk, dv, bias, dz, lq1, lk1, lq2, lk2, subg, *, nq):
    b, _, past = cache_kt.shape
    new = lambda: pl.BlockSpec((nq, WIDTH), lambda i: (i, 0))
    small = lambda n: pl.BlockSpec((1, n), lambda i: (0, 0))
    return pl.pallas_call(
        functools.partial(_diff_sample_kernel, nq=nq, past=past),
        grid=(b,),
        in_specs=[new(), pl.BlockSpec((1, WIDTH, past), lambda i: (i, 0, 0)),
                  pl.BlockSpec((1, past * DIFF_HEADS, 2 * HD), lambda i: (i, 0, 0)), new(), new(),
                  pl.BlockSpec((DIFF_HEADS, nq, past + LANES), lambda i: (0, 0, 0)),
                  new(), small(HD), small(HD), small(HD), small(HD), small(2 * HD)],
        out_specs=new(),
        out_shape=jax.ShapeDtypeStruct((b * nq, WIDTH), BF16),
        compiler_params=pltpu.CompilerParams(
            dimension_semantics=("parallel",), vmem_limit_bytes=VMEM_LIMIT),
        name="diff_sample",
    )(dq, cache_kt, cache_v, dk, dv, bias, dz, lq1, lk1, lq2, lk2, subg)


def _out_kernel(yf_ref, yd_ref, ga_ref, gb_ref, x_ref, wuf_ref, wud_ref, wo_ref, g_ref, o_ref):
    merged = (ga_ref[...].astype(F32) * _mm(yf_ref[...], wuf_ref[...])
              + gb_ref[...].astype(F32) * _mm(yd_ref[...], wud_ref[...]))
    out = x_ref[...] + _mm(merged.astype(BF16), wo_ref[...])
    ms = jnp.mean(out * out, axis=-1, keepdims=True)
    o_ref[...] = out * lax.rsqrt(ms + NORM_EPS) * g_ref[...]


def _output(yf, yd, ga, gb, x2d, wuf, wud, wo, g, *, tm):
    rows = x2d.shape[0]
    assert rows % tm == 0
    row_spec = lambda n: pl.BlockSpec((tm, n), lambda i: (i, 0))
    const = lambda shape: pl.BlockSpec(shape, lambda i: (0, 0))
    return pl.pallas_call(
        _out_kernel,
        grid=(rows // tm,),
        in_specs=[row_spec(WIDTH), row_spec(WIDTH), row_spec(D_MODEL), row_spec(D_MODEL), row_spec(D_MODEL),
                  const((WIDTH, D_MODEL)), const((WIDTH, D_MODEL)), const((D_MODEL, D_MODEL)),
                  const((1, D_MODEL))],
        out_specs=row_spec(D_MODEL),
        out_shape=jax.ShapeDtypeStruct((rows, D_MODEL), F32),
        compiler_params=pltpu.CompilerParams(
            dimension_semantics=("parallel",), vmem_limit_bytes=VMEM_LIMIT),
        name="out_proj",
    )(yf, yd, ga, gb, x2d, wuf, wud, wo, g)


def _pack_w_in(w_in, b_forget):
    sizes = (WIDTH, WIDTH, WIDTH, WIDTH, FOX_HEADS, WIDTH, WIDTH, WIDTH, WIDTH, D_MODEL, D_MODEL)
    offs = np.cumsum((0,) + sizes)
    fq, fk, fv, fz, fl, dq, dk, dv, dz, ga, gb = (w_in[:, offs[i]:offs[i + 1]] for i in range(len(sizes)))
    swap = np.arange(FOX_HEADS) ^ 1
    flw = jnp.repeat(fl[:, swap], HD, axis=1)
    fl8 = jnp.pad(fl, ((0, 0), (0, LANES - FOX_HEADS)))
    w_all = jnp.concatenate([fq, fk, fv, fz, dq, dk, dv, dz, ga, gb, flw, fl8], axis=1).astype(BF16)
    bw = jnp.repeat(b_forget[swap], HD)[None, :]
    return w_all, bw, jnp.pad(b_forget, (0, LANES - FOX_HEADS))[None, :]


def kernel(x_prompt, x_sample, cache_fox_k, cache_fox_v, cache_fox_logf, cache_diff_k, cache_diff_v,
           norm_in_g, w_in, b_forget, lambda_q1, lambda_k1, lambda_q2, lambda_k2, subln_g,
           w_up_fox, w_up_diff, w_o, rel_bias, final_norm_g):
    bp, tp, _ = x_prompt.shape
    bs, ts, _ = x_sample.shape
    past = cache_fox_k.shape[2]
    blk = 256
    tm = 512

    w_all, bw, b8 = _pack_w_in(w_in[0], b_forget[0])
    g_in = norm_in_g[0][None, :]
    g_out = final_norm_g[None, :]
    wuf, wud, wo = w_up_fox[0].astype(BF16), w_up_diff[0].astype(BF16), w_o[0].astype(BF16)
    lq1, lk1, lq2, lk2 = (a[0][None, :] for a in (lambda_q1, lambda_k1, lambda_q2, lambda_k2))
    subg = subln_g[0][None, :]

    xp = x_prompt.reshape(bp * tp, D_MODEL)
    (fkt, fvt, lft, dkt, dv4, qa, ka, vtf, fz, dq, dkb, vtd, dz, ga, gb) = _project(
        xp, g_in, w_all, bw, b8, seq_len=tp, aug=True, tm=tm)
    r3 = lambda a: a.reshape(bp, tp, a.shape[-1])
    yf = _fox_attention(r3(qa), r3(ka), vtf, r3(fz), blk=blk, nh=FOX_STEP_HEADS)
    bias_p = _bias_tiles(rel_bias, q_start=blk, nq=blk, segments=((0, blk, blk), (blk, blk, blk)),
                         transposed=True, scale=LOG2E)
    yd = _diff_attention(r3(dq), r3(dkb), vtd, bias_p, r3(dz), lq1, lk1, lq2, lk2, subg, blk=blk,
                         nh=DIFF_STEP_HEADS)
    y_p = _output(yf.reshape(bp * tp, WIDTH), yd.reshape(bp * tp, WIDTH), ga, gb, xp, wuf, wud, wo, g_out, tm=tm)
    fox_k_p = jnp.transpose(fkt.reshape(bp, FOX_HEADS, HD, tp), (0, 3, 1, 2))[None]
    fox_v_p = jnp.transpose(fvt.reshape(bp, FOX_HEADS, HD, tp), (0, 3, 1, 2))[None]
    fox_lf_p = jnp.transpose(lft, (0, 2, 1))[None]
    diff_k_p = jnp.transpose(dkt.reshape(bp, DIFF_HEADS, 2, HD, tp), (0, 4, 1, 2, 3))[None]
    diff_v_p = dv4.reshape(1, bp, tp, DIFF_HEADS, 2 * HD)

    xs = x_sample.reshape(bs * ts, D_MODEL)
    (fk_s, fv_s, lf_s, dk_s, dv_s, fq_s, fz_s, dq_s, dz_s, ga_s, gb_s) = _project(
        xs, g_in, w_all, bw, b8, seq_len=ts, aug=False, tm=tm)
    fox_kt = jnp.transpose(cache_fox_k[0], (0, 2, 3, 1)).reshape(bs, WIDTH, past)
    fox_vt = jnp.transpose(cache_fox_v[0], (0, 2, 3, 1)).reshape(bs, WIDTH, past)
    diff_kt = jnp.transpose(cache_diff_k[0], (0, 2, 3, 4, 1)).reshape(bs, WIDTH, past)
    diff_v = cache_diff_v[0].reshape(bs, past * DIFF_HEADS, 2 * HD)
    lf_past_t = jnp.swapaxes(cache_fox_logf[0], 1, 2)
    lf_new_t = jnp.pad(jnp.swapaxes(lf_s.reshape(bs, ts, FOX_HEADS), 1, 2), ((0, 0), (0, 0), (0, LANES - ts)))
    yf_s = _fox_sample(fq_s, fox_kt, fox_vt, fk_s, fv_s, lf_past_t, lf_new_t, fz_s, nq=ts)
    bias_s = _bias_tiles(rel_bias, q_start=past, nq=ts, segments=((0, past, past), (past, LANES, ts)),
                         transposed=False, scale=1.0)
    yd_s = _diff_sample(dq_s, diff_kt, diff_v, dk_s, dv_s, bias_s, dz_s, lq1, lk1, lq2, lk2, subg, nq=ts)
    y_s = _output(yf_s, yd_s, ga_s, gb_s, xs, wuf, wud, wo, g_out, tm=tm)

    return (y_p.reshape(bp, tp, D_MODEL), y_s.reshape(bs, ts, D_MODEL),
            fox_k_p, fox_v_p, fox_lf_p, diff_k_p, diff_v_p,
            fk_s.reshape(1, bs, ts, FOX_HEADS, HD), fv_s.reshape(1, bs, ts, FOX_HEADS, HD),
            lf_s.reshape(1, bs, ts, FOX_HEADS),
            dk_s.reshape(1, bs, ts, DIFF_HEADS, 2, HD), dv_s.reshape(1, bs, ts, DIFF_HEADS, 2 * HD))
```

```python
import functools
import math

import numpy as np
import jax
import jax.numpy as jnp
from jax import lax
from jax.experimental import pallas as pl
from jax.experimental.pallas import tpu as pltpu

F32 = jnp.float32
BF16 = jnp.bfloat16

D_MODEL = 1024
HD = 64
FOX_HEADS = 8
DIFF_HEADS = 4
WIDTH = 512
CHUNK = 64
CHUNK_SHIFT = 6
N_BUCKETS = 32
MAX_DISTANCE = 128
NORM_EPS = 1e-6
SCALE = HD ** -0.5
NEG = -1e30
LAMBDA_INIT = 0.8 - 0.6 * math.exp(-0.3 * 0)
LOG2E = math.log2(math.e)

LANES = 128
PAIRS = WIDTH // LANES
VT_PAD = 16
VT_ROWS = LANES + VT_PAD
FVT_ROWS = HD + VT_PAD

_O_FQ, _O_FK, _O_FV, _O_FZ = 0, 512, 1024, 1536
_O_DQ, _O_DK, _O_DV, _O_DZ = 2048, 2560, 3072, 3584
_O_GA, _O_GB = 4096, 5120
_O_FLW = 6144
_O_FL8 = 6656
_N_ALL = 6784

VMEM_LIMIT = 56 * 1024 * 1024

FOX_STEP_HEADS = 8
DIFF_STEP_HEADS = 4
HEADROOM = 64.0
WIDE_BLOCKS = 2


def _t5_thresholds():
    nb = N_BUCKETS // 2
    max_exact = nb // 2
    n = np.arange(0, 4 * MAX_DISTANCE)
    large = max_exact + (np.log(np.maximum(n, 1).astype(np.float32) / max_exact)
                         / math.log(MAX_DISTANCE / max_exact) * (nb - max_exact)).astype(np.int32)
    large = np.minimum(large, nb - 1)
    thr = [int(np.argmax(large >= b)) for b in range(max_exact + 1, nb)]
    return max_exact, tuple(thr)


_MAX_EXACT, _T5_THR = _t5_thresholds()
FAR_DIST = _T5_THR[-1]


def _mm(a, b):
    return jnp.dot(a, b, preferred_element_type=F32)


def _mm_nt(a, b):
    return lax.dot_general(a, b, (((1,), (1,)), ((), ())), preferred_element_type=F32)


def _log_sigmoid(x):
    return -(jnp.maximum(-x, 0.0) + jnp.log1p(jnp.exp(-jnp.abs(x))))


def _sigmoid(x):
    return 1.0 / (1.0 + jnp.exp(-x))


def _silu(x):
    return x * _sigmoid(x)


def _split3(x):
    p1 = x.astype(BF16).astype(F32)
    r1 = x - p1
    p2 = r1.astype(BF16).astype(F32)
    p3 = r1 - p2
    return p1, p2, p3


def _proj_kernel(x_ref, g_ref, w_ref, bw_ref, b8_ref, *refs, tm, seq_len, aug):
    if aug:
        (fkt_ref, fvt_ref, lft_ref, dkt_ref, dv4_ref, qa_ref, ka_ref, vtf_ref,
         fz_ref, dq_ref, dkb_ref, vtd_ref, dz_ref, ga_ref, gb_ref, carry_ref) = refs
    else:
        (fk_ref, fv_ref, lf_ref, dk_ref, dv_ref, fq_ref,
         fz_ref, dq_ref, dz_ref, ga_ref, gb_ref) = refs

    x = x_ref[...]
    ms = jnp.mean(x * x, axis=-1, keepdims=True)
    h = (x * lax.rsqrt(ms + NORM_EPS) * g_ref[...]).astype(BF16)

    def grp(off, n=WIDTH):
        return _mm(h, w_ref[:, off:off + n])

    if not aug:
        fq_ref[...] = (grp(_O_FQ) * SCALE).astype(BF16)
        fk_ref[...] = grp(_O_FK)
        fv_ref[...] = grp(_O_FV)
        fz_ref[...] = grp(_O_FZ).astype(BF16)
        dq_ref[...] = (grp(_O_DQ) * SCALE).astype(BF16)
        dk_ref[...] = grp(_O_DK)
        dv_ref[...] = grp(_O_DV)
        dz_ref[...] = grp(_O_DZ).astype(BF16)
        ga_ref[...] = _sigmoid(grp(_O_GA, D_MODEL)).astype(BF16)
        gb_ref[...] = _sigmoid(grp(_O_GB, D_MODEL)).astype(BF16)
        lf_ref[...] = _log_sigmoid(grp(_O_FL8, LANES) + b8_ref[...])[:, :FOX_HEADS]
        return

    lfw = _log_sigmoid(grp(_O_FLW) + bw_ref[...])
    row = lax.broadcasted_iota(jnp.int32, (tm, tm), 0)
    col = lax.broadcasted_iota(jnp.int32, (tm, tm), 1)
    tri = jnp.where(col <= row, 1.0, 0.0).astype(BF16)
    c = sum(_mm(tri, p.astype(BF16)) for p in _split3(lfw))
    tiles_per_seq = seq_len // tm

    @pl.when(pl.program_id(0) % tiles_per_seq == 0)
    def _():
        carry_ref[...] = jnp.zeros_like(carry_ref)

    c = c + carry_ref[...]
    carry_ref[...] = c[tm - 1:tm, :]
    c_hi, c_mid, c_lo = _split3(c * LOG2E)

    fq = grp(_O_FQ) * (SCALE * LOG2E)
    fk = grp(_O_FK)
    fv = grp(_O_FV)
    fvt = fv.T
    fkt_ref[0] = fk.T
    fvt_ref[0] = fvt
    tail = jnp.where(lax.broadcasted_iota(jnp.int32, (VT_PAD, tm), 0) == 0, 1.0, 0.0).astype(BF16)
    lane = lax.broadcasted_iota(jnp.int32, (tm, LANES), 1)
    for hd in range(FOX_HEADS):
        blk, par = divmod(hd, 2)
        sl = slice(blk * LANES, (blk + 1) * LANES)
        data = (lane < HD) if par == 0 else (lane >= HD)
        a = lane - (1 - par) * HD
        hi, mid, lo = c_hi[:, sl], c_mid[:, sl], c_lo[:, sl]
        q_aug = jnp.where(a == 0, hi, jnp.where(a == 1, mid, jnp.where(
            a == 2, lo, jnp.where((a >= 3) & (a < 6), 1.0, 0.0))))
        k_aug = jnp.where((a >= 0) & (a < 3), 1.0, jnp.where(a == 3, -hi, jnp.where(
            a == 4, -mid, jnp.where(a == 5, -lo, 0.0))))
        osl = slice(hd * LANES, (hd + 1) * LANES)
        qa_ref[:, osl] = jnp.where(data, fq[:, sl], q_aug).astype(BF16)
        ka_ref[:, osl] = jnp.where(data, fk[:, sl], k_aug).astype(BF16)
        base = hd * FVT_ROWS
        vtf_ref[0, base:base + HD, :] = fvt[hd * HD:(hd + 1) * HD, :].astype(BF16)
        vtf_ref[0, base + HD:base + FVT_ROWS, :] = tail

    ga_ref[...] = _sigmoid(grp(_O_GA, D_MODEL)).astype(BF16)
    gb_ref[...] = _sigmoid(grp(_O_GB, D_MODEL)).astype(BF16)

    dk = grp(_O_DK)
    dkt_ref[0] = dk.T
    dkb_ref[...] = dk.astype(BF16)
    dv = grp(_O_DV)
    dv4_ref[...] = dv.reshape(tm, DIFF_HEADS, 2 * HD)
    for hd in range(DIFF_HEADS):
        base = hd * VT_ROWS
        vtd_ref[0, base:base + LANES, :] = dv[:, hd * LANES:(hd + 1) * LANES].T.astype(BF16)
        vtd_ref[0, base + LANES:base + VT_ROWS, :] = tail

    lft_ref[0] = _log_sigmoid(grp(_O_FL8, LANES) + b8_ref[...]).T[:FOX_HEADS, :]
    dq_ref[...] = (grp(_O_DQ) * (SCALE * LOG2E)).astype(BF16)
    fz_ref[...] = grp(_O_FZ).astype(BF16)
    dz_ref[...] = grp(_O_DZ).astype(BF16)


def _project(x2d, g, w_all, bw, b8, *, seq_len, aug, tm):
    rows = x2d.shape[0]
    assert rows % tm == 0
    row_spec = lambda n: pl.BlockSpec((tm, n), lambda i: (i, 0))
    const = lambda shape: pl.BlockSpec(shape, lambda i: (0, 0))
    f32o = lambda n: jax.ShapeDtypeStruct((rows, n), F32)
    b16o = lambda n: jax.ShapeDtypeStruct((rows, n), BF16)
    if aug:
        assert seq_len % tm == 0
        wide = FOX_HEADS * LANES
        tps = seq_len // tm
        t_shape = lambda n, dt: jax.ShapeDtypeStruct((rows // seq_len, n, seq_len), dt)
        t_spec = lambda n: pl.BlockSpec((1, n, tm), lambda i: (i // tps, 0, i % tps))
        out_shape = [t_shape(WIDTH, F32), t_shape(WIDTH, F32), t_shape(FOX_HEADS, F32), t_shape(WIDTH, F32),
                     jax.ShapeDtypeStruct((rows, DIFF_HEADS, 2 * HD), F32)]
        out_specs = [t_spec(WIDTH), t_spec(WIDTH), t_spec(FOX_HEADS), t_spec(WIDTH),
                     pl.BlockSpec((tm, DIFF_HEADS, 2 * HD), lambda i: (i, 0, 0))]
        out_shape += ([b16o(wide)] * 2 + [t_shape(FOX_HEADS * FVT_ROWS, BF16)] + [b16o(WIDTH)] * 3
                      + [t_shape(DIFF_HEADS * VT_ROWS, BF16)] + [b16o(WIDTH)] + [b16o(D_MODEL)] * 2)
        out_specs += ([row_spec(wide)] * 2 + [t_spec(FOX_HEADS * FVT_ROWS)] + [row_spec(WIDTH)] * 3
                      + [t_spec(DIFF_HEADS * VT_ROWS)] + [row_spec(WIDTH)] + [row_spec(D_MODEL)] * 2)
        scratch = [pltpu.VMEM((1, WIDTH), F32)]
    else:
        out_shape = [f32o(WIDTH), f32o(WIDTH), f32o(FOX_HEADS), f32o(WIDTH), f32o(WIDTH)]
        out_specs = [row_spec(WIDTH), row_spec(WIDTH), row_spec(FOX_HEADS), row_spec(WIDTH), row_spec(WIDTH)]
        out_shape += [b16o(WIDTH)] * 4 + [b16o(D_MODEL)] * 2
        out_specs += [row_spec(WIDTH)] * 4 + [row_spec(D_MODEL)] * 2
        scratch = []
    return pl.pallas_call(
        functools.partial(_proj_kernel, tm=tm, seq_len=seq_len, aug=aug),
        grid=(rows // tm,),
        in_specs=[row_spec(D_MODEL), const((1, D_MODEL)),
                  pl.BlockSpec((D_MODEL, _N_ALL), lambda i: (0, 0), pipeline_mode=pl.Buffered(1)),
                  const((1, WIDTH)), const((1, LANES))],
        out_specs=out_specs,
        out_shape=out_shape,
        scratch_shapes=scratch,
        compiler_params=pltpu.CompilerParams(
            dimension_semantics=("arbitrary",), vmem_limit_bytes=VMEM_LIMIT),
        name="proj_aug" if aug else "proj_plain",
    )(x2d, g, w_all, bw, b8)


def _bias_kernel(tbl_ref, o_ref, *, q_start, nq, segments, transposed, scale):
    hd = pl.program_id(0)
    far = tbl_ref[N_BUCKETS // 2 - 1, hd]
    q_axis, k_axis = (1, 0) if transposed else (0, 1)
    off = 0
    for k_start, nk, valid in segments:
        shape = (nk, nq) if transposed else (nq, nk)
        qpos = q_start + lax.broadcasted_iota(jnp.int32, shape, q_axis)
        kidx = lax.broadcasted_iota(jnp.int32, shape, k_axis)
        kpos = k_start + kidx
        rel = kpos - qpos
        n = jnp.abs(rel)
        large = jnp.full(shape, _MAX_EXACT, jnp.int32)
        for thr in _T5_THR:
            large = large + jnp.where(n >= thr, 1, 0)
        bucket = jnp.where(rel > 0, N_BUCKETS // 2, 0) + jnp.where(n < _MAX_EXACT, n, large)
        val = jnp.zeros(shape, F32)
        for b in range(N_BUCKETS):
            val = jnp.where(bucket == b, tbl_ref[b, hd], val)
        val = (val - far) * scale
        visible = ((kpos >> CHUNK_SHIFT) <= (qpos >> CHUNK_SHIFT)) & (kidx < valid)
        val = jnp.where(visible, val, NEG)
        if transposed:
            o_ref[0, off:off + nk, :] = val
        else:
            o_ref[0, :, off:off + nk] = val
        off += nk


def _bias_tiles(rel_bias, *, q_start, nq, segments, transposed, scale):
    total = sum(s[1] for s in segments)
    shape = (total, nq) if transposed else (nq, total)
    return pl.pallas_call(
        functools.partial(_bias_kernel, q_start=q_start, nq=nq, segments=segments, transposed=transposed,
                          scale=scale),
        grid=(DIFF_HEADS,),
        in_specs=[pl.BlockSpec(memory_space=pltpu.SMEM)],
        out_specs=pl.BlockSpec((1,) + shape, lambda h: (h, 0, 0)),
        out_shape=jax.ShapeDtypeStruct((DIFF_HEADS,) + shape, F32),
        name="t5_bias",
    )(rel_bias)


def _flash_t(k_ref, vt_ref, acc_ref, bad_ref, streams, qi, blk, prev_bias, diag_bias, diag_mask):
    n = len(streams)

    def scores(first_blk, width):
        start = pl.multiple_of(first_blk * blk, blk)
        return [_mm_nt(k_ref[0, pl.ds(start, width), ksl], q) for q, ksl, _ in streams]

    def step(first_blk, width, ms, biases, mask):
        start = pl.multiple_of(first_blk * blk, blk)
        sts = scores(first_blk, width)
        out, ps, alphas = [], [], []
        for s, st in enumerate(sts):
            if biases is not None:
                st = st + biases[s]()
            if mask is not None:
                st = jnp.where(mask, st, NEG)
            m_new = jnp.maximum(ms[s], jnp.max(st, axis=0, keepdims=True))
            ps.append(jnp.exp2(st - m_new).astype(BF16))
            alphas.append(jnp.exp2(ms[s] - m_new))
            out.append(m_new)
        for s, (_, _, vsl) in enumerate(streams):
            acc_ref[s] = alphas[s] * acc_ref[s] + _mm(vt_ref[0, vsl, pl.ds(start, width)], ps[s])
        return tuple(out)

    def first_step():
        acc_ref[...] = jnp.zeros_like(acc_ref)
        return step(qi, blk, tuple(jnp.full((1, blk), NEG, F32) for _ in range(n)), diag_bias, diag_mask)

    ms = first_step()
    bad_ref[...] = jnp.zeros_like(bad_ref)

    def consume(sts, first_blk, width):
        start = pl.multiple_of(first_blk * blk, blk)
        ps = [jnp.exp2(st - ms[s]).astype(BF16) for s, st in enumerate(sts)]
        worst = None
        for p in ps:
            d = jnp.max(p, axis=0, keepdims=True)
            worst = d if worst is None else jnp.maximum(worst, d)
        bad_ref[...] = jnp.maximum(bad_ref[...], worst.astype(F32))
        for s, (_, _, vsl) in enumerate(streams):
            acc_ref[s] = acc_ref[s] + _mm(vt_ref[0, vsl, pl.ds(start, width)], ps[s])

    if prev_bias is not None:
        @pl.when(qi >= 1)
        def _():
            consume([st + prev_bias[s]() for s, st in enumerate(scores(qi - 1, blk))], qi - 1, blk)

    n_plain = qi if prev_bias is None else jnp.maximum(qi - 1, 0)
    n_wide = n_plain // WIDE_BLOCKS

    def wide_step(j, carry):
        consume(scores(j * WIDE_BLOCKS, WIDE_BLOCKS * blk), j * WIDE_BLOCKS, WIDE_BLOCKS * blk)
        return carry

    lax.fori_loop(0, n_wide, wide_step, 0)

    def single_step(j, carry):
        consume(scores(j, blk), j, blk)
        return carry

    lax.fori_loop(n_wide * WIDE_BLOCKS, n_plain, single_step, 0)

    @pl.when(jnp.max(bad_ref[...]) > 2.0 ** HEADROOM)
    def _():
        c = first_step()
        if prev_bias is not None:
            c = lax.cond(qi >= 1, lambda c: step(qi - 1, blk, c, prev_bias, None), lambda c: c, c)
        lax.fori_loop(0, n_plain, lambda j, c: step(j, blk, c, None, None), c)


def _fox_kernel(qa_ref, ka_ref, vt_ref, z_ref, o_ref, acc_ref, bad_ref, *, blk, nh):
    qi = pl.program_id(2)
    key = lax.broadcasted_iota(jnp.int32, (blk, blk), 0)
    qry = lax.broadcasted_iota(jnp.int32, (blk, blk), 1)
    hsl = [slice(hd * LANES, (hd + 1) * LANES) for hd in range(nh)]
    streams = [(qa_ref[0, :, sl], sl, slice(hd * FVT_ROWS, (hd + 1) * FVT_ROWS)) for hd, sl in enumerate(hsl)]
    _flash_t(ka_ref, vt_ref, acc_ref, bad_ref, streams, qi, blk, None, None, key <= qry)
    for pair in range(nh // 2):
        halves = []
        for par in range(2):
            acc = acc_ref[2 * pair + par]
            halves.append(acc[0:HD, :] / acc[HD:HD + 1, :])
        o = jnp.concatenate(halves, axis=0).T
        o_ref[0, :, hsl[pair]] = (o * _silu(z_ref[0, :, hsl[pair]].astype(F32))).astype(BF16)


def _fox_attention(qa, ka, vt, fz, *, blk, nh):
    b, t, _ = qa.shape
    return pl.pallas_call(
        functools.partial(_fox_kernel, blk=blk, nh=nh),
        grid=(b, FOX_HEADS // nh, t // blk),
        in_specs=[pl.BlockSpec((1, blk, nh * LANES), lambda b, p, i: (b, i, p)),
                  pl.BlockSpec((1, t, nh * LANES), lambda b, p, i: (b, 0, p)),
                  pl.BlockSpec((1, nh * FVT_ROWS, t), lambda b, p, i: (b, p, 0)),
                  pl.BlockSpec((1, blk, nh * HD), lambda b, p, i: (b, i, p))],
        out_specs=pl.BlockSpec((1, blk, nh * HD), lambda b, p, i: (b, i, p)),
        out_shape=jax.ShapeDtypeStruct((b, t, WIDTH), BF16),
        scratch_shapes=[pltpu.VMEM((nh, FVT_ROWS, blk), F32), pltpu.VMEM((1, blk), F32)],
        compiler_params=pltpu.CompilerParams(
            dimension_semantics=("parallel", "parallel", "arbitrary"), vmem_limit_bytes=VMEM_LIMIT),
        name="fox_prompt",
    )(qa, ka, vt, fz)


def _lambda(lq1_ref, lk1_ref, lq2_ref, lk2_ref):
    s1 = jnp.sum(lq1_ref[...] * lk1_ref[...], axis=-1, keepdims=True)
    s2 = jnp.sum(lq2_ref[...] * lk2_ref[...], axis=-1, keepdims=True)
    return jnp.exp(s1) - jnp.exp(s2) + LAMBDA_INIT


def _diff_finish(o0, o1, lam, subg, z):
    o = o0 - lam * o1
    ms = jnp.mean(o * o, axis=-1, keepdims=True)
    od = o * lax.rsqrt(ms + NORM_EPS) * subg * (1.0 - LAMBDA_INIT)
    return (od * _silu(z)).astype(BF16)


def _diff_kernel(q_ref, k_ref, vt_ref, bias_ref, z_ref, lq1_ref, lk1_ref, lq2_ref, lk2_ref, subg_ref,
                 o_ref, acc_ref, bad_ref, *, blk, nh):
    qi = pl.program_id(2)
    lane = lax.broadcasted_iota(jnp.int32, (blk, LANES), 1)
    streams, prev_bias, diag_bias = [], [], []
    for hd in range(nh):
        q = q_ref[0, :, hd * LANES:(hd + 1) * LANES]
        for mp in range(2):
            streams.append((jnp.where((lane < HD) if mp == 0 else (lane >= HD), q, jnp.zeros_like(q)),
                            slice(hd * LANES, (hd + 1) * LANES), slice(hd * VT_ROWS, (hd + 1) * VT_ROWS)))
            prev_bias.append(lambda hd=hd: bias_ref[hd, 0:blk, :])
            diag_bias.append(lambda hd=hd: bias_ref[hd, blk:2 * blk, :])
    _flash_t(k_ref, vt_ref, acc_ref, bad_ref, streams, qi, blk, prev_bias, diag_bias, None)
    lam = _lambda(lq1_ref, lk1_ref, lq2_ref, lk2_ref)
    for hd in range(nh):
        outs = []
        for mp in range(2):
            acc = acc_ref[2 * hd + mp]
            outs.append((acc[0:LANES, :] / acc[LANES:LANES + 1, :]).T)
        sl = slice(hd * LANES, (hd + 1) * LANES)
        o_ref[0, :, sl] = _diff_finish(outs[0], outs[1], lam, subg_ref[...], z_ref[0, :, sl].astype(F32))


def _diff_attention(dq, dk, dvt, bias, dz, lq1, lk1, lq2, lk2, subg, *, blk, nh):
    b, t, _ = dq.shape
    small = lambda n: pl.BlockSpec((1, n), lambda b, h, i: (0, 0))
    return pl.pallas_call(
        functools.partial(_diff_kernel, blk=blk, nh=nh),
        grid=(b, DIFF_HEADS // nh, t // blk),
        in_specs=[pl.BlockSpec((1, blk, nh * LANES), lambda b, h, i: (b, i, h)),
                  pl.BlockSpec((1, t, nh * LANES), lambda b, h, i: (b, 0, h)),
                  pl.BlockSpec((1, nh * VT_ROWS, t), lambda b, h, i: (b, h, 0)),
                  pl.BlockSpec((nh, 2 * blk, blk), lambda b, h, i: (h, 0, 0)),
                  pl.BlockSpec((1, blk, nh * LANES), lambda b, h, i: (b, i, h)),
                  small(HD), small(HD), small(HD), small(HD), small(2 * HD)],
        out_specs=pl.BlockSpec((1, blk, nh * LANES), lambda b, h, i: (b, i, h)),
        out_shape=jax.ShapeDtypeStruct((b, t, WIDTH), BF16),
        scratch_shapes=[pltpu.VMEM((2 * nh, VT_ROWS, blk), F32), pltpu.VMEM((1, blk), F32)],
        compiler_params=pltpu.CompilerParams(
            dimension_semantics=("parallel", "parallel", "arbitrary"), vmem_limit_bytes=VMEM_LIMIT),
        name="diff_prompt",
    )(dq, dk, dvt, bias, dz, lq1, lk1, lq2, lk2, subg)


def _suffix_sums(x):
    n = x.shape[1] // LANES
    lane = lax.broadcasted_iota(jnp.int32, (x.shape[0], LANES), 1)
    after = jnp.zeros((x.shape[0], 1), F32)
    pieces = [None] * n
    for blk in reversed(range(n)):
        piece = x[:, blk * LANES:(blk + 1) * LANES]
        y = piece
        shift = 1
        while shift < LANES:
            y = y + jnp.where(lane + shift < LANES, pltpu.roll(y, LANES - shift, 1), 0.0)
            shift *= 2
        pieces[blk] = y - piece + after
        after = after + y[:, 0:1]
    return jnp.concatenate(pieces, axis=1), after


def _pad_rows(x, rows):
    return jnp.concatenate([x, jnp.zeros((rows - x.shape[0], x.shape[1]), x.dtype)], axis=0)


def _joint_softmax_pv(s_p, s_n, vt_p, v_n):
    m = jnp.maximum(jnp.max(s_p, axis=-1, keepdims=True), jnp.max(s_n, axis=-1, keepdims=True))
    p_p = jnp.exp(s_p - m)
    p_n = jnp.exp(s_n - m)
    l = jnp.sum(p_p, axis=-1, keepdims=True) + jnp.sum(p_n, axis=-1, keepdims=True)
    return (_mm_nt(p_p.astype(BF16), vt_p) + _mm(p_n.astype(BF16), v_n)) / l


def _fox_sample_kernel(q_ref, kt_ref, vt_ref, kn_ref, vn_ref, lfp_ref, lfn_ref, z_ref, o_ref, *, nq):
    r_new, total_new = _suffix_sums(lfn_ref[0])
    r_past, _ = _suffix_sums(lfp_ref[0])
    r_past = r_past + total_new
    lane = lax.broadcasted_iota(jnp.int32, (nq, LANES), 1)
    causal = lane <= lax.broadcasted_iota(jnp.int32, (nq, LANES), 0)
    for blk in range(PAIRS):
        sl = slice(blk * LANES, (blk + 1) * LANES)
        q = q_ref[:, sl]
        kt = kt_ref[0, sl, :].astype(BF16)
        vt = vt_ref[0, sl, :].astype(BF16)
        kn = _pad_rows(kn_ref[:, sl], LANES).astype(BF16)
        vn = _pad_rows(vn_ref[:, sl], LANES).astype(BF16)
        outs = []
        for par in range(2):
            hd = 2 * blk + par
            qm = jnp.where((lane < HD) if par == 0 else (lane >= HD), q, jnp.zeros_like(q))
            s_p = _mm(qm, kt) + r_past[hd:hd + 1, :]
            s_n = jnp.where(causal, _mm_nt(qm, kn) + r_new[hd:hd + 1, :], NEG)
            outs.append(_joint_softmax_pv(s_p, s_n, vt, vn))
        o = jnp.where(lane < HD, outs[0], outs[1])
        o_ref[:, sl] = (o * _silu(z_ref[:, sl].astype(F32))).astype(BF16)


def _fox_sample(fq, cache_kt, cache_vt, fk, fv, lf_past_t, lf_new_t, fz, *, nq):
    b, _, past = cache_kt.shape
    new = lambda: pl.BlockSpec((nq, WIDTH), lambda i: (i, 0))
    cache = lambda: pl.BlockSpec((1, WIDTH, past), lambda i: (i, 0, 0))
    return pl.pallas_call(
        functools.partial(_fox_sample_kernel, nq=nq),
        grid=(b,),
        in_specs=[new(), cache(), cache(), new(), new(),
                  pl.BlockSpec((1, FOX_HEADS, past), lambda i: (i, 0, 0)),
                  pl.BlockSpec((1, FOX_HEADS, LANES), lambda i: (i, 0, 0)),
                  new()],
        out_specs=new(),
        out_shape=jax.ShapeDtypeStruct((b * nq, WIDTH), BF16),
        compiler_params=pltpu.CompilerParams(
            dimension_semantics=("parallel",), vmem_limit_bytes=VMEM_LIMIT),
        name="fox_sample",
    )(fq, cache_kt, cache_vt, fk, fv, lf_past_t, lf_new_t, fz)


def _diff_sample_kernel(q_ref, kt_ref, v_ref, kn_ref, vn_ref, bias_ref, z_ref,
                        lq1_ref, lk1_ref, lq2_ref, lk2_ref, subg_ref, o_ref, *, nq, past):
    lane = lax.broadcasted_iota(jnp.int32, (nq, LANES), 1)
    lam = _lambda(lq1_ref, lk1_ref, lq2_ref, lk2_ref)
    for hd in range(DIFF_HEADS):
        sl = slice(hd * LANES, (hd + 1) * LANES)
        q = q_ref[:, sl]
        kt = kt_ref[0, sl, :].astype(BF16)
        vt = v_ref[0, pl.ds(hd, past, stride=DIFF_HEADS), :].T.astype(BF16)
        kn = _pad_rows(kn_ref[:, sl], LANES).astype(BF16)
        vn = _pad_rows(vn_ref[:, sl], LANES).astype(BF16)
        outs = []
        for mp in range(2):
            qm = jnp.where((lane < HD) if mp == 0 else (lane >= HD), q, jnp.zeros_like(q))
            s_p = _mm(qm, kt) + bias_ref[hd, :, 0:past]
            s_n = _mm_nt(qm, kn) + bias_ref[hd, :, past:past + LANES]
            outs.append(_joint_softmax_pv(s_p, s_n, vt, vn))
        o_ref[:, sl] = _diff_finish(outs[0], outs[1], lam, subg_ref[...], z_ref[:, sl].astype(F32))


def _diff_sample(dq, cache_kt, cache_v, dk, dv, bias, dz, lq1, lk1, lq2, lk2, subg, *, nq):
    b, _, past = cache_kt.shape
    new = lambda: pl.BlockSpec((nq, WIDTH), lambda i: (i, 0))
    small = lambda n: pl.BlockSpec((1, n), lambda i: (0, 0))
    return pl.pallas_call(
        functools.partial(_diff_sample_kernel, nq=nq, past=past),
        grid=(b,),
        in_specs=[new(), pl.BlockSpec((1, WIDTH, past), lambda i: (i, 0, 0)),
                  pl.BlockSpec((1, past * DIFF_HEADS, 2 * HD), lambda i: (i, 0, 0)), new(), new(),
                  pl.BlockSpec((DIFF_HEADS, nq, past + LANES), lambda i: (0, 0, 0)),
                  new(), small(HD), small(HD), small(HD), small(HD), small(2 * HD)],
        out_specs=new(),
        out_shape=jax.ShapeDtypeStruct((b * nq, WIDTH), BF16),
        compiler_params=pltpu.CompilerParams(
            dimension_semantics=("parallel",), vmem_limit_bytes=VMEM_LIMIT),
        name="diff_sample",
    )(dq, cache_kt, cache_v, dk, dv, bias, dz, lq1, lk1, lq2, lk2, subg)


def _out_kernel(yf_ref, yd_ref, ga_ref, gb_ref, x_ref, wuf_ref, wud_ref, wo_ref, g_ref, o_ref):
    merged = (ga_ref[...].astype(F32) * _mm(yf_ref[...], wuf_ref[...])
              + gb_ref[...].astype(F32) * _mm(yd_ref[...], wud_ref[...]))
    out = x_ref[...] + _mm(merged.astype(BF16), wo_ref[...])
    ms = jnp.mean(out * out, axis=-1, keepdims=True)
    o_ref[...] = out * lax.rsqrt(ms + NORM_EPS) * g_ref[...]


def _output(yf, yd, ga, gb, x2d, wuf, wud, wo, g, *, tm):
    rows = x2d.shape[0]
    assert rows % tm == 0
    row_spec = lambda n: pl.BlockSpec((tm, n), lambda i: (i, 0))
    const = lambda shape: pl.BlockSpec(shape, lambda i: (0, 0))
    return pl.pallas_call(
        _out_kernel,
        grid=(rows // tm,),
        in_specs=[row_spec(WIDTH), row_spec(WIDTH), row_spec(D_MODEL), row_spec(D_MODEL), row_spec(D_MODEL),
                  const((WIDTH, D_MODEL)), const((WIDTH, D_MODEL)), const((D_MODEL, D_MODEL)),
                  const((1, D_MODEL))],
        out_specs=row_spec(D_MODEL),
        out_shape=jax.ShapeDtypeStruct((rows, D_MODEL), F32),
        compiler_params=pltpu.CompilerParams(
            dimension_semantics=("parallel",), vmem_limit_bytes=VMEM_LIMIT),
        name="out_proj",
    )(yf, yd, ga, gb, x2d, wuf, wud, wo, g)


def _pack_w_in(w_in, b_forget):
    sizes = (WIDTH, WIDTH, WIDTH, WIDTH, FOX_HEADS, WIDTH, WIDTH, WIDTH, WIDTH, D_MODEL, D_MODEL)
    offs = np.cumsum((0,) + sizes)
    fq, fk, fv, fz, fl, dq, dk, dv, dz, ga, gb = (w_in[:, offs[i]:offs[i + 1]] for i in range(len(sizes)))
    swap = np.arange(FOX_HEADS) ^ 1
    flw = jnp.repeat(fl[:, swap], HD, axis=1)
    fl8 = jnp.pad(fl, ((0, 0), (0, LANES - FOX_HEADS)))
    w_all = jnp.concatenate([fq, fk, fv, fz, dq, dk, dv, dz, ga, gb, flw, fl8], axis=1).astype(BF16)
    bw = jnp.repeat(b_forget[swap], HD)[None, :]
    return w_all, bw, jnp.pad(b_forget, (0, LANES - FOX_HEADS))[None, :]


def kernel(x_prompt, x_sample, cache_fox_k, cache_fox_v, cache_fox_logf, cache_diff_k, cache_diff_v,
           norm_in_g, w_in, b_forget, lambda_q1, lambda_k1, lambda_q2, lambda_k2, subln_g,
           w_up_fox, w_up_diff, w_o, rel_bias, final_norm_g):
    bp, tp, _ = x_prompt.shape
    bs, ts, _ = x_sample.shape
    past = cache_fox_k.shape[2]
    blk = 256
    tm = 512

    w_all, bw, b8 = _pack_w_in(w_in[0], b_forget[0])
    g_in = norm_in_g[0][None, :]
    g_out = final_norm_g[None, :]
    wuf, wud, wo = w_up_fox[0].astype(BF16), w_up_diff[0].astype(BF16), w_o[0].astype(BF16)
    lq1, lk1, lq2, lk2 = (a[0][None, :] for a in (lambda_q1, lambda_k1, lambda_q2, lambda_k2))
    subg = subln_g[0][None, :]

    xp = x_prompt.reshape(bp * tp, D_MODEL)
    (fkt, fvt, lft, dkt, dv4, qa, ka, vtf, fz, dq, dkb, vtd, dz, ga, gb) = _project(
        xp, g_in, w_all, bw, b8, seq_len=tp, aug=True, tm=tm)
    r3 = lambda a: a.reshape(bp, tp, a.shape[-1])
    yf = _fox_attention(r3(qa), r3(ka), vtf, r3(fz), blk=blk, nh=FOX_STEP_HEADS)
    bias_p = _bias_tiles(rel_bias, q_start=blk, nq=blk, segments=((0, blk, blk), (blk, blk, blk)),
                         transposed=True, scale=LOG2E)
    yd = _diff_attention(r3(dq), r3(dkb), vtd, bias_p, r3(dz), lq1, lk1, lq2, lk2, subg, blk=blk,
                         nh=DIFF_STEP_HEADS)
    y_p = _output(yf.reshape(bp * tp, WIDTH), yd.reshape(bp * tp, WIDTH), ga, gb, xp, wuf, wud, wo, g_out, tm=tm)
    fox_k_p = jnp.transpose(fkt.reshape(bp, FOX_HEADS, HD, tp), (0, 3, 1, 2))[None]
    fox_v_p = jnp.transpose(fvt.reshape(bp, FOX_HEADS, HD, tp), (0, 3, 1, 2))[None]
    fox_lf_p = jnp.transpose(lft, (0, 2, 1))[None]
    diff_k_p = jnp.transpose(dkt.reshape(bp, DIFF_HEADS, 2, HD, tp), (0, 4, 1, 2, 3))[None]
    diff_v_p = dv4.reshape(1, bp, tp, DIFF_HEADS, 2 * HD)

    xs = x_sample.reshape(bs * ts, D_MODEL)
    (fk_s, fv_s, lf_s, dk_s, dv_s, fq_s, fz_s, dq_s, dz_s, ga_s, gb_s) = _project(
        xs, g_in, w_all, bw, b8, seq_len=ts, aug=False, tm=tm)
    fox_kt = jnp.transpose(cache_fox_k[0], (0, 2, 3, 1)).reshape(bs, WIDTH, past)
    fox_vt = jnp.transpose(cache_fox_v[0], (0, 2, 3, 1)).reshape(bs, WIDTH, past)
    diff_kt = jnp.transpose(cache_diff_k[0], (0, 2, 3, 4, 1)).reshape(bs, WIDTH, past)
    diff_v = cache_diff_v[0].reshape(bs, past * DIFF_HEADS, 2 * HD)
    lf_past_t = jnp.swapaxes(cache_fox_logf[0], 1, 2)
    lf_new_t = jnp.pad(jnp.swapaxes(lf_s.reshape(bs, ts, FOX_HEADS), 1, 2), ((0, 0), (0, 0), (0, LANES - ts)))
    yf_s = _fox_sample(fq_s, fox_kt, fox_vt, fk_s, fv_s, lf_past_t, lf_new_t, fz_s, nq=ts)
    bias_s = _bias_tiles(rel_bias, q_start=past, nq=ts, segments=((0, past, past), (past, LANES, ts)),
                         transposed=False, scale=1.0)
    yd_s = _diff_sample(dq_s, diff_kt, diff_v, dk_s, dv_s, bias_s, dz_s, lq1, lk1, lq2, lk2, subg, nq=ts)
    y_s = _output(yf_s, yd_s, ga_s, gb_s, xs, wuf, wud, wo, g_out, tm=tm)

    return (y_p.reshape(bp, tp, D_MODEL), y_s.reshape(bs, ts, D_MODEL),
            fox_k_p, fox_v_p, fox_lf_p, diff_k_p, diff_v_p,
            fk_s.reshape(1, bs, ts, FOX_HEADS, HD), fv_s.reshape(1, bs, ts, FOX_HEADS, HD),
            lf_s.reshape(1, bs, ts, FOX_HEADS),
            dk_s.reshape(1, bs, ts, DIFF_HEADS, 2, HD), dv_s.reshape(1, bs, ts, DIFF_HEADS, 2 * HD))
```

```python
import functools
import math

import numpy as np
import jax
import jax.numpy as jnp
from jax import lax
from jax.experimental import pallas as pl
from jax.experimental.pallas import tpu as pltpu

F32 = jnp.float32
BF16 = jnp.bfloat16

D_MODEL = 1024
HD = 64
FOX_HEADS = 8
DIFF_HEADS = 4
WIDTH = 512
CHUNK = 64
CHUNK_SHIFT = 6
N_BUCKETS = 32
MAX_DISTANCE = 128
NORM_EPS = 1e-6
SCALE = HD ** -0.5
NEG = -1e30
LAMBDA_INIT = 0.8 - 0.6 * math.exp(-0.3 * 0)
LOG2E = math.log2(math.e)

LANES = 128
PAIRS = WIDTH // LANES
VT_PAD = 16
VT_ROWS = LANES + VT_PAD
FVT_ROWS = HD + VT_PAD

_O_FQ, _O_FK, _O_FV, _O_FZ = 0, 512, 1024, 1536
_O_DQ, _O_DK, _O_DV, _O_DZ = 2048, 2560, 3072, 3584
_O_GA, _O_GB = 4096, 5120
_O_FL = 6144
_N_ALL = 6272
FL_SLOT = 8

VMEM_LIMIT = 56 * 1024 * 1024

FOX_STEP_HEADS = 8
DIFF_STEP_HEADS = 4
HEADROOM = 64.0
WIDE_BLOCKS = 2


def _t5_thresholds():
    nb = N_BUCKETS // 2
    max_exact = nb // 2
    n = np.arange(0, 4 * MAX_DISTANCE)
    large = max_exact + (np.log(np.maximum(n, 1).astype(np.float32) / max_exact)
                         / math.log(MAX_DISTANCE / max_exact) * (nb - max_exact)).astype(np.int32)
    large = np.minimum(large, nb - 1)
    thr = [int(np.argmax(large >= b)) for b in range(max_exact + 1, nb)]
    return max_exact, tuple(thr)


_MAX_EXACT, _T5_THR = _t5_thresholds()
FAR_DIST = _T5_THR[-1]


def _mm(a, b):
    return jnp.dot(a, b, preferred_element_type=F32)


def _mm_nt(a, b):
    return lax.dot_general(a, b, (((1,), (1,)), ((), ())), preferred_element_type=F32)


def _log_sigmoid(x):
    return -(jnp.maximum(-x, 0.0) + jnp.log1p(jnp.exp(-jnp.abs(x))))


def _sigmoid(x):
    return 1.0 / (1.0 + jnp.exp(-x))


def _silu(x):
    return x * _sigmoid(x)


def _split3(x):
    p1 = x.astype(BF16).astype(F32)
    r1 = x - p1
    p2 = r1.astype(BF16).astype(F32)
    p3 = r1 - p2
    return p1, p2, p3


def _proj_kernel(x_ref, g_ref, w_ref, bfl_ref, *refs, tm, seq_len, aug):
    if aug:
        (fkt_ref, fvt_ref, lft_ref, dkt_ref, dv4_ref, qa_ref, ka_ref, vtf_ref,
         fz_ref, dq_ref, dkb_ref, vtd_ref, dz_ref, ga_ref, gb_ref, carry_ref) = refs
    else:
        (fk_ref, fv_ref, lf_ref, dk_ref, dv_ref, fq_ref,
         fz_ref, dq_ref, dz_ref, ga_ref, gb_ref) = refs

    x = x_ref[...]
    ms = jnp.mean(x * x, axis=-1, keepdims=True)
    h = (x * lax.rsqrt(ms + NORM_EPS) * g_ref[...]).astype(BF16)

    def grp(off, n=WIDTH):
        return _mm(h, w_ref[:, off:off + n])

    if not aug:
        fq_ref[...] = (grp(_O_FQ) * SCALE).astype(BF16)
        fk_ref[...] = grp(_O_FK)
        fv_ref[...] = grp(_O_FV)
        fz_ref[...] = grp(_O_FZ).astype(BF16)
        dq_ref[...] = (grp(_O_DQ) * SCALE).astype(BF16)
        dk_ref[...] = grp(_O_DK)
        dv_ref[...] = grp(_O_DV)
        dz_ref[...] = grp(_O_DZ).astype(BF16)
        ga_ref[...] = _sigmoid(grp(_O_GA, D_MODEL)).astype(BF16)
        gb_ref[...] = _sigmoid(grp(_O_GB, D_MODEL)).astype(BF16)
        lf_ref[...] = _log_sigmoid(grp(_O_FL, LANES) + bfl_ref[...])[:, :FOX_HEADS]
        return

    lf = _log_sigmoid(grp(_O_FL, LANES) + bfl_ref[...])
    lft_ref[0] = lf.T[:FOX_HEADS, :]
    row = lax.broadcasted_iota(jnp.int32, (tm, LANES), 0)
    c = lf
    shift = 1
    while shift < tm:
        c = c + jnp.where(row >= shift, pltpu.roll(c, shift, 0), 0.0)
        shift *= 2
    tiles_per_seq = seq_len // tm

    @pl.when(pl.program_id(0) % tiles_per_seq == 0)
    def _():
        carry_ref[...] = jnp.zeros_like(carry_ref)

    c = c + carry_ref[...]
    carry_ref[...] = c[tm - 1:tm, :]
    c_hi, c_mid, c_lo = _split3(c * LOG2E)

    fq = grp(_O_FQ) * (SCALE * LOG2E)
    fk = grp(_O_FK)
    fv = grp(_O_FV)
    fvt = fv.T
    fkt_ref[0] = fk.T
    fvt_ref[0] = fvt
    tail = jnp.where(lax.broadcasted_iota(jnp.int32, (VT_PAD, tm), 0) == 0, 1.0, 0.0).astype(BF16)
    lane = lax.broadcasted_iota(jnp.int32, (tm, LANES), 1)
    for hd in range(FOX_HEADS):
        blk, par = divmod(hd, 2)
        sl = slice(blk * LANES, (blk + 1) * LANES)
        data = (lane < HD) if par == 0 else (lane >= HD)
        a = lane - ((1 - par) * HD + hd * FL_SLOT)
        q_aug = jnp.where(a == 0, c_hi, jnp.where(a == 1, c_mid, jnp.where(
            a == 2, c_lo, jnp.where((a >= 3) & (a < 6), 1.0, 0.0))))
        k_aug = jnp.where((a >= 0) & (a < 3), 1.0, jnp.where(a == 3, -c_hi, jnp.where(
            a == 4, -c_mid, jnp.where(a == 5, -c_lo, 0.0))))
        osl = slice(hd * LANES, (hd + 1) * LANES)
        qa_ref[:, osl] = jnp.where(data, fq[:, sl], q_aug).astype(BF16)
        ka_ref[:, osl] = jnp.where(data, fk[:, sl], k_aug).astype(BF16)
        base = hd * FVT_ROWS
        vtf_ref[0, base:base + HD, :] = fvt[hd * HD:(hd + 1) * HD, :].astype(BF16)
        vtf_ref[0, base + HD:base + FVT_ROWS, :] = tail

    ga_ref[...] = _sigmoid(grp(_O_GA, D_MODEL)).astype(BF16)
    gb_ref[...] = _sigmoid(grp(_O_GB, D_MODEL)).astype(BF16)

    dk = grp(_O_DK)
    dkt_ref[0] = dk.T
    dkb_ref[...] = dk.astype(BF16)
    dv = grp(_O_DV)
    dv4_ref[...] = dv.reshape(tm, DIFF_HEADS, 2 * HD)
    for hd in range(DIFF_HEADS):
        base = hd * VT_ROWS
        vtd_ref[0, base:base + LANES, :] = dv[:, hd * LANES:(hd + 1) * LANES].T.astype(BF16)
        vtd_ref[0, base + LANES:base + VT_ROWS, :] = tail

    dq_ref[...] = (grp(_O_DQ) * (SCALE * LOG2E)).astype(BF16)
    fz_ref[...] = grp(_O_FZ).astype(BF16)
    dz_ref[...] = grp(_O_DZ).astype(BF16)


def _project(x2d, g, w_all, bfl, *, seq_len, aug, tm):
    rows = x2d.shape[0]
    assert rows % tm == 0
    row_spec = lambda n: pl.BlockSpec((tm, n), lambda i: (i, 0))
    const = lambda shape: pl.BlockSpec(shape, lambda i: (0, 0))
    f32o = lambda n: jax.ShapeDtypeStruct((rows, n), F32)
    b16o = lambda n: jax.ShapeDtypeStruct((rows, n), BF16)
    if aug:
        assert seq_len % tm == 0
        wide = FOX_HEADS * LANES
        tps = seq_len // tm
        t_shape = lambda n, dt: jax.ShapeDtypeStruct((rows // seq_len, n, seq_len), dt)
        t_spec = lambda n: pl.BlockSpec((1, n, tm), lambda i: (i // tps, 0, i % tps))
        out_shape = [t_shape(WIDTH, F32), t_shape(WIDTH, F32), t_shape(FOX_HEADS, F32), t_shape(WIDTH, F32),
                     jax.ShapeDtypeStruct((rows, DIFF_HEADS, 2 * HD), F32)]
        out_specs = [t_spec(WIDTH), t_spec(WIDTH), t_spec(FOX_HEADS), t_spec(WIDTH),
                     pl.BlockSpec((tm, DIFF_HEADS, 2 * HD), lambda i: (i, 0, 0))]
        out_shape += ([b16o(wide)] * 2 + [t_shape(FOX_HEADS * FVT_ROWS, BF16)] + [b16o(WIDTH)] * 3
                      + [t_shape(DIFF_HEADS * VT_ROWS, BF16)] + [b16o(WIDTH)] + [b16o(D_MODEL)] * 2)
        out_specs += ([row_spec(wide)] * 2 + [t_spec(FOX_HEADS * FVT_ROWS)] + [row_spec(WIDTH)] * 3
                      + [t_spec(DIFF_HEADS * VT_ROWS)] + [row_spec(WIDTH)] + [row_spec(D_MODEL)] * 2)
        scratch = [pltpu.VMEM((1, LANES), F32)]
    else:
        out_shape = [f32o(WIDTH), f32o(WIDTH), f32o(FOX_HEADS), f32o(WIDTH), f32o(WIDTH)]
        out_specs = [row_spec(WIDTH), row_spec(WIDTH), row_spec(FOX_HEADS), row_spec(WIDTH), row_spec(WIDTH)]
        out_shape += [b16o(WIDTH)] * 4 + [b16o(D_MODEL)] * 2
        out_specs += [row_spec(WIDTH)] * 4 + [row_spec(D_MODEL)] * 2
        scratch = []
    return pl.pallas_call(
        functools.partial(_proj_kernel, tm=tm, seq_len=seq_len, aug=aug),
        grid=(rows // tm,),
        in_specs=[row_spec(D_MODEL), const((1, D_MODEL)),
                  pl.BlockSpec((D_MODEL, _N_ALL), lambda i: (0, 0), pipeline_mode=pl.Buffered(1)),
                  const((1, LANES))],
        out_specs=out_specs,
        out_shape=out_shape,
        scratch_shapes=scratch,
        compiler_params=pltpu.CompilerParams(
            dimension_semantics=("arbitrary",), vmem_limit_bytes=VMEM_LIMIT),
        name="proj_aug" if aug else "proj_plain",
    )(x2d, g, w_all, bfl)


def _bias_kernel(tbl_ref, o_ref, *, q_start, nq, segments, transposed, scale):
    hd = pl.program_id(0)
    far = tbl_ref[N_BUCKETS // 2 - 1, hd]
    q_axis, k_axis = (1, 0) if transposed else (0, 1)
    off = 0
    for k_start, nk, valid in segments:
        shape = (nk, nq) if transposed else (nq, nk)
        qpos = q_start + lax.broadcasted_iota(jnp.int32, shape, q_axis)
        kidx = lax.broadcasted_iota(jnp.int32, shape, k_axis)
        kpos = k_start + kidx
        rel = kpos - qpos
        n = jnp.abs(rel)
        large = jnp.full(shape, _MAX_EXACT, jnp.int32)
        for thr in _T5_THR:
            large = large + jnp.where(n >= thr, 1, 0)
        bucket = jnp.where(rel > 0, N_BUCKETS // 2, 0) + jnp.where(n < _MAX_EXACT, n, large)
        val = jnp.zeros(shape, F32)
        for b in range(N_BUCKETS):
            val = jnp.where(bucket == b, tbl_ref[b, hd], val)
        val = (val - far) * scale
        visible = ((kpos >> CHUNK_SHIFT) <= (qpos >> CHUNK_SHIFT)) & (kidx < valid)
        val = jnp.where(visible, val, NEG)
        if transposed:
            o_ref[0, off:off + nk, :] = val
        else:
            o_ref[0, :, off:off + nk] = val
        off += nk


def _bias_tiles(rel_bias, *, q_start, nq, segments, transposed, scale):
    total = sum(s[1] for s in segments)
    shape = (total, nq) if transposed else (nq, total)
    return pl.pallas_call(
        functools.partial(_bias_kernel, q_start=q_start, nq=nq, segments=segments, transposed=transposed,
                          scale=scale),
        grid=(DIFF_HEADS,),
        in_specs=[pl.BlockSpec(memory_space=pltpu.SMEM)],
        out_specs=pl.BlockSpec((1,) + shape, lambda h: (h, 0, 0)),
        out_shape=jax.ShapeDtypeStruct((DIFF_HEADS,) + shape, F32),
        name="t5_bias",
    )(rel_bias)


def _flash_t(k_ref, vt_ref, acc_ref, bad_ref, streams, qi, blk, prev_bias, diag_bias, diag_mask):
    n = len(streams)

    def scores(first_blk, width):
        start = pl.multiple_of(first_blk * blk, blk)
        return [_mm_nt(k_ref[0, pl.ds(start, width), ksl], q) for q, ksl, _ in streams]

    def step(first_blk, width, ms, biases, mask):
        start = pl.multiple_of(first_blk * blk, blk)
        sts = scores(first_blk, width)
        out, ps, alphas = [], [], []
        for s, st in enumerate(sts):
            if biases is not None:
                st = st + biases[s]()
            if mask is not None:
                st = jnp.where(mask, st, NEG)
            m_new = jnp.maximum(ms[s], jnp.max(st, axis=0, keepdims=True))
            ps.append(jnp.exp2(st - m_new).astype(BF16))
            alphas.append(jnp.exp2(ms[s] - m_new))
            out.append(m_new)
        for s, (_, _, vsl) in enumerate(streams):
            acc_ref[s] = alphas[s] * acc_ref[s] + _mm(vt_ref[0, vsl, pl.ds(start, width)], ps[s])
        return tuple(out)

    def first_step():
        acc_ref[...] = jnp.zeros_like(acc_ref)
        return step(qi, blk, tuple(jnp.full((1, blk), NEG, F32) for _ in range(n)), diag_bias, diag_mask)

    ms = first_step()
    bad_ref[...] = jnp.zeros_like(bad_ref)

    def consume(sts, first_blk, width):
        start = pl.multiple_of(first_blk * blk, blk)
        ps = [jnp.exp2(st - ms[s]).astype(BF16) for s, st in enumerate(sts)]
        worst = None
        for p in ps:
            d = jnp.max(p, axis=0, keepdims=True)
            worst = d if worst is None else jnp.maximum(worst, d)
        bad_ref[...] = jnp.maximum(bad_ref[...], worst.astype(F32))
        for s, (_, _, vsl) in enumerate(streams):
            acc_ref[s] = acc_ref[s] + _mm(vt_ref[0, vsl, pl.ds(start, width)], ps[s])

    if prev_bias is not None:
        @pl.when(qi >= 1)
        def _():
            consume([st + prev_bias[s]() for s, st in enumerate(scores(qi - 1, blk))], qi - 1, blk)

    n_plain = qi if prev_bias is None else jnp.maximum(qi - 1, 0)
    n_wide = n_plain // WIDE_BLOCKS

    def wide_step(j, carry):
        consume(scores(j * WIDE_BLOCKS, WIDE_BLOCKS * blk), j * WIDE_BLOCKS, WIDE_BLOCKS * blk)
        return carry

    lax.fori_loop(0, n_wide, wide_step, 0)

    def single_step(j, carry):
        consume(scores(j, blk), j, blk)
        return carry

    lax.fori_loop(n_wide * WIDE_BLOCKS, n_plain, single_step, 0)

    @pl.when(jnp.max(bad_ref[...]) > 2.0 ** HEADROOM)
    def _():
        c = first_step()
        if prev_bias is not None:
            c = lax.cond(qi >= 1, lambda c: step(qi - 1, blk, c, prev_bias, None), lambda c: c, c)
        lax.fori_loop(0, n_plain, lambda j, c: step(j, blk, c, None, None), c)


def _fox_kernel(qa_ref, ka_ref, vt_ref, z_ref, o_ref, acc_ref, bad_ref, *, blk, nh):
    qi = pl.program_id(2)
    key = lax.broadcasted_iota(jnp.int32, (blk, blk), 0)
    qry = lax.broadcasted_iota(jnp.int32, (blk, blk), 1)
    hsl = [slice(hd * LANES, (hd + 1) * LANES) for hd in range(nh)]
    streams = [(qa_ref[0, :, sl], sl, slice(hd * FVT_ROWS, (hd + 1) * FVT_ROWS)) for hd, sl in enumerate(hsl)]
    _flash_t(ka_ref, vt_ref, acc_ref, bad_ref, streams, qi, blk, None, None, key <= qry)
    for pair in range(nh // 2):
        halves = []
        for par in range(2):
            acc = acc_ref[2 * pair + par]
            halves.append(acc[0:HD, :] / acc[HD:HD + 1, :])
        o = jnp.concatenate(halves, axis=0).T
        o_ref[0, :, hsl[pair]] = (o * _silu(z_ref[0, :, hsl[pair]].astype(F32))).astype(BF16)


def _fox_attention(qa, ka, vt, fz, *, blk, nh):
    b, t, _ = qa.shape
    return pl.pallas_call(
        functools.partial(_fox_kernel, blk=blk, nh=nh),
        grid=(b, FOX_HEADS // nh, t // blk),
        in_specs=[pl.BlockSpec((1, blk, nh * LANES), lambda b, p, i: (b, i, p)),
                  pl.BlockSpec((1, t, nh * LANES), lambda b, p, i: (b, 0, p)),
                  pl.BlockSpec((1, nh * FVT_ROWS, t), lambda b, p, i: (b, p, 0)),
                  pl.BlockSpec((1, blk, nh * HD), lambda b, p, i: (b, i, p))],
        out_specs=pl.BlockSpec((1, blk, nh * HD), lambda b, p, i: (b, i, p)),
        out_shape=jax.ShapeDtypeStruct((b, t, WIDTH), BF16),
        scratch_shapes=[pltpu.VMEM((nh, FVT_ROWS, blk), F32), pltpu.VMEM((1, blk), F32)],
        compiler_params=pltpu.CompilerParams(
            dimension_semantics=("parallel", "parallel", "arbitrary"), vmem_limit_bytes=VMEM_LIMIT),
        name="fox_prompt",
    )(qa, ka, vt, fz)


def _lambda(lq1_ref, lk1_ref, lq2_ref, lk2_ref):
    s1 = jnp.sum(lq1_ref[...] * lk1_ref[...], axis=-1, keepdims=True)
    s2 = jnp.sum(lq2_ref[...] * lk2_ref[...], axis=-1, keepdims=True)
    return jnp.exp(s1) - jnp.exp(s2) + LAMBDA_INIT


def _diff_finish(o0, o1, lam, subg, z):
    o = o0 - lam * o1
    ms = jnp.mean(o * o, axis=-1, keepdims=True)
    od = o * lax.rsqrt(ms + NORM_EPS) * subg * (1.0 - LAMBDA_INIT)
    return (od * _silu(z)).astype(BF16)


def _diff_kernel(q_ref, k_ref, vt_ref, bias_ref, z_ref, lq1_ref, lk1_ref, lq2_ref, lk2_ref, subg_ref,
                 o_ref, acc_ref, bad_ref, *, blk, nh):
    qi = pl.program_id(2)
    lane = lax.broadcasted_iota(jnp.int32, (blk, LANES), 1)
    streams, prev_bias, diag_bias = [], [], []
    for hd in range(nh):
        q = q_ref[0, :, hd * LANES:(hd + 1) * LANES]
        for mp in range(2):
            streams.append((jnp.where((lane < HD) if mp == 0 else (lane >= HD), q, jnp.zeros_like(q)),
                            slice(hd * LANES, (hd + 1) * LANES), slice(hd * VT_ROWS, (hd + 1) * VT_ROWS)))
            prev_bias.append(lambda hd=hd: bias_ref[hd, 0:blk, :])
            diag_bias.append(lambda hd=hd: bias_ref[hd, blk:2 * blk, :])
    _flash_t(k_ref, vt_ref, acc_ref, bad_ref, streams, qi, blk, prev_bias, diag_bias, None)
    lam = _lambda(lq1_ref, lk1_ref, lq2_ref, lk2_ref)
    for hd in range(nh):
        outs = []
        for mp in range(2):
            acc = acc_ref[2 * hd + mp]
            outs.append((acc[0:LANES, :] / acc[LANES:LANES + 1, :]).T)
        sl = slice(hd * LANES, (hd + 1) * LANES)
        o_ref[0, :, sl] = _diff_finish(outs[0], outs[1], lam, subg_ref[...], z_ref[0, :, sl].astype(F32))


def _diff_attention(dq, dk, dvt, bias, dz, lq1, lk1, lq2, lk2, subg, *, blk, nh):
    b, t, _ = dq.shape
    small = lambda n: pl.BlockSpec((1, n), lambda b, h, i: (0, 0))
    return pl.pallas_call(
        functools.partial(_diff_kernel, blk=blk, nh=nh),
        grid=(b, DIFF_HEADS // nh, t // blk),
        in_specs=[pl.BlockSpec((1, blk, nh * LANES), lambda b, h, i: (b, i, h)),
                  pl.BlockSpec((1, t, nh * LANES), lambda b, h, i: (b, 0, h)),
                  pl.BlockSpec((1, nh * VT_ROWS, t), lambda b, h, i: (b, h, 0)),
                  pl.BlockSpec((nh, 2 * blk, blk), lambda b, h, i: (h, 0, 0)),
                  pl.BlockSpec((1, blk, nh * LANES), lambda b, h, i: (b, i, h)),
                  small(HD), small(HD), small(HD), small(HD), small(2 * HD)],
        out_specs=pl.BlockSpec((1, blk, nh * LANES), lambda b, h, i: (b, i, h)),
        out_shape=jax.ShapeDtypeStruct((b, t, WIDTH), BF16),
        scratch_shapes=[pltpu.VMEM((2 * nh, VT_ROWS, blk), F32), pltpu.VMEM((1, blk), F32)],
        compiler_params=pltpu.CompilerParams(
            dimension_semantics=("parallel", "parallel", "arbitrary"), vmem_limit_bytes=VMEM_LIMIT),
        name="diff_prompt",
    )(dq, dk, dvt, bias, dz, lq1, lk1, lq2, lk2, subg)


def _suffix_sums(x):
    n = x.shape[1] // LANES
    lane = lax.broadcasted_iota(jnp.int32, (x.shape[0], LANES), 1)
    after = jnp.zeros((x.shape[0], 1), F32)
    pieces = [None] * n
    for blk in reversed(range(n)):
        piece = x[:, blk * LANES:(blk + 1) * LANES]
        y = piece
        shift = 1
        while shift < LANES:
            y = y + jnp.where(lane + shift < LANES, pltpu.roll(y, LANES - shift, 1), 0.0)
            shift *= 2
        pieces[blk] = y - piece + after
        after = after + y[:, 0:1]
    return jnp.concatenate(pieces, axis=1), after


def _pad_rows(x, rows):
    return jnp.concatenate([x, jnp.zeros((rows - x.shape[0], x.shape[1]), x.dtype)], axis=0)


def _joint_softmax_pv(s_p, s_n, vt_p, v_n):
    m = jnp.maximum(jnp.max(s_p, axis=-1, keepdims=True), jnp.max(s_n, axis=-1, keepdims=True))
    p_p = jnp.exp(s_p - m)
    p_n = jnp.exp(s_n - m)
    l = jnp.sum(p_p, axis=-1, keepdims=True) + jnp.sum(p_n, axis=-1, keepdims=True)
    return (_mm_nt(p_p.astype(BF16), vt_p) + _mm(p_n.astype(BF16), v_n)) / l


def _fox_sample_kernel(q_ref, kt_ref, vt_ref, kn_ref, vn_ref, lfp_ref, lfn_ref, z_ref, o_ref, *, nq):
    r_new, total_new = _suffix_sums(lfn_ref[0])
    r_past, _ = _suffix_sums(lfp_ref[0])
    r_past = r_past + total_new
    lane = lax.broadcasted_iota(jnp.int32, (nq, LANES), 1)
    causal = lane <= lax.broadcasted_iota(jnp.int32, (nq, LANES), 0)
    for blk in range(PAIRS):
        sl = slice(blk * LANES, (blk + 1) * LANES)
        q = q_ref[:, sl]
        kt = kt_ref[0, sl, :].astype(BF16)
        vt = vt_ref[0, sl, :].astype(BF16)
        kn = _pad_rows(kn_ref[:, sl], LANES).astype(BF16)
        vn = _pad_rows(vn_ref[:, sl], LANES).astype(BF16)
        outs = []
        for par in range(2):
            hd = 2 * blk + par
            qm = jnp.where((lane < HD) if par == 0 else (lane >= HD), q, jnp.zeros_like(q))
            s_p = _mm(qm, kt) + r_past[hd:hd + 1, :]
            s_n = jnp.where(causal, _mm_nt(qm, kn) + r_new[hd:hd + 1, :], NEG)
            outs.append(_joint_softmax_pv(s_p, s_n, vt, vn))
        o = jnp.where(lane < HD, outs[0], outs[1])
        o_ref[:, sl] = (o * _silu(z_ref[:, sl].astype(F32))).astype(BF16)


def _fox_sample(fq, cache_kt, cache_vt, fk, fv, lf_past_t, lf_new_t, fz, *, nq):
    b, _, past = cache_kt.shape
    new = lambda: pl.BlockSpec((nq, WIDTH), lambda i: (i, 0))
    cache = lambda: pl.BlockSpec((1, WIDTH, past), lambda i: (i, 0, 0))
    return pl.pallas_call(
        functools.partial(_fox_sample_kernel, nq=nq),
        grid=(b,),
        in_specs=[new(), cache(), cache(), new(), new(),
                  pl.BlockSpec((1, FOX_HEADS, past), lambda i: (i, 0, 0)),
                  pl.BlockSpec((1, FOX_HEADS, LANES), lambda i: (i, 0, 0)),
                  new()],
        out_specs=new(),
        out_shape=jax.ShapeDtypeStruct((b * nq, WIDTH), BF16),
        compiler_params=pltpu.CompilerParams(
            dimension_semantics=("parallel",), vmem_limit_bytes=VMEM_LIMIT),
        name="fox_sample",
    )(fq, cache_kt, cache_vt, fk, fv, lf_past_t, lf_new_t, fz)


def _diff_sample_kernel(q_ref, kt_ref, v_ref, kn_ref, vn_ref, bias_ref, z_ref,
                        lq1_ref, lk1_ref, lq2_ref, lk2_ref, subg_ref, o_ref, *, nq, past):
    lane = lax.broadcasted_iota(jnp.int32, (nq, LANES), 1)
    lam = _lambda(lq1_ref, lk1_ref, lq2_ref, lk2_ref)
    for hd in range(DIFF_HEADS):
        sl = slice(hd * LANES, (hd + 1) * LANES)
        q = q_ref[:, sl]
        kt = kt_ref[0, sl, :].astype(BF16)
        vt = v_ref[0, pl.ds(hd, past, stride=DIFF_HEADS), :].T.astype(BF16)
        kn = _pad_rows(kn_ref[:, sl], LANES).astype(BF16)
        vn = _pad_rows(vn_ref[:, sl], LANES).astype(BF16)
        outs = []
        for mp in range(2):
            qm = jnp.where((lane < HD) if mp == 0 else (lane >= HD), q, jnp.zeros_like(q))
            s_p = _mm(qm, kt) + bias_ref[hd, :, 0:past]
            s_n = _mm_nt(qm, kn) + bias_ref[hd, :, past:past + LANES]
            outs.append(_joint_softmax_pv(s_p, s_n, vt, vn))
        o_ref[:, sl] = _diff_finish(outs[0], outs[1], lam, subg_ref[...], z_ref[:, sl].astype(F32))


def _diff_sample(dq, cache_kt, cache_v, dk, dv, bias, dz, lq1, lk1, lq2, lk2, subg, *, nq):
    b, _, past = cache_kt.shape
    new = lambda: pl.BlockSpec((nq, WIDTH), lambda i: (i, 0))
    small = lambda n: pl.BlockSpec((1, n), lambda i: (0, 0))
    return pl.pallas_call(
        functools.partial(_diff_sample_kernel, nq=nq, past=past),
        grid=(b,),
        in_specs=[new(), pl.BlockSpec((1, WIDTH, past), lambda i: (i, 0, 0)),
                  pl.BlockSpec((1, past * DIFF_HEADS, 2 * HD), lambda i: (i, 0, 0)), new(), new(),
                  pl.BlockSpec((DIFF_HEADS, nq, past + LANES), lambda i: (0, 0, 0)),
                  new(), small(HD), small(HD), small(HD), small(HD), small(2 * HD)],
        out_specs=new(),
        out_shape=jax.ShapeDtypeStruct((b * nq, WIDTH), BF16),
        compiler_params=pltpu.CompilerParams(
            dimension_semantics=("parallel",), vmem_limit_bytes=VMEM_LIMIT),
        name="diff_sample",
    )(dq, cache_kt, cache_v, dk, dv, bias, dz, lq1, lk1, lq2, lk2, subg)


def _out_kernel(yf_ref, yd_ref, ga_ref, gb_ref, x_ref, wuf_ref, wud_ref, wo_ref, g_ref, o_ref):
    merged = (ga_ref[...].astype(F32) * _mm(yf_ref[...], wuf_ref[...])
              + gb_ref[...].astype(F32) * _mm(yd_ref[...], wud_ref[...]))
    out = x_ref[...] + _mm(merged.astype(BF16), wo_ref[...])
    ms = jnp.mean(out * out, axis=-1, keepdims=True)
    o_ref[...] = out * lax.rsqrt(ms + NORM_EPS) * g_ref[...]


def _output(yf, yd, ga, gb, x2d, wuf, wud, wo, g, *, tm):
    rows = x2d.shape[0]
    assert rows % tm == 0
    row_spec = lambda n: pl.BlockSpec((tm, n), lambda i: (i, 0))
    const = lambda shape: pl.BlockSpec(shape, lambda i: (0, 0))
    return pl.pallas_call(
        _out_kernel,
        grid=(rows // tm,),
        in_specs=[row_spec(WIDTH), row_spec(WIDTH), row_spec(D_MODEL), row_spec(D_MODEL), row_spec(D_MODEL),
                  const((WIDTH, D_MODEL)), const((WIDTH, D_MODEL)), const((D_MODEL, D_MODEL)),
                  const((1, D_MODEL))],
        out_specs=row_spec(D_MODEL),
        out_shape=jax.ShapeDtypeStruct((rows, D_MODEL), F32),
        compiler_params=pltpu.CompilerParams(
            dimension_semantics=("parallel",), vmem_limit_bytes=VMEM_LIMIT),
        name="out_proj",
    )(yf, yd, ga, gb, x2d, wuf, wud, wo, g)


def _forget_lane_heads():
    lanes = np.arange(LANES)
    heads = (lanes % HD) // FL_SLOT
    heads[:FOX_HEADS] = np.arange(FOX_HEADS)
    return heads


def _pack_w_in(w_in, b_forget):
    sizes = (WIDTH, WIDTH, WIDTH, WIDTH, FOX_HEADS, WIDTH, WIDTH, WIDTH, WIDTH, D_MODEL, D_MODEL)
    offs = np.cumsum((0,) + sizes)
    fq, fk, fv, fz, fl, dq, dk, dv, dz, ga, gb = (w_in[:, offs[i]:offs[i + 1]] for i in range(len(sizes)))
    heads = _forget_lane_heads()
    w_all = jnp.concatenate([fq, fk, fv, fz, dq, dk, dv, dz, ga, gb, fl[:, heads]], axis=1).astype(BF16)
    return w_all, b_forget[heads][None, :]


def kernel(x_prompt, x_sample, cache_fox_k, cache_fox_v, cache_fox_logf, cache_diff_k, cache_diff_v,
           norm_in_g, w_in, b_forget, lambda_q1, lambda_k1, lambda_q2, lambda_k2, subln_g,
           w_up_fox, w_up_diff, w_o, rel_bias, final_norm_g):
    bp, tp, _ = x_prompt.shape
    bs, ts, _ = x_sample.shape
    past = cache_fox_k.shape[2]
    blk = 256
    tm = 512

    w_all, bfl = _pack_w_in(w_in[0], b_forget[0])
    g_in = norm_in_g[0][None, :]
    g_out = final_norm_g[None, :]
    wuf, wud, wo = w_up_fox[0].astype(BF16), w_up_diff[0].astype(BF16), w_o[0].astype(BF16)
    lq1, lk1, lq2, lk2 = (a[0][None, :] for a in (lambda_q1, lambda_k1, lambda_q2, lambda_k2))
    subg = subln_g[0][None, :]

    xp = x_prompt.reshape(bp * tp, D_MODEL)
    (fkt, fvt, lft, dkt, dv4, qa, ka, vtf, fz, dq, dkb, vtd, dz, ga, gb) = _project(
        xp, g_in, w_all, bfl, seq_len=tp, aug=True, tm=tm)
    r3 = lambda a: a.reshape(bp, tp, a.shape[-1])
    yf = _fox_attention(r3(qa), r3(ka), vtf, r3(fz), blk=blk, nh=FOX_STEP_HEADS)
    bias_p = _bias_tiles(rel_bias, q_start=blk, nq=blk, segments=((0, blk, blk), (blk, blk, blk)),
                         transposed=True, scale=LOG2E)
    yd = _diff_attention(r3(dq), r3(dkb), vtd, bias_p, r3(dz), lq1, lk1, lq2, lk2, subg, blk=blk,
                         nh=DIFF_STEP_HEADS)
    y_p = _output(yf.reshape(bp * tp, WIDTH), yd.reshape(bp * tp, WIDTH), ga, gb, xp, wuf, wud, wo, g_out, tm=tm)
    fox_k_p = jnp.transpose(fkt.reshape(bp, FOX_HEADS, HD, tp), (0, 3, 1, 2))[None]
    fox_v_p = jnp.transpose(fvt.reshape(bp, FOX_HEADS, HD, tp), (0, 3, 1, 2))[None]
    fox_lf_p = jnp.transpose(lft, (0, 2, 1))[None]
    diff_k_p = jnp.transpose(dkt.reshape(bp, DIFF_HEADS, 2, HD, tp), (0, 4, 1, 2, 3))[None]
    diff_v_p = dv4.reshape(1, bp, tp, DIFF_HEADS, 2 * HD)

    xs = x_sample.reshape(bs * ts, D_MODEL)
    (fk_s, fv_s, lf_s, dk_s, dv_s, fq_s, fz_s, dq_s, dz_s, ga_s, gb_s) = _project(
        xs, g_in, w_all, bfl, seq_len=ts, aug=False, tm=tm)
    fox_kt = jnp.transpose(cache_fox_k[0], (0, 2, 3, 1)).reshape(bs, WIDTH, past)
    fox_vt = jnp.transpose(cache_fox_v[0], (0, 2, 3, 1)).reshape(bs, WIDTH, past)
    diff_kt = jnp.transpose(cache_diff_k[0], (0, 2, 3, 4, 1)).reshape(bs, WIDTH, past)
    diff_v = cache_diff_v[0].reshape(bs, past * DIFF_HEADS, 2 * HD)
    lf_past_t = jnp.swapaxes(cache_fox_logf[0], 1, 2)
    lf_new_t = jnp.pad(jnp.swapaxes(lf_s.reshape(bs, ts, FOX_HEADS), 1, 2), ((0, 0), (0, 0), (0, LANES - ts)))
    yf_s = _fox_sample(fq_s, fox_kt, fox_vt, fk_s, fv_s, lf_past_t, lf_new_t, fz_s, nq=ts)
    bias_s = _bias_tiles(rel_bias, q_start=past, nq=ts, segments=((0, past, past), (past, LANES, ts)),
                         transposed=False, scale=1.0)
    yd_s = _diff_sample(dq_s, diff_kt, diff_v, dk_s, dv_s, bias_s, dz_s, lq1, lk1, lq2, lk2, subg, nq=ts)
    y_s = _output(yf_s, yd_s, ga_s, gb_s, xs, wuf, wud, wo, g_out, tm=tm)

    return (y_p.reshape(bp, tp, D_MODEL), y_s.reshape(bs, ts, D_MODEL),
            fox_k_p, fox_v_p, fox_lf_p, diff_k_p, diff_v_p,
            fk_s.reshape(1, bs, ts, FOX_HEADS, HD), fv_s.reshape(1, bs, ts, FOX_HEADS, HD),
            lf_s.reshape(1, bs, ts, FOX_HEADS),
            dk_s.reshape(1, bs, ts, DIFF_HEADS, 2, HD), dv_s.reshape(1, bs, ts, DIFF_HEADS, 2 * HD))
```

```python
import functools
import math

import numpy as np
import jax
import jax.numpy as jnp
from jax import lax
from jax.experimental import pallas as pl
from jax.experimental.pallas import tpu as pltpu

F32 = jnp.float32
BF16 = jnp.bfloat16

D_MODEL = 1024
HD = 64
FOX_HEADS = 8
DIFF_HEADS = 4
WIDTH = 512
CHUNK = 64
CHUNK_SHIFT = 6
N_BUCKETS = 32
MAX_DISTANCE = 128
NORM_EPS = 1e-6
SCALE = HD ** -0.5
NEG = -1e30
LAMBDA_INIT = 0.8 - 0.6 * math.exp(-0.3 * 0)
LOG2E = math.log2(math.e)

LANES = 128
PAIRS = WIDTH // LANES
VT_PAD = 16
VT_ROWS = LANES + VT_PAD
FVT_ROWS = HD + VT_PAD

_O_FQ, _O_FK, _O_FV, _O_FZ = 0, 512, 1024, 1536
_O_DQ, _O_DK, _O_DV, _O_DZ = 2048, 2560, 3072, 3584
_O_GA, _O_GB = 4096, 5120
_O_FL = 6144
_N_ALL = 6272
FL_SLOT = 8

VMEM_LIMIT = 56 * 1024 * 1024

FOX_STEP_HEADS = 8
DIFF_STEP_HEADS = 4
HEADROOM = 64.0
WIDE_BLOCKS = 2


def _t5_thresholds():
    nb = N_BUCKETS // 2
    max_exact = nb // 2
    n = np.arange(0, 4 * MAX_DISTANCE)
    large = max_exact + (np.log(np.maximum(n, 1).astype(np.float32) / max_exact)
                         / math.log(MAX_DISTANCE / max_exact) * (nb - max_exact)).astype(np.int32)
    large = np.minimum(large, nb - 1)
    thr = [int(np.argmax(large >= b)) for b in range(max_exact + 1, nb)]
    return max_exact, tuple(thr)


_MAX_EXACT, _T5_THR = _t5_thresholds()
FAR_DIST = _T5_THR[-1]


def _mm(a, b):
    return jnp.dot(a, b, preferred_element_type=F32)


def _mm_nt(a, b):
    return lax.dot_general(a, b, (((1,), (1,)), ((), ())), preferred_element_type=F32)


def _log_sigmoid(x):
    return -(jnp.maximum(-x, 0.0) + jnp.log1p(jnp.exp(-jnp.abs(x))))


def _sigmoid(x):
    return 1.0 / (1.0 + jnp.exp(-x))


def _silu(x):
    return x * _sigmoid(x)


def _split3(x):
    p1 = x.astype(BF16).astype(F32)
    r1 = x - p1
    p2 = r1.astype(BF16).astype(F32)
    p3 = r1 - p2
    return p1, p2, p3


def _proj_kernel(x_ref, g_ref, w_ref, bfl_ref, *refs, tm, seq_len, aug):
    if aug:
        (fkt_ref, fvt_ref, lft_ref, dkt_ref, dv4_ref, qa_ref, ka_ref, vtf_ref,
         fz_ref, dq_ref, dkb_ref, vtd_ref, dz_ref, ga_ref, gb_ref, carry_ref) = refs
    else:
        (fk_ref, fv_ref, lf_ref, dk_ref, dv_ref, fq_ref,
         fz_ref, dq_ref, dz_ref, ga_ref, gb_ref) = refs

    x = x_ref[...]
    ms = jnp.mean(x * x, axis=-1, keepdims=True)
    h = (x * lax.rsqrt(ms + NORM_EPS) * g_ref[...]).astype(BF16)

    def grp(off, n=WIDTH):
        return _mm(h, w_ref[:, off:off + n])

    if not aug:
        fq_ref[...] = (grp(_O_FQ) * SCALE).astype(BF16)
        fk_ref[...] = grp(_O_FK)
        fv_ref[...] = grp(_O_FV)
        fz_ref[...] = grp(_O_FZ).astype(BF16)
        dq_ref[...] = (grp(_O_DQ) * SCALE).astype(BF16)
        dk_ref[...] = grp(_O_DK)
        dv_ref[...] = grp(_O_DV)
        dz_ref[...] = grp(_O_DZ).astype(BF16)
        ga_ref[...] = _sigmoid(grp(_O_GA, D_MODEL)).astype(BF16)
        gb_ref[...] = _sigmoid(grp(_O_GB, D_MODEL)).astype(BF16)
        lf_ref[...] = _log_sigmoid(grp(_O_FL, LANES) + bfl_ref[...])[:, :FOX_HEADS]
        return

    lf = _log_sigmoid(grp(_O_FL, LANES) + bfl_ref[...])
    lft_ref[0] = lf.T[:FOX_HEADS, :]
    row = lax.broadcasted_iota(jnp.int32, (tm, LANES), 0)
    c = lf
    shift = 1
    while shift < tm:
        c = c + jnp.where(row >= shift, pltpu.roll(c, shift, 0), 0.0)
        shift *= 2
    tiles_per_seq = seq_len // tm

    @pl.when(pl.program_id(0) % tiles_per_seq == 0)
    def _():
        carry_ref[...] = jnp.zeros_like(carry_ref)

    c = c + carry_ref[...]
    carry_ref[...] = c[tm - 1:tm, :]
    c_hi, c_mid, c_lo = _split3(c * LOG2E)

    fq = grp(_O_FQ) * (SCALE * LOG2E)
    fk = grp(_O_FK)
    fv = grp(_O_FV)
    fvt = fv.T
    fkt_ref[0] = fk.T
    fvt_ref[0] = fvt
    tail = jnp.where(lax.broadcasted_iota(jnp.int32, (VT_PAD, tm), 0) == 0, 1.0, 0.0).astype(BF16)
    lane = lax.broadcasted_iota(jnp.int32, (tm, LANES), 1)
    pos = lane & (FL_SLOT - 1)
    q_slots = jnp.where(pos == 0, c_hi, jnp.where(pos == 1, c_mid, jnp.where(
        pos == 2, c_lo, jnp.where(pos < 6, 1.0, 0.0))))
    k_slots = jnp.where(pos < 3, 1.0, jnp.where(pos == 3, -c_hi, jnp.where(
        pos == 4, -c_mid, jnp.where(pos == 5, -c_lo, 0.0))))
    for hd in range(FOX_HEADS):
        blk, par = divmod(hd, 2)
        sl = slice(blk * LANES, (blk + 1) * LANES)
        data = (lane < HD) if par == 0 else (lane >= HD)
        slot0 = (1 - par) * HD + hd * FL_SLOT
        own = (lane >= slot0) & (lane < slot0 + FL_SLOT)
        q_aug = jnp.where(own, q_slots, 0.0)
        k_aug = jnp.where(own, k_slots, 0.0)
        osl = slice(hd * LANES, (hd + 1) * LANES)
        qa_ref[:, osl] = jnp.where(data, fq[:, sl], q_aug).astype(BF16)
        ka_ref[:, osl] = jnp.where(data, fk[:, sl], k_aug).astype(BF16)
        base = hd * FVT_ROWS
        vtf_ref[0, base:base + HD, :] = fvt[hd * HD:(hd + 1) * HD, :].astype(BF16)
        vtf_ref[0, base + HD:base + FVT_ROWS, :] = tail

    ga_ref[...] = _sigmoid(grp(_O_GA, D_MODEL)).astype(BF16)
    gb_ref[...] = _sigmoid(grp(_O_GB, D_MODEL)).astype(BF16)

    dk = grp(_O_DK)
    dkt_ref[0] = dk.T
    dkb_ref[...] = dk.astype(BF16)
    dv = grp(_O_DV)
    dv4_ref[...] = dv.reshape(tm, DIFF_HEADS, 2 * HD)
    for hd in range(DIFF_HEADS):
        base = hd * VT_ROWS
        vtd_ref[0, base:base + LANES, :] = dv[:, hd * LANES:(hd + 1) * LANES].T.astype(BF16)
        vtd_ref[0, base + LANES:base + VT_ROWS, :] = tail

    dq_ref[...] = (grp(_O_DQ) * (SCALE * LOG2E)).astype(BF16)
    fz_ref[...] = grp(_O_FZ).astype(BF16)
    dz_ref[...] = grp(_O_DZ).astype(BF16)


def _project(x2d, g, w_all, bfl, *, seq_len, aug, tm):
    rows = x2d.shape[0]
    assert rows % tm == 0
    row_spec = lambda n: pl.BlockSpec((tm, n), lambda i: (i, 0))
    const = lambda shape: pl.BlockSpec(shape, lambda i: (0, 0))
    f32o = lambda n: jax.ShapeDtypeStruct((rows, n), F32)
    b16o = lambda n: jax.ShapeDtypeStruct((rows, n), BF16)
    if aug:
        assert seq_len % tm == 0
        wide = FOX_HEADS * LANES
        tps = seq_len // tm
        t_shape = lambda n, dt: jax.ShapeDtypeStruct((rows // seq_len, n, seq_len), dt)
        t_spec = lambda n: pl.BlockSpec((1, n, tm), lambda i: (i // tps, 0, i % tps))
        out_shape = [t_shape(WIDTH, F32), t_shape(WIDTH, F32), t_shape(FOX_HEADS, F32), t_shape(WIDTH, F32),
                     jax.ShapeDtypeStruct((rows, DIFF_HEADS, 2 * HD), F32)]
        out_specs = [t_spec(WIDTH), t_spec(WIDTH), t_spec(FOX_HEADS), t_spec(WIDTH),
                     pl.BlockSpec((tm, DIFF_HEADS, 2 * HD), lambda i: (i, 0, 0))]
        out_shape += ([b16o(wide)] * 2 + [t_shape(FOX_HEADS * FVT_ROWS, BF16)] + [b16o(WIDTH)] * 3
                      + [t_shape(DIFF_HEADS * VT_ROWS, BF16)] + [b16o(WIDTH)] + [b16o(D_MODEL)] * 2)
        out_specs += ([row_spec(wide)] * 2 + [t_spec(FOX_HEADS * FVT_ROWS)] + [row_spec(WIDTH)] * 3
                      + [t_spec(DIFF_HEADS * VT_ROWS)] + [row_spec(WIDTH)] + [row_spec(D_MODEL)] * 2)
        scratch = [pltpu.VMEM((1, LANES), F32)]
    else:
        out_shape = [f32o(WIDTH), f32o(WIDTH), f32o(FOX_HEADS), f32o(WIDTH), f32o(WIDTH)]
        out_specs = [row_spec(WIDTH), row_spec(WIDTH), row_spec(FOX_HEADS), row_spec(WIDTH), row_spec(WIDTH)]
        out_shape += [b16o(WIDTH)] * 4 + [b16o(D_MODEL)] * 2
        out_specs += [row_spec(WIDTH)] * 4 + [row_spec(D_MODEL)] * 2
        scratch = []
    return pl.pallas_call(
        functools.partial(_proj_kernel, tm=tm, seq_len=seq_len, aug=aug),
        grid=(rows // tm,),
        in_specs=[row_spec(D_MODEL), const((1, D_MODEL)),
                  pl.BlockSpec((D_MODEL, _N_ALL), lambda i: (0, 0), pipeline_mode=pl.Buffered(1)),
                  const((1, LANES))],
        out_specs=out_specs,
        out_shape=out_shape,
        scratch_shapes=scratch,
        compiler_params=pltpu.CompilerParams(
            dimension_semantics=("arbitrary",), vmem_limit_bytes=VMEM_LIMIT),
        name="proj_aug" if aug else "proj_plain",
    )(x2d, g, w_all, bfl)


def _bias_kernel(tbl_ref, o_ref, *, q_start, nq, segments, transposed, scale):
    hd = pl.program_id(0)
    far = tbl_ref[N_BUCKETS // 2 - 1, hd]
    q_axis, k_axis = (1, 0) if transposed else (0, 1)
    off = 0
    for k_start, nk, valid in segments:
        shape = (nk, nq) if transposed else (nq, nk)
        qpos = q_start + lax.broadcasted_iota(jnp.int32, shape, q_axis)
        kidx = lax.broadcasted_iota(jnp.int32, shape, k_axis)
        kpos = k_start + kidx
        rel = kpos - qpos
        n = jnp.abs(rel)
        large = jnp.full(shape, _MAX_EXACT, jnp.int32)
        for thr in _T5_THR:
            large = large + jnp.where(n >= thr, 1, 0)
        bucket = jnp.where(rel > 0, N_BUCKETS // 2, 0) + jnp.where(n < _MAX_EXACT, n, large)
        val = jnp.zeros(shape, F32)
        for b in range(N_BUCKETS):
            val = jnp.where(bucket == b, tbl_ref[b, hd], val)
        val = (val - far) * scale
        visible = ((kpos >> CHUNK_SHIFT) <= (qpos >> CHUNK_SHIFT)) & (kidx < valid)
        val = jnp.where(visible, val, NEG)
        if transposed:
            o_ref[0, off:off + nk, :] = val
        else:
            o_ref[0, :, off:off + nk] = val
        off += nk


def _bias_tiles(rel_bias, *, q_start, nq, segments, transposed, scale):
    total = sum(s[1] for s in segments)
    shape = (total, nq) if transposed else (nq, total)
    return pl.pallas_call(
        functools.partial(_bias_kernel, q_start=q_start, nq=nq, segments=segments, transposed=transposed,
                          scale=scale),
        grid=(DIFF_HEADS,),
        in_specs=[pl.BlockSpec(memory_space=pltpu.SMEM)],
        out_specs=pl.BlockSpec((1,) + shape, lambda h: (h, 0, 0)),
        out_shape=jax.ShapeDtypeStruct((DIFF_HEADS,) + shape, F32),
        name="t5_bias",
    )(rel_bias)


def _flash_t(k_ref, vt_ref, acc_ref, bad_ref, streams, qi, blk, prev_bias, diag_bias, diag_mask):
    n = len(streams)

    def scores(first_blk, width):
        start = pl.multiple_of(first_blk * blk, blk)
        return [_mm_nt(k_ref[0, pl.ds(start, width), ksl], q) for q, ksl, _ in streams]

    def step(first_blk, width, ms, biases, mask):
        start = pl.multiple_of(first_blk * blk, blk)
        sts = scores(first_blk, width)
        out, ps, alphas = [], [], []
        for s, st in enumerate(sts):
            if biases is not None:
                st = st + biases[s]()
            if mask is not None:
                st = jnp.where(mask, st, NEG)
            m_new = jnp.maximum(ms[s], jnp.max(st, axis=0, keepdims=True))
            ps.append(jnp.exp2(st - m_new).astype(BF16))
            alphas.append(jnp.exp2(ms[s] - m_new))
            out.append(m_new)
        for s, (_, _, vsl) in enumerate(streams):
            acc_ref[s] = alphas[s] * acc_ref[s] + _mm(vt_ref[0, vsl, pl.ds(start, width)], ps[s])
        return tuple(out)

    def first_step():
        start = pl.multiple_of(qi * blk, blk)
        out, ps = [], []
        for s, st in enumerate(scores(qi, blk)):
            if diag_bias is not None:
                st = st + diag_bias[s]()
            if diag_mask is not None:
                st = jnp.where(diag_mask, st, NEG)
            m = jnp.max(st, axis=0, keepdims=True)
            ps.append(jnp.exp2(st - m).astype(BF16))
            out.append(m)
        for s, (_, _, vsl) in enumerate(streams):
            acc_ref[s] = _mm(vt_ref[0, vsl, pl.ds(start, blk)], ps[s])
        return tuple(out)

    ms = first_step()
    bad_ref[...] = jnp.zeros_like(bad_ref)

    def consume(sts, first_blk, width):
        start = pl.multiple_of(first_blk * blk, blk)
        ps = [jnp.exp2(st - ms[s]).astype(BF16) for s, st in enumerate(sts)]
        worst = None
        for p in ps:
            d = jnp.max(p, axis=0, keepdims=True)
            worst = d if worst is None else jnp.maximum(worst, d)
        bad_ref[...] = jnp.maximum(bad_ref[...], worst.astype(F32))
        for s, (_, _, vsl) in enumerate(streams):
            acc_ref[s] = acc_ref[s] + _mm(vt_ref[0, vsl, pl.ds(start, width)], ps[s])

    if prev_bias is not None:
        @pl.when(qi >= 1)
        def _():
            consume([st + prev_bias[s]() for s, st in enumerate(scores(qi - 1, blk))], qi - 1, blk)

    n_plain = qi if prev_bias is None else jnp.maximum(qi - 1, 0)
    n_wide = n_plain // WIDE_BLOCKS

    def wide_step(j, carry):
        consume(scores(j * WIDE_BLOCKS, WIDE_BLOCKS * blk), j * WIDE_BLOCKS, WIDE_BLOCKS * blk)
        return carry

    lax.fori_loop(0, n_wide, wide_step, 0)

    def single_step(j, carry):
        consume(scores(j, blk), j, blk)
        return carry

    lax.fori_loop(n_wide * WIDE_BLOCKS, n_plain, single_step, 0)

    @pl.when(jnp.max(bad_ref[...]) > 2.0 ** HEADROOM)
    def _():
        c = first_step()
        if prev_bias is not None:
            c = lax.cond(qi >= 1, lambda c: step(qi - 1, blk, c, prev_bias, None), lambda c: c, c)
        lax.fori_loop(0, n_plain, lambda j, c: step(j, blk, c, None, None), c)


def _fox_kernel(qa_ref, ka_ref, vt_ref, z_ref, o_ref, acc_ref, bad_ref, *, blk, nh):
    qi = pl.program_id(2)
    key = lax.broadcasted_iota(jnp.int32, (blk, blk), 0)
    qry = lax.broadcasted_iota(jnp.int32, (blk, blk), 1)
    hsl = [slice(hd * LANES, (hd + 1) * LANES) for hd in range(nh)]
    streams = [(qa_ref[0, :, sl], sl, slice(hd * FVT_ROWS, (hd + 1) * FVT_ROWS)) for hd, sl in enumerate(hsl)]
    _flash_t(ka_ref, vt_ref, acc_ref, bad_ref, streams, qi, blk, None, None, key <= qry)
    for pair in range(nh // 2):
        halves = []
        for par in range(2):
            acc = acc_ref[2 * pair + par]
            halves.append(acc[0:HD, :] / acc[HD:HD + 1, :])
        o = jnp.concatenate(halves, axis=0).T
        o_ref[0, :, hsl[pair]] = (o * _silu(z_ref[0, :, hsl[pair]].astype(F32))).astype(BF16)


def _fox_attention(qa, ka, vt, fz, *, blk, nh):
    b, t, _ = qa.shape
    return pl.pallas_call(
        functools.partial(_fox_kernel, blk=blk, nh=nh),
        grid=(b, FOX_HEADS // nh, t // blk),
        in_specs=[pl.BlockSpec((1, blk, nh * LANES), lambda b, p, i: (b, i, p)),
                  pl.BlockSpec((1, t, nh * LANES), lambda b, p, i: (b, 0, p)),
                  pl.BlockSpec((1, nh * FVT_ROWS, t), lambda b, p, i: (b, p, 0)),
                  pl.BlockSpec((1, blk, nh * HD), lambda b, p, i: (b, i, p))],
        out_specs=pl.BlockSpec((1, blk, nh * HD), lambda b, p, i: (b, i, p)),
        out_shape=jax.ShapeDtypeStruct((b, t, WIDTH), BF16),
        scratch_shapes=[pltpu.VMEM((nh, FVT_ROWS, blk), F32), pltpu.VMEM((1, blk), F32)],
        compiler_params=pltpu.CompilerParams(
            dimension_semantics=("parallel", "parallel", "arbitrary"), vmem_limit_bytes=VMEM_LIMIT),
        name="fox_prompt",
    )(qa, ka, vt, fz)


def _lambda(lq1_ref, lk1_ref, lq2_ref, lk2_ref):
    s1 = jnp.sum(lq1_ref[...] * lk1_ref[...], axis=-1, keepdims=True)
    s2 = jnp.sum(lq2_ref[...] * lk2_ref[...], axis=-1, keepdims=True)
    return jnp.exp(s1) - jnp.exp(s2) + LAMBDA_INIT


def _diff_finish(o, subg, z):
    ms = jnp.mean(o * o, axis=-1, keepdims=True)
    od = o * lax.rsqrt(ms + NORM_EPS) * subg * (1.0 - LAMBDA_INIT)
    return (od * _silu(z)).astype(BF16)


def _diff_kernel(q_ref, k_ref, vt_ref, bias_ref, z_ref, lq1_ref, lk1_ref, lq2_ref, lk2_ref, subg_ref,
                 o_ref, acc_ref, bad_ref, *, blk, nh):
    qi = pl.program_id(2)
    lane = lax.broadcasted_iota(jnp.int32, (blk, LANES), 1)
    streams, prev_bias, diag_bias = [], [], []
    for hd in range(nh):
        q = q_ref[0, :, hd * LANES:(hd + 1) * LANES]
        for mp in range(2):
            streams.append((jnp.where((lane < HD) if mp == 0 else (lane >= HD), q, jnp.zeros_like(q)),
                            slice(hd * LANES, (hd + 1) * LANES), slice(hd * VT_ROWS, (hd + 1) * VT_ROWS)))
            prev_bias.append(lambda hd=hd: bias_ref[hd, 0:blk, :])
            diag_bias.append(lambda hd=hd: bias_ref[hd, blk:2 * blk, :])
    _flash_t(k_ref, vt_ref, acc_ref, bad_ref, streams, qi, blk, prev_bias, diag_bias, None)
    lam = _lambda(lq1_ref, lk1_ref, lq2_ref, lk2_ref)
    for hd in range(nh):
        a0, a1 = acc_ref[2 * hd], acc_ref[2 * hd + 1]
        o_t = a0[0:LANES, :] / a0[LANES:LANES + 1, :] - lam * (a1[0:LANES, :] / a1[LANES:LANES + 1, :])
        sl = slice(hd * LANES, (hd + 1) * LANES)
        o_ref[0, :, sl] = _diff_finish(o_t.T, subg_ref[...], z_ref[0, :, sl].astype(F32))


def _diff_attention(dq, dk, dvt, bias, dz, lq1, lk1, lq2, lk2, subg, *, blk, nh):
    b, t, _ = dq.shape
    small = lambda n: pl.BlockSpec((1, n), lambda b, h, i: (0, 0))
    return pl.pallas_call(
        functools.partial(_diff_kernel, blk=blk, nh=nh),
        grid=(b, DIFF_HEADS // nh, t // blk),
        in_specs=[pl.BlockSpec((1, blk, nh * LANES), lambda b, h, i: (b, i, h)),
                  pl.BlockSpec((1, t, nh * LANES), lambda b, h, i: (b, 0, h)),
                  pl.BlockSpec((1, nh * VT_ROWS, t), lambda b, h, i: (b, h, 0)),
                  pl.BlockSpec((nh, 2 * blk, blk), lambda b, h, i: (h, 0, 0)),
                  pl.BlockSpec((1, blk, nh * LANES), lambda b, h, i: (b, i, h)),
                  small(HD), small(HD), small(HD), small(HD), small(2 * HD)],
        out_specs=pl.BlockSpec((1, blk, nh * LANES), lambda b, h, i: (b, i, h)),
        out_shape=jax.ShapeDtypeStruct((b, t, WIDTH), BF16),
        scratch_shapes=[pltpu.VMEM((2 * nh, VT_ROWS, blk), F32), pltpu.VMEM((1, blk), F32)],
        compiler_params=pltpu.CompilerParams(
            dimension_semantics=("parallel", "parallel", "arbitrary"), vmem_limit_bytes=VMEM_LIMIT),
        name="diff_prompt",
    )(dq, dk, dvt, bias, dz, lq1, lk1, lq2, lk2, subg)


def _suffix_sums(x):
    n = x.shape[1] // LANES
    lane = lax.broadcasted_iota(jnp.int32, (x.shape[0], LANES), 1)
    after = jnp.zeros((x.shape[0], 1), F32)
    pieces = [None] * n
    for blk in reversed(range(n)):
        piece = x[:, blk * LANES:(blk + 1) * LANES]
        y = piece
        shift = 1
        while shift < LANES:
            y = y + jnp.where(lane + shift < LANES, pltpu.roll(y, LANES - shift, 1), 0.0)
            shift *= 2
        pieces[blk] = y - piece + after
        after = after + y[:, 0:1]
    return jnp.concatenate(pieces, axis=1), after


def _pad_rows(x, rows):
    return jnp.concatenate([x, jnp.zeros((rows - x.shape[0], x.shape[1]), x.dtype)], axis=0)


def _joint_softmax_pv(s_p, s_n, vt_p, v_n):
    m = jnp.maximum(jnp.max(s_p, axis=-1, keepdims=True), jnp.max(s_n, axis=-1, keepdims=True))
    p_p = jnp.exp(s_p - m)
    p_n = jnp.exp(s_n - m)
    l = jnp.sum(p_p, axis=-1, keepdims=True) + jnp.sum(p_n, axis=-1, keepdims=True)
    return (_mm_nt(p_p.astype(BF16), vt_p) + _mm(p_n.astype(BF16), v_n)) / l


def _fox_sample_kernel(q_ref, kt_ref, vt_ref, kn_ref, vn_ref, lfp_ref, lfn_ref, z_ref, o_ref, *, nq):
    r_new, total_new = _suffix_sums(lfn_ref[0])
    r_past, _ = _suffix_sums(lfp_ref[0])
    r_past = r_past + total_new
    lane = lax.broadcasted_iota(jnp.int32, (nq, LANES), 1)
    causal = lane <= lax.broadcasted_iota(jnp.int32, (nq, LANES), 0)
    for blk in range(PAIRS):
        sl = slice(blk * LANES, (blk + 1) * LANES)
        q = q_ref[:, sl]
        kt = kt_ref[0, sl, :].astype(BF16)
        vt = vt_ref[0, sl, :].astype(BF16)
        kn = _pad_rows(kn_ref[:, sl], LANES).astype(BF16)
        vn = _pad_rows(vn_ref[:, sl], LANES).astype(BF16)
        outs = []
        for par in range(2):
            hd = 2 * blk + par
            qm = jnp.where((lane < HD) if par == 0 else (lane >= HD), q, jnp.zeros_like(q))
            s_p = _mm(qm, kt) + r_past[hd:hd + 1, :]
            s_n = jnp.where(causal, _mm_nt(qm, kn) + r_new[hd:hd + 1, :], NEG)
            outs.append(_joint_softmax_pv(s_p, s_n, vt, vn))
        o = jnp.where(lane < HD, outs[0], outs[1])
        o_ref[:, sl] = (o * _silu(z_ref[:, sl].astype(F32))).astype(BF16)


def _fox_sample(fq, cache_kt, cache_vt, fk, fv, lf_past_t, lf_new_t, fz, *, nq):
    b, _, past = cache_kt.shape
    new = lambda: pl.BlockSpec((nq, WIDTH), lambda i: (i, 0))
    cache = lambda: pl.BlockSpec((1, WIDTH, past), lambda i: (i, 0, 0))
    return pl.pallas_call(
        functools.partial(_fox_sample_kernel, nq=nq),
        grid=(b,),
        in_specs=[new(), cache(), cache(), new(), new(),
                  pl.BlockSpec((1, FOX_HEADS, past), lambda i: (i, 0, 0)),
                  pl.BlockSpec((1, FOX_HEADS, LANES), lambda i: (i, 0, 0)),
                  new()],
        out_specs=new(),
        out_shape=jax.ShapeDtypeStruct((b * nq, WIDTH), BF16),
        compiler_params=pltpu.CompilerParams(
            dimension_semantics=("parallel",), vmem_limit_bytes=VMEM_LIMIT),
        name="fox_sample",
    )(fq, cache_kt, cache_vt, fk, fv, lf_past_t, lf_new_t, fz)


def _diff_sample_kernel(q_ref, kt_ref, v_ref, kn_ref, vn_ref, bias_ref, z_ref,
                        lq1_ref, lk1_ref, lq2_ref, lk2_ref, subg_ref, o_ref, *, nq, past):
    lane = lax.broadcasted_iota(jnp.int32, (nq, LANES), 1)
    lam = _lambda(lq1_ref, lk1_ref, lq2_ref, lk2_ref)
    for hd in range(DIFF_HEADS):
        sl = slice(hd * LANES, (hd + 1) * LANES)
        q = q_ref[:, sl]
        kt = kt_ref[0, sl, :].astype(BF16)
        vt = v_ref[0, pl.ds(hd, past, stride=DIFF_HEADS), :].T.astype(BF16)
        kn = _pad_rows(kn_ref[:, sl], LANES).astype(BF16)
        vn = _pad_rows(vn_ref[:, sl], LANES).astype(BF16)
        outs = []
        for mp in range(2):
            qm = jnp.where((lane < HD) if mp == 0 else (lane >= HD), q, jnp.zeros_like(q))
            s_p = _mm(qm, kt) + bias_ref[hd, :, 0:past]
            s_n = _mm_nt(qm, kn) + bias_ref[hd, :, past:past + LANES]
            outs.append(_joint_softmax_pv(s_p, s_n, vt, vn))
        o_ref[:, sl] = _diff_finish(outs[0] - lam * outs[1], subg_ref[...], z_ref[:, sl].astype(F32))


def _diff_sample(dq, cache_kt, cache_v, dk, dv, bias, dz, lq1, lk1, lq2, lk2, subg, *, nq):
    b, _, past = cache_kt.shape
    new = lambda: pl.BlockSpec((nq, WIDTH), lambda i: (i, 0))
    small = lambda n: pl.BlockSpec((1, n), lambda i: (0, 0))
    return pl.pallas_call(
        functools.partial(_diff_sample_kernel, nq=nq, past=past),
        grid=(b,),
        in_specs=[new(), pl.BlockSpec((1, WIDTH, past), lambda i: (i, 0, 0)),
                  pl.BlockSpec((1, past * DIFF_HEADS, 2 * HD), lambda i: (i, 0, 0)), new(), new(),
                  pl.BlockSpec((DIFF_HEADS, nq, past + LANES), lambda i: (0, 0, 0)),
                  new(), small(HD), small(HD), small(HD), small(HD), small(2 * HD)],
        out_specs=new(),
        out_shape=jax.ShapeDtypeStruct((b * nq, WIDTH), BF16),
        compiler_params=pltpu.CompilerParams(
            dimension_semantics=("parallel",), vmem_limit_bytes=VMEM_LIMIT),
        name="diff_sample",
    )(dq, cache_kt, cache_v, dk, dv, bias, dz, lq1, lk1, lq2, lk2, subg)


def _out_kernel(yf_ref, yd_ref, ga_ref, gb_ref, x_ref, wuf_ref, wud_ref, wo_ref, g_ref, o_ref):
    merged = (ga_ref[...].astype(F32) * _mm(yf_ref[...], wuf_ref[...])
              + gb_ref[...].astype(F32) * _mm(yd_ref[...], wud_ref[...]))
    out = x_ref[...] + _mm(merged.astype(BF16), wo_ref[...])
    ms = jnp.mean(out * out, axis=-1, keepdims=True)
    o_ref[...] = out * lax.rsqrt(ms + NORM_EPS) * g_ref[...]


def _output(yf, yd, ga, gb, x2d, wuf, wud, wo, g, *, tm):
    rows = x2d.shape[0]
    assert rows % tm == 0
    row_spec = lambda n: pl.BlockSpec((tm, n), lambda i: (i, 0))
    const = lambda shape: pl.BlockSpec(shape, lambda i: (0, 0))
    return pl.pallas_call(
        _out_kernel,
        grid=(rows // tm,),
        in_specs=[row_spec(WIDTH), row_spec(WIDTH), row_spec(D_MODEL), row_spec(D_MODEL), row_spec(D_MODEL),
                  const((WIDTH, D_MODEL)), const((WIDTH, D_MODEL)), const((D_MODEL, D_MODEL)),
                  const((1, D_MODEL))],
        out_specs=row_spec(D_MODEL),
        out_shape=jax.ShapeDtypeStruct((rows, D_MODEL), F32),
        compiler_params=pltpu.CompilerParams(
            dimension_semantics=("parallel",), vmem_limit_bytes=VMEM_LIMIT),
        name="out_proj",
    )(yf, yd, ga, gb, x2d, wuf, wud, wo, g)


def _forget_lane_heads():
    lanes = np.arange(LANES)
    heads = (lanes % HD) // FL_SLOT
    heads[:FOX_HEADS] = np.arange(FOX_HEADS)
    return heads


def _pack_w_in(w_in, b_forget):
    sizes = (WIDTH, WIDTH, WIDTH, WIDTH, FOX_HEADS, WIDTH, WIDTH, WIDTH, WIDTH, D_MODEL, D_MODEL)
    offs = np.cumsum((0,) + sizes)
    fq, fk, fv, fz, fl, dq, dk, dv, dz, ga, gb = (w_in[:, offs[i]:offs[i + 1]] for i in range(len(sizes)))
    heads = _forget_lane_heads()
    w_all = jnp.concatenate([fq, fk, fv, fz, dq, dk, dv, dz, ga, gb, fl[:, heads]], axis=1).astype(BF16)
    return w_all, b_forget[heads][None, :]


def kernel(x_prompt, x_sample, cache_fox_k, cache_fox_v, cache_fox_logf, cache_diff_k, cache_diff_v,
           norm_in_g, w_in, b_forget, lambda_q1, lambda_k1, lambda_q2, lambda_k2, subln_g,
           w_up_fox, w_up_diff, w_o, rel_bias, final_norm_g):
    bp, tp, _ = x_prompt.shape
    bs, ts, _ = x_sample.shape
    past = cache_fox_k.shape[2]
    blk = 256
    tm = 512

    w_all, bfl = _pack_w_in(w_in[0], b_forget[0])
    g_in = norm_in_g[0][None, :]
    g_out = final_norm_g[None, :]
    wuf, wud, wo = w_up_fox[0].astype(BF16), w_up_diff[0].astype(BF16), w_o[0].astype(BF16)
    lq1, lk1, lq2, lk2 = (a[0][None, :] for a in (lambda_q1, lambda_k1, lambda_q2, lambda_k2))
    subg = subln_g[0][None, :]

    xp = x_prompt.reshape(bp * tp, D_MODEL)
    (fkt, fvt, lft, dkt, dv4, qa, ka, vtf, fz, dq, dkb, vtd, dz, ga, gb) = _project(
        xp, g_in, w_all, bfl, seq_len=tp, aug=True, tm=tm)
    r3 = lambda a: a.reshape(bp, tp, a.shape[-1])
    yf = _fox_attention(r3(qa), r3(ka), vtf, r3(fz), blk=blk, nh=FOX_STEP_HEADS)
    bias_p = _bias_tiles(rel_bias, q_start=blk, nq=blk, segments=((0, blk, blk), (blk, blk, blk)),
                         transposed=True, scale=LOG2E)
    yd = _diff_attention(r3(dq), r3(dkb), vtd, bias_p, r3(dz), lq1, lk1, lq2, lk2, subg, blk=blk,
                         nh=DIFF_STEP_HEADS)
    y_p = _output(yf.reshape(bp * tp, WIDTH), yd.reshape(bp * tp, WIDTH), ga, gb, xp, wuf, wud, wo, g_out, tm=tm)
    fox_k_p = jnp.transpose(fkt.reshape(bp, FOX_HEADS, HD, tp), (0, 3, 1, 2))[None]
    fox_v_p = jnp.transpose(fvt.reshape(bp, FOX_HEADS, HD, tp), (0, 3, 1, 2))[None]
    fox_lf_p = jnp.transpose(lft, (0, 2, 1))[None]
    diff_k_p = jnp.transpose(dkt.reshape(bp, DIFF_HEADS, 2, HD, tp), (0, 4, 1, 2, 3))[None]
    diff_v_p = dv4.reshape(1, bp, tp, DIFF_HEADS, 2 * HD)

    xs = x_sample.reshape(bs * ts, D_MODEL)
    (fk_s, fv_s, lf_s, dk_s, dv_s, fq_s, fz_s, dq_s, dz_s, ga_s, gb_s) = _project(
        xs, g_in, w_all, bfl, seq_len=ts, aug=False, tm=tm)
    fox_kt = jnp.transpose(cache_fox_k[0], (0, 2, 3, 1)).reshape(bs, WIDTH, past)
    fox_vt = jnp.transpose(cache_fox_v[0], (0, 2, 3, 1)).reshape(bs, WIDTH, past)
    diff_kt = jnp.transpose(cache_diff_k[0], (0, 2, 3, 4, 1)).reshape(bs, WIDTH, past)
    diff_v = cache_diff_v[0].reshape(bs, past * DIFF_HEADS, 2 * HD)
    lf_past_t = jnp.swapaxes(cache_fox_logf[0], 1, 2)
    lf_new_t = jnp.pad(jnp.swapaxes(lf_s.reshape(bs, ts, FOX_HEADS), 1, 2), ((0, 0), (0, 0), (0, LANES - ts)))
    yf_s = _fox_sample(fq_s, fox_kt, fox_vt, fk_s, fv_s, lf_past_t, lf_new_t, fz_s, nq=ts)
    bias_s = _bias_tiles(rel_bias, q_start=past, nq=ts, segments=((0, past, past), (past, LANES, ts)),
                         transposed=False, scale=1.0)
    yd_s = _diff_sample(dq_s, diff_kt, diff_v, dk_s, dv_s, bias_s, dz_s, lq1, lk1, lq2, lk2, subg, nq=ts)
    y_s = _output(yf_s, yd_s, ga_s, gb_s, xs, wuf, wud, wo, g_out, tm=tm)

    return (y_p.reshape(bp, tp, D_MODEL), y_s.reshape(bs, ts, D_MODEL),
            fox_k_p, fox_v_p, fox_lf_p, diff_k_p, diff_v_p,
            fk_s.reshape(1, bs, ts, FOX_HEADS, HD), fv_s.reshape(1, bs, ts, FOX_HEADS, HD),
            lf_s.reshape(1, bs, ts, FOX_HEADS),
            dk_s.reshape(1, bs, ts, DIFF_HEADS, 2, HD), dv_s.reshape(1, bs, ts, DIFF_HEADS, 2 * HD))
```

```python
import functools
import math

import numpy as np
import jax
import jax.numpy as jnp
from jax import lax
from jax.experimental import pallas as pl
from jax.experimental.pallas import tpu as pltpu

F32 = jnp.float32
BF16 = jnp.bfloat16

D_MODEL = 1024
HD = 64
FOX_HEADS = 8
DIFF_HEADS = 4
WIDTH = 512
CHUNK = 64
CHUNK_SHIFT = 6
N_BUCKETS = 32
MAX_DISTANCE = 128
NORM_EPS = 1e-6
SCALE = HD ** -0.5
NEG = -1e30
LAMBDA_INIT = 0.8 - 0.6 * math.exp(-0.3 * 0)
LOG2E = math.log2(math.e)

LANES = 128
PAIRS = WIDTH // LANES
VT_PAD = 16
VT_ROWS = LANES + VT_PAD
FVT_ROWS = HD + VT_PAD

_O_FQ, _O_FK, _O_FV, _O_FZ = 0, 512, 1024, 1536
_O_DQ, _O_DK, _O_DV, _O_DZ = 2048, 2560, 3072, 3584
_O_GA, _O_GB = 4096, 5120
_O_FL = 6144
_N_ALL = 6272
FL_SLOT = 8

VMEM_LIMIT = 56 * 1024 * 1024

FOX_STEP_HEADS = 8
DIFF_STEP_HEADS = 4
HEADROOM = 64.0
WIDE_BLOCKS = 2


def _t5_thresholds():
    nb = N_BUCKETS // 2
    max_exact = nb // 2
    n = np.arange(0, 4 * MAX_DISTANCE)
    large = max_exact + (np.log(np.maximum(n, 1).astype(np.float32) / max_exact)
                         / math.log(MAX_DISTANCE / max_exact) * (nb - max_exact)).astype(np.int32)
    large = np.minimum(large, nb - 1)
    thr = [int(np.argmax(large >= b)) for b in range(max_exact + 1, nb)]
    return max_exact, tuple(thr)


_MAX_EXACT, _T5_THR = _t5_thresholds()
FAR_DIST = _T5_THR[-1]


def _mm(a, b):
    return jnp.dot(a, b, preferred_element_type=F32)


def _mm_nt(a, b):
    return lax.dot_general(a, b, (((1,), (1,)), ((), ())), preferred_element_type=F32)


def _log_sigmoid(x):
    return -(jnp.maximum(-x, 0.0) + jnp.log1p(jnp.exp(-jnp.abs(x))))


def _sigmoid(x):
    return 1.0 / (1.0 + jnp.exp(-x))


def _silu(x):
    return x * _sigmoid(x)


def _split3(x):
    p1 = x.astype(BF16).astype(F32)
    r1 = x - p1
    p2 = r1.astype(BF16).astype(F32)
    p3 = r1 - p2
    return p1, p2, p3


def _proj_kernel(x_ref, g_ref, w_ref, bfl_ref, *refs, tm, seq_len, aug):
    if aug:
        (fkt_ref, fvt_ref, lft_ref, dkt_ref, dv4_ref, qa_ref, ka_ref, vtf_ref,
         fz_ref, dq_ref, dkb_ref, vtd_ref, dz_ref, ga_ref, gb_ref, carry_ref) = refs
    else:
        (fk_ref, fv_ref, lf_ref, dk_ref, dv_ref, fq_ref,
         fz_ref, dq_ref, dz_ref, ga_ref, gb_ref) = refs

    x = x_ref[...]
    ms = jnp.mean(x * x, axis=-1, keepdims=True)
    h = (x * lax.rsqrt(ms + NORM_EPS) * g_ref[...]).astype(BF16)

    def grp(off, n=WIDTH):
        return _mm(h, w_ref[:, off:off + n])

    if not aug:
        fq_ref[...] = (grp(_O_FQ) * SCALE).astype(BF16)
        fk_ref[...] = grp(_O_FK)
        fv_ref[...] = grp(_O_FV)
        fz_ref[...] = grp(_O_FZ).astype(BF16)
        dq_ref[...] = (grp(_O_DQ) * SCALE).astype(BF16)
        dk_ref[...] = grp(_O_DK)
        dv_ref[...] = grp(_O_DV)
        dz_ref[...] = grp(_O_DZ).astype(BF16)
        ga_ref[...] = _sigmoid(grp(_O_GA, D_MODEL)).astype(BF16)
        gb_ref[...] = _sigmoid(grp(_O_GB, D_MODEL)).astype(BF16)
        lf_ref[...] = _log_sigmoid(grp(_O_FL, LANES) + bfl_ref[...])[:, :FOX_HEADS]
        return

    lf = _log_sigmoid(grp(_O_FL, LANES) + bfl_ref[...])
    lft_ref[0] = lf.T[:FOX_HEADS, :]
    row = lax.broadcasted_iota(jnp.int32, (tm, LANES), 0)
    c = lf
    shift = 1
    while shift < tm:
        c = c + jnp.where(row >= shift, pltpu.roll(c, shift, 0), 0.0)
        shift *= 2
    tiles_per_seq = seq_len // tm

    @pl.when(pl.program_id(0) % tiles_per_seq == 0)
    def _():
        carry_ref[...] = jnp.zeros_like(carry_ref)

    c = c + carry_ref[...]
    carry_ref[...] = c[tm - 1:tm, :]
    c_hi, c_mid, c_lo = _split3(c * LOG2E)

    fq = grp(_O_FQ) * (SCALE * LOG2E)
    fk = grp(_O_FK)
    fv = grp(_O_FV)
    fvt = fv.T
    fkt_ref[0] = fk.T
    fvt_ref[0] = fvt
    tail = jnp.where(lax.broadcasted_iota(jnp.int32, (VT_PAD, tm), 0) == 0, 1.0, 0.0).astype(BF16)
    lane = lax.broadcasted_iota(jnp.int32, (tm, LANES), 1)
    pos = lane & (FL_SLOT - 1)
    q_slots = jnp.where(pos == 0, c_hi, jnp.where(pos == 1, c_mid, jnp.where(
        pos == 2, c_lo, jnp.where(pos < 6, 1.0, 0.0))))
    k_slots = jnp.where(pos < 3, 1.0, jnp.where(pos == 3, -c_hi, jnp.where(
        pos == 4, -c_mid, jnp.where(pos == 5, -c_lo, 0.0))))
    for hd in range(FOX_HEADS):
        blk, par = divmod(hd, 2)
        sl = slice(blk * LANES, (blk + 1) * LANES)
        data = (lane < HD) if par == 0 else (lane >= HD)
        slot0 = (1 - par) * HD + hd * FL_SLOT
        own = (lane >= slot0) & (lane < slot0 + FL_SLOT)
        q_aug = jnp.where(own, q_slots, 0.0)
        k_aug = jnp.where(own, k_slots, 0.0)
        osl = slice(hd * LANES, (hd + 1) * LANES)
        qa_ref[:, osl] = jnp.where(data, fq[:, sl], q_aug).astype(BF16)
        ka_ref[:, osl] = jnp.where(data, fk[:, sl], k_aug).astype(BF16)
        base = hd * FVT_ROWS
        vtf_ref[0, base:base + HD, :] = fvt[hd * HD:(hd + 1) * HD, :].astype(BF16)
        vtf_ref[0, base + HD:base + FVT_ROWS, :] = tail

    ga_ref[...] = _sigmoid(grp(_O_GA, D_MODEL)).astype(BF16)
    gb_ref[...] = _sigmoid(grp(_O_GB, D_MODEL)).astype(BF16)

    dk = grp(_O_DK)
    dkt_ref[0] = dk.T
    dkb_ref[...] = dk.astype(BF16)
    dv = grp(_O_DV)
    dv4_ref[...] = dv.reshape(tm, DIFF_HEADS, 2 * HD)
    for hd in range(DIFF_HEADS):
        base = hd * VT_ROWS
        vtd_ref[0, base:base + LANES, :] = dv[:, hd * LANES:(hd + 1) * LANES].T.astype(BF16)
        vtd_ref[0, base + LANES:base + VT_ROWS, :] = tail

    dq_ref[...] = (grp(_O_DQ) * (SCALE * LOG2E)).astype(BF16)
    fz_ref[...] = grp(_O_FZ).astype(BF16)
    dz_ref[...] = grp(_O_DZ).astype(BF16)


def _project(x2d, g, w_all, bfl, *, seq_len, aug, tm):
    rows = x2d.shape[0]
    assert rows % tm == 0
    row_spec = lambda n: pl.BlockSpec((tm, n), lambda i: (i, 0))
    const = lambda shape: pl.BlockSpec(shape, lambda i: (0, 0))
    f32o = lambda n: jax.ShapeDtypeStruct((rows, n), F32)
    b16o = lambda n: jax.ShapeDtypeStruct((rows, n), BF16)
    if aug:
        assert seq_len % tm == 0
        wide = FOX_HEADS * LANES
        tps = seq_len // tm
        t_shape = lambda n, dt: jax.ShapeDtypeStruct((rows // seq_len, n, seq_len), dt)
        t_spec = lambda n: pl.BlockSpec((1, n, tm), lambda i: (i // tps, 0, i % tps))
        out_shape = [t_shape(WIDTH, F32), t_shape(WIDTH, F32), t_shape(FOX_HEADS, F32), t_shape(WIDTH, F32),
                     jax.ShapeDtypeStruct((rows, DIFF_HEADS, 2 * HD), F32)]
        out_specs = [t_spec(WIDTH), t_spec(WIDTH), t_spec(FOX_HEADS), t_spec(WIDTH),
                     pl.BlockSpec((tm, DIFF_HEADS, 2 * HD), lambda i: (i, 0, 0))]
        out_shape += ([b16o(wide)] * 2 + [t_shape(FOX_HEADS * FVT_ROWS, BF16)] + [b16o(WIDTH)] * 3
                      + [t_shape(DIFF_HEADS * VT_ROWS, BF16)] + [b16o(WIDTH)] + [b16o(D_MODEL)] * 2)
        out_specs += ([row_spec(wide)] * 2 + [t_spec(FOX_HEADS * FVT_ROWS)] + [row_spec(WIDTH)] * 3
                      + [t_spec(DIFF_HEADS * VT_ROWS)] + [row_spec(WIDTH)] + [row_spec(D_MODEL)] * 2)
        scratch = [pltpu.VMEM((1, LANES), F32)]
    else:
        out_shape = [f32o(WIDTH), f32o(WIDTH), f32o(FOX_HEADS), f32o(WIDTH), f32o(WIDTH)]
        out_specs = [row_spec(WIDTH), row_spec(WIDTH), row_spec(FOX_HEADS), row_spec(WIDTH), row_spec(WIDTH)]
        out_shape += [b16o(WIDTH)] * 4 + [b16o(D_MODEL)] * 2
        out_specs += [row_spec(WIDTH)] * 4 + [row_spec(D_MODEL)] * 2
        scratch = []
    return pl.pallas_call(
        functools.partial(_proj_kernel, tm=tm, seq_len=seq_len, aug=aug),
        grid=(rows // tm,),
        in_specs=[row_spec(D_MODEL), const((1, D_MODEL)),
                  pl.BlockSpec((D_MODEL, _N_ALL), lambda i: (0, 0), pipeline_mode=pl.Buffered(1)),
                  const((1, LANES))],
        out_specs=out_specs,
        out_shape=out_shape,
        scratch_shapes=scratch,
        compiler_params=pltpu.CompilerParams(
            dimension_semantics=("arbitrary",), vmem_limit_bytes=VMEM_LIMIT),
        name="proj_aug" if aug else "proj_plain",
    )(x2d, g, w_all, bfl)


def _bias_kernel(tbl_ref, o_ref, *, q_start, nq, segments, transposed, scale):
    hd = pl.program_id(0)
    far = tbl_ref[N_BUCKETS // 2 - 1, hd]
    q_axis, k_axis = (1, 0) if transposed else (0, 1)
    off = 0
    for k_start, nk, valid in segments:
        shape = (nk, nq) if transposed else (nq, nk)
        qpos = q_start + lax.broadcasted_iota(jnp.int32, shape, q_axis)
        kidx = lax.broadcasted_iota(jnp.int32, shape, k_axis)
        kpos = k_start + kidx
        rel = kpos - qpos
        n = jnp.abs(rel)
        large = jnp.full(shape, _MAX_EXACT, jnp.int32)
        for thr in _T5_THR:
            large = large + jnp.where(n >= thr, 1, 0)
        bucket = jnp.where(rel > 0, N_BUCKETS // 2, 0) + jnp.where(n < _MAX_EXACT, n, large)
        val = jnp.zeros(shape, F32)
        for b in range(N_BUCKETS):
            val = jnp.where(bucket == b, tbl_ref[b, hd], val)
        val = (val - far) * scale
        visible = ((kpos >> CHUNK_SHIFT) <= (qpos >> CHUNK_SHIFT)) & (kidx < valid)
        val = jnp.where(visible, val, NEG)
        if transposed:
            o_ref[0, off:off + nk, :] = val
        else:
            o_ref[0, :, off:off + nk] = val
        off += nk


def _bias_tiles(rel_bias, *, q_start, nq, segments, transposed, scale):
    total = sum(s[1] for s in segments)
    shape = (total, nq) if transposed else (nq, total)
    return pl.pallas_call(
        functools.partial(_bias_kernel, q_start=q_start, nq=nq, segments=segments, transposed=transposed,
                          scale=scale),
        grid=(DIFF_HEADS,),
        in_specs=[pl.BlockSpec(memory_space=pltpu.SMEM)],
        out_specs=pl.BlockSpec((1,) + shape, lambda h: (h, 0, 0)),
        out_shape=jax.ShapeDtypeStruct((DIFF_HEADS,) + shape, F32),
        name="t5_bias",
    )(rel_bias)


def _flash_t(k_ref, vt_ref, acc_ref, bad_ref, streams, qi, blk, prev_bias, diag_bias, diag_mask):
    n = len(streams)

    def scores(first_blk, width):
        start = pl.multiple_of(first_blk * blk, blk)
        return [_mm_nt(k_ref[0, pl.ds(start, width), ksl], q) for q, ksl, _ in streams]

    def step(first_blk, width, ms, biases, mask):
        start = pl.multiple_of(first_blk * blk, blk)
        sts = scores(first_blk, width)
        out, ps, alphas = [], [], []
        for s, st in enumerate(sts):
            if biases is not None:
                st = st + biases[s]()
            if mask is not None:
                st = jnp.where(mask, st, NEG)
            m_new = jnp.maximum(ms[s], jnp.max(st, axis=0, keepdims=True))
            ps.append(jnp.exp2(st - m_new).astype(BF16))
            alphas.append(jnp.exp2(ms[s] - m_new))
            out.append(m_new)
        for s, (_, _, vsl) in enumerate(streams):
            acc_ref[s] = alphas[s] * acc_ref[s] + _mm(vt_ref[0, vsl, pl.ds(start, width)], ps[s])
        return tuple(out)

    def first_step():
        start = pl.multiple_of(qi * blk, blk)
        out, ps = [], []
        for s, st in enumerate(scores(qi, blk)):
            if diag_bias is not None:
                st = st + diag_bias[s]()
            if diag_mask is not None:
                st = jnp.where(diag_mask, st, NEG)
            m = jnp.max(st, axis=0, keepdims=True)
            ps.append(jnp.exp2(st - m).astype(BF16))
            out.append(m)
        for s, (_, _, vsl) in enumerate(streams):
            acc_ref[s] = _mm(vt_ref[0, vsl, pl.ds(start, blk)], ps[s])
        return tuple(out)

    ms = first_step()
    bad_ref[...] = jnp.zeros_like(bad_ref)

    def consume(sts, first_blk, width):
        start = pl.multiple_of(first_blk * blk, blk)
        ps = [jnp.exp2(st - ms[s]).astype(BF16) for s, st in enumerate(sts)]
        worst = None
        for p in ps:
            d = jnp.max(p, axis=0, keepdims=True)
            worst = d if worst is None else jnp.maximum(worst, d)
        bad_ref[...] = jnp.maximum(bad_ref[...], worst.astype(F32))
        for s, (_, _, vsl) in enumerate(streams):
            acc_ref[s] = acc_ref[s] + _mm(vt_ref[0, vsl, pl.ds(start, width)], ps[s])

    if prev_bias is not None:
        @pl.when(qi >= 1)
        def _():
            consume([st + prev_bias[s]() for s, st in enumerate(scores(qi - 1, blk))], qi - 1, blk)

    n_plain = qi if prev_bias is None else jnp.maximum(qi - 1, 0)
    n_wide = n_plain // WIDE_BLOCKS

    def wide_step(j, carry):
        consume(scores(j * WIDE_BLOCKS, WIDE_BLOCKS * blk), j * WIDE_BLOCKS, WIDE_BLOCKS * blk)
        return carry

    lax.fori_loop(0, n_wide, wide_step, 0)

    def single_step(j, carry):
        consume(scores(j, blk), j, blk)
        return carry

    lax.fori_loop(n_wide * WIDE_BLOCKS, n_plain, single_step, 0)

    @pl.when(jnp.max(bad_ref[...]) > 2.0 ** HEADROOM)
    def _():
        c = first_step()
        if prev_bias is not None:
            c = lax.cond(qi >= 1, lambda c: step(qi - 1, blk, c, prev_bias, None), lambda c: c, c)
        lax.fori_loop(0, n_plain, lambda j, c: step(j, blk, c, None, None), c)


def _fox_kernel(qa_ref, ka_ref, vt_ref, z_ref, o_ref, acc_ref, bad_ref, *, blk, nh):
    qi = pl.program_id(2)
    key = lax.broadcasted_iota(jnp.int32, (blk, blk), 0)
    qry = lax.broadcasted_iota(jnp.int32, (blk, blk), 1)
    hsl = [slice(hd * LANES, (hd + 1) * LANES) for hd in range(nh)]
    streams = [(qa_ref[0, :, sl], sl, slice(hd * FVT_ROWS, (hd + 1) * FVT_ROWS)) for hd, sl in enumerate(hsl)]
    _flash_t(ka_ref, vt_ref, acc_ref, bad_ref, streams, qi, blk, None, None, key <= qry)
    for pair in range(nh // 2):
        halves = []
        for par in range(2):
            acc = acc_ref[2 * pair + par]
            halves.append(acc[0:HD, :] / acc[HD:HD + 1, :])
        o = jnp.concatenate(halves, axis=0).T
        o_ref[0, :, hsl[pair]] = (o * _silu(z_ref[0, :, hsl[pair]].astype(F32))).astype(BF16)


def _fox_attention(qa, ka, vt, fz, *, blk, nh):
    b, t, _ = qa.shape
    return pl.pallas_call(
        functools.partial(_fox_kernel, blk=blk, nh=nh),
        grid=(b, FOX_HEADS // nh, t // blk),
        in_specs=[pl.BlockSpec((1, blk, nh * LANES), lambda b, p, i: (b, i, p)),
                  pl.BlockSpec((1, t, nh * LANES), lambda b, p, i: (b, 0, p)),
                  pl.BlockSpec((1, nh * FVT_ROWS, t), lambda b, p, i: (b, p, 0)),
                  pl.BlockSpec((1, blk, nh * HD), lambda b, p, i: (b, i, p))],
        out_specs=pl.BlockSpec((1, blk, nh * HD), lambda b, p, i: (b, i, p)),
        out_shape=jax.ShapeDtypeStruct((b, t, WIDTH), BF16),
        scratch_shapes=[pltpu.VMEM((nh, FVT_ROWS, blk), F32), pltpu.VMEM((1, blk), F32)],
        compiler_params=pltpu.CompilerParams(
            dimension_semantics=("parallel", "parallel", "arbitrary"), vmem_limit_bytes=VMEM_LIMIT),
        name="fox_prompt",
    )(qa, ka, vt, fz)


def _lambda(lq1_ref, lk1_ref, lq2_ref, lk2_ref):
    s1 = jnp.sum(lq1_ref[...] * lk1_ref[...], axis=-1, keepdims=True)
    s2 = jnp.sum(lq2_ref[...] * lk2_ref[...], axis=-1, keepdims=True)
    return jnp.exp(s1) - jnp.exp(s2) + LAMBDA_INIT


def _diff_finish(o, subg, z):
    ms = jnp.mean(o * o, axis=-1, keepdims=True)
    od = o * lax.rsqrt(ms + NORM_EPS) * subg * (1.0 - LAMBDA_INIT)
    return (od * _silu(z)).astype(BF16)


def _diff_kernel(q_ref, k_ref, vt_ref, bias_ref, z_ref, lq1_ref, lk1_ref, lq2_ref, lk2_ref, subg_ref,
                 o_ref, acc_ref, bad_ref, *, blk, nh):
    qi = pl.program_id(2)
    lane = lax.broadcasted_iota(jnp.int32, (blk, LANES), 1)
    streams, prev_bias, diag_bias = [], [], []
    for hd in range(nh):
        q = q_ref[0, :, hd * LANES:(hd + 1) * LANES]
        for mp in range(2):
            streams.append((jnp.where((lane < HD) if mp == 0 else (lane >= HD), q, jnp.zeros_like(q)),
                            slice(hd * LANES, (hd + 1) * LANES), slice(hd * VT_ROWS, (hd + 1) * VT_ROWS)))
            prev_bias.append(lambda hd=hd: bias_ref[hd, 0:blk, :])
            diag_bias.append(lambda hd=hd: bias_ref[hd, blk:2 * blk, :])
    _flash_t(k_ref, vt_ref, acc_ref, bad_ref, streams, qi, blk, prev_bias, diag_bias, None)
    lam = _lambda(lq1_ref, lk1_ref, lq2_ref, lk2_ref)
    for hd in range(nh):
        a0, a1 = acc_ref[2 * hd], acc_ref[2 * hd + 1]
        o_t = a0[0:LANES, :] / a0[LANES:LANES + 1, :] - lam * (a1[0:LANES, :] / a1[LANES:LANES + 1, :])
        sl = slice(hd * LANES, (hd + 1) * LANES)
        o_ref[0, :, sl] = _diff_finish(o_t.T, subg_ref[...], z_ref[0, :, sl].astype(F32))


def _diff_attention(dq, dk, dvt, bias, dz, lq1, lk1, lq2, lk2, subg, *, blk, nh):
    b, t, _ = dq.shape
    small = lambda n: pl.BlockSpec((1, n), lambda b, h, i: (0, 0))
    return pl.pallas_call(
        functools.partial(_diff_kernel, blk=blk, nh=nh),
        grid=(b, DIFF_HEADS // nh, t // blk),
        in_specs=[pl.BlockSpec((1, blk, nh * LANES), lambda b, h, i: (b, i, h)),
                  pl.BlockSpec((1, t, nh * LANES), lambda b, h, i: (b, 0, h)),
                  pl.BlockSpec((1, nh * VT_ROWS, t), lambda b, h, i: (b, h, 0)),
                  pl.BlockSpec((nh, 2 * blk, blk), lambda b, h, i: (h, 0, 0)),
                  pl.BlockSpec((1, blk, nh * LANES), lambda b, h, i: (b, i, h)),
                  small(HD), small(HD), small(HD), small(HD), small(2 * HD)],
        out_specs=pl.BlockSpec((1, blk, nh * LANES), lambda b, h, i: (b, i, h)),
        out_shape=jax.ShapeDtypeStruct((b, t, WIDTH), BF16),
        scratch_shapes=[pltpu.VMEM((2 * nh, VT_ROWS, blk), F32), pltpu.VMEM((1, blk), F32)],
        compiler_params=pltpu.CompilerParams(
            dimension_semantics=("parallel", "parallel", "arbitrary"), vmem_limit_bytes=VMEM_LIMIT),
        name="diff_prompt",
    )(dq, dk, dvt, bias, dz, lq1, lk1, lq2, lk2, subg)


def _suffix_sums(x):
    n = x.shape[1] // LANES
    lane = lax.broadcasted_iota(jnp.int32, (x.shape[0], LANES), 1)
    after = jnp.zeros((x.shape[0], 1), F32)
    pieces = [None] * n
    for blk in reversed(range(n)):
        piece = x[:, blk * LANES:(blk + 1) * LANES]
        y = piece
        shift = 1
        while shift < LANES:
            y = y + jnp.where(lane + shift < LANES, pltpu.roll(y, LANES - shift, 1), 0.0)
            shift *= 2
        pieces[blk] = y - piece + after
        after = after + y[:, 0:1]
    return jnp.concatenate(pieces, axis=1), after


def _pad_rows(x, rows):
    return jnp.concatenate([x, jnp.zeros((rows - x.shape[0], x.shape[1]), x.dtype)], axis=0)


def _joint_softmax(s_p, s_n):
    m = jnp.maximum(jnp.max(s_p, axis=-1, keepdims=True), jnp.max(s_n, axis=-1, keepdims=True))
    p_p = jnp.exp(s_p - m)
    p_n = jnp.exp(s_n - m)
    l = jnp.sum(p_p, axis=-1, keepdims=True) + jnp.sum(p_n, axis=-1, keepdims=True)
    return p_p.astype(BF16), p_n.astype(BF16), l


def _joint_pv(p_p, p_n, l, vt_p, v_n):
    return (_mm_nt(p_p, vt_p) + _mm(p_n, v_n)) / l


def _fox_sample_kernel(q_ref, kt_ref, vt_ref, kn_ref, vn_ref, lfp_ref, lfn_ref, z_ref, o_ref, *, nq):
    r_new, total_new = _suffix_sums(lfn_ref[0])
    r_past, _ = _suffix_sums(lfp_ref[0])
    r_past = r_past + total_new
    lane = lax.broadcasted_iota(jnp.int32, (nq, LANES), 1)
    causal = lane <= lax.broadcasted_iota(jnp.int32, (nq, LANES), 0)
    sls = [slice(blk * LANES, (blk + 1) * LANES) for blk in range(PAIRS)]
    scores = []
    for blk, sl in enumerate(sls):
        q = q_ref[:, sl]
        kt = kt_ref[0, sl, :].astype(BF16)
        kn = _pad_rows(kn_ref[:, sl], LANES).astype(BF16)
        for par in range(2):
            hd = 2 * blk + par
            qm = jnp.where((lane < HD) if par == 0 else (lane >= HD), q, jnp.zeros_like(q))
            scores.append((_mm(qm, kt) + r_past[hd:hd + 1, :],
                           jnp.where(causal, _mm_nt(qm, kn) + r_new[hd:hd + 1, :], NEG)))
    weights = [_joint_softmax(s_p, s_n) for s_p, s_n in scores]
    for blk, sl in enumerate(sls):
        vt = vt_ref[0, sl, :].astype(BF16)
        vn = _pad_rows(vn_ref[:, sl], LANES).astype(BF16)
        outs = [_joint_pv(*weights[2 * blk + par], vt, vn) for par in range(2)]
        o = jnp.where(lane < HD, outs[0], outs[1])
        o_ref[:, sl] = (o * _silu(z_ref[:, sl].astype(F32))).astype(BF16)


def _fox_sample(fq, cache_kt, cache_vt, fk, fv, lf_past_t, lf_new_t, fz, *, nq):
    b, _, past = cache_kt.shape
    new = lambda: pl.BlockSpec((nq, WIDTH), lambda i: (i, 0))
    cache = lambda: pl.BlockSpec((1, WIDTH, past), lambda i: (i, 0, 0))
    return pl.pallas_call(
        functools.partial(_fox_sample_kernel, nq=nq),
        grid=(b,),
        in_specs=[new(), cache(), cache(), new(), new(),
                  pl.BlockSpec((1, FOX_HEADS, past), lambda i: (i, 0, 0)),
                  pl.BlockSpec((1, FOX_HEADS, LANES), lambda i: (i, 0, 0)),
                  new()],
        out_specs=new(),
        out_shape=jax.ShapeDtypeStruct((b * nq, WIDTH), BF16),
        compiler_params=pltpu.CompilerParams(
            dimension_semantics=("parallel",), vmem_limit_bytes=VMEM_LIMIT),
        name="fox_sample",
    )(fq, cache_kt, cache_vt, fk, fv, lf_past_t, lf_new_t, fz)


def _diff_sample_kernel(q_ref, kt_ref, v_ref, kn_ref, vn_ref, bias_ref, z_ref,
                        lq1_ref, lk1_ref, lq2_ref, lk2_ref, subg_ref, o_ref, *, nq, past):
    lane = lax.broadcasted_iota(jnp.int32, (nq, LANES), 1)
    lam = _lambda(lq1_ref, lk1_ref, lq2_ref, lk2_ref)
    sls = [slice(hd * LANES, (hd + 1) * LANES) for hd in range(DIFF_HEADS)]
    scores = []
    for hd, sl in enumerate(sls):
        q = q_ref[:, sl]
        kt = kt_ref[0, sl, :].astype(BF16)
        kn = _pad_rows(kn_ref[:, sl], LANES).astype(BF16)
        for mp in range(2):
            qm = jnp.where((lane < HD) if mp == 0 else (lane >= HD), q, jnp.zeros_like(q))
            scores.append((_mm(qm, kt) + bias_ref[hd, :, 0:past],
                           _mm_nt(qm, kn) + bias_ref[hd, :, past:past + LANES]))
    weights = [_joint_softmax(s_p, s_n) for s_p, s_n in scores]
    for hd, sl in enumerate(sls):
        vt = v_ref[0, pl.ds(hd, past, stride=DIFF_HEADS), :].T.astype(BF16)
        vn = _pad_rows(vn_ref[:, sl], LANES).astype(BF16)
        outs = [_joint_pv(*weights[2 * hd + mp], vt, vn) for mp in range(2)]
        o_ref[:, sl] = _diff_finish(outs[0] - lam * outs[1], subg_ref[...], z_ref[:, sl].astype(F32))


def _diff_sample(dq, cache_kt, cache_v, dk, dv, bias, dz, lq1, lk1, lq2, lk2, subg, *, nq):
    b, _, past = cache_kt.shape
    new = lambda: pl.BlockSpec((nq, WIDTH), lambda i: (i, 0))
    small = lambda n: pl.BlockSpec((1, n), lambda i: (0, 0))
    return pl.pallas_call(
        functools.partial(_diff_sample_kernel, nq=nq, past=past),
        grid=(b,),
        in_specs=[new(), pl.BlockSpec((1, WIDTH, past), lambda i: (i, 0, 0)),
                  pl.BlockSpec((1, past * DIFF_HEADS, 2 * HD), lambda i: (i, 0, 0)), new(), new(),
                  pl.BlockSpec((DIFF_HEADS, nq, past + LANES), lambda i: (0, 0, 0)),
                  new(), small(HD), small(HD), small(HD), small(HD), small(2 * HD)],
        out_specs=new(),
        out_shape=jax.ShapeDtypeStruct((b * nq, WIDTH), BF16),
        compiler_params=pltpu.CompilerParams(
            dimension_semantics=("parallel",), vmem_limit_bytes=VMEM_LIMIT),
        name="diff_sample",
    )(dq, cache_kt, cache_v, dk, dv, bias, dz, lq1, lk1, lq2, lk2, subg)


def _out_kernel(yf_ref, yd_ref, ga_ref, gb_ref, x_ref, wuf_ref, wud_ref, wo_ref, g_ref, o_ref):
    merged = (ga_ref[...].astype(F32) * _mm(yf_ref[...], wuf_ref[...])
              + gb_ref[...].astype(F32) * _mm(yd_ref[...], wud_ref[...]))
    out = x_ref[...] + _mm(merged.astype(BF16), wo_ref[...])
    ms = jnp.mean(out * out, axis=-1, keepdims=True)
    o_ref[...] = out * lax.rsqrt(ms + NORM_EPS) * g_ref[...]


def _output(yf, yd, ga, gb, x2d, wuf, wud, wo, g, *, tm):
    rows = x2d.shape[0]
    assert rows % tm == 0
    row_spec = lambda n: pl.BlockSpec((tm, n), lambda i: (i, 0))
    const = lambda shape: pl.BlockSpec(shape, lambda i: (0, 0))
    return pl.pallas_call(
        _out_kernel,
        grid=(rows // tm,),
        in_specs=[row_spec(WIDTH), row_spec(WIDTH), row_spec(D_MODEL), row_spec(D_MODEL), row_spec(D_MODEL),
                  const((WIDTH, D_MODEL)), const((WIDTH, D_MODEL)), const((D_MODEL, D_MODEL)),
                  const((1, D_MODEL))],
        out_specs=row_spec(D_MODEL),
        out_shape=jax.ShapeDtypeStruct((rows, D_MODEL), F32),
        compiler_params=pltpu.CompilerParams(
            dimension_semantics=("parallel",), vmem_limit_bytes=VMEM_LIMIT),
        name="out_proj",
    )(yf, yd, ga, gb, x2d, wuf, wud, wo, g)


def _forget_lane_heads():
    lanes = np.arange(LANES)
    heads = (lanes % HD) // FL_SLOT
    heads[:FOX_HEADS] = np.arange(FOX_HEADS)
    return heads


def _pack_w_in(w_in, b_forget):
    sizes = (WIDTH, WIDTH, WIDTH, WIDTH, FOX_HEADS, WIDTH, WIDTH, WIDTH, WIDTH, D_MODEL, D_MODEL)
    offs = np.cumsum((0,) + sizes)
    fq, fk, fv, fz, fl, dq, dk, dv, dz, ga, gb = (w_in[:, offs[i]:offs[i + 1]] for i in range(len(sizes)))
    heads = _forget_lane_heads()
    w_all = jnp.concatenate([fq, fk, fv, fz, dq, dk, dv, dz, ga, gb, fl[:, heads]], axis=1).astype(BF16)
    return w_all, b_forget[heads][None, :]


def kernel(x_prompt, x_sample, cache_fox_k, cache_fox_v, cache_fox_logf, cache_diff_k, cache_diff_v,
           norm_in_g, w_in, b_forget, lambda_q1, lambda_k1, lambda_q2, lambda_k2, subln_g,
           w_up_fox, w_up_diff, w_o, rel_bias, final_norm_g):
    bp, tp, _ = x_prompt.shape
    bs, ts, _ = x_sample.shape
    past = cache_fox_k.shape[2]
    blk = 256
    tm = 512

    w_all, bfl = _pack_w_in(w_in[0], b_forget[0])
    g_in = norm_in_g[0][None, :]
    g_out = final_norm_g[None, :]
    wuf, wud, wo = w_up_fox[0].astype(BF16), w_up_diff[0].astype(BF16), w_o[0].astype(BF16)
    lq1, lk1, lq2, lk2 = (a[0][None, :] for a in (lambda_q1, lambda_k1, lambda_q2, lambda_k2))
    subg = subln_g[0][None, :]

    xp = x_prompt.reshape(bp * tp, D_MODEL)
    (fkt, fvt, lft, dkt, dv4, qa, ka, vtf, fz, dq, dkb, vtd, dz, ga, gb) = _project(
        xp, g_in, w_all, bfl, seq_len=tp, aug=True, tm=tm)
    r3 = lambda a: a.reshape(bp, tp, a.shape[-1])
    yf = _fox_attention(r3(qa), r3(ka), vtf, r3(fz), blk=blk, nh=FOX_STEP_HEADS)
    bias_p = _bias_tiles(rel_bias, q_start=blk, nq=blk, segments=((0, blk, blk), (blk, blk, blk)),
                         transposed=True, scale=LOG2E)
    yd = _diff_attention(r3(dq), r3(dkb), vtd, bias_p, r3(dz), lq1, lk1, lq2, lk2, subg, blk=blk,
                         nh=DIFF_STEP_HEADS)
    y_p = _output(yf.reshape(bp * tp, WIDTH), yd.reshape(bp * tp, WIDTH), ga, gb, xp, wuf, wud, wo, g_out, tm=tm)
    fox_k_p = jnp.transpose(fkt.reshape(bp, FOX_HEADS, HD, tp), (0, 3, 1, 2))[None]
    fox_v_p = jnp.transpose(fvt.reshape(bp, FOX_HEADS, HD, tp), (0, 3, 1, 2))[None]
    fox_lf_p = jnp.transpose(lft, (0, 2, 1))[None]
    diff_k_p = jnp.transpose(dkt.reshape(bp, DIFF_HEADS, 2, HD, tp), (0, 4, 1, 2, 3))[None]
    diff_v_p = dv4.reshape(1, bp, tp, DIFF_HEADS, 2 * HD)

    xs = x_sample.reshape(bs * ts, D_MODEL)
    (fk_s, fv_s, lf_s, dk_s, dv_s, fq_s, fz_s, dq_s, dz_s, ga_s, gb_s) = _project(
        xs, g_in, w_all, bfl, seq_len=ts, aug=False, tm=tm)
    fox_kt = jnp.transpose(cache_fox_k[0], (0, 2, 3, 1)).reshape(bs, WIDTH, past)
    fox_vt = jnp.transpose(cache_fox_v[0], (0, 2, 3, 1)).reshape(bs, WIDTH, past)
    diff_kt = jnp.transpose(cache_diff_k[0], (0, 2, 3, 4, 1)).reshape(bs, WIDTH, past)
    diff_v = cache_diff_v[0].reshape(bs, past * DIFF_HEADS, 2 * HD)
    lf_past_t = jnp.swapaxes(cache_fox_logf[0], 1, 2)
    lf_new_t = jnp.pad(jnp.swapaxes(lf_s.reshape(bs, ts, FOX_HEADS), 1, 2), ((0, 0), (0, 0), (0, LANES - ts)))
    yf_s = _fox_sample(fq_s, fox_kt, fox_vt, fk_s, fv_s, lf_past_t, lf_new_t, fz_s, nq=ts)
    bias_s = _bias_tiles(rel_bias, q_start=past, nq=ts, segments=((0, past, past), (past, LANES, ts)),
                         transposed=False, scale=1.0)
    yd_s = _diff_sample(dq_s, diff_kt, diff_v, dk_s, dv_s, bias_s, dz_s, lq1, lk1, lq2, lk2, subg, nq=ts)
    y_s = _output(yf_s, yd_s, ga_s, gb_s, xs, wuf, wud, wo, g_out, tm=tm)

    return (y_p.reshape(bp, tp, D_MODEL), y_s.reshape(bs, ts, D_MODEL),
            fox_k_p, fox_v_p, fox_lf_p, diff_k_p, diff_v_p,
            fk_s.reshape(1, bs, ts, FOX_HEADS, HD), fv_s.reshape(1, bs, ts, FOX_HEADS, HD),
            lf_s.reshape(1, bs, ts, FOX_HEADS),
            dk_s.reshape(1, bs, ts, DIFF_HEADS, 2, HD), dv_s.reshape(1, bs, ts, DIFF_HEADS, 2 * HD))
```

```python
import functools
import math

import numpy as np
import jax
import jax.numpy as jnp
from jax import lax
from jax.experimental import pallas as pl
from jax.experimental.pallas import tpu as pltpu

F32 = jnp.float32
BF16 = jnp.bfloat16

D_MODEL = 1024
HD = 64
FOX_HEADS = 8
DIFF_HEADS = 4
WIDTH = 512
CHUNK = 64
CHUNK_SHIFT = 6
N_BUCKETS = 32
MAX_DISTANCE = 128
NORM_EPS = 1e-6
SCALE = HD ** -0.5
NEG = -1e30
LAMBDA_INIT = 0.8 - 0.6 * math.exp(-0.3 * 0)
LOG2E = math.log2(math.e)

LANES = 128
PAIRS = WIDTH // LANES
VT_PAD = 16
VT_ROWS = LANES + VT_PAD
FVT_ROWS = HD + VT_PAD

_O_FQ, _O_FK, _O_FV, _O_FZ = 0, 512, 1024, 1536
_O_DQ, _O_DK, _O_DV, _O_DZ = 2048, 2560, 3072, 3584
_O_GA, _O_GB = 4096, 5120
_O_FL = 6144
_N_ALL = 6272
FL_SLOT = 8

VMEM_LIMIT = 56 * 1024 * 1024

FOX_STEP_HEADS = 8
DIFF_STEP_HEADS = 4
HEADROOM = 64.0
STEP_BLOCKS = (4, 2, 1)


def _t5_thresholds():
    nb = N_BUCKETS // 2
    max_exact = nb // 2
    n = np.arange(0, 4 * MAX_DISTANCE)
    large = max_exact + (np.log(np.maximum(n, 1).astype(np.float32) / max_exact)
                         / math.log(MAX_DISTANCE / max_exact) * (nb - max_exact)).astype(np.int32)
    large = np.minimum(large, nb - 1)
    thr = [int(np.argmax(large >= b)) for b in range(max_exact + 1, nb)]
    return max_exact, tuple(thr)


_MAX_EXACT, _T5_THR = _t5_thresholds()
FAR_DIST = _T5_THR[-1]


def _mm(a, b):
    return jnp.dot(a, b, preferred_element_type=F32)


def _mm_nt(a, b):
    return lax.dot_general(a, b, (((1,), (1,)), ((), ())), preferred_element_type=F32)


def _log_sigmoid(x):
    return -(jnp.maximum(-x, 0.0) + jnp.log1p(jnp.exp(-jnp.abs(x))))


def _sigmoid(x):
    return 1.0 / (1.0 + jnp.exp(-x))


def _silu(x):
    return x * _sigmoid(x)


def _split3(x):
    p1 = x.astype(BF16).astype(F32)
    r1 = x - p1
    p2 = r1.astype(BF16).astype(F32)
    p3 = r1 - p2
    return p1, p2, p3


def _proj_kernel(x_ref, g_ref, w_ref, bfl_ref, *refs, tm, seq_len, aug):
    if aug:
        (fkt_ref, fvt_ref, lft_ref, dkt_ref, dv4_ref, qa_ref, ka_ref, vtf_ref,
         fz_ref, dq_ref, dkb_ref, vtd_ref, dz_ref, ga_ref, gb_ref, carry_ref) = refs
    else:
        (fk_ref, fv_ref, lf_ref, dk_ref, dv_ref, fq_ref,
         fz_ref, dq_ref, dz_ref, ga_ref, gb_ref) = refs

    x = x_ref[...]
    ms = jnp.mean(x * x, axis=-1, keepdims=True)
    h = (x * lax.rsqrt(ms + NORM_EPS) * g_ref[...]).astype(BF16)

    def grp(off, n=WIDTH):
        return _mm(h, w_ref[:, off:off + n])

    if not aug:
        fq_ref[...] = (grp(_O_FQ) * SCALE).astype(BF16)
        fk_ref[...] = grp(_O_FK)
        fv_ref[...] = grp(_O_FV)
        fz_ref[...] = grp(_O_FZ).astype(BF16)
        dq_ref[...] = (grp(_O_DQ) * SCALE).astype(BF16)
        dk_ref[...] = grp(_O_DK)
        dv_ref[...] = grp(_O_DV)
        dz_ref[...] = grp(_O_DZ).astype(BF16)
        ga_ref[...] = _sigmoid(grp(_O_GA, D_MODEL)).astype(BF16)
        gb_ref[...] = _sigmoid(grp(_O_GB, D_MODEL)).astype(BF16)
        lf_ref[...] = _log_sigmoid(grp(_O_FL, LANES) + bfl_ref[...])[:, :FOX_HEADS]
        return

    lf = _log_sigmoid(grp(_O_FL, LANES) + bfl_ref[...])
    lft_ref[0] = lf.T[:FOX_HEADS, :]
    row = lax.broadcasted_iota(jnp.int32, (tm, LANES), 0)
    c = lf
    shift = 1
    while shift < tm:
        c = c + jnp.where(row >= shift, pltpu.roll(c, shift, 0), 0.0)
        shift *= 2
    tiles_per_seq = seq_len // tm

    @pl.when(pl.program_id(0) % tiles_per_seq == 0)
    def _():
        carry_ref[...] = jnp.zeros_like(carry_ref)

    c = c + carry_ref[...]
    carry_ref[...] = c[tm - 1:tm, :]
    c_hi, c_mid, c_lo = _split3(c * LOG2E)

    fq = grp(_O_FQ) * (SCALE * LOG2E)
    fk = grp(_O_FK)
    fv = grp(_O_FV)
    fvt = fv.T
    fkt_ref[0] = fk.T
    fvt_ref[0] = fvt
    tail = jnp.where(lax.broadcasted_iota(jnp.int32, (VT_PAD, tm), 0) == 0, 1.0, 0.0).astype(BF16)
    lane = lax.broadcasted_iota(jnp.int32, (tm, LANES), 1)
    pos = lane & (FL_SLOT - 1)
    q_slots = jnp.where(pos == 0, c_hi, jnp.where(pos == 1, c_mid, jnp.where(
        pos == 2, c_lo, jnp.where(pos < 6, 1.0, 0.0))))
    k_slots = jnp.where(pos < 3, 1.0, jnp.where(pos == 3, -c_hi, jnp.where(
        pos == 4, -c_mid, jnp.where(pos == 5, -c_lo, 0.0))))
    for hd in range(FOX_HEADS):
        blk, par = divmod(hd, 2)
        sl = slice(blk * LANES, (blk + 1) * LANES)
        data = (lane < HD) if par == 0 else (lane >= HD)
        slot0 = (1 - par) * HD + hd * FL_SLOT
        own = (lane >= slot0) & (lane < slot0 + FL_SLOT)
        q_aug = jnp.where(own, q_slots, 0.0)
        k_aug = jnp.where(own, k_slots, 0.0)
        osl = slice(hd * LANES, (hd + 1) * LANES)
        qa_ref[:, osl] = jnp.where(data, fq[:, sl], q_aug).astype(BF16)
        ka_ref[:, osl] = jnp.where(data, fk[:, sl], k_aug).astype(BF16)
        base = hd * FVT_ROWS
        vtf_ref[0, base:base + HD, :] = fvt[hd * HD:(hd + 1) * HD, :].astype(BF16)
        vtf_ref[0, base + HD:base + FVT_ROWS, :] = tail

    ga_ref[...] = _sigmoid(grp(_O_GA, D_MODEL)).astype(BF16)
    gb_ref[...] = _sigmoid(grp(_O_GB, D_MODEL)).astype(BF16)

    dk = grp(_O_DK)
    dkt_ref[0] = dk.T
    dkb_ref[...] = dk.astype(BF16)
    dv = grp(_O_DV)
    dv4_ref[...] = dv.reshape(tm, DIFF_HEADS, 2 * HD)
    for hd in range(DIFF_HEADS):
        base = hd * VT_ROWS
        vtd_ref[0, base:base + LANES, :] = dv[:, hd * LANES:(hd + 1) * LANES].T.astype(BF16)
        vtd_ref[0, base + LANES:base + VT_ROWS, :] = tail

    dq_ref[...] = (grp(_O_DQ) * (SCALE * LOG2E)).astype(BF16)
    fz_ref[...] = grp(_O_FZ).astype(BF16)
    dz_ref[...] = grp(_O_DZ).astype(BF16)


def _project(x2d, g, w_all, bfl, *, seq_len, aug, tm):
    rows = x2d.shape[0]
    assert rows % tm == 0
    row_spec = lambda n: pl.BlockSpec((tm, n), lambda i: (i, 0))
    const = lambda shape: pl.BlockSpec(shape, lambda i: (0, 0))
    f32o = lambda n: jax.ShapeDtypeStruct((rows, n), F32)
    b16o = lambda n: jax.ShapeDtypeStruct((rows, n), BF16)
    if aug:
        assert seq_len % tm == 0
        wide = FOX_HEADS * LANES
        tps = seq_len // tm
        t_shape = lambda n, dt: jax.ShapeDtypeStruct((rows // seq_len, n, seq_len), dt)
        t_spec = lambda n: pl.BlockSpec((1, n, tm), lambda i: (i // tps, 0, i % tps))
        out_shape = [t_shape(WIDTH, F32), t_shape(WIDTH, F32), t_shape(FOX_HEADS, F32), t_shape(WIDTH, F32),
                     jax.ShapeDtypeStruct((rows, DIFF_HEADS, 2 * HD), F32)]
        out_specs = [t_spec(WIDTH), t_spec(WIDTH), t_spec(FOX_HEADS), t_spec(WIDTH),
                     pl.BlockSpec((tm, DIFF_HEADS, 2 * HD), lambda i: (i, 0, 0))]
        out_shape += ([b16o(wide)] * 2 + [t_shape(FOX_HEADS * FVT_ROWS, BF16)] + [b16o(WIDTH)] * 3
                      + [t_shape(DIFF_HEADS * VT_ROWS, BF16)] + [b16o(WIDTH)] + [b16o(D_MODEL)] * 2)
        out_specs += ([row_spec(wide)] * 2 + [t_spec(FOX_HEADS * FVT_ROWS)] + [row_spec(WIDTH)] * 3
                      + [t_spec(DIFF_HEADS * VT_ROWS)] + [row_spec(WIDTH)] + [row_spec(D_MODEL)] * 2)
        scratch = [pltpu.VMEM((1, LANES), F32)]
    else:
        out_shape = [f32o(WIDTH), f32o(WIDTH), f32o(FOX_HEADS), f32o(WIDTH), f32o(WIDTH)]
        out_specs = [row_spec(WIDTH), row_spec(WIDTH), row_spec(FOX_HEADS), row_spec(WIDTH), row_spec(WIDTH)]
        out_shape += [b16o(WIDTH)] * 4 + [b16o(D_MODEL)] * 2
        out_specs += [row_spec(WIDTH)] * 4 + [row_spec(D_MODEL)] * 2
        scratch = []
    return pl.pallas_call(
        functools.partial(_proj_kernel, tm=tm, seq_len=seq_len, aug=aug),
        grid=(rows // tm,),
        in_specs=[row_spec(D_MODEL), const((1, D_MODEL)),
                  pl.BlockSpec((D_MODEL, _N_ALL), lambda i: (0, 0), pipeline_mode=pl.Buffered(1)),
                  const((1, LANES))],
        out_specs=out_specs,
        out_shape=out_shape,
        scratch_shapes=scratch,
        compiler_params=pltpu.CompilerParams(
            dimension_semantics=("arbitrary",), vmem_limit_bytes=VMEM_LIMIT),
        name="proj_aug" if aug else "proj_plain",
    )(x2d, g, w_all, bfl)


def _bias_kernel(tbl_ref, o_ref, *, q_start, nq, segments, transposed, scale):
    hd = pl.program_id(0)
    far = tbl_ref[N_BUCKETS // 2 - 1, hd]
    q_axis, k_axis = (1, 0) if transposed else (0, 1)
    off = 0
    for k_start, nk, valid in segments:
        shape = (nk, nq) if transposed else (nq, nk)
        qpos = q_start + lax.broadcasted_iota(jnp.int32, shape, q_axis)
        kidx = lax.broadcasted_iota(jnp.int32, shape, k_axis)
        kpos = k_start + kidx
        rel = kpos - qpos
        n = jnp.abs(rel)
        large = jnp.full(shape, _MAX_EXACT, jnp.int32)
        for thr in _T5_THR:
            large = large + jnp.where(n >= thr, 1, 0)
        bucket = jnp.where(rel > 0, N_BUCKETS // 2, 0) + jnp.where(n < _MAX_EXACT, n, large)
        val = jnp.zeros(shape, F32)
        for b in range(N_BUCKETS):
            val = jnp.where(bucket == b, tbl_ref[b, hd], val)
        val = (val - far) * scale
        visible = ((kpos >> CHUNK_SHIFT) <= (qpos >> CHUNK_SHIFT)) & (kidx < valid)
        val = jnp.where(visible, val, NEG)
        if transposed:
            o_ref[0, off:off + nk, :] = val
        else:
            o_ref[0, :, off:off + nk] = val
        off += nk


def _bias_tiles(rel_bias, *, q_start, nq, segments, transposed, scale):
    total = sum(s[1] for s in segments)
    shape = (total, nq) if transposed else (nq, total)
    return pl.pallas_call(
        functools.partial(_bias_kernel, q_start=q_start, nq=nq, segments=segments, transposed=transposed,
                          scale=scale),
        grid=(DIFF_HEADS,),
        in_specs=[pl.BlockSpec(memory_space=pltpu.SMEM)],
        out_specs=pl.BlockSpec((1,) + shape, lambda h: (h, 0, 0)),
        out_shape=jax.ShapeDtypeStruct((DIFF_HEADS,) + shape, F32),
        name="t5_bias",
    )(rel_bias)


def _flash_t(k_ref, vt_ref, acc_ref, bad_ref, streams, qi, blk, prev_bias, diag_bias, diag_mask):
    n = len(streams)

    def scores(first_blk, width):
        start = pl.multiple_of(first_blk * blk, blk)
        return [_mm_nt(k_ref[0, pl.ds(start, width), ksl], q) for q, ksl, _ in streams]

    def step(first_blk, width, ms, biases, mask):
        start = pl.multiple_of(first_blk * blk, blk)
        sts = scores(first_blk, width)
        out, ps, alphas = [], [], []
        for s, st in enumerate(sts):
            if biases is not None:
                st = st + biases[s]()
            if mask is not None:
                st = jnp.where(mask, st, NEG)
            m_new = jnp.maximum(ms[s], jnp.max(st, axis=0, keepdims=True))
            ps.append(jnp.exp2(st - m_new).astype(BF16))
            alphas.append(jnp.exp2(ms[s] - m_new))
            out.append(m_new)
        for s, (_, _, vsl) in enumerate(streams):
            acc_ref[s] = alphas[s] * acc_ref[s] + _mm(vt_ref[0, vsl, pl.ds(start, width)], ps[s])
        return tuple(out)

    def first_step():
        start = pl.multiple_of(qi * blk, blk)
        out, ps = [], []
        for s, st in enumerate(scores(qi, blk)):
            if diag_bias is not None:
                st = st + diag_bias[s]()
            if diag_mask is not None:
                st = jnp.where(diag_mask, st, NEG)
            m = jnp.max(st, axis=0, keepdims=True)
            ps.append(jnp.exp2(st - m).astype(BF16))
            out.append(m)
        for s, (_, _, vsl) in enumerate(streams):
            acc_ref[s] = _mm(vt_ref[0, vsl, pl.ds(start, blk)], ps[s])
        return tuple(out)

    ms = first_step()
    bad_ref[...] = jnp.zeros_like(bad_ref)

    def consume(sts, first_blk, width):
        start = pl.multiple_of(first_blk * blk, blk)
        ps = [jnp.exp2(st - ms[s]).astype(BF16) for s, st in enumerate(sts)]
        worst = None
        for p in ps:
            d = jnp.max(p, axis=0, keepdims=True)
            worst = d if worst is None else jnp.maximum(worst, d)
        bad_ref[...] = jnp.maximum(bad_ref[...], worst.astype(F32))
        for s, (_, _, vsl) in enumerate(streams):
            acc_ref[s] = acc_ref[s] + _mm(vt_ref[0, vsl, pl.ds(start, width)], ps[s])

    if prev_bias is not None:
        @pl.when(qi >= 1)
        def _():
            consume([st + prev_bias[s]() for s, st in enumerate(scores(qi - 1, blk))], qi - 1, blk)

    n_plain = qi if prev_bias is None else jnp.maximum(qi - 1, 0)
    done = 0
    for w in STEP_BLOCKS:
        count = (n_plain - done) // w

        def fast_step(j, carry, w=w, done=done):
            consume(scores(done + j * w, w * blk), done + j * w, w * blk)
            return carry

        lax.fori_loop(0, count, fast_step, 0)
        done = done + count * w

    @pl.when(jnp.max(bad_ref[...]) > 2.0 ** HEADROOM)
    def _():
        c = first_step()
        if prev_bias is not None:
            c = lax.cond(qi >= 1, lambda c: step(qi - 1, blk, c, prev_bias, None), lambda c: c, c)
        lax.fori_loop(0, n_plain, lambda j, c: step(j, blk, c, None, None), c)


def _fox_kernel(qa_ref, ka_ref, vt_ref, z_ref, o_ref, acc_ref, bad_ref, *, blk, nh):
    qi = pl.program_id(2)
    key = lax.broadcasted_iota(jnp.int32, (blk, blk), 0)
    qry = lax.broadcasted_iota(jnp.int32, (blk, blk), 1)
    hsl = [slice(hd * LANES, (hd + 1) * LANES) for hd in range(nh)]
    streams = [(qa_ref[0, :, sl], sl, slice(hd * FVT_ROWS, (hd + 1) * FVT_ROWS)) for hd, sl in enumerate(hsl)]
    _flash_t(ka_ref, vt_ref, acc_ref, bad_ref, streams, qi, blk, None, None, key <= qry)
    for pair in range(nh // 2):
        halves = []
        for par in range(2):
            acc = acc_ref[2 * pair + par]
            halves.append(acc[0:HD, :] / acc[HD:HD + 1, :])
        o = jnp.concatenate(halves, axis=0).T
        o_ref[0, :, hsl[pair]] = (o * _silu(z_ref[0, :, hsl[pair]].astype(F32))).astype(BF16)


def _fox_attention(qa, ka, vt, fz, *, blk, nh):
    b, t, _ = qa.shape
    return pl.pallas_call(
        functools.partial(_fox_kernel, blk=blk, nh=nh),
        grid=(b, FOX_HEADS // nh, t // blk),
        in_specs=[pl.BlockSpec((1, blk, nh * LANES), lambda b, p, i: (b, i, p)),
                  pl.BlockSpec((1, t, nh * LANES), lambda b, p, i: (b, 0, p)),
                  pl.BlockSpec((1, nh * FVT_ROWS, t), lambda b, p, i: (b, p, 0)),
                  pl.BlockSpec((1, blk, nh * HD), lambda b, p, i: (b, i, p))],
        out_specs=pl.BlockSpec((1, blk, nh * HD), lambda b, p, i: (b, i, p)),
        out_shape=jax.ShapeDtypeStruct((b, t, WIDTH), BF16),
        scratch_shapes=[pltpu.VMEM((nh, FVT_ROWS, blk), F32), pltpu.VMEM((1, blk), F32)],
        compiler_params=pltpu.CompilerParams(
            dimension_semantics=("parallel", "parallel", "arbitrary"), vmem_limit_bytes=VMEM_LIMIT),
        name="fox_prompt",
    )(qa, ka, vt, fz)


def _lambda(lq1_ref, lk1_ref, lq2_ref, lk2_ref):
    s1 = jnp.sum(lq1_ref[...] * lk1_ref[...], axis=-1, keepdims=True)
    s2 = jnp.sum(lq2_ref[...] * lk2_ref[...], axis=-1, keepdims=True)
    return jnp.exp(s1) - jnp.exp(s2) + LAMBDA_INIT


def _diff_finish(o, subg, z):
    ms = jnp.mean(o * o, axis=-1, keepdims=True)
    od = o * lax.rsqrt(ms + NORM_EPS) * subg * (1.0 - LAMBDA_INIT)
    return (od * _silu(z)).astype(BF16)


def _diff_kernel(q_ref, k_ref, vt_ref, bias_ref, z_ref, lq1_ref, lk1_ref, lq2_ref, lk2_ref, subg_ref,
                 o_ref, acc_ref, bad_ref, *, blk, nh):
    qi = pl.program_id(2)
    lane = lax.broadcasted_iota(jnp.int32, (blk, LANES), 1)
    streams, prev_bias, diag_bias = [], [], []
    for hd in range(nh):
        q = q_ref[0, :, hd * LANES:(hd + 1) * LANES]
        for mp in range(2):
            streams.append((jnp.where((lane < HD) if mp == 0 else (lane >= HD), q, jnp.zeros_like(q)),
                            slice(hd * LANES, (hd + 1) * LANES), slice(hd * VT_ROWS, (hd + 1) * VT_ROWS)))
            prev_bias.append(lambda hd=hd: bias_ref[hd, 0:blk, :])
            diag_bias.append(lambda hd=hd: bias_ref[hd, blk:2 * blk, :])
    _flash_t(k_ref, vt_ref, acc_ref, bad_ref, streams, qi, blk, prev_bias, diag_bias, None)
    lam = _lambda(lq1_ref, lk1_ref, lq2_ref, lk2_ref)
    for hd in range(nh):
        a0, a1 = acc_ref[2 * hd], acc_ref[2 * hd + 1]
        o_t = a0[0:LANES, :] / a0[LANES:LANES + 1, :] - lam * (a1[0:LANES, :] / a1[LANES:LANES + 1, :])
        sl = slice(hd * LANES, (hd + 1) * LANES)
        o_ref[0, :, sl] = _diff_finish(o_t.T, subg_ref[...], z_ref[0, :, sl].astype(F32))


def _diff_attention(dq, dk, dvt, bias, dz, lq1, lk1, lq2, lk2, subg, *, blk, nh):
    b, t, _ = dq.shape
    small = lambda n: pl.BlockSpec((1, n), lambda b, h, i: (0, 0))
    return pl.pallas_call(
        functools.partial(_diff_kernel, blk=blk, nh=nh),
        grid=(b, DIFF_HEADS // nh, t // blk),
        in_specs=[pl.BlockSpec((1, blk, nh * LANES), lambda b, h, i: (b, i, h)),
                  pl.BlockSpec((1, t, nh * LANES), lambda b, h, i: (b, 0, h)),
                  pl.BlockSpec((1, nh * VT_ROWS, t), lambda b, h, i: (b, h, 0)),
                  pl.BlockSpec((nh, 2 * blk, blk), lambda b, h, i: (h, 0, 0)),
                  pl.BlockSpec((1, blk, nh * LANES), lambda b, h, i: (b, i, h)),
                  small(HD), small(HD), small(HD), small(HD), small(2 * HD)],
        out_specs=pl.BlockSpec((1, blk, nh * LANES), lambda b, h, i: (b, i, h)),
        out_shape=jax.ShapeDtypeStruct((b, t, WIDTH), BF16),
        scratch_shapes=[pltpu.VMEM((2 * nh, VT_ROWS, blk), F32), pltpu.VMEM((1, blk), F32)],
        compiler_params=pltpu.CompilerParams(
            dimension_semantics=("parallel", "parallel", "arbitrary"), vmem_limit_bytes=VMEM_LIMIT),
        name="diff_prompt",
    )(dq, dk, dvt, bias, dz, lq1, lk1, lq2, lk2, subg)


def _suffix_sums(x):
    n = x.shape[1] // LANES
    lane = lax.broadcasted_iota(jnp.int32, (x.shape[0], LANES), 1)
    after = jnp.zeros((x.shape[0], 1), F32)
    pieces = [None] * n
    for blk in reversed(range(n)):
        piece = x[:, blk * LANES:(blk + 1) * LANES]
        y = piece
        shift = 1
        while shift < LANES:
            y = y + jnp.where(lane + shift < LANES, pltpu.roll(y, LANES - shift, 1), 0.0)
            shift *= 2
        pieces[blk] = y - piece + after
        after = after + y[:, 0:1]
    return jnp.concatenate(pieces, axis=1), after


def _pad_rows(x, rows):
    return jnp.concatenate([x, jnp.zeros((rows - x.shape[0], x.shape[1]), x.dtype)], axis=0)


def _joint_softmax(s_p, s_n):
    m = jnp.maximum(jnp.max(s_p, axis=-1, keepdims=True), jnp.max(s_n, axis=-1, keepdims=True))
    p_p = jnp.exp(s_p - m)
    p_n = jnp.exp(s_n - m)
    l = jnp.sum(p_p, axis=-1, keepdims=True) + jnp.sum(p_n, axis=-1, keepdims=True)
    return p_p.astype(BF16), p_n.astype(BF16), l


def _joint_pv(p_p, p_n, l, vt_p, v_n):
    return (_mm_nt(p_p, vt_p) + _mm(p_n, v_n)) / l


def _fox_sample_kernel(q_ref, kt_ref, vt_ref, kn_ref, vn_ref, lfp_ref, lfn_ref, z_ref, o_ref, *, nq):
    r_new, total_new = _suffix_sums(lfn_ref[0])
    r_past, _ = _suffix_sums(lfp_ref[0])
    r_past = r_past + total_new
    lane = lax.broadcasted_iota(jnp.int32, (nq, LANES), 1)
    causal = lane <= lax.broadcasted_iota(jnp.int32, (nq, LANES), 0)
    sls = [slice(blk * LANES, (blk + 1) * LANES) for blk in range(PAIRS)]
    scores = []
    for blk, sl in enumerate(sls):
        q = q_ref[:, sl]
        kt = kt_ref[0, sl, :].astype(BF16)
        kn = _pad_rows(kn_ref[:, sl], LANES).astype(BF16)
        for par in range(2):
            hd = 2 * blk + par
            qm = jnp.where((lane < HD) if par == 0 else (lane >= HD), q, jnp.zeros_like(q))
            scores.append((_mm(qm, kt) + r_past[hd:hd + 1, :],
                           jnp.where(causal, _mm_nt(qm, kn) + r_new[hd:hd + 1, :], NEG)))
    weights = [_joint_softmax(s_p, s_n) for s_p, s_n in scores]
    for blk, sl in enumerate(sls):
        vt = vt_ref[0, sl, :].astype(BF16)
        vn = _pad_rows(vn_ref[:, sl], LANES).astype(BF16)
        outs = [_joint_pv(*weights[2 * blk + par], vt, vn) for par in range(2)]
        o = jnp.where(lane < HD, outs[0], outs[1])
        o_ref[:, sl] = (o * _silu(z_ref[:, sl].astype(F32))).astype(BF16)


def _fox_sample(fq, cache_kt, cache_vt, fk, fv, lf_past_t, lf_new_t, fz, *, nq):
    b, _, past = cache_kt.shape
    new = lambda: pl.BlockSpec((nq, WIDTH), lambda i: (i, 0))
    cache = lambda: pl.BlockSpec((1, WIDTH, past), lambda i: (i, 0, 0))
    return pl.pallas_call(
        functools.partial(_fox_sample_kernel, nq=nq),
        grid=(b,),
        in_specs=[new(), cache(), cache(), new(), new(),
                  pl.BlockSpec((1, FOX_HEADS, past), lambda i: (i, 0, 0)),
                  pl.BlockSpec((1, FOX_HEADS, LANES), lambda i: (i, 0, 0)),
                  new()],
        out_specs=new(),
        out_shape=jax.ShapeDtypeStruct((b * nq, WIDTH), BF16),
        compiler_params=pltpu.CompilerParams(
            dimension_semantics=("parallel",), vmem_limit_bytes=VMEM_LIMIT),
        name="fox_sample",
    )(fq, cache_kt, cache_vt, fk, fv, lf_past_t, lf_new_t, fz)


def _diff_sample_kernel(q_ref, kt_ref, v_ref, kn_ref, vn_ref, bias_ref, z_ref,
                        lq1_ref, lk1_ref, lq2_ref, lk2_ref, subg_ref, o_ref, *, nq, past):
    lane = lax.broadcasted_iota(jnp.int32, (nq, LANES), 1)
    lam = _lambda(lq1_ref, lk1_ref, lq2_ref, lk2_ref)
    sls = [slice(hd * LANES, (hd + 1) * LANES) for hd in range(DIFF_HEADS)]
    scores = []
    for hd, sl in enumerate(sls):
        q = q_ref[:, sl]
        kt = kt_ref[0, sl, :].astype(BF16)
        kn = _pad_rows(kn_ref[:, sl], LANES).astype(BF16)
        for mp in range(2):
            qm = jnp.where((lane < HD) if mp == 0 else (lane >= HD), q, jnp.zeros_like(q))
            scores.append((_mm(qm, kt) + bias_ref[hd, :, 0:past],
                           _mm_nt(qm, kn) + bias_ref[hd, :, past:past + LANES]))
    weights = [_joint_softmax(s_p, s_n) for s_p, s_n in scores]
    for hd, sl in enumerate(sls):
        vt = v_ref[0, pl.ds(hd, past, stride=DIFF_HEADS), :].T.astype(BF16)
        vn = _pad_rows(vn_ref[:, sl], LANES).astype(BF16)
        outs = [_joint_pv(*weights[2 * hd + mp], vt, vn) for mp in range(2)]
        o_ref[:, sl] = _diff_finish(outs[0] - lam * outs[1], subg_ref[...], z_ref[:, sl].astype(F32))


def _diff_sample(dq, cache_kt, cache_v, dk, dv, bias, dz, lq1, lk1, lq2, lk2, subg, *, nq):
    b, _, past = cache_kt.shape
    new = lambda: pl.BlockSpec((nq, WIDTH), lambda i: (i, 0))
    small = lambda n: pl.BlockSpec((1, n), lambda i: (0, 0))
    return pl.pallas_call(
        functools.partial(_diff_sample_kernel, nq=nq, past=past),
        grid=(b,),
        in_specs=[new(), pl.BlockSpec((1, WIDTH, past), lambda i: (i, 0, 0)),
                  pl.BlockSpec((1, past * DIFF_HEADS, 2 * HD), lambda i: (i, 0, 0)), new(), new(),
                  pl.BlockSpec((DIFF_HEADS, nq, past + LANES), lambda i: (0, 0, 0)),
                  new(), small(HD), small(HD), small(HD), small(HD), small(2 * HD)],
        out_specs=new(),
        out_shape=jax.ShapeDtypeStruct((b * nq, WIDTH), BF16),
        compiler_params=pltpu.CompilerParams(
            dimension_semantics=("parallel",), vmem_limit_bytes=VMEM_LIMIT),
        name="diff_sample",
    )(dq, cache_kt, cache_v, dk, dv, bias, dz, lq1, lk1, lq2, lk2, subg)


def _out_kernel(yf_ref, yd_ref, ga_ref, gb_ref, x_ref, wuf_ref, wud_ref, wo_ref, g_ref, o_ref):
    merged = (ga_ref[...].astype(F32) * _mm(yf_ref[...], wuf_ref[...])
              + gb_ref[...].astype(F32) * _mm(yd_ref[...], wud_ref[...]))
    out = x_ref[...] + _mm(merged.astype(BF16), wo_ref[...])
    ms = jnp.mean(out * out, axis=-1, keepdims=True)
    o_ref[...] = out * lax.rsqrt(ms + NORM_EPS) * g_ref[...]


def _output(yf, yd, ga, gb, x2d, wuf, wud, wo, g, *, tm):
    rows = x2d.shape[0]
    assert rows % tm == 0
    row_spec = lambda n: pl.BlockSpec((tm, n), lambda i: (i, 0))
    const = lambda shape: pl.BlockSpec(shape, lambda i: (0, 0))
    return pl.pallas_call(
        _out_kernel,
        grid=(rows // tm,),
        in_specs=[row_spec(WIDTH), row_spec(WIDTH), row_spec(D_MODEL), row_spec(D_MODEL), row_spec(D_MODEL),
                  const((WIDTH, D_MODEL)), const((WIDTH, D_MODEL)), const((D_MODEL, D_MODEL)),
                  const((1, D_MODEL))],
        out_specs=row_spec(D_MODEL),
        out_shape=jax.ShapeDtypeStruct((rows, D_MODEL), F32),
        compiler_params=pltpu.CompilerParams(
            dimension_semantics=("parallel",), vmem_limit_bytes=VMEM_LIMIT),
        name="out_proj",
    )(yf, yd, ga, gb, x2d, wuf, wud, wo, g)


def _forget_lane_heads():
    lanes = np.arange(LANES)
    heads = (lanes % HD) // FL_SLOT
    heads[:FOX_HEADS] = np.arange(FOX_HEADS)
    return heads


def _pack_w_in(w_in, b_forget):
    sizes = (WIDTH, WIDTH, WIDTH, WIDTH, FOX_HEADS, WIDTH, WIDTH, WIDTH, WIDTH, D_MODEL, D_MODEL)
    offs = np.cumsum((0,) + sizes)
    fq, fk, fv, fz, fl, dq, dk, dv, dz, ga, gb = (w_in[:, offs[i]:offs[i + 1]] for i in range(len(sizes)))
    heads = _forget_lane_heads()
    w_all = jnp.concatenate([fq, fk, fv, fz, dq, dk, dv, dz, ga, gb, fl[:, heads]], axis=1).astype(BF16)
    return w_all, b_forget[heads][None, :]


def kernel(x_prompt, x_sample, cache_fox_k, cache_fox_v, cache_fox_logf, cache_diff_k, cache_diff_v,
           norm_in_g, w_in, b_forget, lambda_q1, lambda_k1, lambda_q2, lambda_k2, subln_g,
           w_up_fox, w_up_diff, w_o, rel_bias, final_norm_g):
    bp, tp, _ = x_prompt.shape
    bs, ts, _ = x_sample.shape
    past = cache_fox_k.shape[2]
    blk = 256
    tm = 512

    w_all, bfl = _pack_w_in(w_in[0], b_forget[0])
    g_in = norm_in_g[0][None, :]
    g_out = final_norm_g[None, :]
    wuf, wud, wo = w_up_fox[0].astype(BF16), w_up_diff[0].astype(BF16), w_o[0].astype(BF16)
    lq1, lk1, lq2, lk2 = (a[0][None, :] for a in (lambda_q1, lambda_k1, lambda_q2, lambda_k2))
    subg = subln_g[0][None, :]

    xp = x_prompt.reshape(bp * tp, D_MODEL)
    (fkt, fvt, lft, dkt, dv4, qa, ka, vtf, fz, dq, dkb, vtd, dz, ga, gb) = _project(
        xp, g_in, w_all, bfl, seq_len=tp, aug=True, tm=tm)
    r3 = lambda a: a.reshape(bp, tp, a.shape[-1])
    yf = _fox_attention(r3(qa), r3(ka), vtf, r3(fz), blk=blk, nh=FOX_STEP_HEADS)
    bias_p = _bias_tiles(rel_bias, q_start=blk, nq=blk, segments=((0, blk, blk), (blk, blk, blk)),
                         transposed=True, scale=LOG2E)
    yd = _diff_attention(r3(dq), r3(dkb), vtd, bias_p, r3(dz), lq1, lk1, lq2, lk2, subg, blk=blk,
                         nh=DIFF_STEP_HEADS)
    y_p = _output(yf.reshape(bp * tp, WIDTH), yd.reshape(bp * tp, WIDTH), ga, gb, xp, wuf, wud, wo, g_out, tm=tm)
    fox_k_p = jnp.transpose(fkt.reshape(bp, FOX_HEADS, HD, tp), (0, 3, 1, 2))[None]
    fox_v_p = jnp.transpose(fvt.reshape(bp, FOX_HEADS, HD, tp), (0, 3, 1, 2))[None]
    fox_lf_p = jnp.transpose(lft, (0, 2, 1))[None]
    diff_k_p = jnp.transpose(dkt.reshape(bp, DIFF_HEADS, 2, HD, tp), (0, 4, 1, 2, 3))[None]
    diff_v_p = dv4.reshape(1, bp, tp, DIFF_HEADS, 2 * HD)

    xs = x_sample.reshape(bs * ts, D_MODEL)
    (fk_s, fv_s, lf_s, dk_s, dv_s, fq_s, fz_s, dq_s, dz_s, ga_s, gb_s) = _project(
        xs, g_in, w_all, bfl, seq_len=ts, aug=False, tm=tm)
    fox_kt = jnp.transpose(cache_fox_k[0], (0, 2, 3, 1)).reshape(bs, WIDTH, past)
    fox_vt = jnp.transpose(cache_fox_v[0], (0, 2, 3, 1)).reshape(bs, WIDTH, past)
    diff_kt = jnp.transpose(cache_diff_k[0], (0, 2, 3, 4, 1)).reshape(bs, WIDTH, past)
    diff_v = cache_diff_v[0].reshape(bs, past * DIFF_HEADS, 2 * HD)
    lf_past_t = jnp.swapaxes(cache_fox_logf[0], 1, 2)
    lf_new_t = jnp.pad(jnp.swapaxes(lf_s.reshape(bs, ts, FOX_HEADS), 1, 2), ((0, 0), (0, 0), (0, LANES - ts)))
    yf_s = _fox_sample(fq_s, fox_kt, fox_vt, fk_s, fv_s, lf_past_t, lf_new_t, fz_s, nq=ts)
    bias_s = _bias_tiles(rel_bias, q_start=past, nq=ts, segments=((0, past, past), (past, LANES, ts)),
                         transposed=False, scale=1.0)
    yd_s = _diff_sample(dq_s, diff_kt, diff_v, dk_s, dv_s, bias_s, dz_s, lq1, lk1, lq2, lk2, subg, nq=ts)
    y_s = _output(yf_s, yd_s, ga_s, gb_s, xs, wuf, wud, wo, g_out, tm=tm)

    return (y_p.reshape(bp, tp, D_MODEL), y_s.reshape(bs, ts, D_MODEL),
            fox_k_p, fox_v_p, fox_lf_p, diff_k_p, diff_v_p,
            fk_s.reshape(1, bs, ts, FOX_HEADS, HD), fv_s.reshape(1, bs, ts, FOX_HEADS, HD),
            lf_s.reshape(1, bs, ts, FOX_HEADS),
            dk_s.reshape(1, bs, ts, DIFF_HEADS, 2, HD), dv_s.reshape(1, bs, ts, DIFF_HEADS, 2 * HD))
```

```python
import functools
import math

import numpy as np
import jax
import jax.numpy as jnp
from jax import lax
from jax.experimental import pallas as pl
from jax.experimental.pallas import tpu as pltpu

F32 = jnp.float32
BF16 = jnp.bfloat16

D_MODEL = 1024
HD = 64
FOX_HEADS = 8
DIFF_HEADS = 4
WIDTH = 512
CHUNK = 64
CHUNK_SHIFT = CHUNK.bit_length() - 1
N_BUCKETS = 32
MAX_DISTANCE = 128
NORM_EPS = 1e-6
SCALE = HD ** -0.5
NEG = -1e30
LAMBDA_INIT = 0.8 - 0.6 * math.exp(-0.3 * 0)
LOG2E = math.log2(math.e)

LANES = 128
PAIRS = WIDTH // LANES
VT_PAD = 16
VT_ROWS = LANES + VT_PAD
FVT_ROWS = HD + VT_PAD

_O_FQ, _O_FK, _O_FV, _O_FZ = 0, 512, 1024, 1536
_O_DQ, _O_DK, _O_DV, _O_DZ = 2048, 2560, 3072, 3584
_O_GA, _O_GB = 4096, 5120
_O_FL = 6144
_N_ALL = 6272
FL_SLOT = 8

VMEM_LIMIT = 56 * 1024 * 1024
ATTN_BLOCK = 256
ROW_TILE = 512

FOX_STEP_HEADS = 8
DIFF_STEP_HEADS = 4
HEADROOM = 64.0
QUERY_BLOCKS_PER_STEP = 4
STEP_BLOCKS = (4, 2, 1)


def _t5_thresholds():
    nb = N_BUCKETS // 2
    max_exact = nb // 2
    n = np.arange(0, 4 * MAX_DISTANCE)
    large = max_exact + (np.log(np.maximum(n, 1).astype(np.float32) / max_exact)
                         / math.log(MAX_DISTANCE / max_exact) * (nb - max_exact)).astype(np.int32)
    large = np.minimum(large, nb - 1)
    thr = [int(np.argmax(large >= b)) for b in range(max_exact + 1, nb)]
    return max_exact, tuple(thr)


_MAX_EXACT, _T5_THR = _t5_thresholds()


def _mm(a, b):
    return jnp.dot(a, b, preferred_element_type=F32)


def _mm_nt(a, b):
    return lax.dot_general(a, b, (((1,), (1,)), ((), ())), preferred_element_type=F32)


def _log_sigmoid(x):
    return -(jnp.maximum(-x, 0.0) + jnp.log1p(jnp.exp(-jnp.abs(x))))


def _sigmoid(x):
    return 1.0 / (1.0 + jnp.exp(-x))


def _silu(x):
    return x * _sigmoid(x)


def _split3(x):
    p1 = x.astype(BF16).astype(F32)
    r1 = x - p1
    p2 = r1.astype(BF16).astype(F32)
    p3 = r1 - p2
    return p1, p2, p3


def _proj_kernel(x_ref, g_ref, w_ref, bfl_ref, *refs, tm, seq_len, aug):
    if aug:
        (fkt_ref, fvt_ref, lft_ref, dkt_ref, dv4_ref, qa_ref, ka_ref, vtf_ref,
         fz_ref, dq_ref, dkb_ref, vtd_ref, dz_ref, ga_ref, gb_ref, carry_ref) = refs
    else:
        (fk_ref, fv_ref, lf_ref, dk_ref, dv_ref, fq_ref,
         fz_ref, dq_ref, dz_ref, ga_ref, gb_ref) = refs

    x = x_ref[...]
    ms = jnp.mean(x * x, axis=-1, keepdims=True)
    h = (x * lax.rsqrt(ms + NORM_EPS) * g_ref[...]).astype(BF16)

    def grp(off, n=WIDTH):
        return _mm(h, w_ref[:, off:off + n])

    if not aug:
        fq_ref[...] = (grp(_O_FQ) * SCALE).astype(BF16)
        fk_ref[...] = grp(_O_FK)
        fv_ref[...] = grp(_O_FV)
        fz_ref[...] = grp(_O_FZ).astype(BF16)
        dq_ref[...] = (grp(_O_DQ) * SCALE).astype(BF16)
        dk_ref[...] = grp(_O_DK)
        dv_ref[...] = grp(_O_DV)
        dz_ref[...] = grp(_O_DZ).astype(BF16)
        ga_ref[...] = _sigmoid(grp(_O_GA, D_MODEL)).astype(BF16)
        gb_ref[...] = _sigmoid(grp(_O_GB, D_MODEL)).astype(BF16)
        lf_ref[...] = _log_sigmoid(grp(_O_FL, LANES) + bfl_ref[...])[:, :FOX_HEADS]
        return

    lf = _log_sigmoid(grp(_O_FL, LANES) + bfl_ref[...])
    lft_ref[0] = lf.T[:FOX_HEADS, :]
    row = lax.broadcasted_iota(jnp.int32, (tm, LANES), 0)
    c = lf
    shift = 1
    while shift < tm:
        c = c + jnp.where(row >= shift, pltpu.roll(c, shift, 0), 0.0)
        shift *= 2
    tiles_per_seq = seq_len // tm

    @pl.when(pl.program_id(0) % tiles_per_seq == 0)
    def _():
        carry_ref[...] = jnp.zeros_like(carry_ref)

    c = c + carry_ref[...]
    carry_ref[...] = c[tm - 1:tm, :]
    c_hi, c_mid, c_lo = _split3(c * LOG2E)

    fq = grp(_O_FQ) * (SCALE * LOG2E)
    fk = grp(_O_FK)
    fv = grp(_O_FV)
    fvt = fv.T
    fkt_ref[0] = fk.T
    fvt_ref[0] = fvt
    tail = jnp.where(lax.broadcasted_iota(jnp.int32, (VT_PAD, tm), 0) == 0, 1.0, 0.0).astype(BF16)
    lane = lax.broadcasted_iota(jnp.int32, (tm, LANES), 1)
    pos = lane & (FL_SLOT - 1)
    q_slots = jnp.where(pos == 0, c_hi, jnp.where(pos == 1, c_mid, jnp.where(
        pos == 2, c_lo, jnp.where(pos < 6, 1.0, 0.0))))
    k_slots = jnp.where(pos < 3, 1.0, jnp.where(pos == 3, -c_hi, jnp.where(
        pos == 4, -c_mid, jnp.where(pos == 5, -c_lo, 0.0))))
    for hd in range(FOX_HEADS):
        blk, par = divmod(hd, 2)
        sl = slice(blk * LANES, (blk + 1) * LANES)
        data = (lane < HD) if par == 0 else (lane >= HD)
        slot0 = (1 - par) * HD + hd * FL_SLOT
        own = (lane >= slot0) & (lane < slot0 + FL_SLOT)
        q_aug = jnp.where(own, q_slots, 0.0)
        k_aug = jnp.where(own, k_slots, 0.0)
        osl = slice(hd * LANES, (hd + 1) * LANES)
        qa_ref[:, osl] = jnp.where(data, fq[:, sl], q_aug).astype(BF16)
        ka_ref[:, osl] = jnp.where(data, fk[:, sl], k_aug).astype(BF16)
        base = hd * FVT_ROWS
        vtf_ref[0, base:base + HD, :] = fvt[hd * HD:(hd + 1) * HD, :].astype(BF16)
        vtf_ref[0, base + HD:base + FVT_ROWS, :] = tail

    ga_ref[...] = _sigmoid(grp(_O_GA, D_MODEL)).astype(BF16)
    gb_ref[...] = _sigmoid(grp(_O_GB, D_MODEL)).astype(BF16)

    dk = grp(_O_DK)
    dkt_ref[0] = dk.T
    dkb_ref[...] = dk.astype(BF16)
    dv = grp(_O_DV)
    dv4_ref[...] = dv.reshape(tm, DIFF_HEADS, 2 * HD)
    for hd in range(DIFF_HEADS):
        base = hd * VT_ROWS
        vtd_ref[0, base:base + LANES, :] = dv[:, hd * LANES:(hd + 1) * LANES].T.astype(BF16)
        vtd_ref[0, base + LANES:base + VT_ROWS, :] = tail

    dq_ref[...] = (grp(_O_DQ) * (SCALE * LOG2E)).astype(BF16)
    fz_ref[...] = grp(_O_FZ).astype(BF16)
    dz_ref[...] = grp(_O_DZ).astype(BF16)


def _project(x2d, g, w_all, bfl, *, seq_len, aug, tm):
    rows = x2d.shape[0]
    assert rows % tm == 0
    row_spec = lambda n: pl.BlockSpec((tm, n), lambda i: (i, 0))
    const = lambda shape: pl.BlockSpec(shape, lambda i: (0, 0))
    f32o = lambda n: jax.ShapeDtypeStruct((rows, n), F32)
    b16o = lambda n: jax.ShapeDtypeStruct((rows, n), BF16)
    if aug:
        assert seq_len % tm == 0
        wide = FOX_HEADS * LANES
        tps = seq_len // tm
        t_shape = lambda n, dt: jax.ShapeDtypeStruct((rows // seq_len, n, seq_len), dt)
        t_spec = lambda n: pl.BlockSpec((1, n, tm), lambda i: (i // tps, 0, i % tps))
        out_shape = [t_shape(WIDTH, F32), t_shape(WIDTH, F32), t_shape(FOX_HEADS, F32), t_shape(WIDTH, F32),
                     jax.ShapeDtypeStruct((rows, DIFF_HEADS, 2 * HD), F32)]
        out_specs = [t_spec(WIDTH), t_spec(WIDTH), t_spec(FOX_HEADS), t_spec(WIDTH),
                     pl.BlockSpec((tm, DIFF_HEADS, 2 * HD), lambda i: (i, 0, 0))]
        out_shape += ([b16o(wide)] * 2 + [t_shape(FOX_HEADS * FVT_ROWS, BF16)] + [b16o(WIDTH)] * 3
                      + [t_shape(DIFF_HEADS * VT_ROWS, BF16)] + [b16o(WIDTH)] + [b16o(D_MODEL)] * 2)
        out_specs += ([row_spec(wide)] * 2 + [t_spec(FOX_HEADS * FVT_ROWS)] + [row_spec(WIDTH)] * 3
                      + [t_spec(DIFF_HEADS * VT_ROWS)] + [row_spec(WIDTH)] + [row_spec(D_MODEL)] * 2)
        scratch = [pltpu.VMEM((1, LANES), F32)]
    else:
        out_shape = [f32o(WIDTH), f32o(WIDTH), f32o(FOX_HEADS), f32o(WIDTH), f32o(WIDTH)]
        out_specs = [row_spec(WIDTH), row_spec(WIDTH), row_spec(FOX_HEADS), row_spec(WIDTH), row_spec(WIDTH)]
        out_shape += [b16o(WIDTH)] * 4 + [b16o(D_MODEL)] * 2
        out_specs += [row_spec(WIDTH)] * 4 + [row_spec(D_MODEL)] * 2
        scratch = []
    return pl.pallas_call(
        functools.partial(_proj_kernel, tm=tm, seq_len=seq_len, aug=aug),
        grid=(rows // tm,),
        in_specs=[row_spec(D_MODEL), const((1, D_MODEL)),
                  pl.BlockSpec((D_MODEL, _N_ALL), lambda i: (0, 0), pipeline_mode=pl.Buffered(1)),
                  const((1, LANES))],
        out_specs=out_specs,
        out_shape=out_shape,
        scratch_shapes=scratch,
        compiler_params=pltpu.CompilerParams(
            dimension_semantics=("arbitrary",), vmem_limit_bytes=VMEM_LIMIT),
        name="proj_aug" if aug else "proj_plain",
    )(x2d, g, w_all, bfl)


def _bias_kernel(tbl_ref, o_ref, *, q_start, nq, segments, transposed, scale):
    hd = pl.program_id(0)
    far = tbl_ref[N_BUCKETS // 2 - 1, hd]
    q_axis, k_axis = (1, 0) if transposed else (0, 1)
    off = 0
    for k_start, nk, valid in segments:
        shape = (nk, nq) if transposed else (nq, nk)
        qpos = q_start + lax.broadcasted_iota(jnp.int32, shape, q_axis)
        kidx = lax.broadcasted_iota(jnp.int32, shape, k_axis)
        kpos = k_start + kidx
        rel = kpos - qpos
        n = jnp.abs(rel)
        large = jnp.full(shape, _MAX_EXACT, jnp.int32)
        for thr in _T5_THR:
            large = large + jnp.where(n >= thr, 1, 0)
        bucket = jnp.where(rel > 0, N_BUCKETS // 2, 0) + jnp.where(n < _MAX_EXACT, n, large)
        val = jnp.zeros(shape, F32)
        for b in range(N_BUCKETS):
            val = jnp.where(bucket == b, tbl_ref[b, hd], val)
        val = (val - far) * scale
        visible = ((kpos >> CHUNK_SHIFT) <= (qpos >> CHUNK_SHIFT)) & (kidx < valid)
        val = jnp.where(visible, val, NEG)
        if transposed:
            o_ref[0, off:off + nk, :] = val
        else:
            o_ref[0, :, off:off + nk] = val
        off += nk


def _bias_tiles(rel_bias, *, q_start, nq, segments, transposed, scale):
    total = sum(s[1] for s in segments)
    shape = (total, nq) if transposed else (nq, total)
    return pl.pallas_call(
        functools.partial(_bias_kernel, q_start=q_start, nq=nq, segments=segments, transposed=transposed,
                          scale=scale),
        grid=(DIFF_HEADS,),
        in_specs=[pl.BlockSpec(memory_space=pltpu.SMEM)],
        out_specs=pl.BlockSpec((1,) + shape, lambda h: (h, 0, 0)),
        out_shape=jax.ShapeDtypeStruct((DIFF_HEADS,) + shape, F32),
        name="t5_bias",
    )(rel_bias)


def _flash_t(k_ref, vt_ref, acc_ref, bad_ref, streams, qi, blk, prev_bias, diag_bias, diag_mask):
    n = len(streams)

    def scores(first_blk, width):
        start = pl.multiple_of(first_blk * blk, blk)
        return [_mm_nt(k_ref[0, pl.ds(start, width), ksl], q) for q, ksl, _ in streams]

    def step(first_blk, width, ms, biases, mask):
        start = pl.multiple_of(first_blk * blk, blk)
        sts = scores(first_blk, width)
        out, ps, alphas = [], [], []
        for s, st in enumerate(sts):
            if biases is not None:
                st = st + biases[s]()
            if mask is not None:
                st = jnp.where(mask, st, NEG)
            m_new = jnp.maximum(ms[s], jnp.max(st, axis=0, keepdims=True))
            ps.append(jnp.exp2(st - m_new).astype(BF16))
            alphas.append(jnp.exp2(ms[s] - m_new))
            out.append(m_new)
        for s, (_, _, vsl) in enumerate(streams):
            acc_ref[s] = alphas[s] * acc_ref[s] + _mm(vt_ref[0, vsl, pl.ds(start, width)], ps[s])
        return tuple(out)

    def first_step():
        start = pl.multiple_of(qi * blk, blk)
        out, ps = [], []
        for s, st in enumerate(scores(qi, blk)):
            if diag_bias is not None:
                st = st + diag_bias[s]()
            if diag_mask is not None:
                st = jnp.where(diag_mask, st, NEG)
            m = jnp.max(st, axis=0, keepdims=True)
            ps.append(jnp.exp2(st - m).astype(BF16))
            out.append(m)
        for s, (_, _, vsl) in enumerate(streams):
            acc_ref[s] = _mm(vt_ref[0, vsl, pl.ds(start, blk)], ps[s])
        return tuple(out)

    ms = first_step()
    bad_ref[...] = jnp.zeros_like(bad_ref)

    def consume(sts, first_blk, width):
        start = pl.multiple_of(first_blk * blk, blk)
        ps = [jnp.exp2(st - ms[s]).astype(BF16) for s, st in enumerate(sts)]
        worst = None
        for p in ps:
            d = jnp.max(p, axis=0, keepdims=True)
            worst = d if worst is None else jnp.maximum(worst, d)
        bad_ref[...] = jnp.maximum(bad_ref[...], worst.astype(F32))
        for s, (_, _, vsl) in enumerate(streams):
            acc_ref[s] = acc_ref[s] + _mm(vt_ref[0, vsl, pl.ds(start, width)], ps[s])

    if prev_bias is not None:
        @pl.when(qi >= 1)
        def _():
            consume([st + prev_bias[s]() for s, st in enumerate(scores(qi - 1, blk))], qi - 1, blk)

    n_plain = qi if prev_bias is None else jnp.maximum(qi - 1, 0)
    done = 0
    for w in STEP_BLOCKS:
        count = (n_plain - done) // w

        def fast_step(j, carry, w=w, done=done):
            consume(scores(done + j * w, w * blk), done + j * w, w * blk)
            return carry

        lax.fori_loop(0, count, fast_step, 0)
        done = done + count * w

    @pl.when(jnp.max(bad_ref[...]) > 2.0 ** HEADROOM)
    def _():
        c = first_step()
        if prev_bias is not None:
            c = lax.cond(qi >= 1, lambda c: step(qi - 1, blk, c, prev_bias, None), lambda c: c, c)
        lax.fori_loop(0, n_plain, lambda j, c: step(j, blk, c, None, None), c)


def _fox_kernel(qa_ref, ka_ref, vt_ref, z_ref, o_ref, acc_ref, bad_ref, *, blk, nh, sub):
    key = lax.broadcasted_iota(jnp.int32, (blk, blk), 0)
    qry = lax.broadcasted_iota(jnp.int32, (blk, blk), 1)
    hsl = [slice(hd * LANES, (hd + 1) * LANES) for hd in range(nh)]

    def query_block(i, carry):
        qi = pl.program_id(2) * sub + i
        rows = pl.ds(pl.multiple_of(i * blk, blk), blk)
        streams = [(qa_ref[0, rows, sl], sl, slice(hd * FVT_ROWS, (hd + 1) * FVT_ROWS))
                   for hd, sl in enumerate(hsl)]
        _flash_t(ka_ref, vt_ref, acc_ref, bad_ref, streams, qi, blk, None, None, key <= qry)
        for pair in range(nh // 2):
            halves = []
            for par in range(2):
                acc = acc_ref[2 * pair + par]
                halves.append(acc[0:HD, :] / acc[HD:HD + 1, :])
            o = jnp.concatenate(halves, axis=0).T
            o_ref[0, rows, hsl[pair]] = (o * _silu(z_ref[0, rows, hsl[pair]].astype(F32))).astype(BF16)
        return carry

    lax.fori_loop(0, sub, query_block, 0)


def _fox_attention(qa, ka, vt, fz, *, blk, nh, sub):
    b, t, _ = qa.shape
    step = sub * blk
    return pl.pallas_call(
        functools.partial(_fox_kernel, blk=blk, nh=nh, sub=sub),
        grid=(b, FOX_HEADS // nh, t // step),
        in_specs=[pl.BlockSpec((1, step, nh * LANES), lambda b, p, i: (b, i, p)),
                  pl.BlockSpec((1, t, nh * LANES), lambda b, p, i: (b, 0, p)),
                  pl.BlockSpec((1, nh * FVT_ROWS, t), lambda b, p, i: (b, p, 0)),
                  pl.BlockSpec((1, step, nh * HD), lambda b, p, i: (b, i, p))],
        out_specs=pl.BlockSpec((1, step, nh * HD), lambda b, p, i: (b, i, p)),
        out_shape=jax.ShapeDtypeStruct((b, t, WIDTH), BF16),
        scratch_shapes=[pltpu.VMEM((nh, FVT_ROWS, blk), F32), pltpu.VMEM((1, blk), F32)],
        compiler_params=pltpu.CompilerParams(
            dimension_semantics=("parallel", "parallel", "arbitrary"), vmem_limit_bytes=VMEM_LIMIT),
        name="fox_prompt",
    )(qa, ka, vt, fz)


def _lambda(lq1_ref, lk1_ref, lq2_ref, lk2_ref):
    s1 = jnp.sum(lq1_ref[...] * lk1_ref[...], axis=-1, keepdims=True)
    s2 = jnp.sum(lq2_ref[...] * lk2_ref[...], axis=-1, keepdims=True)
    return jnp.exp(s1) - jnp.exp(s2) + LAMBDA_INIT


def _diff_finish(o, subg, z):
    ms = jnp.mean(o * o, axis=-1, keepdims=True)
    od = o * lax.rsqrt(ms + NORM_EPS) * subg * (1.0 - LAMBDA_INIT)
    return (od * _silu(z)).astype(BF16)


def _diff_kernel(q_ref, k_ref, vt_ref, bias_ref, z_ref, lq1_ref, lk1_ref, lq2_ref, lk2_ref, subg_ref,
                 o_ref, acc_ref, bad_ref, *, blk, nh, sub):
    lane = lax.broadcasted_iota(jnp.int32, (blk, LANES), 1)
    lam = _lambda(lq1_ref, lk1_ref, lq2_ref, lk2_ref)

    def query_block(i, carry):
        qi = pl.program_id(2) * sub + i
        rows = pl.ds(pl.multiple_of(i * blk, blk), blk)
        streams, prev_bias, diag_bias = [], [], []
        for hd in range(nh):
            q = q_ref[0, rows, hd * LANES:(hd + 1) * LANES]
            for mp in range(2):
                streams.append((jnp.where((lane < HD) if mp == 0 else (lane >= HD), q, jnp.zeros_like(q)),
                                slice(hd * LANES, (hd + 1) * LANES), slice(hd * VT_ROWS, (hd + 1) * VT_ROWS)))
                prev_bias.append(lambda hd=hd: bias_ref[hd, 0:blk, :])
                diag_bias.append(lambda hd=hd: bias_ref[hd, blk:2 * blk, :])
        _flash_t(k_ref, vt_ref, acc_ref, bad_ref, streams, qi, blk, prev_bias, diag_bias, None)
        for hd in range(nh):
            a0, a1 = acc_ref[2 * hd], acc_ref[2 * hd + 1]
            o_t = a0[0:LANES, :] / a0[LANES:LANES + 1, :] - lam * (a1[0:LANES, :] / a1[LANES:LANES + 1, :])
            sl = slice(hd * LANES, (hd + 1) * LANES)
            o_ref[0, rows, sl] = _diff_finish(o_t.T, subg_ref[...], z_ref[0, rows, sl].astype(F32))
        return carry

    lax.fori_loop(0, sub, query_block, 0)


def _diff_attention(dq, dk, dvt, bias, dz, lq1, lk1, lq2, lk2, subg, *, blk, nh, sub):
    b, t, _ = dq.shape
    step = sub * blk
    small = lambda n: pl.BlockSpec((1, n), lambda b, h, i: (0, 0))
    return pl.pallas_call(
        functools.partial(_diff_kernel, blk=blk, nh=nh, sub=sub),
        grid=(b, DIFF_HEADS // nh, t // step),
        in_specs=[pl.BlockSpec((1, step, nh * LANES), lambda b, h, i: (b, i, h)),
                  pl.BlockSpec((1, t, nh * LANES), lambda b, h, i: (b, 0, h)),
                  pl.BlockSpec((1, nh * VT_ROWS, t), lambda b, h, i: (b, h, 0)),
                  pl.BlockSpec((nh, 2 * blk, blk), lambda b, h, i: (h, 0, 0)),
                  pl.BlockSpec((1, step, nh * LANES), lambda b, h, i: (b, i, h)),
                  small(HD), small(HD), small(HD), small(HD), small(2 * HD)],
        out_specs=pl.BlockSpec((1, step, nh * LANES), lambda b, h, i: (b, i, h)),
        out_shape=jax.ShapeDtypeStruct((b, t, WIDTH), BF16),
        scratch_shapes=[pltpu.VMEM((2 * nh, VT_ROWS, blk), F32), pltpu.VMEM((1, blk), F32)],
        compiler_params=pltpu.CompilerParams(
            dimension_semantics=("parallel", "parallel", "arbitrary"), vmem_limit_bytes=VMEM_LIMIT),
        name="diff_prompt",
    )(dq, dk, dvt, bias, dz, lq1, lk1, lq2, lk2, subg)


def _suffix_sums(x):
    n = x.shape[1] // LANES
    lane = lax.broadcasted_iota(jnp.int32, (x.shape[0], LANES), 1)
    after = jnp.zeros((x.shape[0], 1), F32)
    pieces = [None] * n
    for blk in reversed(range(n)):
        piece = x[:, blk * LANES:(blk + 1) * LANES]
        y = piece
        shift = 1
        while shift < LANES:
            y = y + jnp.where(lane + shift < LANES, pltpu.roll(y, LANES - shift, 1), 0.0)
            shift *= 2
        pieces[blk] = y - piece + after
        after = after + y[:, 0:1]
    return jnp.concatenate(pieces, axis=1), after


def _pad_rows(x, rows):
    return jnp.concatenate([x, jnp.zeros((rows - x.shape[0], x.shape[1]), x.dtype)], axis=0)


def _joint_softmax(s_p, s_n):
    m = jnp.maximum(jnp.max(s_p, axis=-1, keepdims=True), jnp.max(s_n, axis=-1, keepdims=True))
    p_p = jnp.exp(s_p - m)
    p_n = jnp.exp(s_n - m)
    l = jnp.sum(p_p, axis=-1, keepdims=True) + jnp.sum(p_n, axis=-1, keepdims=True)
    return p_p.astype(BF16), p_n.astype(BF16), l


def _joint_pv(p_p, p_n, l, vt_p, v_n):
    return (_mm_nt(p_p, vt_p) + _mm(p_n, v_n)) / l


def _fox_sample_kernel(q_ref, kt_ref, vt_ref, kn_ref, vn_ref, lfp_ref, lfn_ref, z_ref, o_ref, *, nq):
    r_new, total_new = _suffix_sums(lfn_ref[0])
    r_past, _ = _suffix_sums(lfp_ref[0])
    r_past = r_past + total_new
    lane = lax.broadcasted_iota(jnp.int32, (nq, LANES), 1)
    causal = lane <= lax.broadcasted_iota(jnp.int32, (nq, LANES), 0)
    sls = [slice(blk * LANES, (blk + 1) * LANES) for blk in range(PAIRS)]
    scores = []
    for blk, sl in enumerate(sls):
        q = q_ref[:, sl]
        kt = kt_ref[0, sl, :].astype(BF16)
        kn = _pad_rows(kn_ref[:, sl], LANES).astype(BF16)
        for par in range(2):
            hd = 2 * blk + par
            qm = jnp.where((lane < HD) if par == 0 else (lane >= HD), q, jnp.zeros_like(q))
            scores.append((_mm(qm, kt) + r_past[hd:hd + 1, :],
                           jnp.where(causal, _mm_nt(qm, kn) + r_new[hd:hd + 1, :], NEG)))
    weights = [_joint_softmax(s_p, s_n) for s_p, s_n in scores]
    for blk, sl in enumerate(sls):
        vt = vt_ref[0, sl, :].astype(BF16)
        vn = _pad_rows(vn_ref[:, sl], LANES).astype(BF16)
        outs = [_joint_pv(*weights[2 * blk + par], vt, vn) for par in range(2)]
        o = jnp.where(lane < HD, outs[0], outs[1])
        o_ref[:, sl] = (o * _silu(z_ref[:, sl].astype(F32))).astype(BF16)


def _fox_sample(fq, cache_kt, cache_vt, fk, fv, lf_past_t, lf_new_t, fz, *, nq):
    b, _, past = cache_kt.shape
    new = lambda: pl.BlockSpec((nq, WIDTH), lambda i: (i, 0))
    cache = lambda: pl.BlockSpec((1, WIDTH, past), lambda i: (i, 0, 0))
    return pl.pallas_call(
        functools.partial(_fox_sample_kernel, nq=nq),
        grid=(b,),
        in_specs=[new(), cache(), cache(), new(), new(),
                  pl.BlockSpec((1, FOX_HEADS, past), lambda i: (i, 0, 0)),
                  pl.BlockSpec((1, FOX_HEADS, LANES), lambda i: (i, 0, 0)),
                  new()],
        out_specs=new(),
        out_shape=jax.ShapeDtypeStruct((b * nq, WIDTH), BF16),
        compiler_params=pltpu.CompilerParams(
            dimension_semantics=("parallel",), vmem_limit_bytes=VMEM_LIMIT),
        name="fox_sample",
    )(fq, cache_kt, cache_vt, fk, fv, lf_past_t, lf_new_t, fz)


def _diff_sample_kernel(q_ref, kt_ref, v_ref, kn_ref, vn_ref, bias_ref, z_ref,
                        lq1_ref, lk1_ref, lq2_ref, lk2_ref, subg_ref, o_ref, *, nq, past):
    lane = lax.broadcasted_iota(jnp.int32, (nq, LANES), 1)
    lam = _lambda(lq1_ref, lk1_ref, lq2_ref, lk2_ref)
    sls = [slice(hd * LANES, (hd + 1) * LANES) for hd in range(DIFF_HEADS)]
    scores = []
    for hd, sl in enumerate(sls):
        q = q_ref[:, sl]
        kt = kt_ref[0, sl, :].astype(BF16)
        kn = _pad_rows(kn_ref[:, sl], LANES).astype(BF16)
        for mp in range(2):
            qm = jnp.where((lane < HD) if mp == 0 else (lane >= HD), q, jnp.zeros_like(q))
            scores.append((_mm(qm, kt) + bias_ref[hd, :, 0:past],
                           _mm_nt(qm, kn) + bias_ref[hd, :, past:past + LANES]))
    weights = [_joint_softmax(s_p, s_n) for s_p, s_n in scores]
    for hd, sl in enumerate(sls):
        vt = v_ref[0, pl.ds(hd, past, stride=DIFF_HEADS), :].T.astype(BF16)
        vn = _pad_rows(vn_ref[:, sl], LANES).astype(BF16)
        outs = [_joint_pv(*weights[2 * hd + mp], vt, vn) for mp in range(2)]
        o_ref[:, sl] = _diff_finish(outs[0] - lam * outs[1], subg_ref[...], z_ref[:, sl].astype(F32))


def _diff_sample(dq, cache_kt, cache_v, dk, dv, bias, dz, lq1, lk1, lq2, lk2, subg, *, nq):
    b, _, past = cache_kt.shape
    new = lambda: pl.BlockSpec((nq, WIDTH), lambda i: (i, 0))
    small = lambda n: pl.BlockSpec((1, n), lambda i: (0, 0))
    return pl.pallas_call(
        functools.partial(_diff_sample_kernel, nq=nq, past=past),
        grid=(b,),
        in_specs=[new(), pl.BlockSpec((1, WIDTH, past), lambda i: (i, 0, 0)),
                  pl.BlockSpec((1, past * DIFF_HEADS, 2 * HD), lambda i: (i, 0, 0)), new(), new(),
                  pl.BlockSpec((DIFF_HEADS, nq, past + LANES), lambda i: (0, 0, 0)),
                  new(), small(HD), small(HD), small(HD), small(HD), small(2 * HD)],
        out_specs=new(),
        out_shape=jax.ShapeDtypeStruct((b * nq, WIDTH), BF16),
        compiler_params=pltpu.CompilerParams(
            dimension_semantics=("parallel",), vmem_limit_bytes=VMEM_LIMIT),
        name="diff_sample",
    )(dq, cache_kt, cache_v, dk, dv, bias, dz, lq1, lk1, lq2, lk2, subg)


def _out_kernel(yf_ref, yd_ref, ga_ref, gb_ref, x_ref, wuf_ref, wud_ref, wo_ref, g_ref, o_ref):
    merged = (ga_ref[...].astype(F32) * _mm(yf_ref[...], wuf_ref[...])
              + gb_ref[...].astype(F32) * _mm(yd_ref[...], wud_ref[...]))
    out = x_ref[...] + _mm(merged.astype(BF16), wo_ref[...])
    ms = jnp.mean(out * out, axis=-1, keepdims=True)
    o_ref[...] = out * lax.rsqrt(ms + NORM_EPS) * g_ref[...]


def _output(yf, yd, ga, gb, x2d, wuf, wud, wo, g, *, tm):
    rows = x2d.shape[0]
    assert rows % tm == 0
    row_spec = lambda n: pl.BlockSpec((tm, n), lambda i: (i, 0))
    const = lambda shape: pl.BlockSpec(shape, lambda i: (0, 0))
    return pl.pallas_call(
        _out_kernel,
        grid=(rows // tm,),
        in_specs=[row_spec(WIDTH), row_spec(WIDTH), row_spec(D_MODEL), row_spec(D_MODEL), row_spec(D_MODEL),
                  const((WIDTH, D_MODEL)), const((WIDTH, D_MODEL)), const((D_MODEL, D_MODEL)),
                  const((1, D_MODEL))],
        out_specs=row_spec(D_MODEL),
        out_shape=jax.ShapeDtypeStruct((rows, D_MODEL), F32),
        compiler_params=pltpu.CompilerParams(
            dimension_semantics=("parallel",), vmem_limit_bytes=VMEM_LIMIT),
        name="out_proj",
    )(yf, yd, ga, gb, x2d, wuf, wud, wo, g)


def _forget_lane_heads():
    lanes = np.arange(LANES)
    heads = (lanes % HD) // FL_SLOT
    heads[:FOX_HEADS] = np.arange(FOX_HEADS)
    return heads


def _pack_w_in(w_in, b_forget):
    sizes = (WIDTH, WIDTH, WIDTH, WIDTH, FOX_HEADS, WIDTH, WIDTH, WIDTH, WIDTH, D_MODEL, D_MODEL)
    offs = np.cumsum((0,) + sizes)
    fq, fk, fv, fz, fl, dq, dk, dv, dz, ga, gb = (w_in[:, offs[i]:offs[i + 1]] for i in range(len(sizes)))
    heads = _forget_lane_heads()
    w_all = jnp.concatenate([fq, fk, fv, fz, dq, dk, dv, dz, ga, gb, fl[:, heads]], axis=1).astype(BF16)
    return w_all, b_forget[heads][None, :]


def kernel(x_prompt, x_sample, cache_fox_k, cache_fox_v, cache_fox_logf, cache_diff_k, cache_diff_v,
           norm_in_g, w_in, b_forget, lambda_q1, lambda_k1, lambda_q2, lambda_k2, subln_g,
           w_up_fox, w_up_diff, w_o, rel_bias, final_norm_g):
    bp, tp, _ = x_prompt.shape
    bs, ts, _ = x_sample.shape
    past = cache_fox_k.shape[2]
    blk, tm = ATTN_BLOCK, ROW_TILE
    assert tp % (QUERY_BLOCKS_PER_STEP * blk) == 0 and blk > _T5_THR[-1] and blk % CHUNK == 0

    w_all, bfl = _pack_w_in(w_in[0], b_forget[0])
    g_in = norm_in_g[0][None, :]
    g_out = final_norm_g[None, :]
    wuf, wud, wo = w_up_fox[0].astype(BF16), w_up_diff[0].astype(BF16), w_o[0].astype(BF16)
    lq1, lk1, lq2, lk2 = (a[0][None, :] for a in (lambda_q1, lambda_k1, lambda_q2, lambda_k2))
    subg = subln_g[0][None, :]

    xp = x_prompt.reshape(bp * tp, D_MODEL)
    (fkt, fvt, lft, dkt, dv4, qa, ka, vtf, fz, dq, dkb, vtd, dz, ga, gb) = _project(
        xp, g_in, w_all, bfl, seq_len=tp, aug=True, tm=tm)
    r3 = lambda a: a.reshape(bp, tp, a.shape[-1])
    yf = _fox_attention(r3(qa), r3(ka), vtf, r3(fz), blk=blk, nh=FOX_STEP_HEADS, sub=QUERY_BLOCKS_PER_STEP)
    bias_p = _bias_tiles(rel_bias, q_start=blk, nq=blk, segments=((0, blk, blk), (blk, blk, blk)),
                         transposed=True, scale=LOG2E)
    yd = _diff_attention(r3(dq), r3(dkb), vtd, bias_p, r3(dz), lq1, lk1, lq2, lk2, subg, blk=blk,
                         nh=DIFF_STEP_HEADS, sub=QUERY_BLOCKS_PER_STEP)
    y_p = _output(yf.reshape(bp * tp, WIDTH), yd.reshape(bp * tp, WIDTH), ga, gb, xp, wuf, wud, wo, g_out, tm=tm)
    fox_k_p = jnp.transpose(fkt.reshape(bp, FOX_HEADS, HD, tp), (0, 3, 1, 2))[None]
    fox_v_p = jnp.transpose(fvt.reshape(bp, FOX_HEADS, HD, tp), (0, 3, 1, 2))[None]
    fox_lf_p = jnp.transpose(lft, (0, 2, 1))[None]
    diff_k_p = jnp.transpose(dkt.reshape(bp, DIFF_HEADS, 2, HD, tp), (0, 4, 1, 2, 3))[None]
    diff_v_p = dv4.reshape(1, bp, tp, DIFF_HEADS, 2 * HD)

    xs = x_sample.reshape(bs * ts, D_MODEL)
    (fk_s, fv_s, lf_s, dk_s, dv_s, fq_s, fz_s, dq_s, dz_s, ga_s, gb_s) = _project(
        xs, g_in, w_all, bfl, seq_len=ts, aug=False, tm=tm)
    fox_kt = jnp.transpose(cache_fox_k[0], (0, 2, 3, 1)).reshape(bs, WIDTH, past)
    fox_vt = jnp.transpose(cache_fox_v[0], (0, 2, 3, 1)).reshape(bs, WIDTH, past)
    diff_kt = jnp.transpose(cache_diff_k[0], (0, 2, 3, 4, 1)).reshape(bs, WIDTH, past)
    diff_v = cache_diff_v[0].reshape(bs, past * DIFF_HEADS, 2 * HD)
    lf_past_t = jnp.swapaxes(cache_fox_logf[0], 1, 2)
    lf_new_t = jnp.pad(jnp.swapaxes(lf_s.reshape(bs, ts, FOX_HEADS), 1, 2), ((0, 0), (0, 0), (0, LANES - ts)))
    yf_s = _fox_sample(fq_s, fox_kt, fox_vt, fk_s, fv_s, lf_past_t, lf_new_t, fz_s, nq=ts)
    bias_s = _bias_tiles(rel_bias, q_start=past, nq=ts, segments=((0, past, past), (past, LANES, ts)),
                         transposed=False, scale=1.0)
    yd_s = _diff_sample(dq_s, diff_kt, diff_v, dk_s, dv_s, bias_s, dz_s, lq1, lk1, lq2, lk2, subg, nq=ts)
    y_s = _output(yf_s, yd_s, ga_s, gb_s, xs, wuf, wud, wo, g_out, tm=tm)

    return (y_p.reshape(bp, tp, D_MODEL), y_s.reshape(bs, ts, D_MODEL),
            fox_k_p, fox_v_p, fox_lf_p, diff_k_p, diff_v_p,
            fk_s.reshape(1, bs, ts, FOX_HEADS, HD), fv_s.reshape(1, bs, ts, FOX_HEADS, HD),
            lf_s.reshape(1, bs, ts, FOX_HEADS),
            dk_s.reshape(1, bs, ts, DIFF_HEADS, 2, HD), dv_s.reshape(1, bs, ts, DIFF_HEADS, 2 * HD))
```

```python
import functools
import math

import numpy as np
import jax
import jax.numpy as jnp
from jax import lax
from jax.experimental import pallas as pl
from jax.experimental.pallas import tpu as pltpu

F32 = jnp.float32
BF16 = jnp.bfloat16

D_MODEL = 1024
HD = 64
FOX_HEADS = 8
DIFF_HEADS = 4
WIDTH = 512
CHUNK = 64
CHUNK_SHIFT = CHUNK.bit_length() - 1
N_BUCKETS = 32
MAX_DISTANCE = 128
NORM_EPS = 1e-6
SCALE = HD ** -0.5
NEG = -1e30
LAMBDA_INIT = 0.8 - 0.6 * math.exp(-0.3 * 0)
LOG2E = math.log2(math.e)

LANES = 128
PAIRS = WIDTH // LANES
VT_PAD = 16
VT_ROWS = LANES + VT_PAD
FVT_ROWS = HD + VT_PAD

_O_FQ, _O_FK, _O_FV, _O_FZ = 0, 512, 1024, 1536
_O_DQ, _O_DK, _O_DV, _O_DZ = 2048, 2560, 3072, 3584
_O_GA, _O_GB = 4096, 5120
_O_FL = 6144
_N_ALL = 6272
FL_SLOT = 8

VMEM_LIMIT = 56 * 1024 * 1024
ATTN_BLOCK = 256
ROW_TILE = 512

FOX_STEP_HEADS = 8
DIFF_STEP_HEADS = 4
HEADROOM = 64.0
QUERY_BLOCKS_PER_STEP = 4
STEP_BLOCKS = (4, 2, 1)


def _t5_thresholds():
    nb = N_BUCKETS // 2
    max_exact = nb // 2
    n = np.arange(0, 4 * MAX_DISTANCE)
    large = max_exact + (np.log(np.maximum(n, 1).astype(np.float32) / max_exact)
                         / math.log(MAX_DISTANCE / max_exact) * (nb - max_exact)).astype(np.int32)
    large = np.minimum(large, nb - 1)
    thr = [int(np.argmax(large >= b)) for b in range(max_exact + 1, nb)]
    return max_exact, tuple(thr)


_MAX_EXACT, _T5_THR = _t5_thresholds()


def _mm(a, b):
    return jnp.dot(a, b, preferred_element_type=F32)


def _mm_nt(a, b):
    return lax.dot_general(a, b, (((1,), (1,)), ((), ())), preferred_element_type=F32)


def _log_sigmoid(x):
    return -(jnp.maximum(-x, 0.0) + jnp.log1p(jnp.exp(-jnp.abs(x))))


def _sigmoid(x):
    return 1.0 / (1.0 + jnp.exp(-x))


def _silu(x):
    return x * _sigmoid(x)


def _split3(x):
    p1 = x.astype(BF16).astype(F32)
    r1 = x - p1
    p2 = r1.astype(BF16).astype(F32)
    p3 = r1 - p2
    return p1, p2, p3


def _proj_kernel(x_ref, g_ref, w_ref, bfl_ref, *refs, tm, seq_len, aug):
    if aug:
        (fkt_ref, fvt_ref, lft_ref, dkt_ref, dv4_ref, qa_ref, ka_ref, vtf_ref,
         fz_ref, dq_ref, dkb_ref, vtd_ref, dz_ref, ga_ref, gb_ref, carry_ref) = refs
    else:
        (fk_ref, fv_ref, lf_ref, dk_ref, dv_ref, fq_ref,
         fz_ref, dq_ref, dz_ref, ga_ref, gb_ref) = refs

    x = x_ref[...]
    ms = jnp.mean(x * x, axis=-1, keepdims=True)
    h = (x * lax.rsqrt(ms + NORM_EPS) * g_ref[...]).astype(BF16)

    def grp(off, n=WIDTH):
        return _mm(h, w_ref[:, off:off + n])

    if not aug:
        fq_ref[...] = (grp(_O_FQ) * SCALE).astype(BF16)
        fk_ref[...] = grp(_O_FK)
        fv_ref[...] = grp(_O_FV)
        fz_ref[...] = grp(_O_FZ).astype(BF16)
        dq_ref[...] = (grp(_O_DQ) * SCALE).astype(BF16)
        dk_ref[...] = grp(_O_DK)
        dv_ref[...] = grp(_O_DV)
        dz_ref[...] = grp(_O_DZ).astype(BF16)
        ga_ref[...] = _sigmoid(grp(_O_GA, D_MODEL)).astype(BF16)
        gb_ref[...] = _sigmoid(grp(_O_GB, D_MODEL)).astype(BF16)
        lf_ref[...] = _log_sigmoid(grp(_O_FL, LANES) + bfl_ref[...])[:, :FOX_HEADS]
        return

    lf = _log_sigmoid(grp(_O_FL, LANES) + bfl_ref[...])
    lft_ref[0] = lf.T[:FOX_HEADS, :]
    row = lax.broadcasted_iota(jnp.int32, (tm, LANES), 0)
    c = lf
    shift = 1
    while shift < tm:
        c = c + jnp.where(row >= shift, pltpu.roll(c, shift, 0), 0.0)
        shift *= 2
    tiles_per_seq = seq_len // tm

    @pl.when(pl.program_id(0) % tiles_per_seq == 0)
    def _():
        carry_ref[...] = jnp.zeros_like(carry_ref)

    c = c + carry_ref[...]
    carry_ref[...] = c[tm - 1:tm, :]
    c_hi, c_mid, c_lo = _split3(c * LOG2E)

    fq = grp(_O_FQ) * (SCALE * LOG2E)
    fk = grp(_O_FK)
    fv = grp(_O_FV)
    fvt = fv.T
    fkt_ref[0] = fk.T
    fvt_ref[0] = fvt
    tail = jnp.where(lax.broadcasted_iota(jnp.int32, (VT_PAD, tm), 0) == 0, 1.0, 0.0).astype(BF16)
    lane = lax.broadcasted_iota(jnp.int32, (tm, LANES), 1)
    pos = lane & (FL_SLOT - 1)
    q_slots = jnp.where(pos == 0, c_hi, jnp.where(pos == 1, c_mid, jnp.where(
        pos == 2, c_lo, jnp.where(pos < 6, 1.0, 0.0))))
    k_slots = jnp.where(pos < 3, 1.0, jnp.where(pos == 3, -c_hi, jnp.where(
        pos == 4, -c_mid, jnp.where(pos == 5, -c_lo, 0.0))))
    for hd in range(FOX_HEADS):
        blk, par = divmod(hd, 2)
        sl = slice(blk * LANES, (blk + 1) * LANES)
        data = (lane < HD) if par == 0 else (lane >= HD)
        slot0 = (1 - par) * HD + hd * FL_SLOT
        own = (lane >= slot0) & (lane < slot0 + FL_SLOT)
        q_aug = jnp.where(own, q_slots, 0.0)
        k_aug = jnp.where(own, k_slots, 0.0)
        osl = slice(hd * LANES, (hd + 1) * LANES)
        qa_ref[:, osl] = jnp.where(data, fq[:, sl], q_aug).astype(BF16)
        ka_ref[:, osl] = jnp.where(data, fk[:, sl], k_aug).astype(BF16)
        base = hd * FVT_ROWS
        vtf_ref[0, base:base + HD, :] = fvt[hd * HD:(hd + 1) * HD, :].astype(BF16)
        vtf_ref[0, base + HD:base + FVT_ROWS, :] = tail

    ga_ref[...] = _sigmoid(grp(_O_GA, D_MODEL)).astype(BF16)
    gb_ref[...] = _sigmoid(grp(_O_GB, D_MODEL)).astype(BF16)

    dk = grp(_O_DK)
    dkt_ref[0] = dk.T
    dkb_ref[...] = dk.astype(BF16)
    dv = grp(_O_DV)
    dv4_ref[...] = dv.reshape(tm, DIFF_HEADS, 2 * HD)
    for hd in range(DIFF_HEADS):
        base = hd * VT_ROWS
        vtd_ref[0, base:base + LANES, :] = dv[:, hd * LANES:(hd + 1) * LANES].T.astype(BF16)
        vtd_ref[0, base + LANES:base + VT_ROWS, :] = tail

    dq_ref[...] = (grp(_O_DQ) * (SCALE * LOG2E)).astype(BF16)
    fz_ref[...] = grp(_O_FZ).astype(BF16)
    dz_ref[...] = grp(_O_DZ).astype(BF16)


def _project(x2d, g, w_all, bfl, *, seq_len, aug, tm):
    rows = x2d.shape[0]
    assert rows % tm == 0
    row_spec = lambda n: pl.BlockSpec((tm, n), lambda i: (i, 0))
    const = lambda shape: pl.BlockSpec(shape, lambda i: (0, 0))
    f32o = lambda n: jax.ShapeDtypeStruct((rows, n), F32)
    b16o = lambda n: jax.ShapeDtypeStruct((rows, n), BF16)
    if aug:
        assert seq_len % tm == 0
        wide = FOX_HEADS * LANES
        tps = seq_len // tm
        t_shape = lambda n, dt: jax.ShapeDtypeStruct((rows // seq_len, n, seq_len), dt)
        t_spec = lambda n: pl.BlockSpec((1, n, tm), lambda i: (i // tps, 0, i % tps))
        out_shape = [t_shape(WIDTH, F32), t_shape(WIDTH, F32), t_shape(FOX_HEADS, F32), t_shape(WIDTH, F32),
                     jax.ShapeDtypeStruct((rows, DIFF_HEADS, 2 * HD), F32)]
        out_specs = [t_spec(WIDTH), t_spec(WIDTH), t_spec(FOX_HEADS), t_spec(WIDTH),
                     pl.BlockSpec((tm, DIFF_HEADS, 2 * HD), lambda i: (i, 0, 0))]
        out_shape += ([b16o(wide)] * 2 + [t_shape(FOX_HEADS * FVT_ROWS, BF16)] + [b16o(WIDTH)] * 3
                      + [t_shape(DIFF_HEADS * VT_ROWS, BF16)] + [b16o(WIDTH)] + [b16o(D_MODEL)] * 2)
        out_specs += ([row_spec(wide)] * 2 + [t_spec(FOX_HEADS * FVT_ROWS)] + [row_spec(WIDTH)] * 3
                      + [t_spec(DIFF_HEADS * VT_ROWS)] + [row_spec(WIDTH)] + [row_spec(D_MODEL)] * 2)
        scratch = [pltpu.VMEM((1, LANES), F32)]
    else:
        out_shape = [f32o(WIDTH), f32o(WIDTH), f32o(FOX_HEADS), f32o(WIDTH), f32o(WIDTH)]
        out_specs = [row_spec(WIDTH), row_spec(WIDTH), row_spec(FOX_HEADS), row_spec(WIDTH), row_spec(WIDTH)]
        out_shape += [b16o(WIDTH)] * 4 + [b16o(D_MODEL)] * 2
        out_specs += [row_spec(WIDTH)] * 4 + [row_spec(D_MODEL)] * 2
        scratch = []
    return pl.pallas_call(
        functools.partial(_proj_kernel, tm=tm, seq_len=seq_len, aug=aug),
        grid=(rows // tm,),
        in_specs=[row_spec(D_MODEL), const((1, D_MODEL)),
                  pl.BlockSpec((D_MODEL, _N_ALL), lambda i: (0, 0), pipeline_mode=pl.Buffered(1)),
                  const((1, LANES))],
        out_specs=out_specs,
        out_shape=out_shape,
        scratch_shapes=scratch,
        compiler_params=pltpu.CompilerParams(
            dimension_semantics=("arbitrary",), vmem_limit_bytes=VMEM_LIMIT),
        name="proj_aug" if aug else "proj_plain",
    )(x2d, g, w_all, bfl)


def _bias_kernel(tbl_ref, o_ref, *, q_start, nq, segments, transposed, scale):
    hd = pl.program_id(0)
    far = tbl_ref[N_BUCKETS // 2 - 1, hd]
    q_axis, k_axis = (1, 0) if transposed else (0, 1)
    off = 0
    for k_start, nk, valid in segments:
        shape = (nk, nq) if transposed else (nq, nk)
        qpos = q_start + lax.broadcasted_iota(jnp.int32, shape, q_axis)
        kidx = lax.broadcasted_iota(jnp.int32, shape, k_axis)
        kpos = k_start + kidx
        rel = kpos - qpos
        n = jnp.abs(rel)
        large = jnp.full(shape, _MAX_EXACT, jnp.int32)
        for thr in _T5_THR:
            large = large + jnp.where(n >= thr, 1, 0)
        bucket = jnp.where(rel > 0, N_BUCKETS // 2, 0) + jnp.where(n < _MAX_EXACT, n, large)
        val = jnp.zeros(shape, F32)
        for b in range(N_BUCKETS):
            val = jnp.where(bucket == b, tbl_ref[b, hd], val)
        val = (val - far) * scale
        visible = ((kpos >> CHUNK_SHIFT) <= (qpos >> CHUNK_SHIFT)) & (kidx < valid)
        val = jnp.where(visible, val, NEG)
        if transposed:
            o_ref[0, off:off + nk, :] = val
        else:
            o_ref[0, :, off:off + nk] = val
        off += nk


def _bias_tiles(rel_bias, *, q_start, nq, segments, transposed, scale):
    total = sum(s[1] for s in segments)
    shape = (total, nq) if transposed else (nq, total)
    return pl.pallas_call(
        functools.partial(_bias_kernel, q_start=q_start, nq=nq, segments=segments, transposed=transposed,
                          scale=scale),
        grid=(DIFF_HEADS,),
        in_specs=[pl.BlockSpec(memory_space=pltpu.SMEM)],
        out_specs=pl.BlockSpec((1,) + shape, lambda h: (h, 0, 0)),
        out_shape=jax.ShapeDtypeStruct((DIFF_HEADS,) + shape, F32),
        name="t5_bias",
    )(rel_bias)


def _flash_t(k_ref, vt_ref, acc_ref, bad_ref, streams, qi, blk, prev_bias, diag_bias, diag_mask):
    n = len(streams)

    def scores(first_blk, width):
        start = pl.multiple_of(first_blk * blk, blk)
        return [_mm_nt(k_ref[0, pl.ds(start, width), ksl], q) for q, ksl, _ in streams]

    def step(first_blk, width, ms, biases, mask):
        start = pl.multiple_of(first_blk * blk, blk)
        sts = scores(first_blk, width)
        out, ps, alphas = [], [], []
        for s, st in enumerate(sts):
            if biases is not None:
                st = st + biases[s]()
            if mask is not None:
                st = jnp.where(mask, st, NEG)
            m_new = jnp.maximum(ms[s], jnp.max(st, axis=0, keepdims=True))
            ps.append(jnp.exp2(st - m_new).astype(BF16))
            alphas.append(jnp.exp2(ms[s] - m_new))
            out.append(m_new)
        for s, (_, _, vsl) in enumerate(streams):
            acc_ref[s] = alphas[s] * acc_ref[s] + _mm(vt_ref[0, vsl, pl.ds(start, width)], ps[s])
        return tuple(out)

    def first_step():
        start = pl.multiple_of(qi * blk, blk)
        out, ps = [], []
        for s, st in enumerate(scores(qi, blk)):
            if diag_bias is not None:
                st = st + diag_bias[s]()
            if diag_mask is not None:
                st = jnp.where(diag_mask, st, NEG)
            m = jnp.max(st, axis=0, keepdims=True)
            ps.append(jnp.exp2(st - m).astype(BF16))
            out.append(m)
        for s, (_, _, vsl) in enumerate(streams):
            acc_ref[s] = _mm(vt_ref[0, vsl, pl.ds(start, blk)], ps[s])
        return tuple(out)

    ms = first_step()
    bad_ref[...] = jnp.zeros_like(bad_ref)

    def consume(sts, first_blk, width):
        start = pl.multiple_of(first_blk * blk, blk)
        ps = [jnp.exp2(st - ms[s]).astype(BF16) for s, st in enumerate(sts)]
        worst = None
        for p in ps:
            d = jnp.max(p, axis=0, keepdims=True)
            worst = d if worst is None else jnp.maximum(worst, d)
        bad_ref[...] = jnp.maximum(bad_ref[...], worst.astype(F32))
        for s, (_, _, vsl) in enumerate(streams):
            acc_ref[s] = acc_ref[s] + _mm(vt_ref[0, vsl, pl.ds(start, width)], ps[s])

    if prev_bias is not None:
        @pl.when(qi >= 1)
        def _():
            consume([st + prev_bias[s]() for s, st in enumerate(scores(qi - 1, blk))], qi - 1, blk)

    n_plain = qi if prev_bias is None else jnp.maximum(qi - 1, 0)
    done = 0
    for w in STEP_BLOCKS:
        count = (n_plain - done) // w

        def fast_step(j, carry, w=w, done=done):
            consume(scores(done + j * w, w * blk), done + j * w, w * blk)
            return carry

        lax.fori_loop(0, count, fast_step, 0)
        done = done + count * w

    @pl.when(jnp.max(bad_ref[...]) > 2.0 ** HEADROOM)
    def _():
        c = first_step()
        if prev_bias is not None:
            c = lax.cond(qi >= 1, lambda c: step(qi - 1, blk, c, prev_bias, None), lambda c: c, c)
        lax.fori_loop(0, n_plain, lambda j, c: step(j, blk, c, None, None), c)


def _fox_kernel(qa_ref, ka_ref, vt_ref, z_ref, o_ref, acc_ref, bad_ref, *, blk, nh, sub):
    key = lax.broadcasted_iota(jnp.int32, (blk, blk), 0)
    qry = lax.broadcasted_iota(jnp.int32, (blk, blk), 1)
    hsl = [slice(hd * LANES, (hd + 1) * LANES) for hd in range(nh)]

    def query_block(i, carry):
        qi = pl.program_id(2) * sub + i
        rows = pl.ds(pl.multiple_of(i * blk, blk), blk)
        streams = [(qa_ref[0, rows, sl], sl, slice(hd * FVT_ROWS, (hd + 1) * FVT_ROWS))
                   for hd, sl in enumerate(hsl)]
        _flash_t(ka_ref, vt_ref, acc_ref, bad_ref, streams, qi, blk, None, None, key <= qry)
        for pair in range(nh // 2):
            halves = []
            for par in range(2):
                acc = acc_ref[2 * pair + par]
                halves.append(acc[0:HD, :] / acc[HD:HD + 1, :])
            o = jnp.concatenate(halves, axis=0).T
            o_ref[0, rows, hsl[pair]] = (o * _silu(z_ref[0, rows, hsl[pair]].astype(F32))).astype(BF16)
        return carry

    lax.fori_loop(0, sub, query_block, 0)


def _fox_attention(qa, ka, vt, fz, *, blk, nh, sub):
    b, t, _ = qa.shape
    step = sub * blk
    return pl.pallas_call(
        functools.partial(_fox_kernel, blk=blk, nh=nh, sub=sub),
        grid=(b, FOX_HEADS // nh, t // step),
        in_specs=[pl.BlockSpec((1, step, nh * LANES), lambda b, p, i: (b, i, p)),
                  pl.BlockSpec((1, t, nh * LANES), lambda b, p, i: (b, 0, p)),
                  pl.BlockSpec((1, nh * FVT_ROWS, t), lambda b, p, i: (b, p, 0)),
                  pl.BlockSpec((1, step, nh * HD), lambda b, p, i: (b, i, p))],
        out_specs=pl.BlockSpec((1, step, nh * HD), lambda b, p, i: (b, i, p)),
        out_shape=jax.ShapeDtypeStruct((b, t, WIDTH), BF16),
        scratch_shapes=[pltpu.VMEM((nh, FVT_ROWS, blk), F32), pltpu.VMEM((1, blk), F32)],
        compiler_params=pltpu.CompilerParams(
            dimension_semantics=("parallel", "parallel", "arbitrary"), vmem_limit_bytes=VMEM_LIMIT),
        name="fox_prompt",
    )(qa, ka, vt, fz)


def _lambda(lq1_ref, lk1_ref, lq2_ref, lk2_ref):
    s1 = jnp.sum(lq1_ref[...] * lk1_ref[...], axis=-1, keepdims=True)
    s2 = jnp.sum(lq2_ref[...] * lk2_ref[...], axis=-1, keepdims=True)
    return jnp.exp(s1) - jnp.exp(s2) + LAMBDA_INIT


def _diff_finish(o, subg, z):
    ms = jnp.mean(o * o, axis=-1, keepdims=True)
    od = o * lax.rsqrt(ms + NORM_EPS) * subg * (1.0 - LAMBDA_INIT)
    return (od * _silu(z)).astype(BF16)


def _diff_kernel(q_ref, k_ref, vt_ref, bias_ref, z_ref, lq1_ref, lk1_ref, lq2_ref, lk2_ref, subg_ref,
                 o_ref, acc_ref, bad_ref, *, blk, nh, sub):
    lane = lax.broadcasted_iota(jnp.int32, (blk, LANES), 1)
    lam = _lambda(lq1_ref, lk1_ref, lq2_ref, lk2_ref)

    def query_block(i, carry):
        qi = pl.program_id(2) * sub + i
        rows = pl.ds(pl.multiple_of(i * blk, blk), blk)
        streams, prev_bias, diag_bias = [], [], []
        for hd in range(nh):
            q = q_ref[0, rows, hd * LANES:(hd + 1) * LANES]
            for mp in range(2):
                streams.append((jnp.where((lane < HD) if mp == 0 else (lane >= HD), q, jnp.zeros_like(q)),
                                slice(hd * LANES, (hd + 1) * LANES), slice(hd * VT_ROWS, (hd + 1) * VT_ROWS)))
                prev_bias.append(lambda hd=hd: bias_ref[hd, 0:blk, :])
                diag_bias.append(lambda hd=hd: bias_ref[hd, blk:2 * blk, :])
        _flash_t(k_ref, vt_ref, acc_ref, bad_ref, streams, qi, blk, prev_bias, diag_bias, None)
        for hd in range(nh):
            a0, a1 = acc_ref[2 * hd], acc_ref[2 * hd + 1]
            o_t = a0[0:LANES, :] / a0[LANES:LANES + 1, :] - lam * (a1[0:LANES, :] / a1[LANES:LANES + 1, :])
            sl = slice(hd * LANES, (hd + 1) * LANES)
            o_ref[0, rows, sl] = _diff_finish(o_t.T, subg_ref[...], z_ref[0, rows, sl].astype(F32))
        return carry

    lax.fori_loop(0, sub, query_block, 0)


def _diff_attention(dq, dk, dvt, bias, dz, lq1, lk1, lq2, lk2, subg, *, blk, nh, sub):
    b, t, _ = dq.shape
    step = sub * blk
    small = lambda n: pl.BlockSpec((1, n), lambda b, h, i: (0, 0))
    return pl.pallas_call(
        functools.partial(_diff_kernel, blk=blk, nh=nh, sub=sub),
        grid=(b, DIFF_HEADS // nh, t // step),
        in_specs=[pl.BlockSpec((1, step, nh * LANES), lambda b, h, i: (b, i, h)),
                  pl.BlockSpec((1, t, nh * LANES), lambda b, h, i: (b, 0, h)),
                  pl.BlockSpec((1, nh * VT_ROWS, t), lambda b, h, i: (b, h, 0)),
                  pl.BlockSpec((nh, 2 * blk, blk), lambda b, h, i: (h, 0, 0)),
                  pl.BlockSpec((1, step, nh * LANES), lambda b, h, i: (b, i, h)),
                  small(HD), small(HD), small(HD), small(HD), small(2 * HD)],
        out_specs=pl.BlockSpec((1, step, nh * LANES), lambda b, h, i: (b, i, h)),
        out_shape=jax.ShapeDtypeStruct((b, t, WIDTH), BF16),
        scratch_shapes=[pltpu.VMEM((2 * nh, VT_ROWS, blk), F32), pltpu.VMEM((1, blk), F32)],
        compiler_params=pltpu.CompilerParams(
            dimension_semantics=("parallel", "parallel", "arbitrary"), vmem_limit_bytes=VMEM_LIMIT),
        name="diff_prompt",
    )(dq, dk, dvt, bias, dz, lq1, lk1, lq2, lk2, subg)


def _suffix_sums(x):
    n = x.shape[1] // LANES
    lane = lax.broadcasted_iota(jnp.int32, (x.shape[0], LANES), 1)
    after = jnp.zeros((x.shape[0], 1), F32)
    pieces = [None] * n
    for blk in reversed(range(n)):
        piece = x[:, blk * LANES:(blk + 1) * LANES]
        y = piece
        shift = 1
        while shift < LANES:
            y = y + jnp.where(lane + shift < LANES, pltpu.roll(y, LANES - shift, 1), 0.0)
            shift *= 2
        pieces[blk] = y - piece + after
        after = after + y[:, 0:1]
    return jnp.concatenate(pieces, axis=1), after


def _pad_rows(x, rows):
    return jnp.concatenate([x, jnp.zeros((rows - x.shape[0], x.shape[1]), x.dtype)], axis=0)


def _joint_softmax(s_p, s_n):
    m = jnp.maximum(jnp.max(s_p, axis=-1, keepdims=True), jnp.max(s_n, axis=-1, keepdims=True))
    p_p = jnp.exp(s_p - m)
    p_n = jnp.exp(s_n - m)
    l = jnp.sum(p_p, axis=-1, keepdims=True) + jnp.sum(p_n, axis=-1, keepdims=True)
    return p_p.astype(BF16), p_n.astype(BF16), l


def _joint_pv(p_p, p_n, l, vt_p, v_n):
    return (_mm_nt(p_p, vt_p) + _mm(p_n, v_n)) / l


def _fox_sample_kernel(q_ref, kt_ref, vt_ref, kn_ref, vn_ref, lfp_ref, lfn_ref, z_ref, o_ref, *, nq):
    r_new, total_new = _suffix_sums(lfn_ref[0])
    r_past, _ = _suffix_sums(lfp_ref[0])
    r_past = r_past + total_new
    lane = lax.broadcasted_iota(jnp.int32, (nq, LANES), 1)
    causal = lane <= lax.broadcasted_iota(jnp.int32, (nq, LANES), 0)
    sls = [slice(blk * LANES, (blk + 1) * LANES) for blk in range(PAIRS)]
    scores = []
    for blk, sl in enumerate(sls):
        q = q_ref[:, sl]
        kt = kt_ref[0, sl, :].astype(BF16)
        kn = _pad_rows(kn_ref[:, sl], LANES).astype(BF16)
        for par in range(2):
            hd = 2 * blk + par
            qm = jnp.where((lane < HD) if par == 0 else (lane >= HD), q, jnp.zeros_like(q))
            scores.append((_mm(qm, kt) + r_past[hd:hd + 1, :],
                           jnp.where(causal, _mm_nt(qm, kn) + r_new[hd:hd + 1, :], NEG)))
    weights = [_joint_softmax(s_p, s_n) for s_p, s_n in scores]
    for blk, sl in enumerate(sls):
        vt = vt_ref[0, sl, :].astype(BF16)
        vn = _pad_rows(vn_ref[:, sl], LANES).astype(BF16)
        outs = [_joint_pv(*weights[2 * blk + par], vt, vn) for par in range(2)]
        o = jnp.where(lane < HD, outs[0], outs[1])
        o_ref[:, sl] = (o * _silu(z_ref[:, sl].astype(F32))).astype(BF16)


def _fox_sample(fq, cache_kt, cache_vt, fk, fv, lf_past_t, lf_new_t, fz, *, nq):
    b, _, past = cache_kt.shape
    new = lambda: pl.BlockSpec((nq, WIDTH), lambda i: (i, 0))
    cache = lambda: pl.BlockSpec((1, WIDTH, past), lambda i: (i, 0, 0))
    return pl.pallas_call(
        functools.partial(_fox_sample_kernel, nq=nq),
        grid=(b,),
        in_specs=[new(), cache(), cache(), new(), new(),
                  pl.BlockSpec((1, FOX_HEADS, past), lambda i: (i, 0, 0)),
                  pl.BlockSpec((1, FOX_HEADS, LANES), lambda i: (i, 0, 0)),
                  new()],
        out_specs=new(),
        out_shape=jax.ShapeDtypeStruct((b * nq, WIDTH), BF16),
        compiler_params=pltpu.CompilerParams(
            dimension_semantics=("parallel",), vmem_limit_bytes=VMEM_LIMIT),
        name="fox_sample",
    )(fq, cache_kt, cache_vt, fk, fv, lf_past_t, lf_new_t, fz)


def _diff_sample_kernel(q_ref, kt_ref, v_ref, kn_ref, vn_ref, bias_ref, z_ref,
                        lq1_ref, lk1_ref, lq2_ref, lk2_ref, subg_ref, o_ref, *, nq, past):
    lane = lax.broadcasted_iota(jnp.int32, (nq, LANES), 1)
    lam = _lambda(lq1_ref, lk1_ref, lq2_ref, lk2_ref)
    sls = [slice(hd * LANES, (hd + 1) * LANES) for hd in range(DIFF_HEADS)]
    scores = []
    for hd, sl in enumerate(sls):
        q = q_ref[:, sl]
        kt = kt_ref[0, sl, :].astype(BF16)
        kn = _pad_rows(kn_ref[:, sl], LANES).astype(BF16)
        for mp in range(2):
            qm = jnp.where((lane < HD) if mp == 0 else (lane >= HD), q, jnp.zeros_like(q))
            scores.append((_mm(qm, kt) + bias_ref[hd, :, 0:past],
                           _mm_nt(qm, kn) + bias_ref[hd, :, past:past + LANES]))
    weights = [_joint_softmax(s_p, s_n) for s_p, s_n in scores]
    for hd, sl in enumerate(sls):
        vt = v_ref[0, pl.ds(hd, past, stride=DIFF_HEADS), :].T.astype(BF16)
        vn = _pad_rows(vn_ref[:, sl], LANES).astype(BF16)
        outs = [_joint_pv(*weights[2 * hd + mp], vt, vn) for mp in range(2)]
        o_ref[:, sl] = _diff_finish(outs[0] - lam * outs[1], subg_ref[...], z_ref[:, sl].astype(F32))


def _diff_sample(dq, cache_kt, cache_v, dk, dv, bias, dz, lq1, lk1, lq2, lk2, subg, *, nq):
    b, _, past = cache_kt.shape
    new = lambda: pl.BlockSpec((nq, WIDTH), lambda i: (i, 0))
    small = lambda n: pl.BlockSpec((1, n), lambda i: (0, 0))
    return pl.pallas_call(
        functools.partial(_diff_sample_kernel, nq=nq, past=past),
        grid=(b,),
        in_specs=[new(), pl.BlockSpec((1, WIDTH, past), lambda i: (i, 0, 0)),
                  pl.BlockSpec((1, past * DIFF_HEADS, 2 * HD), lambda i: (i, 0, 0)), new(), new(),
                  pl.BlockSpec((DIFF_HEADS, nq, past + LANES), lambda i: (0, 0, 0)),
                  new(), small(HD), small(HD), small(HD), small(HD), small(2 * HD)],
        out_specs=new(),
        out_shape=jax.ShapeDtypeStruct((b * nq, WIDTH), BF16),
        compiler_params=pltpu.CompilerParams(
            dimension_semantics=("parallel",), vmem_limit_bytes=VMEM_LIMIT),
        name="diff_sample",
    )(dq, cache_kt, cache_v, dk, dv, bias, dz, lq1, lk1, lq2, lk2, subg)


def _out_kernel(yf_ref, yd_ref, ga_ref, gb_ref, x_ref, wuf_ref, wud_ref, wo_ref, g_ref, o_ref):
    merged = (ga_ref[...].astype(F32) * _mm(yf_ref[...], wuf_ref[...])
              + gb_ref[...].astype(F32) * _mm(yd_ref[...], wud_ref[...]))
    out = x_ref[...] + _mm(merged.astype(BF16), wo_ref[...])
    ms = jnp.mean(out * out, axis=-1, keepdims=True)
    o_ref[...] = out * lax.rsqrt(ms + NORM_EPS) * g_ref[...]


def _output(yf, yd, ga, gb, x2d, wuf, wud, wo, g, *, tm):
    rows = x2d.shape[0]
    assert rows % tm == 0
    row_spec = lambda n: pl.BlockSpec((tm, n), lambda i: (i, 0))
    const = lambda shape: pl.BlockSpec(shape, lambda i: (0, 0))
    return pl.pallas_call(
        _out_kernel,
        grid=(rows // tm,),
        in_specs=[row_spec(WIDTH), row_spec(WIDTH), row_spec(D_MODEL), row_spec(D_MODEL), row_spec(D_MODEL),
                  const((WIDTH, D_MODEL)), const((WIDTH, D_MODEL)), const((D_MODEL, D_MODEL)),
                  const((1, D_MODEL))],
        out_specs=row_spec(D_MODEL),
        out_shape=jax.ShapeDtypeStruct((rows, D_MODEL), F32),
        compiler_params=pltpu.CompilerParams(
            dimension_semantics=("parallel",), vmem_limit_bytes=VMEM_LIMIT),
        name="out_proj",
    )(yf, yd, ga, gb, x2d, wuf, wud, wo, g)


def _out_sample_kernel(*refs, nq, past, nb):
    out_refs, fox_refs, diff_refs = refs[:10], refs[10:19], refs[19:32]
    _out_kernel(*out_refs)
    step = pl.program_id(0)

    @pl.when(step < nb)
    def _():
        _fox_sample_kernel(*fox_refs, nq=nq)

    @pl.when(jnp.logical_and(step >= nb, step < 2 * nb))
    def _():
        _diff_sample_kernel(*diff_refs, nq=nq, past=past)


def _output_with_sample(out_args, fox_args, diff_args, *, tm, nq):
    yf, yd, ga, gb, x2d, wuf, wud, wo, g = out_args
    rows = x2d.shape[0]
    nb, _, past = fox_args[1].shape
    steps = rows // tm
    assert rows % tm == 0 and steps >= 2 * nb
    row_spec = lambda n: pl.BlockSpec((tm, n), lambda i: (i, 0))
    const = lambda shape: pl.BlockSpec(shape, lambda i: (0,) * len(shape), pipeline_mode=pl.Buffered(1))
    fox_i = lambda i: jnp.minimum(i, nb - 1)
    diff_i = lambda i: jnp.clip(i - nb, 0, nb - 1)
    new = lambda f: pl.BlockSpec((nq, WIDTH), lambda i: (f(i), 0))
    cache = lambda f, r, c: pl.BlockSpec((1, r, c), lambda i: (f(i), 0, 0))
    small = lambda n: pl.BlockSpec((1, n), lambda i: (0, 0))
    in_specs = (
        [row_spec(WIDTH), row_spec(WIDTH), row_spec(D_MODEL), row_spec(D_MODEL), row_spec(D_MODEL),
         const((WIDTH, D_MODEL)), const((WIDTH, D_MODEL)), const((D_MODEL, D_MODEL)), small(D_MODEL)]
        + [new(fox_i), cache(fox_i, WIDTH, past), cache(fox_i, WIDTH, past), new(fox_i), new(fox_i),
           cache(fox_i, FOX_HEADS, past), cache(fox_i, FOX_HEADS, LANES), new(fox_i)]
        + [new(diff_i), cache(diff_i, WIDTH, past), cache(diff_i, past * DIFF_HEADS, 2 * HD),
           new(diff_i), new(diff_i),
           pl.BlockSpec((DIFF_HEADS, nq, past + LANES), lambda i: (0, 0, 0), pipeline_mode=pl.Buffered(1)),
           new(diff_i), small(HD), small(HD), small(HD), small(HD), small(2 * HD)])

    def body(*refs):
        ins, outs = refs[:29], refs[29:]
        _out_sample_kernel(*ins[:9], outs[0], *ins[9:17], outs[1], *ins[17:29], outs[2],
                           nq=nq, past=past, nb=nb)

    return pl.pallas_call(
        body,
        grid=(steps,),
        in_specs=in_specs,
        out_specs=[row_spec(D_MODEL), new(fox_i), new(diff_i)],
        out_shape=[jax.ShapeDtypeStruct((rows, D_MODEL), F32),
                   jax.ShapeDtypeStruct((nb * nq, WIDTH), BF16),
                   jax.ShapeDtypeStruct((nb * nq, WIDTH), BF16)],
        compiler_params=pltpu.CompilerParams(
            dimension_semantics=("arbitrary",), vmem_limit_bytes=VMEM_LIMIT),
        name="out_proj_sample",
    )(*out_args, *fox_args, *diff_args)


def _forget_lane_heads():
    lanes = np.arange(LANES)
    heads = (lanes % HD) // FL_SLOT
    heads[:FOX_HEADS] = np.arange(FOX_HEADS)
    return heads


def _pack_w_in(w_in, b_forget):
    sizes = (WIDTH, WIDTH, WIDTH, WIDTH, FOX_HEADS, WIDTH, WIDTH, WIDTH, WIDTH, D_MODEL, D_MODEL)
    offs = np.cumsum((0,) + sizes)
    fq, fk, fv, fz, fl, dq, dk, dv, dz, ga, gb = (w_in[:, offs[i]:offs[i + 1]] for i in range(len(sizes)))
    heads = _forget_lane_heads()
    w_all = jnp.concatenate([fq, fk, fv, fz, dq, dk, dv, dz, ga, gb, fl[:, heads]], axis=1).astype(BF16)
    return w_all, b_forget[heads][None, :]


def kernel(x_prompt, x_sample, cache_fox_k, cache_fox_v, cache_fox_logf, cache_diff_k, cache_diff_v,
           norm_in_g, w_in, b_forget, lambda_q1, lambda_k1, lambda_q2, lambda_k2, subln_g,
           w_up_fox, w_up_diff, w_o, rel_bias, final_norm_g):
    bp, tp, _ = x_prompt.shape
    bs, ts, _ = x_sample.shape
    past = cache_fox_k.shape[2]
    blk, tm = ATTN_BLOCK, ROW_TILE
    assert tp % (QUERY_BLOCKS_PER_STEP * blk) == 0 and blk > _T5_THR[-1] and blk % CHUNK == 0

    w_all, bfl = _pack_w_in(w_in[0], b_forget[0])
    g_in = norm_in_g[0][None, :]
    g_out = final_norm_g[None, :]
    wuf, wud, wo = w_up_fox[0].astype(BF16), w_up_diff[0].astype(BF16), w_o[0].astype(BF16)
    lq1, lk1, lq2, lk2 = (a[0][None, :] for a in (lambda_q1, lambda_k1, lambda_q2, lambda_k2))
    subg = subln_g[0][None, :]

    xp = x_prompt.reshape(bp * tp, D_MODEL)
    (fkt, fvt, lft, dkt, dv4, qa, ka, vtf, fz, dq, dkb, vtd, dz, ga, gb) = _project(
        xp, g_in, w_all, bfl, seq_len=tp, aug=True, tm=tm)
    r3 = lambda a: a.reshape(bp, tp, a.shape[-1])
    yf = _fox_attention(r3(qa), r3(ka), vtf, r3(fz), blk=blk, nh=FOX_STEP_HEADS, sub=QUERY_BLOCKS_PER_STEP)
    bias_p = _bias_tiles(rel_bias, q_start=blk, nq=blk, segments=((0, blk, blk), (blk, blk, blk)),
                         transposed=True, scale=LOG2E)
    yd = _diff_attention(r3(dq), r3(dkb), vtd, bias_p, r3(dz), lq1, lk1, lq2, lk2, subg, blk=blk,
                         nh=DIFF_STEP_HEADS, sub=QUERY_BLOCKS_PER_STEP)
    fox_k_p = jnp.transpose(fkt.reshape(bp, FOX_HEADS, HD, tp), (0, 3, 1, 2))[None]
    fox_v_p = jnp.transpose(fvt.reshape(bp, FOX_HEADS, HD, tp), (0, 3, 1, 2))[None]
    fox_lf_p = jnp.transpose(lft, (0, 2, 1))[None]
    diff_k_p = jnp.transpose(dkt.reshape(bp, DIFF_HEADS, 2, HD, tp), (0, 4, 1, 2, 3))[None]
    diff_v_p = dv4.reshape(1, bp, tp, DIFF_HEADS, 2 * HD)

    xs = x_sample.reshape(bs * ts, D_MODEL)
    (fk_s, fv_s, lf_s, dk_s, dv_s, fq_s, fz_s, dq_s, dz_s, ga_s, gb_s) = _project(
        xs, g_in, w_all, bfl, seq_len=ts, aug=False, tm=tm)
    fox_kt = jnp.transpose(cache_fox_k[0], (0, 2, 3, 1)).reshape(bs, WIDTH, past)
    fox_vt = jnp.transpose(cache_fox_v[0], (0, 2, 3, 1)).reshape(bs, WIDTH, past)
    diff_kt = jnp.transpose(cache_diff_k[0], (0, 2, 3, 4, 1)).reshape(bs, WIDTH, past)
    diff_v = cache_diff_v[0].reshape(bs, past * DIFF_HEADS, 2 * HD)
    lf_past_t = jnp.swapaxes(cache_fox_logf[0], 1, 2)
    lf_new_t = jnp.pad(jnp.swapaxes(lf_s.reshape(bs, ts, FOX_HEADS), 1, 2), ((0, 0), (0, 0), (0, LANES - ts)))
    bias_s = _bias_tiles(rel_bias, q_start=past, nq=ts, segments=((0, past, past), (past, LANES, ts)),
                         transposed=False, scale=1.0)
    y_p, yf_s, yd_s = _output_with_sample(
        (yf.reshape(bp * tp, WIDTH), yd.reshape(bp * tp, WIDTH), ga, gb, xp, wuf, wud, wo, g_out),
        (fq_s, fox_kt, fox_vt, fk_s, fv_s, lf_past_t, lf_new_t, fz_s),
        (dq_s, diff_kt, diff_v, dk_s, dv_s, bias_s, dz_s, lq1, lk1, lq2, lk2, subg), tm=tm, nq=ts)
    y_s = _output(yf_s, yd_s, ga_s, gb_s, xs, wuf, wud, wo, g_out, tm=tm)

    return (y_p.reshape(bp, tp, D_MODEL), y_s.reshape(bs, ts, D_MODEL),
            fox_k_p, fox_v_p, fox_lf_p, diff_k_p, diff_v_p,
            fk_s.reshape(1, bs, ts, FOX_HEADS, HD), fv_s.reshape(1, bs, ts, FOX_HEADS, HD),
            lf_s.reshape(1, bs, ts, FOX_HEADS),
            dk_s.reshape(1, bs, ts, DIFF_HEADS, 2, HD), dv_s.reshape(1, bs, ts, DIFF_HEADS, 2 * HD))
```

```python
import functools
import math

import numpy as np
import jax
import jax.numpy as jnp
from jax import lax
from jax.experimental import pallas as pl
from jax.experimental.pallas import tpu as pltpu

F32 = jnp.float32
BF16 = jnp.bfloat16

D_MODEL = 1024
HD = 64
FOX_HEADS = 8
DIFF_HEADS = 4
WIDTH = 512
CHUNK = 64
CHUNK_SHIFT = CHUNK.bit_length() - 1
N_BUCKETS = 32
MAX_DISTANCE = 128
NORM_EPS = 1e-6
SCALE = HD ** -0.5
NEG = -1e30
LAMBDA_INIT = 0.8 - 0.6 * math.exp(-0.3 * 0)
LOG2E = math.log2(math.e)

LANES = 128
PAIRS = WIDTH // LANES
VT_PAD = 16
VT_ROWS = LANES + VT_PAD
FVT_ROWS = HD + VT_PAD

_O_FQ, _O_FK, _O_FV, _O_FZ = 0, 512, 1024, 1536
_O_DQ, _O_DK, _O_DV, _O_DZ = 2048, 2560, 3072, 3584
_O_GA, _O_GB = 4096, 5120
_O_FL = 6144
_N_ALL = 6272
FL_SLOT = 8

VMEM_LIMIT = 58 * 1024 * 1024
ATTN_BLOCK = 256
ROW_TILE = 512

FOX_STEP_HEADS = 8
DIFF_STEP_HEADS = 4
HEADROOM = 64.0
QUERY_BLOCKS_PER_STEP = 4
STEP_BLOCKS = (4, 2, 1)


def _t5_thresholds():
    nb = N_BUCKETS // 2
    max_exact = nb // 2
    n = np.arange(0, 4 * MAX_DISTANCE)
    large = max_exact + (np.log(np.maximum(n, 1).astype(np.float32) / max_exact)
                         / math.log(MAX_DISTANCE / max_exact) * (nb - max_exact)).astype(np.int32)
    large = np.minimum(large, nb - 1)
    thr = [int(np.argmax(large >= b)) for b in range(max_exact + 1, nb)]
    return max_exact, tuple(thr)


_MAX_EXACT, _T5_THR = _t5_thresholds()


def _mm(a, b):
    return jnp.dot(a, b, preferred_element_type=F32)


def _mm_nt(a, b):
    return lax.dot_general(a, b, (((1,), (1,)), ((), ())), preferred_element_type=F32)


def _log_sigmoid(x):
    return -(jnp.maximum(-x, 0.0) + jnp.log1p(jnp.exp(-jnp.abs(x))))


def _sigmoid(x):
    return 1.0 / (1.0 + jnp.exp(-x))


def _silu(x):
    return x * _sigmoid(x)


def _split3(x):
    p1 = x.astype(BF16).astype(F32)
    r1 = x - p1
    p2 = r1.astype(BF16).astype(F32)
    p3 = r1 - p2
    return p1, p2, p3


def _proj_kernel(x_ref, g_ref, w_ref, bfl_ref, *refs, tm, seq_len, aug):
    if aug:
        (fkt_ref, fvt_ref, lft_ref, dkt_ref, dv4_ref, qa_ref, ka_ref, vtf_ref,
         fz_ref, dq_ref, dkb_ref, vtd_ref, dz_ref, ga_ref, gb_ref, carry_ref) = refs
    else:
        (fk_ref, fv_ref, lf_ref, dk_ref, dv_ref, fq_ref,
         fz_ref, dq_ref, dz_ref, ga_ref, gb_ref) = refs

    x = x_ref[...]
    ms = jnp.mean(x * x, axis=-1, keepdims=True)
    h = (x * lax.rsqrt(ms + NORM_EPS) * g_ref[...]).astype(BF16)

    def grp(off, n=WIDTH):
        return _mm(h, w_ref[:, off:off + n])

    if not aug:
        fq_ref[...] = (grp(_O_FQ) * SCALE).astype(BF16)
        fk_ref[...] = grp(_O_FK)
        fv_ref[...] = grp(_O_FV)
        fz_ref[...] = grp(_O_FZ).astype(BF16)
        dq_ref[...] = (grp(_O_DQ) * SCALE).astype(BF16)
        dk_ref[...] = grp(_O_DK)
        dv_ref[...] = grp(_O_DV)
        dz_ref[...] = grp(_O_DZ).astype(BF16)
        ga_ref[...] = _sigmoid(grp(_O_GA, D_MODEL)).astype(BF16)
        gb_ref[...] = _sigmoid(grp(_O_GB, D_MODEL)).astype(BF16)
        lf_ref[...] = _log_sigmoid(grp(_O_FL, LANES) + bfl_ref[...])[:, :FOX_HEADS]
        return

    lf = _log_sigmoid(grp(_O_FL, LANES) + bfl_ref[...])
    lft_ref[0] = lf.T[:FOX_HEADS, :]
    row = lax.broadcasted_iota(jnp.int32, (tm, LANES), 0)
    c = lf
    shift = 1
    while shift < tm:
        c = c + jnp.where(row >= shift, pltpu.roll(c, shift, 0), 0.0)
        shift *= 2
    tiles_per_seq = seq_len // tm

    @pl.when(pl.program_id(0) % tiles_per_seq == 0)
    def _():
        carry_ref[...] = jnp.zeros_like(carry_ref)

    c = c + carry_ref[...]
    carry_ref[...] = c[tm - 1:tm, :]
    c_hi, c_mid, c_lo = _split3(c * LOG2E)

    fq = grp(_O_FQ) * (SCALE * LOG2E)
    fk = grp(_O_FK)
    fv = grp(_O_FV)
    fvt = fv.T
    fkt_ref[0] = fk.T
    fvt_ref[0] = fvt
    tail = jnp.where(lax.broadcasted_iota(jnp.int32, (VT_PAD, tm), 0) == 0, 1.0, 0.0).astype(BF16)
    lane = lax.broadcasted_iota(jnp.int32, (tm, LANES), 1)
    pos = lane & (FL_SLOT - 1)
    q_slots = jnp.where(pos == 0, c_hi, jnp.where(pos == 1, c_mid, jnp.where(
        pos == 2, c_lo, jnp.where(pos < 6, 1.0, 0.0))))
    k_slots = jnp.where(pos < 3, 1.0, jnp.where(pos == 3, -c_hi, jnp.where(
        pos == 4, -c_mid, jnp.where(pos == 5, -c_lo, 0.0))))
    for hd in range(FOX_HEADS):
        blk, par = divmod(hd, 2)
        sl = slice(blk * LANES, (blk + 1) * LANES)
        data = (lane < HD) if par == 0 else (lane >= HD)
        slot0 = (1 - par) * HD + hd * FL_SLOT
        own = (lane >= slot0) & (lane < slot0 + FL_SLOT)
        q_aug = jnp.where(own, q_slots, 0.0)
        k_aug = jnp.where(own, k_slots, 0.0)
        osl = slice(hd * LANES, (hd + 1) * LANES)
        qa_ref[:, osl] = jnp.where(data, fq[:, sl], q_aug).astype(BF16)
        ka_ref[:, osl] = jnp.where(data, fk[:, sl], k_aug).astype(BF16)
        base = hd * FVT_ROWS
        vtf_ref[0, base:base + HD, :] = fvt[hd * HD:(hd + 1) * HD, :].astype(BF16)
        vtf_ref[0, base + HD:base + FVT_ROWS, :] = tail

    ga_ref[...] = _sigmoid(grp(_O_GA, D_MODEL)).astype(BF16)
    gb_ref[...] = _sigmoid(grp(_O_GB, D_MODEL)).astype(BF16)

    dk = grp(_O_DK)
    dkt_ref[0] = dk.T
    dkb_ref[...] = dk.astype(BF16)
    dv = grp(_O_DV)
    dv4_ref[...] = dv.reshape(tm, DIFF_HEADS, 2 * HD)
    for hd in range(DIFF_HEADS):
        base = hd * VT_ROWS
        vtd_ref[0, base:base + LANES, :] = dv[:, hd * LANES:(hd + 1) * LANES].T.astype(BF16)
        vtd_ref[0, base + LANES:base + VT_ROWS, :] = tail

    dq_ref[...] = (grp(_O_DQ) * (SCALE * LOG2E)).astype(BF16)
    fz_ref[...] = grp(_O_FZ).astype(BF16)
    dz_ref[...] = grp(_O_DZ).astype(BF16)


def _project(x2d, g, w_all, bfl, *, seq_len, aug, tm):
    rows = x2d.shape[0]
    assert rows % tm == 0
    row_spec = lambda n: pl.BlockSpec((tm, n), lambda i: (i, 0))
    const = lambda shape: pl.BlockSpec(shape, lambda i: (0, 0))
    f32o = lambda n: jax.ShapeDtypeStruct((rows, n), F32)
    b16o = lambda n: jax.ShapeDtypeStruct((rows, n), BF16)
    if aug:
        assert seq_len % tm == 0
        wide = FOX_HEADS * LANES
        tps = seq_len // tm
        t_shape = lambda n, dt: jax.ShapeDtypeStruct((rows // seq_len, n, seq_len), dt)
        t_spec = lambda n: pl.BlockSpec((1, n, tm), lambda i: (i // tps, 0, i % tps))
        out_shape = [t_shape(WIDTH, F32), t_shape(WIDTH, F32), t_shape(FOX_HEADS, F32), t_shape(WIDTH, F32),
                     jax.ShapeDtypeStruct((rows, DIFF_HEADS, 2 * HD), F32)]
        out_specs = [t_spec(WIDTH), t_spec(WIDTH), t_spec(FOX_HEADS), t_spec(WIDTH),
                     pl.BlockSpec((tm, DIFF_HEADS, 2 * HD), lambda i: (i, 0, 0))]
        out_shape += ([b16o(wide)] * 2 + [t_shape(FOX_HEADS * FVT_ROWS, BF16)] + [b16o(WIDTH)] * 3
                      + [t_shape(DIFF_HEADS * VT_ROWS, BF16)] + [b16o(WIDTH)] + [b16o(D_MODEL)] * 2)
        out_specs += ([row_spec(wide)] * 2 + [t_spec(FOX_HEADS * FVT_ROWS)] + [row_spec(WIDTH)] * 3
                      + [t_spec(DIFF_HEADS * VT_ROWS)] + [row_spec(WIDTH)] + [row_spec(D_MODEL)] * 2)
        scratch = [pltpu.VMEM((1, LANES), F32)]
    else:
        out_shape = [f32o(WIDTH), f32o(WIDTH), f32o(FOX_HEADS), f32o(WIDTH), f32o(WIDTH)]
        out_specs = [row_spec(WIDTH), row_spec(WIDTH), row_spec(FOX_HEADS), row_spec(WIDTH), row_spec(WIDTH)]
        out_shape += [b16o(WIDTH)] * 4 + [b16o(D_MODEL)] * 2
        out_specs += [row_spec(WIDTH)] * 4 + [row_spec(D_MODEL)] * 2
        scratch = []
    return pl.pallas_call(
        functools.partial(_proj_kernel, tm=tm, seq_len=seq_len, aug=aug),
        grid=(rows // tm,),
        in_specs=[row_spec(D_MODEL), const((1, D_MODEL)),
                  pl.BlockSpec((D_MODEL, _N_ALL), lambda i: (0, 0), pipeline_mode=pl.Buffered(1)),
                  const((1, LANES))],
        out_specs=out_specs,
        out_shape=out_shape,
        scratch_shapes=scratch,
        compiler_params=pltpu.CompilerParams(
            dimension_semantics=("arbitrary",), vmem_limit_bytes=VMEM_LIMIT),
        name="proj_aug" if aug else "proj_plain",
    )(x2d, g, w_all, bfl)


def _bias_kernel(tbl_ref, o_ref, *, q_start, nq, segments, transposed, scale):
    hd = pl.program_id(0)
    far = tbl_ref[N_BUCKETS // 2 - 1, hd]
    q_axis, k_axis = (1, 0) if transposed else (0, 1)
    off = 0
    for k_start, nk, valid in segments:
        shape = (nk, nq) if transposed else (nq, nk)
        qpos = q_start + lax.broadcasted_iota(jnp.int32, shape, q_axis)
        kidx = lax.broadcasted_iota(jnp.int32, shape, k_axis)
        kpos = k_start + kidx
        rel = kpos - qpos
        n = jnp.abs(rel)
        large = jnp.full(shape, _MAX_EXACT, jnp.int32)
        for thr in _T5_THR:
            large = large + jnp.where(n >= thr, 1, 0)
        bucket = jnp.where(rel > 0, N_BUCKETS // 2, 0) + jnp.where(n < _MAX_EXACT, n, large)
        val = jnp.zeros(shape, F32)
        for b in range(N_BUCKETS):
            val = jnp.where(bucket == b, tbl_ref[b, hd], val)
        val = (val - far) * scale
        visible = ((kpos >> CHUNK_SHIFT) <= (qpos >> CHUNK_SHIFT)) & (kidx < valid)
        val = jnp.where(visible, val, NEG)
        if transposed:
            o_ref[0, off:off + nk, :] = val
        else:
            o_ref[0, :, off:off + nk] = val
        off += nk


def _bias_tiles(rel_bias, *, q_start, nq, segments, transposed, scale):
    total = sum(s[1] for s in segments)
    shape = (total, nq) if transposed else (nq, total)
    return pl.pallas_call(
        functools.partial(_bias_kernel, q_start=q_start, nq=nq, segments=segments, transposed=transposed,
                          scale=scale),
        grid=(DIFF_HEADS,),
        in_specs=[pl.BlockSpec(memory_space=pltpu.SMEM)],
        out_specs=pl.BlockSpec((1,) + shape, lambda h: (h, 0, 0)),
        out_shape=jax.ShapeDtypeStruct((DIFF_HEADS,) + shape, F32),
        name="t5_bias",
    )(rel_bias)


def _flash_t(k_ref, vt_ref, acc_ref, bad_ref, streams, qi, blk, prev_bias, diag_bias, diag_mask):
    n = len(streams)

    def scores(first_blk, width):
        start = pl.multiple_of(first_blk * blk, blk)
        return [_mm_nt(k_ref[0, pl.ds(start, width), ksl], q) for q, ksl, _ in streams]

    def step(first_blk, width, ms, biases, mask):
        start = pl.multiple_of(first_blk * blk, blk)
        sts = scores(first_blk, width)
        out, ps, alphas = [], [], []
        for s, st in enumerate(sts):
            if biases is not None:
                st = st + biases[s]()
            if mask is not None:
                st = jnp.where(mask, st, NEG)
            m_new = jnp.maximum(ms[s], jnp.max(st, axis=0, keepdims=True))
            ps.append(jnp.exp2(st - m_new).astype(BF16))
            alphas.append(jnp.exp2(ms[s] - m_new))
            out.append(m_new)
        for s, (_, _, vsl) in enumerate(streams):
            acc_ref[s] = alphas[s] * acc_ref[s] + _mm(vt_ref[0, vsl, pl.ds(start, width)], ps[s])
        return tuple(out)

    def first_step():
        start = pl.multiple_of(qi * blk, blk)
        out, ps = [], []
        for s, st in enumerate(scores(qi, blk)):
            if diag_bias is not None:
                st = st + diag_bias[s]()
            if diag_mask is not None:
                st = jnp.where(diag_mask, st, NEG)
            m = jnp.max(st, axis=0, keepdims=True)
            ps.append(jnp.exp2(st - m).astype(BF16))
            out.append(m)
        for s, (_, _, vsl) in enumerate(streams):
            acc_ref[s] = _mm(vt_ref[0, vsl, pl.ds(start, blk)], ps[s])
        return tuple(out)

    ms = first_step()
    bad_ref[...] = jnp.zeros_like(bad_ref)

    def consume(sts, first_blk, width):
        start = pl.multiple_of(first_blk * blk, blk)
        ps = [jnp.exp2(st - ms[s]).astype(BF16) for s, st in enumerate(sts)]
        worst = None
        for p in ps:
            d = jnp.max(p, axis=0, keepdims=True)
            worst = d if worst is None else jnp.maximum(worst, d)
        bad_ref[...] = jnp.maximum(bad_ref[...], worst.astype(F32))
        for s, (_, _, vsl) in enumerate(streams):
            acc_ref[s] = acc_ref[s] + _mm(vt_ref[0, vsl, pl.ds(start, width)], ps[s])

    if prev_bias is not None:
        @pl.when(qi >= 1)
        def _():
            consume([st + prev_bias[s]() for s, st in enumerate(scores(qi - 1, blk))], qi - 1, blk)

    n_plain = qi if prev_bias is None else jnp.maximum(qi - 1, 0)
    done = 0
    for w in STEP_BLOCKS:
        count = (n_plain - done) // w

        def fast_step(j, carry, w=w, done=done):
            consume(scores(done + j * w, w * blk), done + j * w, w * blk)
            return carry

        lax.fori_loop(0, count, fast_step, 0)
        done = done + count * w

    @pl.when(jnp.max(bad_ref[...]) > 2.0 ** HEADROOM)
    def _():
        c = first_step()
        if prev_bias is not None:
            c = lax.cond(qi >= 1, lambda c: step(qi - 1, blk, c, prev_bias, None), lambda c: c, c)
        lax.fori_loop(0, n_plain, lambda j, c: step(j, blk, c, None, None), c)


def _fox_kernel(qa_ref, ka_ref, vt_ref, z_ref, *rest, blk, nh, sub, nq):
    sample_in, (o_ref, so_ref, acc_ref, bad_ref) = rest[:8], rest[8:]
    key = lax.broadcasted_iota(jnp.int32, (blk, blk), 0)
    qry = lax.broadcasted_iota(jnp.int32, (blk, blk), 1)
    hsl = [slice(hd * LANES, (hd + 1) * LANES) for hd in range(nh)]

    def query_block(i, carry):
        qi = pl.program_id(2) * sub + i
        rows = pl.ds(pl.multiple_of(i * blk, blk), blk)
        streams = [(qa_ref[0, rows, sl], sl, slice(hd * FVT_ROWS, (hd + 1) * FVT_ROWS))
                   for hd, sl in enumerate(hsl)]
        _flash_t(ka_ref, vt_ref, acc_ref, bad_ref, streams, qi, blk, None, None, key <= qry)
        for pair in range(nh // 2):
            halves = []
            for par in range(2):
                acc = acc_ref[2 * pair + par]
                halves.append(acc[0:HD, :] / acc[HD:HD + 1, :])
            o = jnp.concatenate(halves, axis=0).T
            o_ref[0, rows, hsl[pair]] = (o * _silu(z_ref[0, rows, hsl[pair]].astype(F32))).astype(BF16)
        return carry

    lax.fori_loop(0, sub, query_block, 0)
    _fox_sample_kernel(*sample_in, so_ref, nq=nq)


def _fox_attention(qa, ka, vt, fz, sample_args, *, blk, nh, sub, nq):
    b, t, _ = qa.shape
    step = sub * blk
    per_seq = t // step
    nb, _, past = sample_args[1].shape
    assert nh == FOX_HEADS and nb == b * per_seq
    row = lambda b, p, i: b * per_seq + i
    new = lambda: pl.BlockSpec((nq, WIDTH), lambda b, p, i: (row(b, p, i), 0))
    cache = lambda r, c: pl.BlockSpec((1, r, c), lambda b, p, i: (row(b, p, i), 0, 0))
    return pl.pallas_call(
        functools.partial(_fox_kernel, blk=blk, nh=nh, sub=sub, nq=nq),
        grid=(b, FOX_HEADS // nh, per_seq),
        in_specs=[pl.BlockSpec((1, step, nh * LANES), lambda b, p, i: (b, i, p)),
                  pl.BlockSpec((1, t, nh * LANES), lambda b, p, i: (b, 0, p)),
                  pl.BlockSpec((1, nh * FVT_ROWS, t), lambda b, p, i: (b, p, 0)),
                  pl.BlockSpec((1, step, nh * HD), lambda b, p, i: (b, i, p)),
                  new(), cache(WIDTH, past), cache(WIDTH, past), new(), new(),
                  cache(FOX_HEADS, past), cache(FOX_HEADS, LANES), new()],
        out_specs=[pl.BlockSpec((1, step, nh * HD), lambda b, p, i: (b, i, p)), new()],
        out_shape=[jax.ShapeDtypeStruct((b, t, WIDTH), BF16), jax.ShapeDtypeStruct((nb * nq, WIDTH), BF16)],
        scratch_shapes=[pltpu.VMEM((nh, FVT_ROWS, blk), F32), pltpu.VMEM((1, blk), F32)],
        compiler_params=pltpu.CompilerParams(
            dimension_semantics=("parallel", "parallel", "arbitrary"), vmem_limit_bytes=VMEM_LIMIT),
        name="fox_prompt",
    )(qa, ka, vt, fz, *sample_args)


def _lambda(lq1_ref, lk1_ref, lq2_ref, lk2_ref):
    s1 = jnp.sum(lq1_ref[...] * lk1_ref[...], axis=-1, keepdims=True)
    s2 = jnp.sum(lq2_ref[...] * lk2_ref[...], axis=-1, keepdims=True)
    return jnp.exp(s1) - jnp.exp(s2) + LAMBDA_INIT


def _diff_finish(o, subg, z):
    ms = jnp.mean(o * o, axis=-1, keepdims=True)
    od = o * lax.rsqrt(ms + NORM_EPS) * subg * (1.0 - LAMBDA_INIT)
    return (od * _silu(z)).astype(BF16)


def _diff_kernel(q_ref, k_ref, vt_ref, bias_ref, z_ref, lq1_ref, lk1_ref, lq2_ref, lk2_ref, subg_ref,
                 *rest, blk, nh, sub, nq, past):
    sample_in, (o_ref, so_ref, acc_ref, bad_ref) = rest[:7], rest[7:]
    lane = lax.broadcasted_iota(jnp.int32, (blk, LANES), 1)
    lam = _lambda(lq1_ref, lk1_ref, lq2_ref, lk2_ref)

    def query_block(i, carry):
        qi = pl.program_id(2) * sub + i
        rows = pl.ds(pl.multiple_of(i * blk, blk), blk)
        streams, prev_bias, diag_bias = [], [], []
        for hd in range(nh):
            q = q_ref[0, rows, hd * LANES:(hd + 1) * LANES]
            for mp in range(2):
                streams.append((jnp.where((lane < HD) if mp == 0 else (lane >= HD), q, jnp.zeros_like(q)),
                                slice(hd * LANES, (hd + 1) * LANES), slice(hd * VT_ROWS, (hd + 1) * VT_ROWS)))
                prev_bias.append(lambda hd=hd: bias_ref[hd, 0:blk, :])
                diag_bias.append(lambda hd=hd: bias_ref[hd, blk:2 * blk, :])
        _flash_t(k_ref, vt_ref, acc_ref, bad_ref, streams, qi, blk, prev_bias, diag_bias, None)
        for hd in range(nh):
            a0, a1 = acc_ref[2 * hd], acc_ref[2 * hd + 1]
            o_t = a0[0:LANES, :] / a0[LANES:LANES + 1, :] - lam * (a1[0:LANES, :] / a1[LANES:LANES + 1, :])
            sl = slice(hd * LANES, (hd + 1) * LANES)
            o_ref[0, rows, sl] = _diff_finish(o_t.T, subg_ref[...], z_ref[0, rows, sl].astype(F32))
        return carry

    lax.fori_loop(0, sub, query_block, 0)
    _diff_sample_kernel(*sample_in, lq1_ref, lk1_ref, lq2_ref, lk2_ref, subg_ref, so_ref, nq=nq, past=past)


def _diff_attention(dq, dk, dvt, bias, dz, lq1, lk1, lq2, lk2, subg, sample_args, *, blk, nh, sub, nq):
    b, t, _ = dq.shape
    step = sub * blk
    per_seq = t // step
    nb, _, past = sample_args[1].shape
    assert nh == DIFF_HEADS and nb == b * per_seq
    small = lambda n: pl.BlockSpec((1, n), lambda b, h, i: (0, 0))
    row = lambda b, h, i: b * per_seq + i
    new = lambda: pl.BlockSpec((nq, WIDTH), lambda b, h, i: (row(b, h, i), 0))
    cache = lambda r, c: pl.BlockSpec((1, r, c), lambda b, h, i: (row(b, h, i), 0, 0))
    return pl.pallas_call(
        functools.partial(_diff_kernel, blk=blk, nh=nh, sub=sub, nq=nq, past=past),
        grid=(b, DIFF_HEADS // nh, per_seq),
        in_specs=[pl.BlockSpec((1, step, nh * LANES), lambda b, h, i: (b, i, h)),
                  pl.BlockSpec((1, t, nh * LANES), lambda b, h, i: (b, 0, h)),
                  pl.BlockSpec((1, nh * VT_ROWS, t), lambda b, h, i: (b, h, 0)),
                  pl.BlockSpec((nh, 2 * blk, blk), lambda b, h, i: (h, 0, 0)),
                  pl.BlockSpec((1, step, nh * LANES), lambda b, h, i: (b, i, h)),
                  small(HD), small(HD), small(HD), small(HD), small(2 * HD),
                  new(), cache(WIDTH, past), cache(past * DIFF_HEADS, 2 * HD), new(), new(),
                  pl.BlockSpec((DIFF_HEADS, nq, past + LANES), lambda b, h, i: (0, 0, 0)), new()],
        out_specs=[pl.BlockSpec((1, step, nh * LANES), lambda b, h, i: (b, i, h)), new()],
        out_shape=[jax.ShapeDtypeStruct((b, t, WIDTH), BF16), jax.ShapeDtypeStruct((nb * nq, WIDTH), BF16)],
        scratch_shapes=[pltpu.VMEM((2 * nh, VT_ROWS, blk), F32), pltpu.VMEM((1, blk), F32)],
        compiler_params=pltpu.CompilerParams(
            dimension_semantics=("parallel", "parallel", "arbitrary"), vmem_limit_bytes=VMEM_LIMIT),
        name="diff_prompt",
    )(dq, dk, dvt, bias, dz, lq1, lk1, lq2, lk2, subg, *sample_args)


def _suffix_sums(x):
    n = x.shape[1] // LANES
    lane = lax.broadcasted_iota(jnp.int32, (x.shape[0], LANES), 1)
    after = jnp.zeros((x.shape[0], 1), F32)
    pieces = [None] * n
    for blk in reversed(range(n)):
        piece = x[:, blk * LANES:(blk + 1) * LANES]
        y = piece
        shift = 1
        while shift < LANES:
            y = y + jnp.where(lane + shift < LANES, pltpu.roll(y, LANES - shift, 1), 0.0)
            shift *= 2
        pieces[blk] = y - piece + after
        after = after + y[:, 0:1]
    return jnp.concatenate(pieces, axis=1), after


def _pad_rows(x, rows):
    return jnp.concatenate([x, jnp.zeros((rows - x.shape[0], x.shape[1]), x.dtype)], axis=0)


def _joint_softmax(s_p, s_n):
    m = jnp.maximum(jnp.max(s_p, axis=-1, keepdims=True), jnp.max(s_n, axis=-1, keepdims=True))
    p_p = jnp.exp(s_p - m)
    p_n = jnp.exp(s_n - m)
    l = jnp.sum(p_p, axis=-1, keepdims=True) + jnp.sum(p_n, axis=-1, keepdims=True)
    return p_p.astype(BF16), p_n.astype(BF16), l


def _joint_pv(p_p, p_n, l, vt_p, v_n):
    return (_mm_nt(p_p, vt_p) + _mm(p_n, v_n)) / l


def _fox_sample_kernel(q_ref, kt_ref, vt_ref, kn_ref, vn_ref, lfp_ref, lfn_ref, z_ref, o_ref, *, nq):
    r_new, total_new = _suffix_sums(lfn_ref[0])
    r_past, _ = _suffix_sums(lfp_ref[0])
    r_past = r_past + total_new
    lane = lax.broadcasted_iota(jnp.int32, (nq, LANES), 1)
    causal = lane <= lax.broadcasted_iota(jnp.int32, (nq, LANES), 0)
    sls = [slice(blk * LANES, (blk + 1) * LANES) for blk in range(PAIRS)]
    scores = []
    for blk, sl in enumerate(sls):
        q = q_ref[:, sl]
        kt = kt_ref[0, sl, :].astype(BF16)
        kn = _pad_rows(kn_ref[:, sl], LANES).astype(BF16)
        for par in range(2):
            hd = 2 * blk + par
            qm = jnp.where((lane < HD) if par == 0 else (lane >= HD), q, jnp.zeros_like(q))
            scores.append((_mm(qm, kt) + r_past[hd:hd + 1, :],
                           jnp.where(causal, _mm_nt(qm, kn) + r_new[hd:hd + 1, :], NEG)))
    weights = [_joint_softmax(s_p, s_n) for s_p, s_n in scores]
    for blk, sl in enumerate(sls):
        vt = vt_ref[0, sl, :].astype(BF16)
        vn = _pad_rows(vn_ref[:, sl], LANES).astype(BF16)
        outs = [_joint_pv(*weights[2 * blk + par], vt, vn) for par in range(2)]
        o = jnp.where(lane < HD, outs[0], outs[1])
        o_ref[:, sl] = (o * _silu(z_ref[:, sl].astype(F32))).astype(BF16)


def _diff_sample_kernel(q_ref, kt_ref, v_ref, kn_ref, vn_ref, bias_ref, z_ref,
                        lq1_ref, lk1_ref, lq2_ref, lk2_ref, subg_ref, o_ref, *, nq, past):
    lane = lax.broadcasted_iota(jnp.int32, (nq, LANES), 1)
    lam = _lambda(lq1_ref, lk1_ref, lq2_ref, lk2_ref)
    sls = [slice(hd * LANES, (hd + 1) * LANES) for hd in range(DIFF_HEADS)]
    scores = []
    for hd, sl in enumerate(sls):
        q = q_ref[:, sl]
        kt = kt_ref[0, sl, :].astype(BF16)
        kn = _pad_rows(kn_ref[:, sl], LANES).astype(BF16)
        for mp in range(2):
            qm = jnp.where((lane < HD) if mp == 0 else (lane >= HD), q, jnp.zeros_like(q))
            scores.append((_mm(qm, kt) + bias_ref[hd, :, 0:past],
                           _mm_nt(qm, kn) + bias_ref[hd, :, past:past + LANES]))
    weights = [_joint_softmax(s_p, s_n) for s_p, s_n in scores]
    for hd, sl in enumerate(sls):
        vt = v_ref[0, pl.ds(hd, past, stride=DIFF_HEADS), :].T.astype(BF16)
        vn = _pad_rows(vn_ref[:, sl], LANES).astype(BF16)
        outs = [_joint_pv(*weights[2 * hd + mp], vt, vn) for mp in range(2)]
        o_ref[:, sl] = _diff_finish(outs[0] - lam * outs[1], subg_ref[...], z_ref[:, sl].astype(F32))


def _out_kernel(yf_ref, yd_ref, ga_ref, gb_ref, x_ref, wuf_ref, wud_ref, wo_ref, g_ref, o_ref):
    merged = (ga_ref[...].astype(F32) * _mm(yf_ref[...], wuf_ref[...])
              + gb_ref[...].astype(F32) * _mm(yd_ref[...], wud_ref[...]))
    out = x_ref[...] + _mm(merged.astype(BF16), wo_ref[...])
    ms = jnp.mean(out * out, axis=-1, keepdims=True)
    o_ref[...] = out * lax.rsqrt(ms + NORM_EPS) * g_ref[...]


def _output(yf, yd, ga, gb, x2d, wuf, wud, wo, g, *, tm):
    rows = x2d.shape[0]
    assert rows % tm == 0
    row_spec = lambda n: pl.BlockSpec((tm, n), lambda i: (i, 0))
    const = lambda shape: pl.BlockSpec(shape, lambda i: (0, 0))
    return pl.pallas_call(
        _out_kernel,
        grid=(rows // tm,),
        in_specs=[row_spec(WIDTH), row_spec(WIDTH), row_spec(D_MODEL), row_spec(D_MODEL), row_spec(D_MODEL),
                  const((WIDTH, D_MODEL)), const((WIDTH, D_MODEL)), const((D_MODEL, D_MODEL)),
                  const((1, D_MODEL))],
        out_specs=row_spec(D_MODEL),
        out_shape=jax.ShapeDtypeStruct((rows, D_MODEL), F32),
        compiler_params=pltpu.CompilerParams(
            dimension_semantics=("parallel",), vmem_limit_bytes=VMEM_LIMIT),
        name="out_proj",
    )(yf, yd, ga, gb, x2d, wuf, wud, wo, g)


def _forget_lane_heads():
    lanes = np.arange(LANES)
    heads = (lanes % HD) // FL_SLOT
    heads[:FOX_HEADS] = np.arange(FOX_HEADS)
    return heads


def _pack_w_in(w_in, b_forget):
    sizes = (WIDTH, WIDTH, WIDTH, WIDTH, FOX_HEADS, WIDTH, WIDTH, WIDTH, WIDTH, D_MODEL, D_MODEL)
    offs = np.cumsum((0,) + sizes)
    fq, fk, fv, fz, fl, dq, dk, dv, dz, ga, gb = (w_in[:, offs[i]:offs[i + 1]] for i in range(len(sizes)))
    heads = _forget_lane_heads()
    w_all = jnp.concatenate([fq, fk, fv, fz, dq, dk, dv, dz, ga, gb, fl[:, heads]], axis=1).astype(BF16)
    return w_all, b_forget[heads][None, :]


def kernel(x_prompt, x_sample, cache_fox_k, cache_fox_v, cache_fox_logf, cache_diff_k, cache_diff_v,
           norm_in_g, w_in, b_forget, lambda_q1, lambda_k1, lambda_q2, lambda_k2, subln_g,
           w_up_fox, w_up_diff, w_o, rel_bias, final_norm_g):
    bp, tp, _ = x_prompt.shape
    bs, ts, _ = x_sample.shape
    past = cache_fox_k.shape[2]
    blk, tm = ATTN_BLOCK, ROW_TILE
    assert tp % (QUERY_BLOCKS_PER_STEP * blk) == 0 and blk > _T5_THR[-1] and blk % CHUNK == 0

    w_all, bfl = _pack_w_in(w_in[0], b_forget[0])
    g_in = norm_in_g[0][None, :]
    g_out = final_norm_g[None, :]
    wuf, wud, wo = w_up_fox[0].astype(BF16), w_up_diff[0].astype(BF16), w_o[0].astype(BF16)
    lq1, lk1, lq2, lk2 = (a[0][None, :] for a in (lambda_q1, lambda_k1, lambda_q2, lambda_k2))
    subg = subln_g[0][None, :]

    xs = x_sample.reshape(bs * ts, D_MODEL)
    (fk_s, fv_s, lf_s, dk_s, dv_s, fq_s, fz_s, dq_s, dz_s, ga_s, gb_s) = _project(
        xs, g_in, w_all, bfl, seq_len=ts, aug=False, tm=tm)
    xp = x_prompt.reshape(bp * tp, D_MODEL)
    (fkt, fvt, lft, dkt, dv4, qa, ka, vtf, fz, dq, dkb, vtd, dz, ga, gb) = _project(
        xp, g_in, w_all, bfl, seq_len=tp, aug=True, tm=tm)
    fox_k_p = jnp.transpose(fkt.reshape(bp, FOX_HEADS, HD, tp), (0, 3, 1, 2))[None]
    fox_v_p = jnp.transpose(fvt.reshape(bp, FOX_HEADS, HD, tp), (0, 3, 1, 2))[None]
    fox_lf_p = jnp.transpose(lft, (0, 2, 1))[None]
    diff_k_p = jnp.transpose(dkt.reshape(bp, DIFF_HEADS, 2, HD, tp), (0, 4, 1, 2, 3))[None]
    diff_v_p = dv4.reshape(1, bp, tp, DIFF_HEADS, 2 * HD)
    fox_kt = jnp.transpose(cache_fox_k[0], (0, 2, 3, 1)).reshape(bs, WIDTH, past)
    fox_vt = jnp.transpose(cache_fox_v[0], (0, 2, 3, 1)).reshape(bs, WIDTH, past)
    diff_kt = jnp.transpose(cache_diff_k[0], (0, 2, 3, 4, 1)).reshape(bs, WIDTH, past)
    diff_v = cache_diff_v[0].reshape(bs, past * DIFF_HEADS, 2 * HD)
    lf_past_t = jnp.swapaxes(cache_fox_logf[0], 1, 2)
    lf_new_t = jnp.pad(jnp.swapaxes(lf_s.reshape(bs, ts, FOX_HEADS), 1, 2), ((0, 0), (0, 0), (0, LANES - ts)))

    r3 = lambda a: a.reshape(bp, tp, a.shape[-1])
    yf, yf_s = _fox_attention(r3(qa), r3(ka), vtf, r3(fz),
                              (fq_s, fox_kt, fox_vt, fk_s, fv_s, lf_past_t, lf_new_t, fz_s),
                              blk=blk, nh=FOX_STEP_HEADS, sub=QUERY_BLOCKS_PER_STEP, nq=ts)
    bias_p = _bias_tiles(rel_bias, q_start=blk, nq=blk, segments=((0, blk, blk), (blk, blk, blk)),
                         transposed=True, scale=LOG2E)
    bias_s = _bias_tiles(rel_bias, q_start=past, nq=ts, segments=((0, past, past), (past, LANES, ts)),
                         transposed=False, scale=1.0)
    yd, yd_s = _diff_attention(r3(dq), r3(dkb), vtd, bias_p, r3(dz), lq1, lk1, lq2, lk2, subg,
                               (dq_s, diff_kt, diff_v, dk_s, dv_s, bias_s, dz_s),
                               blk=blk, nh=DIFF_STEP_HEADS, sub=QUERY_BLOCKS_PER_STEP, nq=ts)

    y_p = _output(yf.reshape(bp * tp, WIDTH), yd.reshape(bp * tp, WIDTH), ga, gb, xp, wuf, wud, wo, g_out, tm=tm)
    y_s = _output(yf_s, yd_s, ga_s, gb_s, xs, wuf, wud, wo, g_out, tm=tm)

    return (y_p.reshape(bp, tp, D_MODEL), y_s.reshape(bs, ts, D_MODEL),
            fox_k_p, fox_v_p, fox_lf_p, diff_k_p, diff_v_p,
            fk_s.reshape(1, bs, ts, FOX_HEADS, HD), fv_s.reshape(1, bs, ts, FOX_HEADS, HD),
            lf_s.reshape(1, bs, ts, FOX_HEADS),
            dk_s.reshape(1, bs, ts, DIFF_HEADS, 2, HD), dv_s.reshape(1, bs, ts, DIFF_HEADS, 2 * HD))
```

```python
import functools
import math

import numpy as np
import jax
import jax.numpy as jnp
from jax import lax
from jax.experimental import pallas as pl
from jax.experimental.pallas import tpu as pltpu

F32 = jnp.float32
BF16 = jnp.bfloat16

D_MODEL = 1024
HD = 64
FOX_HEADS = 8
DIFF_HEADS = 4
WIDTH = 512
CHUNK = 64
CHUNK_SHIFT = CHUNK.bit_length() - 1
N_BUCKETS = 32
MAX_DISTANCE = 128
NORM_EPS = 1e-6
SCALE = HD ** -0.5
NEG = -1e30
LAMBDA_INIT = 0.8 - 0.6 * math.exp(-0.3 * 0)
LOG2E = math.log2(math.e)

LANES = 128
PAIRS = WIDTH // LANES
VT_PAD = 16
VT_ROWS = LANES + VT_PAD
FVT_ROWS = HD + VT_PAD

_O_FQ, _O_FK, _O_FV, _O_FZ = 0, 512, 1024, 1536
_O_DQ, _O_DK, _O_DV, _O_DZ = 2048, 2560, 3072, 3584
_O_GA, _O_GB = 4096, 5120
_O_FL = 6144
_N_ALL = 6272
FL_SLOT = 8

VMEM_LIMIT = 58 * 1024 * 1024
ATTN_BLOCK = 256
ROW_TILE = 512

FOX_STEP_HEADS = 8
DIFF_STEP_HEADS = 4
HEADROOM = 64.0
FIRST_STEP_GROUP = 2
QUERY_BLOCKS_PER_STEP = 4
STEP_BLOCKS = (4, 2, 1)


def _t5_thresholds():
    nb = N_BUCKETS // 2
    max_exact = nb // 2
    n = np.arange(0, 4 * MAX_DISTANCE)
    large = max_exact + (np.log(np.maximum(n, 1).astype(np.float32) / max_exact)
                         / math.log(MAX_DISTANCE / max_exact) * (nb - max_exact)).astype(np.int32)
    large = np.minimum(large, nb - 1)
    thr = [int(np.argmax(large >= b)) for b in range(max_exact + 1, nb)]
    return max_exact, tuple(thr)


_MAX_EXACT, _T5_THR = _t5_thresholds()


def _mm(a, b):
    return jnp.dot(a, b, preferred_element_type=F32)


def _mm_nt(a, b):
    return lax.dot_general(a, b, (((1,), (1,)), ((), ())), preferred_element_type=F32)


def _log_sigmoid(x):
    return -(jnp.maximum(-x, 0.0) + jnp.log1p(jnp.exp(-jnp.abs(x))))


def _sigmoid(x):
    return 1.0 / (1.0 + jnp.exp(-x))


def _silu(x):
    return x * _sigmoid(x)


def _split3(x):
    p1 = x.astype(BF16).astype(F32)
    r1 = x - p1
    p2 = r1.astype(BF16).astype(F32)
    p3 = r1 - p2
    return p1, p2, p3


def _proj_kernel(x_ref, g_ref, w_ref, bfl_ref, *refs, tm, seq_len, aug):
    if aug:
        (fkt_ref, fvt_ref, lft_ref, dkt_ref, dv4_ref, qa_ref, ka_ref, vtf_ref,
         fz_ref, dq_ref, dkb_ref, vtd_ref, dz_ref, ga_ref, gb_ref, carry_ref) = refs
    else:
        (fk_ref, fv_ref, lf_ref, dk_ref, dv_ref, fq_ref,
         fz_ref, dq_ref, dz_ref, ga_ref, gb_ref) = refs

    x = x_ref[...]
    ms = jnp.mean(x * x, axis=-1, keepdims=True)
    h = (x * lax.rsqrt(ms + NORM_EPS) * g_ref[...]).astype(BF16)

    def grp(off, n=WIDTH):
        return _mm(h, w_ref[:, off:off + n])

    if not aug:
        fq_ref[...] = (grp(_O_FQ) * SCALE).astype(BF16)
        fk_ref[...] = grp(_O_FK)
        fv_ref[...] = grp(_O_FV)
        fz_ref[...] = grp(_O_FZ).astype(BF16)
        dq_ref[...] = (grp(_O_DQ) * SCALE).astype(BF16)
        dk_ref[...] = grp(_O_DK)
        dv_ref[...] = grp(_O_DV)
        dz_ref[...] = grp(_O_DZ).astype(BF16)
        ga_ref[...] = _sigmoid(grp(_O_GA, D_MODEL)).astype(BF16)
        gb_ref[...] = _sigmoid(grp(_O_GB, D_MODEL)).astype(BF16)
        lf_ref[...] = _log_sigmoid(grp(_O_FL, LANES) + bfl_ref[...])[:, :FOX_HEADS]
        return

    lf = _log_sigmoid(grp(_O_FL, LANES) + bfl_ref[...])
    lft_ref[0] = lf.T[:FOX_HEADS, :]
    row = lax.broadcasted_iota(jnp.int32, (tm, LANES), 0)
    c = lf
    shift = 1
    while shift < tm:
        c = c + jnp.where(row >= shift, pltpu.roll(c, shift, 0), 0.0)
        shift *= 2
    tiles_per_seq = seq_len // tm

    @pl.when(pl.program_id(0) % tiles_per_seq == 0)
    def _():
        carry_ref[...] = jnp.zeros_like(carry_ref)

    c = c + carry_ref[...]
    carry_ref[...] = c[tm - 1:tm, :]
    c_hi, c_mid, c_lo = _split3(c * LOG2E)

    fq = grp(_O_FQ) * (SCALE * LOG2E)
    fk = grp(_O_FK)
    fv = grp(_O_FV)
    fvt = fv.T
    fkt_ref[0] = fk.T
    fvt_ref[0] = fvt
    tail = jnp.where(lax.broadcasted_iota(jnp.int32, (VT_PAD, tm), 0) == 0, 1.0, 0.0).astype(BF16)
    lane = lax.broadcasted_iota(jnp.int32, (tm, LANES), 1)
    pos = lane & (FL_SLOT - 1)
    q_slots = jnp.where(pos == 0, c_hi, jnp.where(pos == 1, c_mid, jnp.where(
        pos == 2, c_lo, jnp.where(pos < 6, 1.0, 0.0))))
    k_slots = jnp.where(pos < 3, 1.0, jnp.where(pos == 3, -c_hi, jnp.where(
        pos == 4, -c_mid, jnp.where(pos == 5, -c_lo, 0.0))))
    for hd in range(FOX_HEADS):
        blk, par = divmod(hd, 2)
        sl = slice(blk * LANES, (blk + 1) * LANES)
        data = (lane < HD) if par == 0 else (lane >= HD)
        slot0 = (1 - par) * HD + hd * FL_SLOT
        own = (lane >= slot0) & (lane < slot0 + FL_SLOT)
        q_aug = jnp.where(own, q_slots, 0.0)
        k_aug = jnp.where(own, k_slots, 0.0)
        osl = slice(hd * LANES, (hd + 1) * LANES)
        qa_ref[:, osl] = jnp.where(data, fq[:, sl], q_aug).astype(BF16)
        ka_ref[:, osl] = jnp.where(data, fk[:, sl], k_aug).astype(BF16)
        base = hd * FVT_ROWS
        vtf_ref[0, base:base + HD, :] = fvt[hd * HD:(hd + 1) * HD, :].astype(BF16)
        vtf_ref[0, base + HD:base + FVT_ROWS, :] = tail

    ga_ref[...] = _sigmoid(grp(_O_GA, D_MODEL)).astype(BF16)
    gb_ref[...] = _sigmoid(grp(_O_GB, D_MODEL)).astype(BF16)

    dk = grp(_O_DK)
    dkt_ref[0] = dk.T
    dkb_ref[...] = dk.astype(BF16)
    dv = grp(_O_DV)
    dv4_ref[...] = dv.reshape(tm, DIFF_HEADS, 2 * HD)
    for hd in range(DIFF_HEADS):
        base = hd * VT_ROWS
        vtd_ref[0, base:base + LANES, :] = dv[:, hd * LANES:(hd + 1) * LANES].T.astype(BF16)
        vtd_ref[0, base + LANES:base + VT_ROWS, :] = tail

    dq_ref[...] = (grp(_O_DQ) * (SCALE * LOG2E)).astype(BF16)
    fz_ref[...] = grp(_O_FZ).astype(BF16)
    dz_ref[...] = grp(_O_DZ).astype(BF16)


def _project(x2d, g, w_all, bfl, *, seq_len, aug, tm):
    rows = x2d.shape[0]
    assert rows % tm == 0
    row_spec = lambda n: pl.BlockSpec((tm, n), lambda i: (i, 0))
    const = lambda shape: pl.BlockSpec(shape, lambda i: (0, 0))
    f32o = lambda n: jax.ShapeDtypeStruct((rows, n), F32)
    b16o = lambda n: jax.ShapeDtypeStruct((rows, n), BF16)
    if aug:
        assert seq_len % tm == 0
        wide = FOX_HEADS * LANES
        tps = seq_len // tm
        t_shape = lambda n, dt: jax.ShapeDtypeStruct((rows // seq_len, n, seq_len), dt)
        t_spec = lambda n: pl.BlockSpec((1, n, tm), lambda i: (i // tps, 0, i % tps))
        out_shape = [t_shape(WIDTH, F32), t_shape(WIDTH, F32), t_shape(FOX_HEADS, F32), t_shape(WIDTH, F32),
                     jax.ShapeDtypeStruct((rows, DIFF_HEADS, 2 * HD), F32)]
        out_specs = [t_spec(WIDTH), t_spec(WIDTH), t_spec(FOX_HEADS), t_spec(WIDTH),
                     pl.BlockSpec((tm, DIFF_HEADS, 2 * HD), lambda i: (i, 0, 0))]
        out_shape += ([b16o(wide)] * 2 + [t_shape(FOX_HEADS * FVT_ROWS, BF16)] + [b16o(WIDTH)] * 3
                      + [t_shape(DIFF_HEADS * VT_ROWS, BF16)] + [b16o(WIDTH)] + [b16o(D_MODEL)] * 2)
        out_specs += ([row_spec(wide)] * 2 + [t_spec(FOX_HEADS * FVT_ROWS)] + [row_spec(WIDTH)] * 3
                      + [t_spec(DIFF_HEADS * VT_ROWS)] + [row_spec(WIDTH)] + [row_spec(D_MODEL)] * 2)
        scratch = [pltpu.VMEM((1, LANES), F32)]
    else:
        out_shape = [f32o(WIDTH), f32o(WIDTH), f32o(FOX_HEADS), f32o(WIDTH), f32o(WIDTH)]
        out_specs = [row_spec(WIDTH), row_spec(WIDTH), row_spec(FOX_HEADS), row_spec(WIDTH), row_spec(WIDTH)]
        out_shape += [b16o(WIDTH)] * 4 + [b16o(D_MODEL)] * 2
        out_specs += [row_spec(WIDTH)] * 4 + [row_spec(D_MODEL)] * 2
        scratch = []
    return pl.pallas_call(
        functools.partial(_proj_kernel, tm=tm, seq_len=seq_len, aug=aug),
        grid=(rows // tm,),
        in_specs=[row_spec(D_MODEL), const((1, D_MODEL)),
                  pl.BlockSpec((D_MODEL, _N_ALL), lambda i: (0, 0), pipeline_mode=pl.Buffered(1)),
                  const((1, LANES))],
        out_specs=out_specs,
        out_shape=out_shape,
        scratch_shapes=scratch,
        compiler_params=pltpu.CompilerParams(
            dimension_semantics=("arbitrary",), vmem_limit_bytes=VMEM_LIMIT),
        name="proj_aug" if aug else "proj_plain",
    )(x2d, g, w_all, bfl)


def _bias_kernel(tbl_ref, o_ref, *, q_start, nq, segments, transposed, scale):
    hd = pl.program_id(0)
    far = tbl_ref[N_BUCKETS // 2 - 1, hd]
    q_axis, k_axis = (1, 0) if transposed else (0, 1)
    off = 0
    for k_start, nk, valid in segments:
        shape = (nk, nq) if transposed else (nq, nk)
        qpos = q_start + lax.broadcasted_iota(jnp.int32, shape, q_axis)
        kidx = lax.broadcasted_iota(jnp.int32, shape, k_axis)
        kpos = k_start + kidx
        rel = kpos - qpos
        n = jnp.abs(rel)
        large = jnp.full(shape, _MAX_EXACT, jnp.int32)
        for thr in _T5_THR:
            large = large + jnp.where(n >= thr, 1, 0)
        bucket = jnp.where(rel > 0, N_BUCKETS // 2, 0) + jnp.where(n < _MAX_EXACT, n, large)
        val = jnp.zeros(shape, F32)
        for b in range(N_BUCKETS):
            val = jnp.where(bucket == b, tbl_ref[b, hd], val)
        val = (val - far) * scale
        visible = ((kpos >> CHUNK_SHIFT) <= (qpos >> CHUNK_SHIFT)) & (kidx < valid)
        val = jnp.where(visible, val, NEG)
        if transposed:
            o_ref[0, off:off + nk, :] = val
        else:
            o_ref[0, :, off:off + nk] = val
        off += nk


def _bias_tiles(rel_bias, *, q_start, nq, segments, transposed, scale):
    total = sum(s[1] for s in segments)
    shape = (total, nq) if transposed else (nq, total)
    return pl.pallas_call(
        functools.partial(_bias_kernel, q_start=q_start, nq=nq, segments=segments, transposed=transposed,
                          scale=scale),
        grid=(DIFF_HEADS,),
        in_specs=[pl.BlockSpec(memory_space=pltpu.SMEM)],
        out_specs=pl.BlockSpec((1,) + shape, lambda h: (h, 0, 0)),
        out_shape=jax.ShapeDtypeStruct((DIFF_HEADS,) + shape, F32),
        name="t5_bias",
    )(rel_bias)


def _flash_t(k_ref, vt_ref, acc_ref, bad_ref, streams, qi, blk, prev_bias, diag_bias, diag_mask):
    n = len(streams)

    def scores(first_blk, width):
        start = pl.multiple_of(first_blk * blk, blk)
        return [_mm_nt(k_ref[0, pl.ds(start, width), ksl], q) for q, ksl, _ in streams]

    def step(first_blk, width, ms, biases, mask):
        start = pl.multiple_of(first_blk * blk, blk)
        sts = scores(first_blk, width)
        out, ps, alphas = [], [], []
        for s, st in enumerate(sts):
            if biases is not None:
                st = st + biases[s]()
            if mask is not None:
                st = jnp.where(mask, st, NEG)
            m_new = jnp.maximum(ms[s], jnp.max(st, axis=0, keepdims=True))
            ps.append(jnp.exp2(st - m_new).astype(BF16))
            alphas.append(jnp.exp2(ms[s] - m_new))
            out.append(m_new)
        for s, (_, _, vsl) in enumerate(streams):
            acc_ref[s] = alphas[s] * acc_ref[s] + _mm(vt_ref[0, vsl, pl.ds(start, width)], ps[s])
        return tuple(out)

    def first_step():
        start = pl.multiple_of(qi * blk, blk)
        out, ps = [], []
        for s, st in enumerate(scores(qi, blk)):
            if diag_bias is not None:
                st = st + diag_bias[s]()
            if diag_mask is not None:
                st = jnp.where(diag_mask, st, NEG)
            m = jnp.max(st, axis=0, keepdims=True)
            ps.append(jnp.exp2(st - m).astype(BF16))
            out.append(m)
        for s, (_, _, vsl) in enumerate(streams):
            acc_ref[s] = _mm(vt_ref[0, vsl, pl.ds(start, blk)], ps[s])
        return tuple(out)

    ms = first_step()
    bad_ref[...] = jnp.zeros_like(bad_ref)

    def consume(sts, first_blk, width):
        start = pl.multiple_of(first_blk * blk, blk)
        ps = [jnp.exp2(st - ms[s]).astype(BF16) for s, st in enumerate(sts)]
        worst = None
        for p in ps:
            d = jnp.max(p, axis=0, keepdims=True)
            worst = d if worst is None else jnp.maximum(worst, d)
        bad_ref[...] = jnp.maximum(bad_ref[...], worst.astype(F32))
        for s, (_, _, vsl) in enumerate(streams):
            acc_ref[s] = acc_ref[s] + _mm(vt_ref[0, vsl, pl.ds(start, width)], ps[s])

    if prev_bias is not None:
        @pl.when(qi >= 1)
        def _():
            consume([st + prev_bias[s]() for s, st in enumerate(scores(qi - 1, blk))], qi - 1, blk)

    n_plain = qi if prev_bias is None else jnp.maximum(qi - 1, 0)
    done = 0
    for w in STEP_BLOCKS:
        count = (n_plain - done) // w

        def fast_step(j, carry, w=w, done=done):
            consume(scores(done + j * w, w * blk), done + j * w, w * blk)
            return carry

        lax.fori_loop(0, count, fast_step, 0)
        done = done + count * w

    @pl.when(jnp.max(bad_ref[...]) > 2.0 ** HEADROOM)
    def _():
        c = first_step()
        if prev_bias is not None:
            c = lax.cond(qi >= 1, lambda c: step(qi - 1, blk, c, prev_bias, None), lambda c: c, c)
        lax.fori_loop(0, n_plain, lambda j, c: step(j, blk, c, None, None), c)


def _fox_kernel(qa_ref, ka_ref, vt_ref, z_ref, *rest, blk, nh, sub, nq):
    sample_in, (o_ref, so_ref, acc_ref, bad_ref) = rest[:8], rest[8:]
    key = lax.broadcasted_iota(jnp.int32, (blk, blk), 0)
    qry = lax.broadcasted_iota(jnp.int32, (blk, blk), 1)
    hsl = [slice(hd * LANES, (hd + 1) * LANES) for hd in range(nh)]

    def query_block(i, carry):
        qi = pl.program_id(2) * sub + i
        rows = pl.ds(pl.multiple_of(i * blk, blk), blk)
        streams = [(qa_ref[0, rows, sl], sl, slice(hd * FVT_ROWS, (hd + 1) * FVT_ROWS))
                   for hd, sl in enumerate(hsl)]
        _flash_t(ka_ref, vt_ref, acc_ref, bad_ref, streams, qi, blk, None, None, key <= qry)
        for pair in range(nh // 2):
            halves = []
            for par in range(2):
                acc = acc_ref[2 * pair + par]
                halves.append(acc[0:HD, :] / acc[HD:HD + 1, :])
            o = jnp.concatenate(halves, axis=0).T
            o_ref[0, rows, hsl[pair]] = (o * _silu(z_ref[0, rows, hsl[pair]].astype(F32))).astype(BF16)
        return carry

    lax.fori_loop(0, sub, query_block, 0)
    _fox_sample_kernel(*sample_in, so_ref, nq=nq)


def _fox_attention(qa, ka, vt, fz, sample_args, *, blk, nh, sub, nq):
    b, t, _ = qa.shape
    step = sub * blk
    per_seq = t // step
    nb, _, past = sample_args[1].shape
    assert nh == FOX_HEADS and nb == b * per_seq
    row = lambda b, p, i: b * per_seq + i
    new = lambda: pl.BlockSpec((nq, WIDTH), lambda b, p, i: (row(b, p, i), 0))
    cache = lambda r, c: pl.BlockSpec((1, r, c), lambda b, p, i: (row(b, p, i), 0, 0))
    return pl.pallas_call(
        functools.partial(_fox_kernel, blk=blk, nh=nh, sub=sub, nq=nq),
        grid=(b, FOX_HEADS // nh, per_seq),
        in_specs=[pl.BlockSpec((1, step, nh * LANES), lambda b, p, i: (b, i, p)),
                  pl.BlockSpec((1, t, nh * LANES), lambda b, p, i: (b, 0, p)),
                  pl.BlockSpec((1, nh * FVT_ROWS, t), lambda b, p, i: (b, p, 0)),
                  pl.BlockSpec((1, step, nh * HD), lambda b, p, i: (b, i, p)),
                  new(), cache(WIDTH, past), cache(WIDTH, past), new(), new(),
                  cache(FOX_HEADS, past), cache(FOX_HEADS, LANES), new()],
        out_specs=[pl.BlockSpec((1, step, nh * HD), lambda b, p, i: (b, i, p)), new()],
        out_shape=[jax.ShapeDtypeStruct((b, t, WIDTH), BF16), jax.ShapeDtypeStruct((nb * nq, WIDTH), BF16)],
        scratch_shapes=[pltpu.VMEM((nh, FVT_ROWS, blk), F32), pltpu.VMEM((1, blk), F32)],
        compiler_params=pltpu.CompilerParams(
            dimension_semantics=("parallel", "parallel", "arbitrary"), vmem_limit_bytes=VMEM_LIMIT),
        name="fox_prompt",
    )(qa, ka, vt, fz, *sample_args)


def _lambda(lq1_ref, lk1_ref, lq2_ref, lk2_ref):
    s1 = jnp.sum(lq1_ref[...] * lk1_ref[...], axis=-1, keepdims=True)
    s2 = jnp.sum(lq2_ref[...] * lk2_ref[...], axis=-1, keepdims=True)
    return jnp.exp(s1) - jnp.exp(s2) + LAMBDA_INIT


def _diff_finish(o, subg, z):
    ms = jnp.mean(o * o, axis=-1, keepdims=True)
    od = o * lax.rsqrt(ms + NORM_EPS) * subg * (1.0 - LAMBDA_INIT)
    return (od * _silu(z)).astype(BF16)


def _diff_kernel(q_ref, k_ref, vt_ref, bias_ref, z_ref, lq1_ref, lk1_ref, lq2_ref, lk2_ref, subg_ref,
                 *rest, blk, nh, sub, nq, past):
    sample_in, (o_ref, so_ref, acc_ref, ms_ref, bad_ref) = rest[:7], rest[7:]
    lane = lax.broadcasted_iota(jnp.int32, (blk, LANES), 1)
    lam = _lambda(lq1_ref, lk1_ref, lq2_ref, lk2_ref)
    n = 2 * nh
    qi0 = pl.program_id(2) * sub
    ksl = [slice((s // 2) * LANES, (s // 2 + 1) * LANES) for s in range(n)]
    vsl = [slice((s // 2) * VT_ROWS, (s // 2 + 1) * VT_ROWS) for s in range(n)]

    def queries(rows):
        out = []
        for s in range(n):
            q = q_ref[0, rows, ksl[s]]
            out.append(jnp.where((lane < HD) if s % 2 == 0 else (lane >= HD), q, jnp.zeros_like(q)))
        return out

    def scores(qs, blk0, width):
        start = pl.multiple_of(blk0 * blk, blk)
        return [_mm_nt(k_ref[0, pl.ds(start, width), ksl[s]], qs[s]) for s in range(n)]

    def values(s, blk0, width):
        return vt_ref[0, vsl[s], pl.ds(pl.multiple_of(blk0 * blk, blk), width)]

    def prev_bias(s):
        return bias_ref[s // 2, 0:blk, :]

    def diag_bias(s):
        return bias_ref[s // 2, blk:2 * blk, :]

    qs_all = [queries(slice(j * blk, (j + 1) * blk)) for j in range(sub)]
    for group in range(0, sub, FIRST_STEP_GROUP):
        js = range(group, min(group + FIRST_STEP_GROUP, sub))
        diag = {j: scores(qs_all[j], qi0 + j, blk) for j in js}
        ms_all, p_diag = {}, {}
        for j in js:
            ms_all[j], p_diag[j] = [], []
            for s in range(n):
                st = diag[j][s] + diag_bias(s)
                m = jnp.max(st, axis=0, keepdims=True)
                p_diag[j].append(jnp.exp2(st - m).astype(BF16))
                ms_all[j].append(m)
                ms_ref[j, s] = m
        prev = {j: scores(qs_all[j], qi0 + j - 1, blk) for j in js if j >= 1}
        p_prev = {j: [jnp.exp2(prev[j][s] + prev_bias(s) - ms_all[j][s]).astype(BF16) for s in range(n)]
                  for j in prev}
        for j in js:
            worst = None
            for s in range(n):
                acc = _mm(values(s, qi0 + j, blk), p_diag[j][s])
                if j >= 1:
                    acc = acc + _mm(values(s, qi0 + j - 1, blk), p_prev[j][s])
                    d = jnp.max(p_prev[j][s], axis=0, keepdims=True)
                    worst = d if worst is None else jnp.maximum(worst, d)
                acc_ref[j, s] = acc
            bad_ref[j] = jnp.zeros((1, blk), F32) if worst is None else worst.astype(F32)

    def consume(j, ms, sts, blk0, width, biased):
        ps = [jnp.exp2((sts[s] + prev_bias(s) if biased else sts[s]) - ms[s]).astype(BF16) for s in range(n)]
        worst = None
        for p in ps:
            d = jnp.max(p, axis=0, keepdims=True)
            worst = d if worst is None else jnp.maximum(worst, d)
        bad_ref[j] = jnp.maximum(bad_ref[j], worst.astype(F32))
        for s in range(n):
            acc_ref[j, s] = acc_ref[j, s] + _mm(values(s, blk0, width), ps[s])

    @pl.when(qi0 >= 1)
    def _():
        ms0 = [ms_ref[0, s] for s in range(n)]
        consume(0, ms0, scores(qs_all[0], qi0 - 1, blk), qi0 - 1, blk, True)

    def query_block(j, carry):
        qi = qi0 + j
        qs = queries(pl.ds(pl.multiple_of(j * blk, blk), blk))
        ms = [ms_ref[j, s] for s in range(n)]
        n_plain = jnp.maximum(qi - 1, 0)
        done = 0
        for w in STEP_BLOCKS:
            count = (n_plain - done) // w

            def fast_step(i, c, w=w, done=done):
                consume(j, ms, scores(qs, done + i * w, w * blk), done + i * w, w * blk, False)
                return c

            lax.fori_loop(0, count, fast_step, 0)
            done = done + count * w

        @pl.when(jnp.max(bad_ref[j]) > 2.0 ** HEADROOM)
        def _():
            def exact_step(blk0, ms_run, bias):
                out, ps, alphas = [], [], []
                for s, st in enumerate(scores(qs, blk0, blk)):
                    if bias is not None:
                        st = st + bias(s)
                    m_new = jnp.maximum(ms_run[s], jnp.max(st, axis=0, keepdims=True))
                    ps.append(jnp.exp2(st - m_new).astype(BF16))
                    alphas.append(jnp.exp2(ms_run[s] - m_new))
                    out.append(m_new)
                for s in range(n):
                    acc_ref[j, s] = alphas[s] * acc_ref[j, s] + _mm(values(s, blk0, blk), ps[s])
                return tuple(out)

            acc_ref[j] = jnp.zeros(acc_ref.shape[1:], F32)
            c = exact_step(qi, tuple(jnp.full((1, blk), NEG, F32) for _ in range(n)), diag_bias)
            c = lax.cond(qi >= 1, lambda c: exact_step(qi - 1, c, prev_bias), lambda c: c, c)
            lax.fori_loop(0, n_plain, lambda i, c: exact_step(i, c, None), c)

        return carry

    lax.fori_loop(0, sub, query_block, 0)

    for j in range(sub):
        rows = slice(j * blk, (j + 1) * blk)
        for hd in range(nh):
            a0, a1 = acc_ref[j, 2 * hd], acc_ref[j, 2 * hd + 1]
            o_t = a0[0:LANES, :] / a0[LANES:LANES + 1, :] - lam * (a1[0:LANES, :] / a1[LANES:LANES + 1, :])
            sl = slice(hd * LANES, (hd + 1) * LANES)
            o_ref[0, rows, sl] = _diff_finish(o_t.T, subg_ref[...], z_ref[0, rows, sl].astype(F32))

    _diff_sample_kernel(*sample_in, lq1_ref, lk1_ref, lq2_ref, lk2_ref, subg_ref, so_ref, nq=nq, past=past)


def _diff_attention(dq, dk, dvt, bias, dz, lq1, lk1, lq2, lk2, subg, sample_args, *, blk, nh, sub, nq):
    b, t, _ = dq.shape
    step = sub * blk
    per_seq = t // step
    nb, _, past = sample_args[1].shape
    assert nh == DIFF_HEADS and nb == b * per_seq
    small = lambda n: pl.BlockSpec((1, n), lambda b, h, i: (0, 0))
    row = lambda b, h, i: b * per_seq + i
    new = lambda: pl.BlockSpec((nq, WIDTH), lambda b, h, i: (row(b, h, i), 0))
    cache = lambda r, c: pl.BlockSpec((1, r, c), lambda b, h, i: (row(b, h, i), 0, 0))
    return pl.pallas_call(
        functools.partial(_diff_kernel, blk=blk, nh=nh, sub=sub, nq=nq, past=past),
        grid=(b, DIFF_HEADS // nh, per_seq),
        in_specs=[pl.BlockSpec((1, step, nh * LANES), lambda b, h, i: (b, i, h)),
                  pl.BlockSpec((1, t, nh * LANES), lambda b, h, i: (b, 0, h)),
                  pl.BlockSpec((1, nh * VT_ROWS, t), lambda b, h, i: (b, h, 0)),
                  pl.BlockSpec((nh, 2 * blk, blk), lambda b, h, i: (h, 0, 0)),
                  pl.BlockSpec((1, step, nh * LANES), lambda b, h, i: (b, i, h)),
                  small(HD), small(HD), small(HD), small(HD), small(2 * HD),
                  new(), cache(WIDTH, past), cache(past * DIFF_HEADS, 2 * HD), new(), new(),
                  pl.BlockSpec((DIFF_HEADS, nq, past + LANES), lambda b, h, i: (0, 0, 0)), new()],
        out_specs=[pl.BlockSpec((1, step, nh * LANES), lambda b, h, i: (b, i, h)), new()],
        out_shape=[jax.ShapeDtypeStruct((b, t, WIDTH), BF16), jax.ShapeDtypeStruct((nb * nq, WIDTH), BF16)],
        scratch_shapes=[pltpu.VMEM((sub, 2 * nh, VT_ROWS, blk), F32), pltpu.VMEM((sub, 2 * nh, 1, blk), F32),
                        pltpu.VMEM((sub, 1, blk), F32)],
        compiler_params=pltpu.CompilerParams(
            dimension_semantics=("parallel", "parallel", "arbitrary"), vmem_limit_bytes=VMEM_LIMIT),
        name="diff_prompt",
    )(dq, dk, dvt, bias, dz, lq1, lk1, lq2, lk2, subg, *sample_args)


def _suffix_sums(x):
    n = x.shape[1] // LANES
    lane = lax.broadcasted_iota(jnp.int32, (x.shape[0], LANES), 1)
    after = jnp.zeros((x.shape[0], 1), F32)
    pieces = [None] * n
    for blk in reversed(range(n)):
        piece = x[:, blk * LANES:(blk + 1) * LANES]
        y = piece
        shift = 1
        while shift < LANES:
            y = y + jnp.where(lane + shift < LANES, pltpu.roll(y, LANES - shift, 1), 0.0)
            shift *= 2
        pieces[blk] = y - piece + after
        after = after + y[:, 0:1]
    return jnp.concatenate(pieces, axis=1), after


def _pad_rows(x, rows):
    return jnp.concatenate([x, jnp.zeros((rows - x.shape[0], x.shape[1]), x.dtype)], axis=0)


def _joint_softmax(s_p, s_n):
    m = jnp.maximum(jnp.max(s_p, axis=-1, keepdims=True), jnp.max(s_n, axis=-1, keepdims=True))
    p_p = jnp.exp(s_p - m)
    p_n = jnp.exp(s_n - m)
    l = jnp.sum(p_p, axis=-1, keepdims=True) + jnp.sum(p_n, axis=-1, keepdims=True)
    return p_p.astype(BF16), p_n.astype(BF16), l


def _joint_pv(p_p, p_n, l, vt_p, v_n):
    return (_mm_nt(p_p, vt_p) + _mm(p_n, v_n)) / l


def _fox_sample_kernel(q_ref, kt_ref, vt_ref, kn_ref, vn_ref, lfp_ref, lfn_ref, z_ref, o_ref, *, nq):
    r_new, total_new = _suffix_sums(lfn_ref[0])
    r_past, _ = _suffix_sums(lfp_ref[0])
    r_past = r_past + total_new
    lane = lax.broadcasted_iota(jnp.int32, (nq, LANES), 1)
    causal = lane <= lax.broadcasted_iota(jnp.int32, (nq, LANES), 0)
    sls = [slice(blk * LANES, (blk + 1) * LANES) for blk in range(PAIRS)]
    scores = []
    for blk, sl in enumerate(sls):
        q = q_ref[:, sl]
        kt = kt_ref[0, sl, :].astype(BF16)
        kn = _pad_rows(kn_ref[:, sl], LANES).astype(BF16)
        for par in range(2):
            hd = 2 * blk + par
            qm = jnp.where((lane < HD) if par == 0 else (lane >= HD), q, jnp.zeros_like(q))
            scores.append((_mm(qm, kt) + r_past[hd:hd + 1, :],
                           jnp.where(causal, _mm_nt(qm, kn) + r_new[hd:hd + 1, :], NEG)))
    weights = [_joint_softmax(s_p, s_n) for s_p, s_n in scores]
    for blk, sl in enumerate(sls):
        vt = vt_ref[0, sl, :].astype(BF16)
        vn = _pad_rows(vn_ref[:, sl], LANES).astype(BF16)
        outs = [_joint_pv(*weights[2 * blk + par], vt, vn) for par in range(2)]
        o = jnp.where(lane < HD, outs[0], outs[1])
        o_ref[:, sl] = (o * _silu(z_ref[:, sl].astype(F32))).astype(BF16)


def _diff_sample_kernel(q_ref, kt_ref, v_ref, kn_ref, vn_ref, bias_ref, z_ref,
                        lq1_ref, lk1_ref, lq2_ref, lk2_ref, subg_ref, o_ref, *, nq, past):
    lane = lax.broadcasted_iota(jnp.int32, (nq, LANES), 1)
    lam = _lambda(lq1_ref, lk1_ref, lq2_ref, lk2_ref)
    sls = [slice(hd * LANES, (hd + 1) * LANES) for hd in range(DIFF_HEADS)]
    scores = []
    for hd, sl in enumerate(sls):
        q = q_ref[:, sl]
        kt = kt_ref[0, sl, :].astype(BF16)
        kn = _pad_rows(kn_ref[:, sl], LANES).astype(BF16)
        for mp in range(2):
            qm = jnp.where((lane < HD) if mp == 0 else (lane >= HD), q, jnp.zeros_like(q))
            scores.append((_mm(qm, kt) + bias_ref[hd, :, 0:past],
                           _mm_nt(qm, kn) + bias_ref[hd, :, past:past + LANES]))
    weights = [_joint_softmax(s_p, s_n) for s_p, s_n in scores]
    for hd, sl in enumerate(sls):
        vt = v_ref[0, pl.ds(hd, past, stride=DIFF_HEADS), :].T.astype(BF16)
        vn = _pad_rows(vn_ref[:, sl], LANES).astype(BF16)
        outs = [_joint_pv(*weights[2 * hd + mp], vt, vn) for mp in range(2)]
        o_ref[:, sl] = _diff_finish(outs[0] - lam * outs[1], subg_ref[...], z_ref[:, sl].astype(F32))


def _out_kernel(yf_ref, yd_ref, ga_ref, gb_ref, x_ref, wuf_ref, wud_ref, wo_ref, g_ref, o_ref):
    merged = (ga_ref[...].astype(F32) * _mm(yf_ref[...], wuf_ref[...])
              + gb_ref[...].astype(F32) * _mm(yd_ref[...], wud_ref[...]))
    out = x_ref[...] + _mm(merged.astype(BF16), wo_ref[...])
    ms = jnp.mean(out * out, axis=-1, keepdims=True)
    o_ref[...] = out * lax.rsqrt(ms + NORM_EPS) * g_ref[...]


def _output(yf, yd, ga, gb, x2d, wuf, wud, wo, g, *, tm):
    rows = x2d.shape[0]
    assert rows % tm == 0
    row_spec = lambda n: pl.BlockSpec((tm, n), lambda i: (i, 0))
    const = lambda shape: pl.BlockSpec(shape, lambda i: (0, 0))
    return pl.pallas_call(
        _out_kernel,
        grid=(rows // tm,),
        in_specs=[row_spec(WIDTH), row_spec(WIDTH), row_spec(D_MODEL), row_spec(D_MODEL), row_spec(D_MODEL),
                  const((WIDTH, D_MODEL)), const((WIDTH, D_MODEL)), const((D_MODEL, D_MODEL)),
                  const((1, D_MODEL))],
        out_specs=row_spec(D_MODEL),
        out_shape=jax.ShapeDtypeStruct((rows, D_MODEL), F32),
        compiler_params=pltpu.CompilerParams(
            dimension_semantics=("parallel",), vmem_limit_bytes=VMEM_LIMIT),
        name="out_proj",
    )(yf, yd, ga, gb, x2d, wuf, wud, wo, g)


def _forget_lane_heads():
    lanes = np.arange(LANES)
    heads = (lanes % HD) // FL_SLOT
    heads[:FOX_HEADS] = np.arange(FOX_HEADS)
    return heads


def _pack_w_in(w_in, b_forget):
    sizes = (WIDTH, WIDTH, WIDTH, WIDTH, FOX_HEADS, WIDTH, WIDTH, WIDTH, WIDTH, D_MODEL, D_MODEL)
    offs = np.cumsum((0,) + sizes)
    fq, fk, fv, fz, fl, dq, dk, dv, dz, ga, gb = (w_in[:, offs[i]:offs[i + 1]] for i in range(len(sizes)))
    heads = _forget_lane_heads()
    w_all = jnp.concatenate([fq, fk, fv, fz, dq, dk, dv, dz, ga, gb, fl[:, heads]], axis=1).astype(BF16)
    return w_all, b_forget[heads][None, :]


def kernel(x_prompt, x_sample, cache_fox_k, cache_fox_v, cache_fox_logf, cache_diff_k, cache_diff_v,
           norm_in_g, w_in, b_forget, lambda_q1, lambda_k1, lambda_q2, lambda_k2, subln_g,
           w_up_fox, w_up_diff, w_o, rel_bias, final_norm_g):
    bp, tp, _ = x_prompt.shape
    bs, ts, _ = x_sample.shape
    past = cache_fox_k.shape[2]
    blk, tm = ATTN_BLOCK, ROW_TILE
    assert tp % (QUERY_BLOCKS_PER_STEP * blk) == 0 and blk > _T5_THR[-1] and blk % CHUNK == 0

    w_all, bfl = _pack_w_in(w_in[0], b_forget[0])
    g_in = norm_in_g[0][None, :]
    g_out = final_norm_g[None, :]
    wuf, wud, wo = w_up_fox[0].astype(BF16), w_up_diff[0].astype(BF16), w_o[0].astype(BF16)
    lq1, lk1, lq2, lk2 = (a[0][None, :] for a in (lambda_q1, lambda_k1, lambda_q2, lambda_k2))
    subg = subln_g[0][None, :]

    xs = x_sample.reshape(bs * ts, D_MODEL)
    (fk_s, fv_s, lf_s, dk_s, dv_s, fq_s, fz_s, dq_s, dz_s, ga_s, gb_s) = _project(
        xs, g_in, w_all, bfl, seq_len=ts, aug=False, tm=tm)
    xp = x_prompt.reshape(bp * tp, D_MODEL)
    (fkt, fvt, lft, dkt, dv4, qa, ka, vtf, fz, dq, dkb, vtd, dz, ga, gb) = _project(
        xp, g_in, w_all, bfl, seq_len=tp, aug=True, tm=tm)
    fox_k_p = jnp.transpose(fkt.reshape(bp, FOX_HEADS, HD, tp), (0, 3, 1, 2))[None]
    fox_v_p = jnp.transpose(fvt.reshape(bp, FOX_HEADS, HD, tp), (0, 3, 1, 2))[None]
    fox_lf_p = jnp.transpose(lft, (0, 2, 1))[None]
    diff_k_p = jnp.transpose(dkt.reshape(bp, DIFF_HEADS, 2, HD, tp), (0, 4, 1, 2, 3))[None]
    diff_v_p = dv4.reshape(1, bp, tp, DIFF_HEADS, 2 * HD)
    fox_kt = jnp.transpose(cache_fox_k[0], (0, 2, 3, 1)).reshape(bs, WIDTH, past)
    fox_vt = jnp.transpose(cache_fox_v[0], (0, 2, 3, 1)).reshape(bs, WIDTH, past)
    diff_kt = jnp.transpose(cache_diff_k[0], (0, 2, 3, 4, 1)).reshape(bs, WIDTH, past)
    diff_v = cache_diff_v[0].reshape(bs, past * DIFF_HEADS, 2 * HD)
    lf_past_t = jnp.swapaxes(cache_fox_logf[0], 1, 2)
    lf_new_t = jnp.pad(jnp.swapaxes(lf_s.reshape(bs, ts, FOX_HEADS), 1, 2), ((0, 0), (0, 0), (0, LANES - ts)))

    r3 = lambda a: a.reshape(bp, tp, a.shape[-1])
    yf, yf_s = _fox_attention(r3(qa), r3(ka), vtf, r3(fz),
                              (fq_s, fox_kt, fox_vt, fk_s, fv_s, lf_past_t, lf_new_t, fz_s),
                              blk=blk, nh=FOX_STEP_HEADS, sub=QUERY_BLOCKS_PER_STEP, nq=ts)
    bias_p = _bias_tiles(rel_bias, q_start=blk, nq=blk, segments=((0, blk, blk), (blk, blk, blk)),
                         transposed=True, scale=LOG2E)
    bias_s = _bias_tiles(rel_bias, q_start=past, nq=ts, segments=((0, past, past), (past, LANES, ts)),
                         transposed=False, scale=1.0)
    yd, yd_s = _diff_attention(r3(dq), r3(dkb), vtd, bias_p, r3(dz), lq1, lk1, lq2, lk2, subg,
                               (dq_s, diff_kt, diff_v, dk_s, dv_s, bias_s, dz_s),
                               blk=blk, nh=DIFF_STEP_HEADS, sub=QUERY_BLOCKS_PER_STEP, nq=ts)

    y_p = _output(yf.reshape(bp * tp, WIDTH), yd.reshape(bp * tp, WIDTH), ga, gb, xp, wuf, wud, wo, g_out, tm=tm)
    y_s = _output(yf_s, yd_s, ga_s, gb_s, xs, wuf, wud, wo, g_out, tm=tm)

    return (y_p.reshape(bp, tp, D_MODEL), y_s.reshape(bs, ts, D_MODEL),
            fox_k_p, fox_v_p, fox_lf_p, diff_k_p, diff_v_p,
            fk_s.reshape(1, bs, ts, FOX_HEADS, HD), fv_s.reshape(1, bs, ts, FOX_HEADS, HD),
            lf_s.reshape(1, bs, ts, FOX_HEADS),
            dk_s.reshape(1, bs, ts, DIFF_HEADS, 2, HD), dv_s.reshape(1, bs, ts, DIFF_HEADS, 2 * HD))
```

```python
import functools
import math

import numpy as np
import jax
import jax.numpy as jnp
from jax import lax
from jax.experimental import pallas as pl
from jax.experimental.pallas import tpu as pltpu

F32 = jnp.float32
BF16 = jnp.bfloat16

D_MODEL = 1024
HD = 64
FOX_HEADS = 8
DIFF_HEADS = 4
WIDTH = 512
CHUNK = 64
CHUNK_SHIFT = CHUNK.bit_length() - 1
N_BUCKETS = 32
MAX_DISTANCE = 128
NORM_EPS = 1e-6
SCALE = HD ** -0.5
NEG = -1e30
LAMBDA_INIT = 0.8 - 0.6 * math.exp(-0.3 * 0)
LOG2E = math.log2(math.e)

LANES = 128
PAIRS = WIDTH // LANES
VT_PAD = 16
VT_ROWS = LANES + VT_PAD
FVT_ROWS = HD + VT_PAD

_O_FQ, _O_FK, _O_FV, _O_FZ = 0, 512, 1024, 1536
_O_DQ, _O_DK, _O_DV, _O_DZ = 2048, 2560, 3072, 3584
_O_GA, _O_GB = 4096, 5120
_O_FL = 6144
_N_ALL = 6272
FL_SLOT = 8

VMEM_LIMIT = 58 * 1024 * 1024
ATTN_BLOCK = 256
ROW_TILE = 512

FOX_STEP_HEADS = 8
DIFF_STEP_HEADS = 4
HEADROOM = 64.0
FIRST_STEP_GROUP = 2
QUERY_BLOCKS_PER_STEP = 4
STEP_BLOCKS = (4, 2, 1)


def _t5_thresholds():
    nb = N_BUCKETS // 2
    max_exact = nb // 2
    n = np.arange(0, 4 * MAX_DISTANCE)
    large = max_exact + (np.log(np.maximum(n, 1).astype(np.float32) / max_exact)
                         / math.log(MAX_DISTANCE / max_exact) * (nb - max_exact)).astype(np.int32)
    large = np.minimum(large, nb - 1)
    thr = [int(np.argmax(large >= b)) for b in range(max_exact + 1, nb)]
    return max_exact, tuple(thr)


_MAX_EXACT, _T5_THR = _t5_thresholds()


def _mm(a, b):
    return jnp.dot(a, b, preferred_element_type=F32)


def _mm_nt(a, b):
    return lax.dot_general(a, b, (((1,), (1,)), ((), ())), preferred_element_type=F32)


def _log_sigmoid(x):
    return -(jnp.maximum(-x, 0.0) + jnp.log1p(jnp.exp(-jnp.abs(x))))


def _sigmoid(x):
    return 1.0 / (1.0 + jnp.exp(-x))


def _silu(x):
    return x * _sigmoid(x)


def _split3(x):
    p1 = x.astype(BF16).astype(F32)
    r1 = x - p1
    p2 = r1.astype(BF16).astype(F32)
    p3 = r1 - p2
    return p1, p2, p3


def _proj_kernel(x_ref, g_ref, w_ref, bfl_ref, *refs, tm, seq_len, aug):
    if aug:
        (fkt_ref, fvt_ref, lft_ref, dkt_ref, dv4_ref, qa_ref, ka_ref, vtf_ref,
         fz_ref, dq_ref, dkb_ref, vtd_ref, dz_ref, ga_ref, gb_ref, carry_ref) = refs
    else:
        (fk_ref, fv_ref, lf_ref, dk_ref, dv_ref, fq_ref,
         fz_ref, dq_ref, dz_ref, ga_ref, gb_ref) = refs

    x = x_ref[...]
    ms = jnp.mean(x * x, axis=-1, keepdims=True)
    h = (x * lax.rsqrt(ms + NORM_EPS) * g_ref[...]).astype(BF16)

    def grp(off, n=WIDTH):
        return _mm(h, w_ref[:, off:off + n])

    if not aug:
        fq_ref[...] = (grp(_O_FQ) * SCALE).astype(BF16)
        fk_ref[...] = grp(_O_FK)
        fv_ref[...] = grp(_O_FV)
        fz_ref[...] = grp(_O_FZ).astype(BF16)
        dq_ref[...] = (grp(_O_DQ) * SCALE).astype(BF16)
        dk_ref[...] = grp(_O_DK)
        dv_ref[...] = grp(_O_DV)
        dz_ref[...] = grp(_O_DZ).astype(BF16)
        ga_ref[...] = _sigmoid(grp(_O_GA, D_MODEL)).astype(BF16)
        gb_ref[...] = _sigmoid(grp(_O_GB, D_MODEL)).astype(BF16)
        lf_ref[...] = _log_sigmoid(grp(_O_FL, LANES) + bfl_ref[...])[:, :FOX_HEADS]
        return

    lf = _log_sigmoid(grp(_O_FL, LANES) + bfl_ref[...])
    lft_ref[0] = lf.T[:FOX_HEADS, :]
    row = lax.broadcasted_iota(jnp.int32, (tm, LANES), 0)
    c = lf
    shift = 1
    while shift < tm:
        c = c + jnp.where(row >= shift, pltpu.roll(c, shift, 0), 0.0)
        shift *= 2
    tiles_per_seq = seq_len // tm

    @pl.when(pl.program_id(0) % tiles_per_seq == 0)
    def _():
        carry_ref[...] = jnp.zeros_like(carry_ref)

    c = c + carry_ref[...]
    carry_ref[...] = c[tm - 1:tm, :]
    c_hi, c_mid, c_lo = _split3(c * LOG2E)

    fq = grp(_O_FQ) * (SCALE * LOG2E)
    fk = grp(_O_FK)
    fv = grp(_O_FV)
    fvt = fv.T
    fkt_ref[0] = fk.T
    fvt_ref[0] = fvt
    tail = jnp.where(lax.broadcasted_iota(jnp.int32, (VT_PAD, tm), 0) == 0, 1.0, 0.0).astype(BF16)
    lane = lax.broadcasted_iota(jnp.int32, (tm, LANES), 1)
    pos = lane & (FL_SLOT - 1)
    q_slots = jnp.where(pos == 0, c_hi, jnp.where(pos == 1, c_mid, jnp.where(
        pos == 2, c_lo, jnp.where(pos < 6, 1.0, 0.0))))
    k_slots = jnp.where(pos < 3, 1.0, jnp.where(pos == 3, -c_hi, jnp.where(
        pos == 4, -c_mid, jnp.where(pos == 5, -c_lo, 0.0))))
    for hd in range(FOX_HEADS):
        blk, par = divmod(hd, 2)
        sl = slice(blk * LANES, (blk + 1) * LANES)
        data = (lane < HD) if par == 0 else (lane >= HD)
        slot0 = (1 - par) * HD + hd * FL_SLOT
        own = (lane >= slot0) & (lane < slot0 + FL_SLOT)
        q_aug = jnp.where(own, q_slots, 0.0)
        k_aug = jnp.where(own, k_slots, 0.0)
        osl = slice(hd * LANES, (hd + 1) * LANES)
        qa_ref[:, osl] = jnp.where(data, fq[:, sl], q_aug).astype(BF16)
        ka_ref[:, osl] = jnp.where(data, fk[:, sl], k_aug).astype(BF16)
        base = hd * FVT_ROWS
        vtf_ref[0, base:base + HD, :] = fvt[hd * HD:(hd + 1) * HD, :].astype(BF16)
        vtf_ref[0, base + HD:base + FVT_ROWS, :] = tail

    ga_ref[...] = _sigmoid(grp(_O_GA, D_MODEL)).astype(BF16)
    gb_ref[...] = _sigmoid(grp(_O_GB, D_MODEL)).astype(BF16)

    dk = grp(_O_DK)
    dkt_ref[0] = dk.T
    dkb_ref[...] = dk.astype(BF16)
    dv = grp(_O_DV)
    dv4_ref[...] = dv.reshape(tm, DIFF_HEADS, 2 * HD)
    for hd in range(DIFF_HEADS):
        base = hd * VT_ROWS
        vtd_ref[0, base:base + LANES, :] = dv[:, hd * LANES:(hd + 1) * LANES].T.astype(BF16)
        vtd_ref[0, base + LANES:base + VT_ROWS, :] = tail

    dq_ref[...] = (grp(_O_DQ) * (SCALE * LOG2E)).astype(BF16)
    fz_ref[...] = grp(_O_FZ).astype(BF16)
    dz_ref[...] = grp(_O_DZ).astype(BF16)


def _project(x2d, g, w_all, bfl, *, seq_len, aug, tm):
    rows = x2d.shape[0]
    assert rows % tm == 0
    row_spec = lambda n: pl.BlockSpec((tm, n), lambda i: (i, 0))
    const = lambda shape: pl.BlockSpec(shape, lambda i: (0, 0))
    f32o = lambda n: jax.ShapeDtypeStruct((rows, n), F32)
    b16o = lambda n: jax.ShapeDtypeStruct((rows, n), BF16)
    if aug:
        assert seq_len % tm == 0
        wide = FOX_HEADS * LANES
        tps = seq_len // tm
        t_shape = lambda n, dt: jax.ShapeDtypeStruct((rows // seq_len, n, seq_len), dt)
        t_spec = lambda n: pl.BlockSpec((1, n, tm), lambda i: (i // tps, 0, i % tps))
        out_shape = [t_shape(WIDTH, F32), t_shape(WIDTH, F32), t_shape(FOX_HEADS, F32), t_shape(WIDTH, F32),
                     jax.ShapeDtypeStruct((rows, DIFF_HEADS, 2 * HD), F32)]
        out_specs = [t_spec(WIDTH), t_spec(WIDTH), t_spec(FOX_HEADS), t_spec(WIDTH),
                     pl.BlockSpec((tm, DIFF_HEADS, 2 * HD), lambda i: (i, 0, 0))]
        out_shape += ([b16o(wide)] * 2 + [t_shape(FOX_HEADS * FVT_ROWS, BF16)] + [b16o(WIDTH)] * 3
                      + [t_shape(DIFF_HEADS * VT_ROWS, BF16)] + [b16o(WIDTH)] + [b16o(D_MODEL)] * 2)
        out_specs += ([row_spec(wide)] * 2 + [t_spec(FOX_HEADS * FVT_ROWS)] + [row_spec(WIDTH)] * 3
                      + [t_spec(DIFF_HEADS * VT_ROWS)] + [row_spec(WIDTH)] + [row_spec(D_MODEL)] * 2)
        scratch = [pltpu.VMEM((1, LANES), F32)]
    else:
        out_shape = [f32o(WIDTH), f32o(WIDTH), f32o(FOX_HEADS), f32o(WIDTH), f32o(WIDTH)]
        out_specs = [row_spec(WIDTH), row_spec(WIDTH), row_spec(FOX_HEADS), row_spec(WIDTH), row_spec(WIDTH)]
        out_shape += [b16o(WIDTH)] * 4 + [b16o(D_MODEL)] * 2
        out_specs += [row_spec(WIDTH)] * 4 + [row_spec(D_MODEL)] * 2
        scratch = []
    return pl.pallas_call(
        functools.partial(_proj_kernel, tm=tm, seq_len=seq_len, aug=aug),
        grid=(rows // tm,),
        in_specs=[row_spec(D_MODEL), const((1, D_MODEL)),
                  pl.BlockSpec((D_MODEL, _N_ALL), lambda i: (0, 0), pipeline_mode=pl.Buffered(1)),
                  const((1, LANES))],
        out_specs=out_specs,
        out_shape=out_shape,
        scratch_shapes=scratch,
        compiler_params=pltpu.CompilerParams(
            dimension_semantics=("arbitrary",), vmem_limit_bytes=VMEM_LIMIT),
        name="proj_aug" if aug else "proj_plain",
    )(x2d, g, w_all, bfl)


def _bias_kernel(tbl_ref, o_ref, *, q_start, nq, segments, transposed, scale):
    hd = pl.program_id(0)
    far = tbl_ref[N_BUCKETS // 2 - 1, hd]
    q_axis, k_axis = (1, 0) if transposed else (0, 1)
    off = 0
    for k_start, nk, valid in segments:
        shape = (nk, nq) if transposed else (nq, nk)
        qpos = q_start + lax.broadcasted_iota(jnp.int32, shape, q_axis)
        kidx = lax.broadcasted_iota(jnp.int32, shape, k_axis)
        kpos = k_start + kidx
        rel = kpos - qpos
        n = jnp.abs(rel)
        large = jnp.full(shape, _MAX_EXACT, jnp.int32)
        for thr in _T5_THR:
            large = large + jnp.where(n >= thr, 1, 0)
        bucket = jnp.where(rel > 0, N_BUCKETS // 2, 0) + jnp.where(n < _MAX_EXACT, n, large)
        val = jnp.zeros(shape, F32)
        for b in range(N_BUCKETS):
            val = jnp.where(bucket == b, tbl_ref[b, hd], val)
        val = (val - far) * scale
        visible = ((kpos >> CHUNK_SHIFT) <= (qpos >> CHUNK_SHIFT)) & (kidx < valid)
        val = jnp.where(visible, val, NEG)
        if transposed:
            o_ref[0, off:off + nk, :] = val
        else:
            o_ref[0, :, off:off + nk] = val
        off += nk


def _bias_tiles(rel_bias, *, q_start, nq, segments, transposed, scale):
    total = sum(s[1] for s in segments)
    shape = (total, nq) if transposed else (nq, total)
    return pl.pallas_call(
        functools.partial(_bias_kernel, q_start=q_start, nq=nq, segments=segments, transposed=transposed,
                          scale=scale),
        grid=(DIFF_HEADS,),
        in_specs=[pl.BlockSpec(memory_space=pltpu.SMEM)],
        out_specs=pl.BlockSpec((1,) + shape, lambda h: (h, 0, 0)),
        out_shape=jax.ShapeDtypeStruct((DIFF_HEADS,) + shape, F32),
        name="t5_bias",
    )(rel_bias)


def _flash_t(k_ref, vt_ref, acc_ref, bad_ref, streams, qi, blk, prev_bias, diag_bias, diag_mask):
    n = len(streams)

    def scores(first_blk, width):
        start = pl.multiple_of(first_blk * blk, blk)
        return [_mm_nt(k_ref[0, pl.ds(start, width), ksl], q) for q, ksl, _ in streams]

    def step(first_blk, width, ms, biases, mask):
        start = pl.multiple_of(first_blk * blk, blk)
        sts = scores(first_blk, width)
        out, ps, alphas = [], [], []
        for s, st in enumerate(sts):
            if biases is not None:
                st = st + biases[s]()
            if mask is not None:
                st = jnp.where(mask, st, NEG)
            m_new = jnp.maximum(ms[s], jnp.max(st, axis=0, keepdims=True))
            ps.append(jnp.exp2(st - m_new).astype(BF16))
            alphas.append(jnp.exp2(ms[s] - m_new))
            out.append(m_new)
        for s, (_, _, vsl) in enumerate(streams):
            acc_ref[s] = alphas[s] * acc_ref[s] + _mm(vt_ref[0, vsl, pl.ds(start, width)], ps[s])
        return tuple(out)

    def first_step():
        start = pl.multiple_of(qi * blk, blk)
        out, ps = [], []
        for s, st in enumerate(scores(qi, blk)):
            if diag_bias is not None:
                st = st + diag_bias[s]()
            if diag_mask is not None:
                st = jnp.where(diag_mask, st, NEG)
            m = jnp.max(st, axis=0, keepdims=True)
            ps.append(jnp.exp2(st - m).astype(BF16))
            out.append(m)
        for s, (_, _, vsl) in enumerate(streams):
            acc_ref[s] = _mm(vt_ref[0, vsl, pl.ds(start, blk)], ps[s])
        return tuple(out)

    ms = first_step()
    bad_ref[...] = jnp.zeros_like(bad_ref)

    def consume(sts, first_blk, width):
        start = pl.multiple_of(first_blk * blk, blk)
        ps = [jnp.exp2(st - ms[s]).astype(BF16) for s, st in enumerate(sts)]
        worst = None
        for p in ps:
            d = jnp.max(p, axis=0, keepdims=True)
            worst = d if worst is None else jnp.maximum(worst, d)
        bad_ref[...] = jnp.maximum(bad_ref[...], worst.astype(F32))
        for s, (_, _, vsl) in enumerate(streams):
            acc_ref[s] = acc_ref[s] + _mm(vt_ref[0, vsl, pl.ds(start, width)], ps[s])

    if prev_bias is not None:
        @pl.when(qi >= 1)
        def _():
            consume([st + prev_bias[s]() for s, st in enumerate(scores(qi - 1, blk))], qi - 1, blk)

    n_plain = qi if prev_bias is None else jnp.maximum(qi - 1, 0)
    done = 0
    for w in STEP_BLOCKS:
        count = (n_plain - done) // w

        def fast_step(j, carry, w=w, done=done):
            consume(scores(done + j * w, w * blk), done + j * w, w * blk)
            return carry

        lax.fori_loop(0, count, fast_step, 0)
        done = done + count * w

    @pl.when(jnp.max(bad_ref[...]) > 2.0 ** HEADROOM)
    def _():
        c = first_step()
        if prev_bias is not None:
            c = lax.cond(qi >= 1, lambda c: step(qi - 1, blk, c, prev_bias, None), lambda c: c, c)
        lax.fori_loop(0, n_plain, lambda j, c: step(j, blk, c, None, None), c)


def _fox_kernel(qa_ref, ka_ref, vt_ref, z_ref, *rest, blk, nh, sub, nq):
    sample_in, (o_ref, so_ref, acc_ref, bad_ref) = rest[:8], rest[8:]
    key = lax.broadcasted_iota(jnp.int32, (blk, blk), 0)
    qry = lax.broadcasted_iota(jnp.int32, (blk, blk), 1)
    hsl = [slice(hd * LANES, (hd + 1) * LANES) for hd in range(nh)]

    def query_block(i, carry):
        qi = pl.program_id(2) * sub + i
        rows = pl.ds(pl.multiple_of(i * blk, blk), blk)
        streams = [(qa_ref[0, rows, sl], sl, slice(hd * FVT_ROWS, (hd + 1) * FVT_ROWS))
                   for hd, sl in enumerate(hsl)]
        _flash_t(ka_ref, vt_ref, acc_ref, bad_ref, streams, qi, blk, None, None, key <= qry)
        for pair in range(nh // 2):
            halves = []
            for par in range(2):
                acc = acc_ref[2 * pair + par]
                halves.append(acc[0:HD, :] / acc[HD:HD + 1, :])
            o = jnp.concatenate(halves, axis=0).T
            o_ref[0, rows, hsl[pair]] = (o * _silu(z_ref[0, rows, hsl[pair]].astype(F32))).astype(BF16)
        return carry

    lax.fori_loop(0, sub, query_block, 0)
    _fox_sample_kernel(*sample_in, so_ref, nq=nq)


def _fox_attention(qa, ka, vt, fz, sample_args, *, blk, nh, sub, nq):
    b, t, _ = qa.shape
    step = sub * blk
    per_seq = t // step
    nb, _, past = sample_args[1].shape
    assert nh == FOX_HEADS and nb == b * per_seq
    row = lambda b, p, i: b * per_seq + i
    new = lambda: pl.BlockSpec((nq, WIDTH), lambda b, p, i: (row(b, p, i), 0))
    cache = lambda r, c: pl.BlockSpec((1, r, c), lambda b, p, i: (row(b, p, i), 0, 0))
    return pl.pallas_call(
        functools.partial(_fox_kernel, blk=blk, nh=nh, sub=sub, nq=nq),
        grid=(b, FOX_HEADS // nh, per_seq),
        in_specs=[pl.BlockSpec((1, step, nh * LANES), lambda b, p, i: (b, i, p)),
                  pl.BlockSpec((1, t, nh * LANES), lambda b, p, i: (b, 0, p)),
                  pl.BlockSpec((1, nh * FVT_ROWS, t), lambda b, p, i: (b, p, 0)),
                  pl.BlockSpec((1, step, nh * HD), lambda b, p, i: (b, i, p)),
                  new(), cache(WIDTH, past), cache(WIDTH, past), new(), new(),
                  cache(FOX_HEADS, past), cache(FOX_HEADS, LANES), new()],
        out_specs=[pl.BlockSpec((1, step, nh * HD), lambda b, p, i: (b, i, p)), new()],
        out_shape=[jax.ShapeDtypeStruct((b, t, WIDTH), BF16), jax.ShapeDtypeStruct((nb * nq, WIDTH), BF16)],
        scratch_shapes=[pltpu.VMEM((nh, FVT_ROWS, blk), F32), pltpu.VMEM((1, blk), F32)],
        compiler_params=pltpu.CompilerParams(
            dimension_semantics=("parallel", "parallel", "arbitrary"), vmem_limit_bytes=VMEM_LIMIT),
        name="fox_prompt",
    )(qa, ka, vt, fz, *sample_args)


def _lambda(lq1_ref, lk1_ref, lq2_ref, lk2_ref):
    s1 = jnp.sum(lq1_ref[...] * lk1_ref[...], axis=-1, keepdims=True)
    s2 = jnp.sum(lq2_ref[...] * lk2_ref[...], axis=-1, keepdims=True)
    return jnp.exp(s1) - jnp.exp(s2) + LAMBDA_INIT


def _diff_finish(o, subg, z):
    ms = jnp.mean(o * o, axis=-1, keepdims=True)
    od = o * lax.rsqrt(ms + NORM_EPS) * subg * (1.0 - LAMBDA_INIT)
    return (od * _silu(z)).astype(BF16)


def _diff_kernel(q_ref, k_ref, vt_ref, bias_ref, z_ref, lq1_ref, lk1_ref, lq2_ref, lk2_ref, subg_ref,
                 *rest, blk, nh, sub, nq, past):
    sample_in, (o_ref, so_ref, acc_ref, ms_ref, bad_ref) = rest[:7], rest[7:]
    lane = lax.broadcasted_iota(jnp.int32, (blk, LANES), 1)
    lam = _lambda(lq1_ref, lk1_ref, lq2_ref, lk2_ref)
    n = 2 * nh
    qi0 = pl.program_id(2) * sub
    ksl = [slice((s // 2) * LANES, (s // 2 + 1) * LANES) for s in range(n)]
    vsl = [slice((s // 2) * VT_ROWS, (s // 2 + 1) * VT_ROWS) for s in range(n)]

    def queries(rows):
        out = []
        for s in range(n):
            q = q_ref[0, rows, ksl[s]]
            out.append(jnp.where((lane < HD) if s % 2 == 0 else (lane >= HD), q, jnp.zeros_like(q)))
        return out

    def scores(qs, blk0, width):
        start = pl.multiple_of(blk0 * blk, blk)
        return [_mm_nt(k_ref[0, pl.ds(start, width), ksl[s]], qs[s]) for s in range(n)]

    def values(s, blk0, width):
        return vt_ref[0, vsl[s], pl.ds(pl.multiple_of(blk0 * blk, blk), width)]

    def prev_bias(s):
        return bias_ref[s // 2, 0:blk, :]

    def diag_bias(s):
        return bias_ref[s // 2, blk:2 * blk, :]

    qs_all = [queries(slice(j * blk, (j + 1) * blk)) for j in range(sub)]
    for group in range(0, sub, FIRST_STEP_GROUP):
        js = range(group, min(group + FIRST_STEP_GROUP, sub))
        diag = {j: scores(qs_all[j], qi0 + j, blk) for j in js}
        ms_all, p_diag = {}, {}
        for j in js:
            ms_all[j], p_diag[j] = [], []
            for s in range(n):
                st = diag[j][s] + diag_bias(s)
                m = jnp.max(st, axis=0, keepdims=True)
                p_diag[j].append(jnp.exp2(st - m).astype(BF16))
                ms_all[j].append(m)
                ms_ref[j, s] = m
        prev = {j: scores(qs_all[j], qi0 + j - 1, blk) for j in js if j >= 1}
        p_prev = {j: [jnp.exp2(prev[j][s] + prev_bias(s) - ms_all[j][s]).astype(BF16) for s in range(n)]
                  for j in prev}
        for j in js:
            worst = None
            for s in range(n):
                acc = _mm(values(s, qi0 + j, blk), p_diag[j][s])
                if j >= 1:
                    acc = acc + _mm(values(s, qi0 + j - 1, blk), p_prev[j][s])
                    d = jnp.max(p_prev[j][s], axis=0, keepdims=True)
                    worst = d if worst is None else jnp.maximum(worst, d)
                acc_ref[j, s] = acc
            bad_ref[j] = jnp.zeros((1, blk), F32) if worst is None else worst.astype(F32)

    def consume(j, ms, sts, blk0, width, biased):
        ps = [jnp.exp2((sts[s] + prev_bias(s) if biased else sts[s]) - ms[s]).astype(BF16) for s in range(n)]
        worst = None
        for p in ps:
            d = jnp.max(p, axis=0, keepdims=True)
            worst = d if worst is None else jnp.maximum(worst, d)
        bad_ref[j] = jnp.maximum(bad_ref[j], worst.astype(F32))
        for s in range(n):
            acc_ref[j, s] = acc_ref[j, s] + _mm(values(s, blk0, width), ps[s])

    @pl.when(qi0 >= 1)
    def _():
        ms0 = [ms_ref[0, s] for s in range(n)]
        consume(0, ms0, scores(qs_all[0], qi0 - 1, blk), qi0 - 1, blk, True)

    def query_block(j, carry):
        qi = qi0 + j
        qs = queries(pl.ds(pl.multiple_of(j * blk, blk), blk))
        ms = [ms_ref[j, s] for s in range(n)]
        n_plain = jnp.maximum(qi - 1, 0)
        done = 0
        for w in STEP_BLOCKS:
            count = (n_plain - done) // w

            def fast_step(i, c, w=w, done=done):
                consume(j, ms, scores(qs, done + i * w, w * blk), done + i * w, w * blk, False)
                return c

            lax.fori_loop(0, count, fast_step, 0)
            done = done + count * w

        @pl.when(jnp.max(bad_ref[j]) > 2.0 ** HEADROOM)
        def _():
            def exact_step(blk0, ms_run, bias):
                out, ps, alphas = [], [], []
                for s, st in enumerate(scores(qs, blk0, blk)):
                    if bias is not None:
                        st = st + bias(s)
                    m_new = jnp.maximum(ms_run[s], jnp.max(st, axis=0, keepdims=True))
                    ps.append(jnp.exp2(st - m_new).astype(BF16))
                    alphas.append(jnp.exp2(ms_run[s] - m_new))
                    out.append(m_new)
                for s in range(n):
                    acc_ref[j, s] = alphas[s] * acc_ref[j, s] + _mm(values(s, blk0, blk), ps[s])
                return tuple(out)

            acc_ref[j] = jnp.zeros(acc_ref.shape[1:], F32)
            c = exact_step(qi, tuple(jnp.full((1, blk), NEG, F32) for _ in range(n)), diag_bias)
            c = lax.cond(qi >= 1, lambda c: exact_step(qi - 1, c, prev_bias), lambda c: c, c)
            lax.fori_loop(0, n_plain, lambda i, c: exact_step(i, c, None), c)

        return carry

    lax.fori_loop(0, sub, query_block, 0)

    for j in range(sub):
        rows = slice(j * blk, (j + 1) * blk)
        for hd in range(nh):
            a0, a1 = acc_ref[j, 2 * hd], acc_ref[j, 2 * hd + 1]
            o_t = a0[0:LANES, :] / a0[LANES:LANES + 1, :] - lam * (a1[0:LANES, :] / a1[LANES:LANES + 1, :])
            sl = slice(hd * LANES, (hd + 1) * LANES)
            o_ref[0, rows, sl] = _diff_finish(o_t.T, subg_ref[...], z_ref[0, rows, sl].astype(F32))

    _diff_sample_kernel(*sample_in, lq1_ref, lk1_ref, lq2_ref, lk2_ref, subg_ref, so_ref, nq=nq, past=past)


def _diff_attention(dq, dk, dvt, bias, dz, lq1, lk1, lq2, lk2, subg, sample_args, *, blk, nh, sub, nq):
    b, t, _ = dq.shape
    step = sub * blk
    per_seq = t // step
    nb, _, past = sample_args[1].shape
    assert nh == DIFF_HEADS and nb == b * per_seq
    small = lambda n: pl.BlockSpec((1, n), lambda b, h, i: (0, 0))
    row = lambda b, h, i: b * per_seq + i
    new = lambda: pl.BlockSpec((nq, WIDTH), lambda b, h, i: (row(b, h, i), 0))
    cache = lambda r, c: pl.BlockSpec((1, r, c), lambda b, h, i: (row(b, h, i), 0, 0))
    return pl.pallas_call(
        functools.partial(_diff_kernel, blk=blk, nh=nh, sub=sub, nq=nq, past=past),
        grid=(b, DIFF_HEADS // nh, per_seq),
        in_specs=[pl.BlockSpec((1, step, nh * LANES), lambda b, h, i: (b, i, h)),
                  pl.BlockSpec((1, t, nh * LANES), lambda b, h, i: (b, 0, h)),
                  pl.BlockSpec((1, nh * VT_ROWS, t), lambda b, h, i: (b, h, 0)),
                  pl.BlockSpec((nh, 2 * blk, blk), lambda b, h, i: (h, 0, 0)),
                  pl.BlockSpec((1, step, nh * LANES), lambda b, h, i: (b, i, h)),
                  small(HD), small(HD), small(HD), small(HD), small(2 * HD),
                  new(), cache(WIDTH, past), cache(past * DIFF_HEADS, 2 * HD), new(), new(),
                  pl.BlockSpec((DIFF_HEADS, nq, past + LANES), lambda b, h, i: (0, 0, 0)), new()],
        out_specs=[pl.BlockSpec((1, step, nh * LANES), lambda b, h, i: (b, i, h)), new()],
        out_shape=[jax.ShapeDtypeStruct((b, t, WIDTH), BF16), jax.ShapeDtypeStruct((nb * nq, WIDTH), BF16)],
        scratch_shapes=[pltpu.VMEM((sub, 2 * nh, VT_ROWS, blk), F32), pltpu.VMEM((sub, 2 * nh, 1, blk), F32),
                        pltpu.VMEM((sub, 1, blk), F32)],
        compiler_params=pltpu.CompilerParams(
            dimension_semantics=("parallel", "parallel", "arbitrary"), vmem_limit_bytes=VMEM_LIMIT),
        name="diff_prompt",
    )(dq, dk, dvt, bias, dz, lq1, lk1, lq2, lk2, subg, *sample_args)


def _suffix_sums(x):
    n = x.shape[1] // LANES
    lane = lax.broadcasted_iota(jnp.int32, (x.shape[0], LANES), 1)
    after = jnp.zeros((x.shape[0], 1), F32)
    pieces = [None] * n
    for blk in reversed(range(n)):
        piece = x[:, blk * LANES:(blk + 1) * LANES]
        y = piece
        shift = 1
        while shift < LANES:
            y = y + jnp.where(lane + shift < LANES, pltpu.roll(y, LANES - shift, 1), 0.0)
            shift *= 2
        pieces[blk] = y - piece + after
        after = after + y[:, 0:1]
    return jnp.concatenate(pieces, axis=1), after


def _pad_rows(x, rows):
    return jnp.concatenate([x, jnp.zeros((rows - x.shape[0], x.shape[1]), x.dtype)], axis=0)


def _joint_softmax(s_p, s_n):
    m = jnp.maximum(jnp.max(s_p, axis=-1, keepdims=True), jnp.max(s_n, axis=-1, keepdims=True))
    p_p = jnp.exp(s_p - m)
    p_n = jnp.exp(s_n - m)
    l = jnp.sum(p_p, axis=-1, keepdims=True) + jnp.sum(p_n, axis=-1, keepdims=True)
    return p_p.astype(BF16), p_n.astype(BF16), l


def _joint_pv(p_p, p_n, l, vt_p, v_n):
    return (_mm_nt(p_p, vt_p) + _mm(p_n, v_n)) / l


def _fox_sample_kernel(q_ref, kt_ref, vt_ref, kn_ref, vn_ref, lfp_ref, lfn_ref, z_ref, o_ref, *, nq):
    r_new, total_new = _suffix_sums(lfn_ref[0])
    r_past, _ = _suffix_sums(lfp_ref[0])
    r_past = r_past + total_new
    lane = lax.broadcasted_iota(jnp.int32, (nq, LANES), 1)
    lane2 = lax.broadcasted_iota(jnp.int32, (2 * nq, LANES), 1)
    row2 = lax.broadcasted_iota(jnp.int32, (2 * nq, LANES), 0)
    upper = row2 >= nq
    causal = lane2 <= jnp.where(upper, row2 - nq, row2)
    sls = [slice(blk * LANES, (blk + 1) * LANES) for blk in range(PAIRS)]
    scores = []
    for blk, sl in enumerate(sls):
        q = q_ref[:, sl]
        zero = jnp.zeros_like(q)
        q2 = jnp.concatenate([jnp.where(lane < HD, q, zero), jnp.where(lane >= HD, q, zero)], axis=0)
        kt = kt_ref[0, sl, :].astype(BF16)
        kn = _pad_rows(kn_ref[:, sl], LANES).astype(BF16)
        past_rows = lax.broadcasted_iota(jnp.int32, (2 * nq, r_past.shape[1]), 0) >= nq
        bias_p = jnp.where(past_rows, r_past[2 * blk + 1:2 * blk + 2, :], r_past[2 * blk:2 * blk + 1, :])
        bias_n = jnp.where(upper, r_new[2 * blk + 1:2 * blk + 2, :], r_new[2 * blk:2 * blk + 1, :])
        scores.append((_mm(q2, kt) + bias_p, jnp.where(causal, _mm_nt(q2, kn) + bias_n, NEG)))
    weights = [_joint_softmax(s_p, s_n) for s_p, s_n in scores]
    for blk, sl in enumerate(sls):
        vt = vt_ref[0, sl, :].astype(BF16)
        vn = _pad_rows(vn_ref[:, sl], LANES).astype(BF16)
        o2 = _joint_pv(*weights[blk], vt, vn)
        o = jnp.where(lane < HD, o2[0:nq, :], o2[nq:2 * nq, :])
        o_ref[:, sl] = (o * _silu(z_ref[:, sl].astype(F32))).astype(BF16)


def _diff_sample_kernel(q_ref, kt_ref, v_ref, kn_ref, vn_ref, bias_ref, z_ref,
                        lq1_ref, lk1_ref, lq2_ref, lk2_ref, subg_ref, o_ref, *, nq, past):
    lane = lax.broadcasted_iota(jnp.int32, (nq, LANES), 1)
    lam = _lambda(lq1_ref, lk1_ref, lq2_ref, lk2_ref)
    sls = [slice(hd * LANES, (hd + 1) * LANES) for hd in range(DIFF_HEADS)]
    scores = []
    for hd, sl in enumerate(sls):
        q = q_ref[:, sl]
        kt = kt_ref[0, sl, :].astype(BF16)
        kn = _pad_rows(kn_ref[:, sl], LANES).astype(BF16)
        for mp in range(2):
            qm = jnp.where((lane < HD) if mp == 0 else (lane >= HD), q, jnp.zeros_like(q))
            scores.append((_mm(qm, kt) + bias_ref[hd, :, 0:past],
                           _mm_nt(qm, kn) + bias_ref[hd, :, past:past + LANES]))
    weights = [_joint_softmax(s_p, s_n) for s_p, s_n in scores]
    for hd, sl in enumerate(sls):
        vt = v_ref[0, pl.ds(hd, past, stride=DIFF_HEADS), :].T.astype(BF16)
        vn = _pad_rows(vn_ref[:, sl], LANES).astype(BF16)
        outs = [_joint_pv(*weights[2 * hd + mp], vt, vn) for mp in range(2)]
        o_ref[:, sl] = _diff_finish(outs[0] - lam * outs[1], subg_ref[...], z_ref[:, sl].astype(F32))


def _out_kernel(yf_ref, yd_ref, ga_ref, gb_ref, x_ref, wuf_ref, wud_ref, wo_ref, g_ref, o_ref):
    merged = (ga_ref[...].astype(F32) * _mm(yf_ref[...], wuf_ref[...])
              + gb_ref[...].astype(F32) * _mm(yd_ref[...], wud_ref[...]))
    out = x_ref[...] + _mm(merged.astype(BF16), wo_ref[...])
    ms = jnp.mean(out * out, axis=-1, keepdims=True)
    o_ref[...] = out * lax.rsqrt(ms + NORM_EPS) * g_ref[...]


def _output(yf, yd, ga, gb, x2d, wuf, wud, wo, g, *, tm):
    rows = x2d.shape[0]
    assert rows % tm == 0
    row_spec = lambda n: pl.BlockSpec((tm, n), lambda i: (i, 0))
    const = lambda shape: pl.BlockSpec(shape, lambda i: (0, 0))
    return pl.pallas_call(
        _out_kernel,
        grid=(rows // tm,),
        in_specs=[row_spec(WIDTH), row_spec(WIDTH), row_spec(D_MODEL), row_spec(D_MODEL), row_spec(D_MODEL),
                  const((WIDTH, D_MODEL)), const((WIDTH, D_MODEL)), const((D_MODEL, D_MODEL)),
                  const((1, D_MODEL))],
        out_specs=row_spec(D_MODEL),
        out_shape=jax.ShapeDtypeStruct((rows, D_MODEL), F32),
        compiler_params=pltpu.CompilerParams(
            dimension_semantics=("parallel",), vmem_limit_bytes=VMEM_LIMIT),
        name="out_proj",
    )(yf, yd, ga, gb, x2d, wuf, wud, wo, g)


def _forget_lane_heads():
    lanes = np.arange(LANES)
    heads = (lanes % HD) // FL_SLOT
    heads[:FOX_HEADS] = np.arange(FOX_HEADS)
    return heads


def _pack_w_in(w_in, b_forget):
    sizes = (WIDTH, WIDTH, WIDTH, WIDTH, FOX_HEADS, WIDTH, WIDTH, WIDTH, WIDTH, D_MODEL, D_MODEL)
    offs = np.cumsum((0,) + sizes)
    fq, fk, fv, fz, fl, dq, dk, dv, dz, ga, gb = (w_in[:, offs[i]:offs[i + 1]] for i in range(len(sizes)))
    heads = _forget_lane_heads()
    w_all = jnp.concatenate([fq, fk, fv, fz, dq, dk, dv, dz, ga, gb, fl[:, heads]], axis=1).astype(BF16)
    return w_all, b_forget[heads][None, :]


def kernel(x_prompt, x_sample, cache_fox_k, cache_fox_v, cache_fox_logf, cache_diff_k, cache_diff_v,
           norm_in_g, w_in, b_forget, lambda_q1, lambda_k1, lambda_q2, lambda_k2, subln_g,
           w_up_fox, w_up_diff, w_o, rel_bias, final_norm_g):
    bp, tp, _ = x_prompt.shape
    bs, ts, _ = x_sample.shape
    past = cache_fox_k.shape[2]
    blk, tm = ATTN_BLOCK, ROW_TILE
    assert tp % (QUERY_BLOCKS_PER_STEP * blk) == 0 and blk > _T5_THR[-1] and blk % CHUNK == 0

    w_all, bfl = _pack_w_in(w_in[0], b_forget[0])
    g_in = norm_in_g[0][None, :]
    g_out = final_norm_g[None, :]
    wuf, wud, wo = w_up_fox[0].astype(BF16), w_up_diff[0].astype(BF16), w_o[0].astype(BF16)
    lq1, lk1, lq2, lk2 = (a[0][None, :] for a in (lambda_q1, lambda_k1, lambda_q2, lambda_k2))
    subg = subln_g[0][None, :]

    xs = x_sample.reshape(bs * ts, D_MODEL)
    (fk_s, fv_s, lf_s, dk_s, dv_s, fq_s, fz_s, dq_s, dz_s, ga_s, gb_s) = _project(
        xs, g_in, w_all, bfl, seq_len=ts, aug=False, tm=tm)
    xp = x_prompt.reshape(bp * tp, D_MODEL)
    (fkt, fvt, lft, dkt, dv4, qa, ka, vtf, fz, dq, dkb, vtd, dz, ga, gb) = _project(
        xp, g_in, w_all, bfl, seq_len=tp, aug=True, tm=tm)
    fox_k_p = jnp.transpose(fkt.reshape(bp, FOX_HEADS, HD, tp), (0, 3, 1, 2))[None]
    fox_v_p = jnp.transpose(fvt.reshape(bp, FOX_HEADS, HD, tp), (0, 3, 1, 2))[None]
    fox_lf_p = jnp.transpose(lft, (0, 2, 1))[None]
    diff_k_p = jnp.transpose(dkt.reshape(bp, DIFF_HEADS, 2, HD, tp), (0, 4, 1, 2, 3))[None]
    diff_v_p = dv4.reshape(1, bp, tp, DIFF_HEADS, 2 * HD)
    fox_kt = jnp.transpose(cache_fox_k[0], (0, 2, 3, 1)).reshape(bs, WIDTH, past)
    fox_vt = jnp.transpose(cache_fox_v[0], (0, 2, 3, 1)).reshape(bs, WIDTH, past)
    diff_kt = jnp.transpose(cache_diff_k[0], (0, 2, 3, 4, 1)).reshape(bs, WIDTH, past)
    diff_v = cache_diff_v[0].reshape(bs, past * DIFF_HEADS, 2 * HD)
    lf_past_t = jnp.swapaxes(cache_fox_logf[0], 1, 2)
    lf_new_t = jnp.pad(jnp.swapaxes(lf_s.reshape(bs, ts, FOX_HEADS), 1, 2), ((0, 0), (0, 0), (0, LANES - ts)))

    r3 = lambda a: a.reshape(bp, tp, a.shape[-1])
    yf, yf_s = _fox_attention(r3(qa), r3(ka), vtf, r3(fz),
                              (fq_s, fox_kt, fox_vt, fk_s, fv_s, lf_past_t, lf_new_t, fz_s),
                              blk=blk, nh=FOX_STEP_HEADS, sub=QUERY_BLOCKS_PER_STEP, nq=ts)
    bias_p = _bias_tiles(rel_bias, q_start=blk, nq=blk, segments=((0, blk, blk), (blk, blk, blk)),
                         transposed=True, scale=LOG2E)
    bias_s = _bias_tiles(rel_bias, q_start=past, nq=ts, segments=((0, past, past), (past, LANES, ts)),
                         transposed=False, scale=1.0)
    yd, yd_s = _diff_attention(r3(dq), r3(dkb), vtd, bias_p, r3(dz), lq1, lk1, lq2, lk2, subg,
                               (dq_s, diff_kt, diff_v, dk_s, dv_s, bias_s, dz_s),
                               blk=blk, nh=DIFF_STEP_HEADS, sub=QUERY_BLOCKS_PER_STEP, nq=ts)

    y_p = _output(yf.reshape(bp * tp, WIDTH), yd.reshape(bp * tp, WIDTH), ga, gb, xp, wuf, wud, wo, g_out, tm=tm)
    y_s = _output(yf_s, yd_s, ga_s, gb_s, xs, wuf, wud, wo, g_out, tm=tm)

    return (y_p.reshape(bp, tp, D_MODEL), y_s.reshape(bs, ts, D_MODEL),
            fox_k_p, fox_v_p, fox_lf_p, diff_k_p, diff_v_p,
            fk_s.reshape(1, bs, ts, FOX_HEADS, HD), fv_s.reshape(1, bs, ts, FOX_HEADS, HD),
            lf_s.reshape(1, bs, ts, FOX_HEADS),
            dk_s.reshape(1, bs, ts, DIFF_HEADS, 2, HD), dv_s.reshape(1, bs, ts, DIFF_HEADS, 2 * HD))
```

```python
import functools
import math

import numpy as np
import jax
import jax.numpy as jnp
from jax import lax
from jax.experimental import pallas as pl
from jax.experimental.pallas import tpu as pltpu

F32 = jnp.float32
BF16 = jnp.bfloat16

D_MODEL = 1024
HD = 64
FOX_HEADS = 8
DIFF_HEADS = 4
WIDTH = 512
CHUNK = 64
CHUNK_SHIFT = CHUNK.bit_length() - 1
N_BUCKETS = 32
MAX_DISTANCE = 128
NORM_EPS = 1e-6
SCALE = HD ** -0.5
NEG = -1e30
LAMBDA_INIT = 0.8 - 0.6 * math.exp(-0.3 * 0)
LOG2E = math.log2(math.e)

LANES = 128
PAIRS = WIDTH // LANES
VT_PAD = 16
VT_ROWS = LANES + VT_PAD
FVT_ROWS = HD + VT_PAD

_O_FQ, _O_FK, _O_FV, _O_FZ = 0, 512, 1024, 1536
_O_DQ, _O_DK, _O_DV, _O_DZ = 2048, 2560, 3072, 3584
_O_GA, _O_GB = 4096, 5120
_O_FL = 6144
_N_ALL = 6272
FL_SLOT = 8

VMEM_LIMIT = 58 * 1024 * 1024
ATTN_BLOCK = 256
ROW_TILE = 512

FOX_STEP_HEADS = 8
DIFF_STEP_HEADS = 4
HEADROOM = 64.0
FIRST_STEP_GROUP = 2
QUERY_BLOCKS_PER_STEP = 4
STEP_BLOCKS = (4, 2, 1)


def _t5_thresholds():
    nb = N_BUCKETS // 2
    max_exact = nb // 2
    n = np.arange(0, 4 * MAX_DISTANCE)
    large = max_exact + (np.log(np.maximum(n, 1).astype(np.float32) / max_exact)
                         / math.log(MAX_DISTANCE / max_exact) * (nb - max_exact)).astype(np.int32)
    large = np.minimum(large, nb - 1)
    thr = [int(np.argmax(large >= b)) for b in range(max_exact + 1, nb)]
    return max_exact, tuple(thr)


_MAX_EXACT, _T5_THR = _t5_thresholds()


def _mm(a, b):
    return jnp.dot(a, b, preferred_element_type=F32)


def _mm_nt(a, b):
    return lax.dot_general(a, b, (((1,), (1,)), ((), ())), preferred_element_type=F32)


def _log_sigmoid(x):
    return -(jnp.maximum(-x, 0.0) + jnp.log1p(jnp.exp(-jnp.abs(x))))


def _sigmoid(x):
    return 1.0 / (1.0 + jnp.exp(-x))


def _silu(x):
    return x * _sigmoid(x)


def _split3(x):
    p1 = x.astype(BF16).astype(F32)
    r1 = x - p1
    p2 = r1.astype(BF16).astype(F32)
    p3 = r1 - p2
    return p1, p2, p3


def _proj_kernel(x_ref, g_ref, w_ref, bfl_ref, *refs, tm, seq_len, aug):
    if aug:
        (fkt_ref, fvt_ref, lft_ref, dkt_ref, dv4_ref, qa_ref, ka_ref, vtf_ref,
         fz_ref, dq_ref, dkb_ref, vtd_ref, dz_ref, ga_ref, gb_ref, carry_ref) = refs
    else:
        (fk_ref, fv_ref, lf_ref, dk_ref, dv_ref, fq_ref,
         fz_ref, dq_ref, dz_ref, ga_ref, gb_ref) = refs

    x = x_ref[...]
    ms = jnp.mean(x * x, axis=-1, keepdims=True)
    h = (x * lax.rsqrt(ms + NORM_EPS) * g_ref[...]).astype(BF16)

    def grp(off, n=WIDTH):
        return _mm_nt(h, w_ref[off:off + n, :])

    if not aug:
        fq_ref[...] = (grp(_O_FQ) * SCALE).astype(BF16)
        fk_ref[...] = grp(_O_FK)
        fv_ref[...] = grp(_O_FV)
        fz_ref[...] = grp(_O_FZ).astype(BF16)
        dq_ref[...] = (grp(_O_DQ) * SCALE).astype(BF16)
        dk_ref[...] = grp(_O_DK)
        dv_ref[...] = grp(_O_DV)
        dz_ref[...] = grp(_O_DZ).astype(BF16)
        ga_ref[...] = _sigmoid(grp(_O_GA, D_MODEL)).astype(BF16)
        gb_ref[...] = _sigmoid(grp(_O_GB, D_MODEL)).astype(BF16)
        lf_ref[...] = _log_sigmoid(grp(_O_FL, LANES) + bfl_ref[...])[:, :FOX_HEADS]
        return

    lf = _log_sigmoid(grp(_O_FL, LANES) + bfl_ref[...])
    lft_ref[0] = lf.T[:FOX_HEADS, :]
    row = lax.broadcasted_iota(jnp.int32, (tm, LANES), 0)
    c = lf
    shift = 1
    while shift < tm:
        c = c + jnp.where(row >= shift, pltpu.roll(c, shift, 0), 0.0)
        shift *= 2
    tiles_per_seq = seq_len // tm

    @pl.when(pl.program_id(0) % tiles_per_seq == 0)
    def _():
        carry_ref[...] = jnp.zeros_like(carry_ref)

    c = c + carry_ref[...]
    carry_ref[...] = c[tm - 1:tm, :]
    c_hi, c_mid, c_lo = _split3(c * LOG2E)

    fq = grp(_O_FQ) * (SCALE * LOG2E)
    fk = grp(_O_FK)
    fv = grp(_O_FV)
    fvt = fv.T
    fkt_ref[0] = fk.T
    fvt_ref[0] = fvt
    tail = jnp.where(lax.broadcasted_iota(jnp.int32, (VT_PAD, tm), 0) == 0, 1.0, 0.0).astype(BF16)
    lane = lax.broadcasted_iota(jnp.int32, (tm, LANES), 1)
    pos = lane & (FL_SLOT - 1)
    q_slots = jnp.where(pos == 0, c_hi, jnp.where(pos == 1, c_mid, jnp.where(
        pos == 2, c_lo, jnp.where(pos < 6, 1.0, 0.0))))
    k_slots = jnp.where(pos < 3, 1.0, jnp.where(pos == 3, -c_hi, jnp.where(
        pos == 4, -c_mid, jnp.where(pos == 5, -c_lo, 0.0))))
    for hd in range(FOX_HEADS):
        blk, par = divmod(hd, 2)
        sl = slice(blk * LANES, (blk + 1) * LANES)
        data = (lane < HD) if par == 0 else (lane >= HD)
        slot0 = (1 - par) * HD + hd * FL_SLOT
        own = (lane >= slot0) & (lane < slot0 + FL_SLOT)
        q_aug = jnp.where(own, q_slots, 0.0)
        k_aug = jnp.where(own, k_slots, 0.0)
        osl = slice(hd * LANES, (hd + 1) * LANES)
        qa_ref[:, osl] = jnp.where(data, fq[:, sl], q_aug).astype(BF16)
        ka_ref[:, osl] = jnp.where(data, fk[:, sl], k_aug).astype(BF16)
        base = hd * FVT_ROWS
        vtf_ref[0, base:base + HD, :] = fvt[hd * HD:(hd + 1) * HD, :].astype(BF16)
        vtf_ref[0, base + HD:base + FVT_ROWS, :] = tail

    ga_ref[...] = _sigmoid(grp(_O_GA, D_MODEL)).astype(BF16)
    gb_ref[...] = _sigmoid(grp(_O_GB, D_MODEL)).astype(BF16)

    dk = grp(_O_DK)
    dkt_ref[0] = dk.T
    dkb_ref[...] = dk.astype(BF16)
    dv = grp(_O_DV)
    dv4_ref[...] = dv.reshape(tm, DIFF_HEADS, 2 * HD)
    for hd in range(DIFF_HEADS):
        base = hd * VT_ROWS
        vtd_ref[0, base:base + LANES, :] = dv[:, hd * LANES:(hd + 1) * LANES].T.astype(BF16)
        vtd_ref[0, base + LANES:base + VT_ROWS, :] = tail

    dq_ref[...] = (grp(_O_DQ) * (SCALE * LOG2E)).astype(BF16)
    fz_ref[...] = grp(_O_FZ).astype(BF16)
    dz_ref[...] = grp(_O_DZ).astype(BF16)


def _project(x2d, g, w_all, bfl, *, seq_len, aug, tm):
    rows = x2d.shape[0]
    assert rows % tm == 0
    row_spec = lambda n: pl.BlockSpec((tm, n), lambda i: (i, 0))
    const = lambda shape: pl.BlockSpec(shape, lambda i: (0, 0))
    f32o = lambda n: jax.ShapeDtypeStruct((rows, n), F32)
    b16o = lambda n: jax.ShapeDtypeStruct((rows, n), BF16)
    if aug:
        assert seq_len % tm == 0
        wide = FOX_HEADS * LANES
        tps = seq_len // tm
        t_shape = lambda n, dt: jax.ShapeDtypeStruct((rows // seq_len, n, seq_len), dt)
        t_spec = lambda n: pl.BlockSpec((1, n, tm), lambda i: (i // tps, 0, i % tps))
        out_shape = [t_shape(WIDTH, F32), t_shape(WIDTH, F32), t_shape(FOX_HEADS, F32), t_shape(WIDTH, F32),
                     jax.ShapeDtypeStruct((rows, DIFF_HEADS, 2 * HD), F32)]
        out_specs = [t_spec(WIDTH), t_spec(WIDTH), t_spec(FOX_HEADS), t_spec(WIDTH),
                     pl.BlockSpec((tm, DIFF_HEADS, 2 * HD), lambda i: (i, 0, 0))]
        out_shape += ([b16o(wide)] * 2 + [t_shape(FOX_HEADS * FVT_ROWS, BF16)] + [b16o(WIDTH)] * 3
                      + [t_shape(DIFF_HEADS * VT_ROWS, BF16)] + [b16o(WIDTH)] + [b16o(D_MODEL)] * 2)
        out_specs += ([row_spec(wide)] * 2 + [t_spec(FOX_HEADS * FVT_ROWS)] + [row_spec(WIDTH)] * 3
                      + [t_spec(DIFF_HEADS * VT_ROWS)] + [row_spec(WIDTH)] + [row_spec(D_MODEL)] * 2)
        scratch = [pltpu.VMEM((1, LANES), F32)]
    else:
        out_shape = [f32o(WIDTH), f32o(WIDTH), f32o(FOX_HEADS), f32o(WIDTH), f32o(WIDTH)]
        out_specs = [row_spec(WIDTH), row_spec(WIDTH), row_spec(FOX_HEADS), row_spec(WIDTH), row_spec(WIDTH)]
        out_shape += [b16o(WIDTH)] * 4 + [b16o(D_MODEL)] * 2
        out_specs += [row_spec(WIDTH)] * 4 + [row_spec(D_MODEL)] * 2
        scratch = []
    return pl.pallas_call(
        functools.partial(_proj_kernel, tm=tm, seq_len=seq_len, aug=aug),
        grid=(rows // tm,),
        in_specs=[row_spec(D_MODEL), const((1, D_MODEL)),
                  pl.BlockSpec((_N_ALL, D_MODEL), lambda i: (0, 0), pipeline_mode=pl.Buffered(1)),
                  const((1, LANES))],
        out_specs=out_specs,
        out_shape=out_shape,
        scratch_shapes=scratch,
        compiler_params=pltpu.CompilerParams(
            dimension_semantics=("arbitrary",), vmem_limit_bytes=VMEM_LIMIT),
        name="proj_aug" if aug else "proj_plain",
    )(x2d, g, w_all, bfl)


def _bias_kernel(tbl_ref, o_ref, *, q_start, nq, segments, transposed, scale):
    hd = pl.program_id(0)
    far = tbl_ref[N_BUCKETS // 2 - 1, hd]
    q_axis, k_axis = (1, 0) if transposed else (0, 1)
    off = 0
    for k_start, nk, valid in segments:
        shape = (nk, nq) if transposed else (nq, nk)
        qpos = q_start + lax.broadcasted_iota(jnp.int32, shape, q_axis)
        kidx = lax.broadcasted_iota(jnp.int32, shape, k_axis)
        kpos = k_start + kidx
        rel = kpos - qpos
        n = jnp.abs(rel)
        large = jnp.full(shape, _MAX_EXACT, jnp.int32)
        for thr in _T5_THR:
            large = large + jnp.where(n >= thr, 1, 0)
        bucket = jnp.where(rel > 0, N_BUCKETS // 2, 0) + jnp.where(n < _MAX_EXACT, n, large)
        val = jnp.zeros(shape, F32)
        for b in range(N_BUCKETS):
            val = jnp.where(bucket == b, tbl_ref[b, hd], val)
        val = (val - far) * scale
        visible = ((kpos >> CHUNK_SHIFT) <= (qpos >> CHUNK_SHIFT)) & (kidx < valid)
        val = jnp.where(visible, val, NEG)
        if transposed:
            o_ref[0, off:off + nk, :] = val
        else:
            o_ref[0, :, off:off + nk] = val
        off += nk


def _bias_tiles(rel_bias, *, q_start, nq, segments, transposed, scale):
    total = sum(s[1] for s in segments)
    shape = (total, nq) if transposed else (nq, total)
    return pl.pallas_call(
        functools.partial(_bias_kernel, q_start=q_start, nq=nq, segments=segments, transposed=transposed,
                          scale=scale),
        grid=(DIFF_HEADS,),
        in_specs=[pl.BlockSpec(memory_space=pltpu.SMEM)],
        out_specs=pl.BlockSpec((1,) + shape, lambda h: (h, 0, 0)),
        out_shape=jax.ShapeDtypeStruct((DIFF_HEADS,) + shape, F32),
        name="t5_bias",
    )(rel_bias)


def _flash_t(k_ref, vt_ref, acc_ref, bad_ref, streams, qi, blk, prev_bias, diag_bias, diag_mask):
    n = len(streams)

    def scores(first_blk, width):
        start = pl.multiple_of(first_blk * blk, blk)
        return [_mm_nt(k_ref[0, pl.ds(start, width), ksl], q) for q, ksl, _ in streams]

    def step(first_blk, width, ms, biases, mask):
        start = pl.multiple_of(first_blk * blk, blk)
        sts = scores(first_blk, width)
        out, ps, alphas = [], [], []
        for s, st in enumerate(sts):
            if biases is not None:
                st = st + biases[s]()
            if mask is not None:
                st = jnp.where(mask, st, NEG)
            m_new = jnp.maximum(ms[s], jnp.max(st, axis=0, keepdims=True))
            ps.append(jnp.exp2(st - m_new).astype(BF16))
            alphas.append(jnp.exp2(ms[s] - m_new))
            out.append(m_new)
        for s, (_, _, vsl) in enumerate(streams):
            acc_ref[s] = alphas[s] * acc_ref[s] + _mm(vt_ref[0, vsl, pl.ds(start, width)], ps[s])
        return tuple(out)

    def first_step():
        start = pl.multiple_of(qi * blk, blk)
        out, ps = [], []
        for s, st in enumerate(scores(qi, blk)):
            if diag_bias is not None:
                st = st + diag_bias[s]()
            if diag_mask is not None:
                st = jnp.where(diag_mask, st, NEG)
            m = jnp.max(st, axis=0, keepdims=True)
            ps.append(jnp.exp2(st - m).astype(BF16))
            out.append(m)
        for s, (_, _, vsl) in enumerate(streams):
            acc_ref[s] = _mm(vt_ref[0, vsl, pl.ds(start, blk)], ps[s])
        return tuple(out)

    ms = first_step()
    bad_ref[...] = jnp.zeros_like(bad_ref)

    def consume(sts, first_blk, width):
        start = pl.multiple_of(first_blk * blk, blk)
        ps = [jnp.exp2(st - ms[s]).astype(BF16) for s, st in enumerate(sts)]
        worst = None
        for p in ps:
            d = jnp.max(p, axis=0, keepdims=True)
            worst = d if worst is None else jnp.maximum(worst, d)
        bad_ref[...] = jnp.maximum(bad_ref[...], worst.astype(F32))
        for s, (_, _, vsl) in enumerate(streams):
            acc_ref[s] = acc_ref[s] + _mm(vt_ref[0, vsl, pl.ds(start, width)], ps[s])

    if prev_bias is not None:
        @pl.when(qi >= 1)
        def _():
            consume([st + prev_bias[s]() for s, st in enumerate(scores(qi - 1, blk))], qi - 1, blk)

    n_plain = qi if prev_bias is None else jnp.maximum(qi - 1, 0)
    done = 0
    for w in STEP_BLOCKS:
        count = (n_plain - done) // w

        def fast_step(j, carry, w=w, done=done):
            consume(scores(done + j * w, w * blk), done + j * w, w * blk)
            return carry

        lax.fori_loop(0, count, fast_step, 0)
        done = done + count * w

    @pl.when(jnp.max(bad_ref[...]) > 2.0 ** HEADROOM)
    def _():
        c = first_step()
        if prev_bias is not None:
            c = lax.cond(qi >= 1, lambda c: step(qi - 1, blk, c, prev_bias, None), lambda c: c, c)
        lax.fori_loop(0, n_plain, lambda j, c: step(j, blk, c, None, None), c)


def _fox_kernel(qa_ref, ka_ref, vt_ref, z_ref, *rest, blk, nh, sub, nq):
    sample_in, (o_ref, so_ref, acc_ref, bad_ref) = rest[:8], rest[8:]
    key = lax.broadcasted_iota(jnp.int32, (blk, blk), 0)
    qry = lax.broadcasted_iota(jnp.int32, (blk, blk), 1)
    hsl = [slice(hd * LANES, (hd + 1) * LANES) for hd in range(nh)]

    def query_block(i, carry):
        qi = pl.program_id(2) * sub + i
        rows = pl.ds(pl.multiple_of(i * blk, blk), blk)
        streams = [(qa_ref[0, rows, sl], sl, slice(hd * FVT_ROWS, (hd + 1) * FVT_ROWS))
                   for hd, sl in enumerate(hsl)]
        _flash_t(ka_ref, vt_ref, acc_ref, bad_ref, streams, qi, blk, None, None, key <= qry)
        for pair in range(nh // 2):
            halves = []
            for par in range(2):
                acc = acc_ref[2 * pair + par]
                halves.append(acc[0:HD, :] / acc[HD:HD + 1, :])
            o = jnp.concatenate(halves, axis=0).T
            o_ref[0, rows, hsl[pair]] = (o * _silu(z_ref[0, rows, hsl[pair]].astype(F32))).astype(BF16)
        return carry

    lax.fori_loop(0, sub, query_block, 0)
    _fox_sample_kernel(*sample_in, so_ref, nq=nq)


def _fox_attention(qa, ka, vt, fz, sample_args, *, blk, nh, sub, nq):
    b, t, _ = qa.shape
    step = sub * blk
    per_seq = t // step
    nb, _, past = sample_args[1].shape
    assert nh == FOX_HEADS and nb == b * per_seq
    row = lambda b, p, i: b * per_seq + i
    new = lambda: pl.BlockSpec((nq, WIDTH), lambda b, p, i: (row(b, p, i), 0))
    cache = lambda r, c: pl.BlockSpec((1, r, c), lambda b, p, i: (row(b, p, i), 0, 0))
    return pl.pallas_call(
        functools.partial(_fox_kernel, blk=blk, nh=nh, sub=sub, nq=nq),
        grid=(b, FOX_HEADS // nh, per_seq),
        in_specs=[pl.BlockSpec((1, step, nh * LANES), lambda b, p, i: (b, i, p)),
                  pl.BlockSpec((1, t, nh * LANES), lambda b, p, i: (b, 0, p)),
                  pl.BlockSpec((1, nh * FVT_ROWS, t), lambda b, p, i: (b, p, 0)),
                  pl.BlockSpec((1, step, nh * HD), lambda b, p, i: (b, i, p)),
                  new(), cache(WIDTH, past), cache(WIDTH, past), new(), new(),
                  cache(FOX_HEADS, past), cache(FOX_HEADS, LANES), new()],
        out_specs=[pl.BlockSpec((1, step, nh * HD), lambda b, p, i: (b, i, p)), new()],
        out_shape=[jax.ShapeDtypeStruct((b, t, WIDTH), BF16), jax.ShapeDtypeStruct((nb * nq, WIDTH), BF16)],
        scratch_shapes=[pltpu.VMEM((nh, FVT_ROWS, blk), F32), pltpu.VMEM((1, blk), F32)],
        compiler_params=pltpu.CompilerParams(
            dimension_semantics=("parallel", "parallel", "arbitrary"), vmem_limit_bytes=VMEM_LIMIT),
        name="fox_prompt",
    )(qa, ka, vt, fz, *sample_args)


def _lambda(lq1_ref, lk1_ref, lq2_ref, lk2_ref):
    s1 = jnp.sum(lq1_ref[...] * lk1_ref[...], axis=-1, keepdims=True)
    s2 = jnp.sum(lq2_ref[...] * lk2_ref[...], axis=-1, keepdims=True)
    return jnp.exp(s1) - jnp.exp(s2) + LAMBDA_INIT


def _diff_finish(o, subg, z):
    ms = jnp.mean(o * o, axis=-1, keepdims=True)
    od = o * lax.rsqrt(ms + NORM_EPS) * subg * (1.0 - LAMBDA_INIT)
    return (od * _silu(z)).astype(BF16)


def _diff_kernel(q_ref, k_ref, vt_ref, bias_ref, z_ref, lq1_ref, lk1_ref, lq2_ref, lk2_ref, subg_ref,
                 *rest, blk, nh, sub, nq, past):
    sample_in, (o_ref, so_ref, acc_ref, ms_ref, bad_ref) = rest[:7], rest[7:]
    lane = lax.broadcasted_iota(jnp.int32, (blk, LANES), 1)
    lam = _lambda(lq1_ref, lk1_ref, lq2_ref, lk2_ref)
    n = 2 * nh
    qi0 = pl.program_id(2) * sub
    ksl = [slice((s // 2) * LANES, (s // 2 + 1) * LANES) for s in range(n)]
    vsl = [slice((s // 2) * VT_ROWS, (s // 2 + 1) * VT_ROWS) for s in range(n)]

    def queries(rows):
        out = []
        for s in range(n):
            q = q_ref[0, rows, ksl[s]]
            out.append(jnp.where((lane < HD) if s % 2 == 0 else (lane >= HD), q, jnp.zeros_like(q)))
        return out

    def scores(qs, blk0, width):
        start = pl.multiple_of(blk0 * blk, blk)
        return [_mm_nt(k_ref[0, pl.ds(start, width), ksl[s]], qs[s]) for s in range(n)]

    def values(s, blk0, width):
        return vt_ref[0, vsl[s], pl.ds(pl.multiple_of(blk0 * blk, blk), width)]

    def prev_bias(s):
        return bias_ref[s // 2, 0:blk, :]

    def diag_bias(s):
        return bias_ref[s // 2, blk:2 * blk, :]

    qs_all = [queries(slice(j * blk, (j + 1) * blk)) for j in range(sub)]
    for group in range(0, sub, FIRST_STEP_GROUP):
        js = range(group, min(group + FIRST_STEP_GROUP, sub))
        diag = {j: scores(qs_all[j], qi0 + j, blk) for j in js}
        ms_all, p_diag = {}, {}
        for j in js:
            ms_all[j], p_diag[j] = [], []
            for s in range(n):
                st = diag[j][s] + diag_bias(s)
                m = jnp.max(st, axis=0, keepdims=True)
                p_diag[j].append(jnp.exp2(st - m).astype(BF16))
                ms_all[j].append(m)
                ms_ref[j, s] = m
        prev = {j: scores(qs_all[j], qi0 + j - 1, blk) for j in js if j >= 1}
        p_prev = {j: [jnp.exp2(prev[j][s] + prev_bias(s) - ms_all[j][s]).astype(BF16) for s in range(n)]
                  for j in prev}
        for j in js:
            worst = None
            for s in range(n):
                acc = _mm(values(s, qi0 + j, blk), p_diag[j][s])
                if j >= 1:
                    acc = acc + _mm(values(s, qi0 + j - 1, blk), p_prev[j][s])
                    d = jnp.max(p_prev[j][s], axis=0, keepdims=True)
                    worst = d if worst is None else jnp.maximum(worst, d)
                acc_ref[j, s] = acc
            bad_ref[j] = jnp.zeros((1, blk), F32) if worst is None else worst.astype(F32)

    def consume(j, ms, sts, blk0, width, biased):
        ps = [jnp.exp2((sts[s] + prev_bias(s) if biased else sts[s]) - ms[s]).astype(BF16) for s in range(n)]
        worst = None
        for p in ps:
            d = jnp.max(p, axis=0, keepdims=True)
            worst = d if worst is None else jnp.maximum(worst, d)
        bad_ref[j] = jnp.maximum(bad_ref[j], worst.astype(F32))
        for s in range(n):
            acc_ref[j, s] = acc_ref[j, s] + _mm(values(s, blk0, width), ps[s])

    @pl.when(qi0 >= 1)
    def _():
        ms0 = [ms_ref[0, s] for s in range(n)]
        consume(0, ms0, scores(qs_all[0], qi0 - 1, blk), qi0 - 1, blk, True)

    def query_block(j, carry):
        qi = qi0 + j
        qs = queries(pl.ds(pl.multiple_of(j * blk, blk), blk))
        ms = [ms_ref[j, s] for s in range(n)]
        n_plain = jnp.maximum(qi - 1, 0)
        done = 0
        for w in STEP_BLOCKS:
            count = (n_plain - done) // w

            def fast_step(i, c, w=w, done=done):
                consume(j, ms, scores(qs, done + i * w, w * blk), done + i * w, w * blk, False)
                return c

            lax.fori_loop(0, count, fast_step, 0)
            done = done + count * w

        @pl.when(jnp.max(bad_ref[j]) > 2.0 ** HEADROOM)
        def _():
            def exact_step(blk0, ms_run, bias):
                out, ps, alphas = [], [], []
                for s, st in enumerate(scores(qs, blk0, blk)):
                    if bias is not None:
                        st = st + bias(s)
                    m_new = jnp.maximum(ms_run[s], jnp.max(st, axis=0, keepdims=True))
                    ps.append(jnp.exp2(st - m_new).astype(BF16))
                    alphas.append(jnp.exp2(ms_run[s] - m_new))
                    out.append(m_new)
                for s in range(n):
                    acc_ref[j, s] = alphas[s] * acc_ref[j, s] + _mm(values(s, blk0, blk), ps[s])
                return tuple(out)

            acc_ref[j] = jnp.zeros(acc_ref.shape[1:], F32)
            c = exact_step(qi, tuple(jnp.full((1, blk), NEG, F32) for _ in range(n)), diag_bias)
            c = lax.cond(qi >= 1, lambda c: exact_step(qi - 1, c, prev_bias), lambda c: c, c)
            lax.fori_loop(0, n_plain, lambda i, c: exact_step(i, c, None), c)

        return carry

    lax.fori_loop(0, sub, query_block, 0)

    for j in range(sub):
        rows = slice(j * blk, (j + 1) * blk)
        for hd in range(nh):
            a0, a1 = acc_ref[j, 2 * hd], acc_ref[j, 2 * hd + 1]
            o_t = a0[0:LANES, :] / a0[LANES:LANES + 1, :] - lam * (a1[0:LANES, :] / a1[LANES:LANES + 1, :])
            sl = slice(hd * LANES, (hd + 1) * LANES)
            o_ref[0, rows, sl] = _diff_finish(o_t.T, subg_ref[...], z_ref[0, rows, sl].astype(F32))

    _diff_sample_kernel(*sample_in, lq1_ref, lk1_ref, lq2_ref, lk2_ref, subg_ref, so_ref, nq=nq, past=past)


def _diff_attention(dq, dk, dvt, bias, dz, lq1, lk1, lq2, lk2, subg, sample_args, *, blk, nh, sub, nq):
    b, t, _ = dq.shape
    step = sub * blk
    per_seq = t // step
    nb, _, past = sample_args[1].shape
    assert nh == DIFF_HEADS and nb == b * per_seq
    small = lambda n: pl.BlockSpec((1, n), lambda b, h, i: (0, 0))
    row = lambda b, h, i: b * per_seq + i
    new = lambda: pl.BlockSpec((nq, WIDTH), lambda b, h, i: (row(b, h, i), 0))
    cache = lambda r, c: pl.BlockSpec((1, r, c), lambda b, h, i: (row(b, h, i), 0, 0))
    return pl.pallas_call(
        functools.partial(_diff_kernel, blk=blk, nh=nh, sub=sub, nq=nq, past=past),
        grid=(b, DIFF_HEADS // nh, per_seq),
        in_specs=[pl.BlockSpec((1, step, nh * LANES), lambda b, h, i: (b, i, h)),
                  pl.BlockSpec((1, t, nh * LANES), lambda b, h, i: (b, 0, h)),
                  pl.BlockSpec((1, nh * VT_ROWS, t), lambda b, h, i: (b, h, 0)),
                  pl.BlockSpec((nh, 2 * blk, blk), lambda b, h, i: (h, 0, 0)),
                  pl.BlockSpec((1, step, nh * LANES), lambda b, h, i: (b, i, h)),
                  small(HD), small(HD), small(HD), small(HD), small(2 * HD),
                  new(), cache(WIDTH, past), cache(past * DIFF_HEADS, 2 * HD), new(), new(),
                  pl.BlockSpec((DIFF_HEADS, nq, past + LANES), lambda b, h, i: (0, 0, 0)), new()],
        out_specs=[pl.BlockSpec((1, step, nh * LANES), lambda b, h, i: (b, i, h)), new()],
        out_shape=[jax.ShapeDtypeStruct((b, t, WIDTH), BF16), jax.ShapeDtypeStruct((nb * nq, WIDTH), BF16)],
        scratch_shapes=[pltpu.VMEM((sub, 2 * nh, VT_ROWS, blk), F32), pltpu.VMEM((sub, 2 * nh, 1, blk), F32),
                        pltpu.VMEM((sub, 1, blk), F32)],
        compiler_params=pltpu.CompilerParams(
            dimension_semantics=("parallel", "parallel", "arbitrary"), vmem_limit_bytes=VMEM_LIMIT),
        name="diff_prompt",
    )(dq, dk, dvt, bias, dz, lq1, lk1, lq2, lk2, subg, *sample_args)


def _suffix_sums(x):
    n = x.shape[1] // LANES
    lane = lax.broadcasted_iota(jnp.int32, (x.shape[0], LANES), 1)
    after = jnp.zeros((x.shape[0], 1), F32)
    pieces = [None] * n
    for blk in reversed(range(n)):
        piece = x[:, blk * LANES:(blk + 1) * LANES]
        y = piece
        shift = 1
        while shift < LANES:
            y = y + jnp.where(lane + shift < LANES, pltpu.roll(y, LANES - shift, 1), 0.0)
            shift *= 2
        pieces[blk] = y - piece + after
        after = after + y[:, 0:1]
    return jnp.concatenate(pieces, axis=1), after


def _pad_rows(x, rows):
    return jnp.concatenate([x, jnp.zeros((rows - x.shape[0], x.shape[1]), x.dtype)], axis=0)


def _joint_softmax(s_p, s_n):
    m = jnp.maximum(jnp.max(s_p, axis=-1, keepdims=True), jnp.max(s_n, axis=-1, keepdims=True))
    p_p = jnp.exp(s_p - m)
    p_n = jnp.exp(s_n - m)
    l = jnp.sum(p_p, axis=-1, keepdims=True) + jnp.sum(p_n, axis=-1, keepdims=True)
    return p_p.astype(BF16), p_n.astype(BF16), l


def _joint_pv(p_p, p_n, l, vt_p, v_n):
    return (_mm_nt(p_p, vt_p) + _mm(p_n, v_n)) / l


def _fox_sample_kernel(q_ref, kt_ref, vt_ref, kn_ref, vn_ref, lfp_ref, lfn_ref, z_ref, o_ref, *, nq):
    r_new, total_new = _suffix_sums(lfn_ref[0])
    r_past, _ = _suffix_sums(lfp_ref[0])
    r_past = r_past + total_new
    lane = lax.broadcasted_iota(jnp.int32, (nq, LANES), 1)
    lane2 = lax.broadcasted_iota(jnp.int32, (2 * nq, LANES), 1)
    row2 = lax.broadcasted_iota(jnp.int32, (2 * nq, LANES), 0)
    upper = row2 >= nq
    causal = lane2 <= jnp.where(upper, row2 - nq, row2)
    sls = [slice(blk * LANES, (blk + 1) * LANES) for blk in range(PAIRS)]
    scores = []
    for blk, sl in enumerate(sls):
        q = q_ref[:, sl]
        zero = jnp.zeros_like(q)
        q2 = jnp.concatenate([jnp.where(lane < HD, q, zero), jnp.where(lane >= HD, q, zero)], axis=0)
        kt = kt_ref[0, sl, :].astype(BF16)
        kn = _pad_rows(kn_ref[:, sl], LANES).astype(BF16)
        past_rows = lax.broadcasted_iota(jnp.int32, (2 * nq, r_past.shape[1]), 0) >= nq
        bias_p = jnp.where(past_rows, r_past[2 * blk + 1:2 * blk + 2, :], r_past[2 * blk:2 * blk + 1, :])
        bias_n = jnp.where(upper, r_new[2 * blk + 1:2 * blk + 2, :], r_new[2 * blk:2 * blk + 1, :])
        scores.append((_mm(q2, kt) + bias_p, jnp.where(causal, _mm_nt(q2, kn) + bias_n, NEG)))
    weights = [_joint_softmax(s_p, s_n) for s_p, s_n in scores]
    for blk, sl in enumerate(sls):
        vt = vt_ref[0, sl, :].astype(BF16)
        vn = _pad_rows(vn_ref[:, sl], LANES).astype(BF16)
        o2 = _joint_pv(*weights[blk], vt, vn)
        o = jnp.where(lane < HD, o2[0:nq, :], o2[nq:2 * nq, :])
        o_ref[:, sl] = (o * _silu(z_ref[:, sl].astype(F32))).astype(BF16)


def _diff_sample_kernel(q_ref, kt_ref, v_ref, kn_ref, vn_ref, bias_ref, z_ref,
                        lq1_ref, lk1_ref, lq2_ref, lk2_ref, subg_ref, o_ref, *, nq, past):
    lane = lax.broadcasted_iota(jnp.int32, (nq, LANES), 1)
    lam = _lambda(lq1_ref, lk1_ref, lq2_ref, lk2_ref)
    sls = [slice(hd * LANES, (hd + 1) * LANES) for hd in range(DIFF_HEADS)]
    scores = []
    for hd, sl in enumerate(sls):
        q = q_ref[:, sl]
        kt = kt_ref[0, sl, :].astype(BF16)
        kn = _pad_rows(kn_ref[:, sl], LANES).astype(BF16)
        for mp in range(2):
            qm = jnp.where((lane < HD) if mp == 0 else (lane >= HD), q, jnp.zeros_like(q))
            scores.append((_mm(qm, kt) + bias_ref[hd, :, 0:past],
                           _mm_nt(qm, kn) + bias_ref[hd, :, past:past + LANES]))
    weights = [_joint_softmax(s_p, s_n) for s_p, s_n in scores]
    for hd, sl in enumerate(sls):
        vt = v_ref[0, pl.ds(hd, past, stride=DIFF_HEADS), :].T.astype(BF16)
        vn = _pad_rows(vn_ref[:, sl], LANES).astype(BF16)
        outs = [_joint_pv(*weights[2 * hd + mp], vt, vn) for mp in range(2)]
        o_ref[:, sl] = _diff_finish(outs[0] - lam * outs[1], subg_ref[...], z_ref[:, sl].astype(F32))


def _out_kernel(yf_ref, yd_ref, ga_ref, gb_ref, x_ref, wuf_ref, wud_ref, wo_ref, g_ref, o_ref):
    merged = (ga_ref[...].astype(F32) * _mm(yf_ref[...], wuf_ref[...])
              + gb_ref[...].astype(F32) * _mm(yd_ref[...], wud_ref[...]))
    out = x_ref[...] + _mm(merged.astype(BF16), wo_ref[...])
    ms = jnp.mean(out * out, axis=-1, keepdims=True)
    o_ref[...] = out * lax.rsqrt(ms + NORM_EPS) * g_ref[...]


def _output(yf, yd, ga, gb, x2d, wuf, wud, wo, g, *, tm):
    rows = x2d.shape[0]
    assert rows % tm == 0
    row_spec = lambda n: pl.BlockSpec((tm, n), lambda i: (i, 0))
    const = lambda shape: pl.BlockSpec(shape, lambda i: (0, 0))
    return pl.pallas_call(
        _out_kernel,
        grid=(rows // tm,),
        in_specs=[row_spec(WIDTH), row_spec(WIDTH), row_spec(D_MODEL), row_spec(D_MODEL), row_spec(D_MODEL),
                  const((WIDTH, D_MODEL)), const((WIDTH, D_MODEL)), const((D_MODEL, D_MODEL)),
                  const((1, D_MODEL))],
        out_specs=row_spec(D_MODEL),
        out_shape=jax.ShapeDtypeStruct((rows, D_MODEL), F32),
        compiler_params=pltpu.CompilerParams(
            dimension_semantics=("parallel",), vmem_limit_bytes=VMEM_LIMIT),
        name="out_proj",
    )(yf, yd, ga, gb, x2d, wuf, wud, wo, g)


def _forget_lane_heads():
    lanes = np.arange(LANES)
    heads = (lanes % HD) // FL_SLOT
    heads[:FOX_HEADS] = np.arange(FOX_HEADS)
    return heads


def _pack_w_in(w_in, b_forget):
    sizes = (WIDTH, WIDTH, WIDTH, WIDTH, FOX_HEADS, WIDTH, WIDTH, WIDTH, WIDTH, D_MODEL, D_MODEL)
    offs = np.cumsum((0,) + sizes)
    wt = jnp.swapaxes(w_in, 0, 1)
    fq, fk, fv, fz, fl, dq, dk, dv, dz, ga, gb = (wt[offs[i]:offs[i + 1]] for i in range(len(sizes)))
    heads = _forget_lane_heads()
    w_all = jnp.concatenate([fq, fk, fv, fz, dq, dk, dv, dz, ga, gb, fl[heads]], axis=0).astype(BF16)
    return w_all, b_forget[heads][None, :]


def kernel(x_prompt, x_sample, cache_fox_k, cache_fox_v, cache_fox_logf, cache_diff_k, cache_diff_v,
           norm_in_g, w_in, b_forget, lambda_q1, lambda_k1, lambda_q2, lambda_k2, subln_g,
           w_up_fox, w_up_diff, w_o, rel_bias, final_norm_g):
    bp, tp, _ = x_prompt.shape
    bs, ts, _ = x_sample.shape
    past = cache_fox_k.shape[2]
    blk, tm = ATTN_BLOCK, ROW_TILE
    assert tp % (QUERY_BLOCKS_PER_STEP * blk) == 0 and blk > _T5_THR[-1] and blk % CHUNK == 0

    w_all, bfl = _pack_w_in(w_in[0], b_forget[0])
    g_in = norm_in_g[0][None, :]
    g_out = final_norm_g[None, :]
    wuf, wud, wo = w_up_fox[0].astype(BF16), w_up_diff[0].astype(BF16), w_o[0].astype(BF16)
    lq1, lk1, lq2, lk2 = (a[0][None, :] for a in (lambda_q1, lambda_k1, lambda_q2, lambda_k2))
    subg = subln_g[0][None, :]

    xs = x_sample.reshape(bs * ts, D_MODEL)
    (fk_s, fv_s, lf_s, dk_s, dv_s, fq_s, fz_s, dq_s, dz_s, ga_s, gb_s) = _project(
        xs, g_in, w_all, bfl, seq_len=ts, aug=False, tm=tm)
    xp = x_prompt.reshape(bp * tp, D_MODEL)
    (fkt, fvt, lft, dkt, dv4, qa, ka, vtf, fz, dq, dkb, vtd, dz, ga, gb) = _project(
        xp, g_in, w_all, bfl, seq_len=tp, aug=True, tm=tm)
    fox_k_p = jnp.transpose(fkt.reshape(bp, FOX_HEADS, HD, tp), (0, 3, 1, 2))[None]
    fox_v_p = jnp.transpose(fvt.reshape(bp, FOX_HEADS, HD, tp), (0, 3, 1, 2))[None]
    fox_lf_p = jnp.transpose(lft, (0, 2, 1))[None]
    diff_k_p = jnp.transpose(dkt.reshape(bp, DIFF_HEADS, 2, HD, tp), (0, 4, 1, 2, 3))[None]
    diff_v_p = dv4.reshape(1, bp, tp, DIFF_HEADS, 2 * HD)
    fox_kt = jnp.transpose(cache_fox_k[0], (0, 2, 3, 1)).reshape(bs, WIDTH, past)
    fox_vt = jnp.transpose(cache_fox_v[0], (0, 2, 3, 1)).reshape(bs, WIDTH, past)
    diff_kt = jnp.transpose(cache_diff_k[0], (0, 2, 3, 4, 1)).reshape(bs, WIDTH, past)
    diff_v = cache_diff_v[0].reshape(bs, past * DIFF_HEADS, 2 * HD)
    lf_past_t = jnp.swapaxes(cache_fox_logf[0], 1, 2)
    lf_new_t = jnp.pad(jnp.swapaxes(lf_s.reshape(bs, ts, FOX_HEADS), 1, 2), ((0, 0), (0, 0), (0, LANES - ts)))

    r3 = lambda a: a.reshape(bp, tp, a.shape[-1])
    yf, yf_s = _fox_attention(r3(qa), r3(ka), vtf, r3(fz),
                              (fq_s, fox_kt, fox_vt, fk_s, fv_s, lf_past_t, lf_new_t, fz_s),
                              blk=blk, nh=FOX_STEP_HEADS, sub=QUERY_BLOCKS_PER_STEP, nq=ts)
    bias_p = _bias_tiles(rel_bias, q_start=blk, nq=blk, segments=((0, blk, blk), (blk, blk, blk)),
                         transposed=True, scale=LOG2E)
    bias_s = _bias_tiles(rel_bias, q_start=past, nq=ts, segments=((0, past, past), (past, LANES, ts)),
                         transposed=False, scale=1.0)
    yd, yd_s = _diff_attention(r3(dq), r3(dkb), vtd, bias_p, r3(dz), lq1, lk1, lq2, lk2, subg,
                               (dq_s, diff_kt, diff_v, dk_s, dv_s, bias_s, dz_s),
                               blk=blk, nh=DIFF_STEP_HEADS, sub=QUERY_BLOCKS_PER_STEP, nq=ts)

    y_p = _output(yf.reshape(bp * tp, WIDTH), yd.reshape(bp * tp, WIDTH), ga, gb, xp, wuf, wud, wo, g_out, tm=tm)
    y_s = _output(yf_s, yd_s, ga_s, gb_s, xs, wuf, wud, wo, g_out, tm=tm)

    return (y_p.reshape(bp, tp, D_MODEL), y_s.reshape(bs, ts, D_MODEL),
            fox_k_p, fox_v_p, fox_lf_p, diff_k_p, diff_v_p,
            fk_s.reshape(1, bs, ts, FOX_HEADS, HD), fv_s.reshape(1, bs, ts, FOX_HEADS, HD),
            lf_s.reshape(1, bs, ts, FOX_HEADS),
            dk_s.reshape(1, bs, ts, DIFF_HEADS, 2, HD), dv_s.reshape(1, bs, ts, DIFF_HEADS, 2 * HD))
```

```python
import functools
import math

import numpy as np
import jax
import jax.numpy as jnp
from jax import lax
from jax.experimental import pallas as pl
from jax.experimental.pallas import tpu as pltpu

F32 = jnp.float32
BF16 = jnp.bfloat16

D_MODEL = 1024
HD = 64
FOX_HEADS = 8
DIFF_HEADS = 4
WIDTH = 512
CHUNK = 64
CHUNK_SHIFT = CHUNK.bit_length() - 1
N_BUCKETS = 32
MAX_DISTANCE = 128
NORM_EPS = 1e-6
SCALE = HD ** -0.5
NEG = -1e30
LAMBDA_INIT = 0.8 - 0.6 * math.exp(-0.3 * 0)
LOG2E = math.log2(math.e)

LANES = 128
PAIRS = WIDTH // LANES
VT_PAD = 16
VT_ROWS = LANES + VT_PAD
FVT_ROWS = HD + VT_PAD

_O_FQ, _O_FK, _O_FV, _O_FZ = 0, 512, 1024, 1536
_O_DQ, _O_DK, _O_DV, _O_DZ = 2048, 2560, 3072, 3584
_O_GA, _O_GB = 4096, 5120
_O_FL = 6144
_N_ALL = 6272
FL_SLOT = 8

VMEM_LIMIT = 58 * 1024 * 1024
ATTN_BLOCK = 256
ROW_TILE = 512
OUT_ROW_TILE = 1024

FOX_STEP_HEADS = 8
DIFF_STEP_HEADS = 4
HEADROOM = 64.0
FIRST_STEP_GROUP = 2
QUERY_BLOCKS_PER_STEP = 4
STEP_BLOCKS = (4, 2, 1)


def _t5_thresholds():
    nb = N_BUCKETS // 2
    max_exact = nb // 2
    n = np.arange(0, 4 * MAX_DISTANCE)
    large = max_exact + (np.log(np.maximum(n, 1).astype(np.float32) / max_exact)
                         / math.log(MAX_DISTANCE / max_exact) * (nb - max_exact)).astype(np.int32)
    large = np.minimum(large, nb - 1)
    thr = [int(np.argmax(large >= b)) for b in range(max_exact + 1, nb)]
    return max_exact, tuple(thr)


_MAX_EXACT, _T5_THR = _t5_thresholds()


def _mm(a, b):
    return jnp.dot(a, b, preferred_element_type=F32)


def _mm_nt(a, b):
    return lax.dot_general(a, b, (((1,), (1,)), ((), ())), preferred_element_type=F32)


def _log_sigmoid(x):
    return -(jnp.maximum(-x, 0.0) + jnp.log1p(jnp.exp(-jnp.abs(x))))


def _sigmoid(x):
    return 1.0 / (1.0 + jnp.exp(-x))


def _silu(x):
    return x * _sigmoid(x)


def _split3(x):
    p1 = x.astype(BF16).astype(F32)
    r1 = x - p1
    p2 = r1.astype(BF16).astype(F32)
    p3 = r1 - p2
    return p1, p2, p3


def _proj_kernel(x_ref, g_ref, w_ref, bfl_ref, *refs, tm, seq_len, aug):
    if aug:
        (fkt_ref, fvt_ref, lft_ref, dkt_ref, dv4_ref, qa_ref, ka_ref, vtf_ref,
         fz_ref, dq_ref, dkb_ref, vtd_ref, dz_ref, ga_ref, gb_ref, carry_ref) = refs
    else:
        (fk_ref, fv_ref, lf_ref, dk_ref, dv_ref, fq_ref,
         fz_ref, dq_ref, dz_ref, ga_ref, gb_ref) = refs

    x = x_ref[...]
    ms = jnp.mean(x * x, axis=-1, keepdims=True)
    h = (x * lax.rsqrt(ms + NORM_EPS) * g_ref[...]).astype(BF16)

    def grp(off, n=WIDTH):
        return _mm(h, w_ref[:, off:off + n])

    if not aug:
        fq_ref[...] = (grp(_O_FQ) * SCALE).astype(BF16)
        fk_ref[...] = grp(_O_FK)
        fv_ref[...] = grp(_O_FV)
        fz_ref[...] = grp(_O_FZ).astype(BF16)
        dq_ref[...] = (grp(_O_DQ) * SCALE).astype(BF16)
        dk_ref[...] = grp(_O_DK)
        dv_ref[...] = grp(_O_DV)
        dz_ref[...] = grp(_O_DZ).astype(BF16)
        ga_ref[...] = _sigmoid(grp(_O_GA, D_MODEL)).astype(BF16)
        gb_ref[...] = _sigmoid(grp(_O_GB, D_MODEL)).astype(BF16)
        lf_ref[...] = _log_sigmoid(grp(_O_FL, LANES) + bfl_ref[...])[:, :FOX_HEADS]
        return

    lf = _log_sigmoid(grp(_O_FL, LANES) + bfl_ref[...])
    lft_ref[0] = lf.T[:FOX_HEADS, :]
    row = lax.broadcasted_iota(jnp.int32, (tm, LANES), 0)
    c = lf
    shift = 1
    while shift < tm:
        c = c + jnp.where(row >= shift, pltpu.roll(c, shift, 0), 0.0)
        shift *= 2
    tiles_per_seq = seq_len // tm

    @pl.when(pl.program_id(0) % tiles_per_seq == 0)
    def _():
        carry_ref[...] = jnp.zeros_like(carry_ref)

    c = c + carry_ref[...]
    carry_ref[...] = c[tm - 1:tm, :]
    c_hi, c_mid, c_lo = _split3(c * LOG2E)

    fq = grp(_O_FQ) * (SCALE * LOG2E)
    fk = grp(_O_FK)
    fv = grp(_O_FV)
    fvt = fv.T
    fkt_ref[0] = fk.T
    fvt_ref[0] = fvt
    tail = jnp.where(lax.broadcasted_iota(jnp.int32, (VT_PAD, tm), 0) == 0, 1.0, 0.0).astype(BF16)
    lane = lax.broadcasted_iota(jnp.int32, (tm, LANES), 1)
    pos = lane & (FL_SLOT - 1)
    q_slots = jnp.where(pos == 0, c_hi, jnp.where(pos == 1, c_mid, jnp.where(
        pos == 2, c_lo, jnp.where(pos < 6, 1.0, 0.0))))
    k_slots = jnp.where(pos < 3, 1.0, jnp.where(pos == 3, -c_hi, jnp.where(
        pos == 4, -c_mid, jnp.where(pos == 5, -c_lo, 0.0))))
    for hd in range(FOX_HEADS):
        blk, par = divmod(hd, 2)
        sl = slice(blk * LANES, (blk + 1) * LANES)
        data = (lane < HD) if par == 0 else (lane >= HD)
        slot0 = (1 - par) * HD + hd * FL_SLOT
        own = (lane >= slot0) & (lane < slot0 + FL_SLOT)
        q_aug = jnp.where(own, q_slots, 0.0)
        k_aug = jnp.where(own, k_slots, 0.0)
        osl = slice(hd * LANES, (hd + 1) * LANES)
        qa_ref[:, osl] = jnp.where(data, fq[:, sl], q_aug).astype(BF16)
        ka_ref[:, osl] = jnp.where(data, fk[:, sl], k_aug).astype(BF16)
        base = hd * FVT_ROWS
        vtf_ref[0, base:base + HD, :] = fvt[hd * HD:(hd + 1) * HD, :].astype(BF16)
        vtf_ref[0, base + HD:base + FVT_ROWS, :] = tail

    ga_ref[...] = _sigmoid(grp(_O_GA, D_MODEL)).astype(BF16)
    gb_ref[...] = _sigmoid(grp(_O_GB, D_MODEL)).astype(BF16)

    dk = grp(_O_DK)
    dkt_ref[0] = dk.T
    dkb_ref[...] = dk.astype(BF16)
    dv = grp(_O_DV)
    dv4_ref[...] = dv.reshape(tm, DIFF_HEADS, 2 * HD)
    for hd in range(DIFF_HEADS):
        base = hd * VT_ROWS
        vtd_ref[0, base:base + LANES, :] = dv[:, hd * LANES:(hd + 1) * LANES].T.astype(BF16)
        vtd_ref[0, base + LANES:base + VT_ROWS, :] = tail

    dq_ref[...] = (grp(_O_DQ) * (SCALE * LOG2E)).astype(BF16)
    fz_ref[...] = grp(_O_FZ).astype(BF16)
    dz_ref[...] = grp(_O_DZ).astype(BF16)


def _project(x2d, g, w_all, bfl, *, seq_len, aug, tm):
    rows = x2d.shape[0]
    assert rows % tm == 0
    row_spec = lambda n: pl.BlockSpec((tm, n), lambda i: (i, 0))
    const = lambda shape: pl.BlockSpec(shape, lambda i: (0, 0))
    f32o = lambda n: jax.ShapeDtypeStruct((rows, n), F32)
    b16o = lambda n: jax.ShapeDtypeStruct((rows, n), BF16)
    if aug:
        assert seq_len % tm == 0
        wide = FOX_HEADS * LANES
        tps = seq_len // tm
        t_shape = lambda n, dt: jax.ShapeDtypeStruct((rows // seq_len, n, seq_len), dt)
        t_spec = lambda n: pl.BlockSpec((1, n, tm), lambda i: (i // tps, 0, i % tps))
        out_shape = [t_shape(WIDTH, F32), t_shape(WIDTH, F32), t_shape(FOX_HEADS, F32), t_shape(WIDTH, F32),
                     jax.ShapeDtypeStruct((rows, DIFF_HEADS, 2 * HD), F32)]
        out_specs = [t_spec(WIDTH), t_spec(WIDTH), t_spec(FOX_HEADS), t_spec(WIDTH),
                     pl.BlockSpec((tm, DIFF_HEADS, 2 * HD), lambda i: (i, 0, 0))]
        out_shape += ([b16o(wide)] * 2 + [t_shape(FOX_HEADS * FVT_ROWS, BF16)] + [b16o(WIDTH)] * 3
                      + [t_shape(DIFF_HEADS * VT_ROWS, BF16)] + [b16o(WIDTH)] + [b16o(D_MODEL)] * 2)
        out_specs += ([row_spec(wide)] * 2 + [t_spec(FOX_HEADS * FVT_ROWS)] + [row_spec(WIDTH)] * 3
                      + [t_spec(DIFF_HEADS * VT_ROWS)] + [row_spec(WIDTH)] + [row_spec(D_MODEL)] * 2)
        scratch = [pltpu.VMEM((1, LANES), F32)]
    else:
        out_shape = [f32o(WIDTH), f32o(WIDTH), f32o(FOX_HEADS), f32o(WIDTH), f32o(WIDTH)]
        out_specs = [row_spec(WIDTH), row_spec(WIDTH), row_spec(FOX_HEADS), row_spec(WIDTH), row_spec(WIDTH)]
        out_shape += [b16o(WIDTH)] * 4 + [b16o(D_MODEL)] * 2
        out_specs += [row_spec(WIDTH)] * 4 + [row_spec(D_MODEL)] * 2
        scratch = []
    return pl.pallas_call(
        functools.partial(_proj_kernel, tm=tm, seq_len=seq_len, aug=aug),
        grid=(rows // tm,),
        in_specs=[row_spec(D_MODEL), const((1, D_MODEL)),
                  pl.BlockSpec((D_MODEL, _N_ALL), lambda i: (0, 0), pipeline_mode=pl.Buffered(1)),
                  const((1, LANES))],
        out_specs=out_specs,
        out_shape=out_shape,
        scratch_shapes=scratch,
        compiler_params=pltpu.CompilerParams(
            dimension_semantics=("arbitrary",), vmem_limit_bytes=VMEM_LIMIT),
        name="proj_aug" if aug else "proj_plain",
    )(x2d, g, w_all, bfl)


def _bias_kernel(tbl_ref, o_ref, *, q_start, nq, segments, transposed, scale):
    hd = pl.program_id(0)
    far = tbl_ref[N_BUCKETS // 2 - 1, hd]
    q_axis, k_axis = (1, 0) if transposed else (0, 1)
    off = 0
    for k_start, nk, valid in segments:
        shape = (nk, nq) if transposed else (nq, nk)
        qpos = q_start + lax.broadcasted_iota(jnp.int32, shape, q_axis)
        kidx = lax.broadcasted_iota(jnp.int32, shape, k_axis)
        kpos = k_start + kidx
        rel = kpos - qpos
        n = jnp.abs(rel)
        large = jnp.full(shape, _MAX_EXACT, jnp.int32)
        for thr in _T5_THR:
            large = large + jnp.where(n >= thr, 1, 0)
        bucket = jnp.where(rel > 0, N_BUCKETS // 2, 0) + jnp.where(n < _MAX_EXACT, n, large)
        val = jnp.zeros(shape, F32)
        for b in range(N_BUCKETS):
            val = jnp.where(bucket == b, tbl_ref[b, hd], val)
        val = (val - far) * scale
        visible = ((kpos >> CHUNK_SHIFT) <= (qpos >> CHUNK_SHIFT)) & (kidx < valid)
        val = jnp.where(visible, val, NEG)
        if transposed:
            o_ref[0, off:off + nk, :] = val
        else:
            o_ref[0, :, off:off + nk] = val
        off += nk


def _bias_tiles(rel_bias, *, q_start, nq, segments, transposed, scale):
    total = sum(s[1] for s in segments)
    shape = (total, nq) if transposed else (nq, total)
    return pl.pallas_call(
        functools.partial(_bias_kernel, q_start=q_start, nq=nq, segments=segments, transposed=transposed,
                          scale=scale),
        grid=(DIFF_HEADS,),
        in_specs=[pl.BlockSpec(memory_space=pltpu.SMEM)],
        out_specs=pl.BlockSpec((1,) + shape, lambda h: (h, 0, 0)),
        out_shape=jax.ShapeDtypeStruct((DIFF_HEADS,) + shape, F32),
        name="t5_bias",
    )(rel_bias)


def _flash_t(k_ref, vt_ref, acc_ref, bad_ref, streams, qi, blk, prev_bias, diag_bias, diag_mask):
    n = len(streams)

    def scores(first_blk, width):
        start = pl.multiple_of(first_blk * blk, blk)
        return [_mm_nt(k_ref[0, pl.ds(start, width), ksl], q) for q, ksl, _ in streams]

    def step(first_blk, width, ms, biases, mask):
        start = pl.multiple_of(first_blk * blk, blk)
        sts = scores(first_blk, width)
        out, ps, alphas = [], [], []
        for s, st in enumerate(sts):
            if biases is not None:
                st = st + biases[s]()
            if mask is not None:
                st = jnp.where(mask, st, NEG)
            m_new = jnp.maximum(ms[s], jnp.max(st, axis=0, keepdims=True))
            ps.append(jnp.exp2(st - m_new).astype(BF16))
            alphas.append(jnp.exp2(ms[s] - m_new))
            out.append(m_new)
        for s, (_, _, vsl) in enumerate(streams):
            acc_ref[s] = alphas[s] * acc_ref[s] + _mm(vt_ref[0, vsl, pl.ds(start, width)], ps[s])
        return tuple(out)

    def first_step():
        start = pl.multiple_of(qi * blk, blk)
        out, ps = [], []
        for s, st in enumerate(scores(qi, blk)):
            if diag_bias is not None:
                st = st + diag_bias[s]()
            if diag_mask is not None:
                st = jnp.where(diag_mask, st, NEG)
            m = jnp.max(st, axis=0, keepdims=True)
            ps.append(jnp.exp2(st - m).astype(BF16))
            out.append(m)
        for s, (_, _, vsl) in enumerate(streams):
            acc_ref[s] = _mm(vt_ref[0, vsl, pl.ds(start, blk)], ps[s])
        return tuple(out)

    ms = first_step()
    bad_ref[...] = jnp.zeros_like(bad_ref)

    def consume(sts, first_blk, width):
        start = pl.multiple_of(first_blk * blk, blk)
        ps = [jnp.exp2(st - ms[s]).astype(BF16) for s, st in enumerate(sts)]
        worst = None
        for p in ps:
            d = jnp.max(p, axis=0, keepdims=True)
            worst = d if worst is None else jnp.maximum(worst, d)
        bad_ref[...] = jnp.maximum(bad_ref[...], worst.astype(F32))
        for s, (_, _, vsl) in enumerate(streams):
            acc_ref[s] = acc_ref[s] + _mm(vt_ref[0, vsl, pl.ds(start, width)], ps[s])

    if prev_bias is not None:
        @pl.when(qi >= 1)
        def _():
            consume([st + prev_bias[s]() for s, st in enumerate(scores(qi - 1, blk))], qi - 1, blk)

    n_plain = qi if prev_bias is None else jnp.maximum(qi - 1, 0)
    done = 0
    for w in STEP_BLOCKS:
        count = (n_plain - done) // w

        def fast_step(j, carry, w=w, done=done):
            consume(scores(done + j * w, w * blk), done + j * w, w * blk)
            return carry

        lax.fori_loop(0, count, fast_step, 0)
        done = done + count * w

    @pl.when(jnp.max(bad_ref[...]) > 2.0 ** HEADROOM)
    def _():
        c = first_step()
        if prev_bias is not None:
            c = lax.cond(qi >= 1, lambda c: step(qi - 1, blk, c, prev_bias, None), lambda c: c, c)
        lax.fori_loop(0, n_plain, lambda j, c: step(j, blk, c, None, None), c)


def _fox_kernel(qa_ref, ka_ref, vt_ref, z_ref, *rest, blk, nh, sub, nq):
    sample_in, (o_ref, so_ref, acc_ref, bad_ref) = rest[:8], rest[8:]
    key = lax.broadcasted_iota(jnp.int32, (blk, blk), 0)
    qry = lax.broadcasted_iota(jnp.int32, (blk, blk), 1)
    hsl = [slice(hd * LANES, (hd + 1) * LANES) for hd in range(nh)]

    def query_block(i, carry):
        qi = pl.program_id(2) * sub + i
        rows = pl.ds(pl.multiple_of(i * blk, blk), blk)
        streams = [(qa_ref[0, rows, sl], sl, slice(hd * FVT_ROWS, (hd + 1) * FVT_ROWS))
                   for hd, sl in enumerate(hsl)]
        _flash_t(ka_ref, vt_ref, acc_ref, bad_ref, streams, qi, blk, None, None, key <= qry)
        for pair in range(nh // 2):
            halves = []
            for par in range(2):
                acc = acc_ref[2 * pair + par]
                halves.append(acc[0:HD, :] / acc[HD:HD + 1, :])
            o = jnp.concatenate(halves, axis=0).T
            o_ref[0, rows, hsl[pair]] = (o * _silu(z_ref[0, rows, hsl[pair]].astype(F32))).astype(BF16)
        return carry

    lax.fori_loop(0, sub, query_block, 0)
    _fox_sample_kernel(*sample_in, so_ref, nq=nq)


def _fox_attention(qa, ka, vt, fz, sample_args, *, blk, nh, sub, nq):
    b, t, _ = qa.shape
    step = sub * blk
    per_seq = t // step
    nb, _, past = sample_args[1].shape
    assert nh == FOX_HEADS and nb == b * per_seq
    row = lambda b, p, i: b * per_seq + i
    new = lambda: pl.BlockSpec((nq, WIDTH), lambda b, p, i: (row(b, p, i), 0))
    cache = lambda r, c: pl.BlockSpec((1, r, c), lambda b, p, i: (row(b, p, i), 0, 0))
    return pl.pallas_call(
        functools.partial(_fox_kernel, blk=blk, nh=nh, sub=sub, nq=nq),
        grid=(b, FOX_HEADS // nh, per_seq),
        in_specs=[pl.BlockSpec((1, step, nh * LANES), lambda b, p, i: (b, i, p)),
                  pl.BlockSpec((1, t, nh * LANES), lambda b, p, i: (b, 0, p)),
                  pl.BlockSpec((1, nh * FVT_ROWS, t), lambda b, p, i: (b, p, 0)),
                  pl.BlockSpec((1, step, nh * HD), lambda b, p, i: (b, i, p)),
                  new(), cache(WIDTH, past), cache(WIDTH, past), new(), new(),
                  cache(FOX_HEADS, past), cache(FOX_HEADS, LANES), new()],
        out_specs=[pl.BlockSpec((1, step, nh * HD), lambda b, p, i: (b, i, p)), new()],
        out_shape=[jax.ShapeDtypeStruct((b, t, WIDTH), BF16), jax.ShapeDtypeStruct((nb * nq, WIDTH), BF16)],
        scratch_shapes=[pltpu.VMEM((nh, FVT_ROWS, blk), F32), pltpu.VMEM((1, blk), F32)],
        compiler_params=pltpu.CompilerParams(
            dimension_semantics=("parallel", "parallel", "arbitrary"), vmem_limit_bytes=VMEM_LIMIT),
        name="fox_prompt",
    )(qa, ka, vt, fz, *sample_args)


def _lambda(lq1_ref, lk1_ref, lq2_ref, lk2_ref):
    s1 = jnp.sum(lq1_ref[...] * lk1_ref[...], axis=-1, keepdims=True)
    s2 = jnp.sum(lq2_ref[...] * lk2_ref[...], axis=-1, keepdims=True)
    return jnp.exp(s1) - jnp.exp(s2) + LAMBDA_INIT


def _diff_finish(o, subg, z):
    ms = jnp.mean(o * o, axis=-1, keepdims=True)
    od = o * lax.rsqrt(ms + NORM_EPS) * subg * (1.0 - LAMBDA_INIT)
    return (od * _silu(z)).astype(BF16)


def _diff_kernel(q_ref, k_ref, vt_ref, bias_ref, z_ref, lq1_ref, lk1_ref, lq2_ref, lk2_ref, subg_ref,
                 *rest, blk, nh, sub, nq, past):
    sample_in, (o_ref, so_ref, acc_ref, ms_ref, bad_ref) = rest[:7], rest[7:]
    lane = lax.broadcasted_iota(jnp.int32, (blk, LANES), 1)
    lam = _lambda(lq1_ref, lk1_ref, lq2_ref, lk2_ref)
    n = 2 * nh
    qi0 = pl.program_id(2) * sub
    ksl = [slice((s // 2) * LANES, (s // 2 + 1) * LANES) for s in range(n)]
    vsl = [slice((s // 2) * VT_ROWS, (s // 2 + 1) * VT_ROWS) for s in range(n)]

    def queries(rows):
        out = []
        for s in range(n):
            q = q_ref[0, rows, ksl[s]]
            out.append(jnp.where((lane < HD) if s % 2 == 0 else (lane >= HD), q, jnp.zeros_like(q)))
        return out

    def scores(qs, blk0, width):
        start = pl.multiple_of(blk0 * blk, blk)
        return [_mm_nt(k_ref[0, pl.ds(start, width), ksl[s]], qs[s]) for s in range(n)]

    def values(s, blk0, width):
        return vt_ref[0, vsl[s], pl.ds(pl.multiple_of(blk0 * blk, blk), width)]

    def prev_bias(s):
        return bias_ref[s // 2, 0:blk, :]

    def diag_bias(s):
        return bias_ref[s // 2, blk:2 * blk, :]

    qs_all = [queries(slice(j * blk, (j + 1) * blk)) for j in range(sub)]
    for group in range(0, sub, FIRST_STEP_GROUP):
        js = range(group, min(group + FIRST_STEP_GROUP, sub))
        diag = {j: scores(qs_all[j], qi0 + j, blk) for j in js}
        ms_all, p_diag = {}, {}
        for j in js:
            ms_all[j], p_diag[j] = [], []
            for s in range(n):
                st = diag[j][s] + diag_bias(s)
                m = jnp.max(st, axis=0, keepdims=True)
                p_diag[j].append(jnp.exp2(st - m).astype(BF16))
                ms_all[j].append(m)
                ms_ref[j, s] = m
        prev = {j: scores(qs_all[j], qi0 + j - 1, blk) for j in js if j >= 1}
        p_prev = {j: [jnp.exp2(prev[j][s] + prev_bias(s) - ms_all[j][s]).astype(BF16) for s in range(n)]
                  for j in prev}
        for j in js:
            worst = None
            for s in range(n):
                acc = _mm(values(s, qi0 + j, blk), p_diag[j][s])
                if j >= 1:
                    acc = acc + _mm(values(s, qi0 + j - 1, blk), p_prev[j][s])
                    d = jnp.max(p_prev[j][s], axis=0, keepdims=True)
                    worst = d if worst is None else jnp.maximum(worst, d)
                acc_ref[j, s] = acc
            bad_ref[j] = jnp.zeros((1, blk), F32) if worst is None else worst.astype(F32)

    def consume(j, ms, sts, blk0, width, biased):
        ps = [jnp.exp2((sts[s] + prev_bias(s) if biased else sts[s]) - ms[s]).astype(BF16) for s in range(n)]
        worst = None
        for p in ps:
            d = jnp.max(p, axis=0, keepdims=True)
            worst = d if worst is None else jnp.maximum(worst, d)
        bad_ref[j] = jnp.maximum(bad_ref[j], worst.astype(F32))
        for s in range(n):
            acc_ref[j, s] = acc_ref[j, s] + _mm(values(s, blk0, width), ps[s])

    @pl.when(qi0 >= 1)
    def _():
        ms0 = [ms_ref[0, s] for s in range(n)]
        consume(0, ms0, scores(qs_all[0], qi0 - 1, blk), qi0 - 1, blk, True)

    def query_block(j, carry):
        qi = qi0 + j
        qs = queries(pl.ds(pl.multiple_of(j * blk, blk), blk))
        ms = [ms_ref[j, s] for s in range(n)]
        n_plain = jnp.maximum(qi - 1, 0)
        done = 0
        for w in STEP_BLOCKS:
            count = (n_plain - done) // w

            def fast_step(i, c, w=w, done=done):
                consume(j, ms, scores(qs, done + i * w, w * blk), done + i * w, w * blk, False)
                return c

            lax.fori_loop(0, count, fast_step, 0)
            done = done + count * w

        @pl.when(jnp.max(bad_ref[j]) > 2.0 ** HEADROOM)
        def _():
            def exact_step(blk0, ms_run, bias):
                out, ps, alphas = [], [], []
                for s, st in enumerate(scores(qs, blk0, blk)):
                    if bias is not None:
                        st = st + bias(s)
                    m_new = jnp.maximum(ms_run[s], jnp.max(st, axis=0, keepdims=True))
                    ps.append(jnp.exp2(st - m_new).astype(BF16))
                    alphas.append(jnp.exp2(ms_run[s] - m_new))
                    out.append(m_new)
                for s in range(n):
                    acc_ref[j, s] = alphas[s] * acc_ref[j, s] + _mm(values(s, blk0, blk), ps[s])
                return tuple(out)

            acc_ref[j] = jnp.zeros(acc_ref.shape[1:], F32)
            c = exact_step(qi, tuple(jnp.full((1, blk), NEG, F32) for _ in range(n)), diag_bias)
            c = lax.cond(qi >= 1, lambda c: exact_step(qi - 1, c, prev_bias), lambda c: c, c)
            lax.fori_loop(0, n_plain, lambda i, c: exact_step(i, c, None), c)

        return carry

    lax.fori_loop(0, sub, query_block, 0)

    for j in range(sub):
        rows = slice(j * blk, (j + 1) * blk)
        for hd in range(nh):
            a0, a1 = acc_ref[j, 2 * hd], acc_ref[j, 2 * hd + 1]
            o_t = a0[0:LANES, :] / a0[LANES:LANES + 1, :] - lam * (a1[0:LANES, :] / a1[LANES:LANES + 1, :])
            sl = slice(hd * LANES, (hd + 1) * LANES)
            o_ref[0, rows, sl] = _diff_finish(o_t.T, subg_ref[...], z_ref[0, rows, sl].astype(F32))

    _diff_sample_kernel(*sample_in, lq1_ref, lk1_ref, lq2_ref, lk2_ref, subg_ref, so_ref, nq=nq, past=past)


def _diff_attention(dq, dk, dvt, bias, dz, lq1, lk1, lq2, lk2, subg, sample_args, *, blk, nh, sub, nq):
    b, t, _ = dq.shape
    step = sub * blk
    per_seq = t // step
    nb, _, past = sample_args[1].shape
    assert nh == DIFF_HEADS and nb == b * per_seq
    small = lambda n: pl.BlockSpec((1, n), lambda b, h, i: (0, 0))
    row = lambda b, h, i: b * per_seq + i
    new = lambda: pl.BlockSpec((nq, WIDTH), lambda b, h, i: (row(b, h, i), 0))
    cache = lambda r, c: pl.BlockSpec((1, r, c), lambda b, h, i: (row(b, h, i), 0, 0))
    return pl.pallas_call(
        functools.partial(_diff_kernel, blk=blk, nh=nh, sub=sub, nq=nq, past=past),
        grid=(b, DIFF_HEADS // nh, per_seq),
        in_specs=[pl.BlockSpec((1, step, nh * LANES), lambda b, h, i: (b, i, h)),
                  pl.BlockSpec((1, t, nh * LANES), lambda b, h, i: (b, 0, h)),
                  pl.BlockSpec((1, nh * VT_ROWS, t), lambda b, h, i: (b, h, 0)),
                  pl.BlockSpec((nh, 2 * blk, blk), lambda b, h, i: (h, 0, 0)),
                  pl.BlockSpec((1, step, nh * LANES), lambda b, h, i: (b, i, h)),
                  small(HD), small(HD), small(HD), small(HD), small(2 * HD),
                  new(), cache(WIDTH, past), cache(past * DIFF_HEADS, 2 * HD), new(), new(),
                  pl.BlockSpec((DIFF_HEADS, nq, past + LANES), lambda b, h, i: (0, 0, 0)), new()],
        out_specs=[pl.BlockSpec((1, step, nh * LANES), lambda b, h, i: (b, i, h)), new()],
        out_shape=[jax.ShapeDtypeStruct((b, t, WIDTH), BF16), jax.ShapeDtypeStruct((nb * nq, WIDTH), BF16)],
        scratch_shapes=[pltpu.VMEM((sub, 2 * nh, VT_ROWS, blk), F32), pltpu.VMEM((sub, 2 * nh, 1, blk), F32),
                        pltpu.VMEM((sub, 1, blk), F32)],
        compiler_params=pltpu.CompilerParams(
            dimension_semantics=("parallel", "parallel", "arbitrary"), vmem_limit_bytes=VMEM_LIMIT),
        name="diff_prompt",
    )(dq, dk, dvt, bias, dz, lq1, lk1, lq2, lk2, subg, *sample_args)


def _suffix_sums(x):
    n = x.shape[1] // LANES
    lane = lax.broadcasted_iota(jnp.int32, (x.shape[0], LANES), 1)
    after = jnp.zeros((x.shape[0], 1), F32)
    pieces = [None] * n
    for blk in reversed(range(n)):
        piece = x[:, blk * LANES:(blk + 1) * LANES]
        y = piece
        shift = 1
        while shift < LANES:
            y = y + jnp.where(lane + shift < LANES, pltpu.roll(y, LANES - shift, 1), 0.0)
            shift *= 2
        pieces[blk] = y - piece + after
        after = after + y[:, 0:1]
    return jnp.concatenate(pieces, axis=1), after


def _pad_rows(x, rows):
    return jnp.concatenate([x, jnp.zeros((rows - x.shape[0], x.shape[1]), x.dtype)], axis=0)


def _joint_softmax(s_p, s_n):
    m = jnp.maximum(jnp.max(s_p, axis=-1, keepdims=True), jnp.max(s_n, axis=-1, keepdims=True))
    p_p = jnp.exp(s_p - m)
    p_n = jnp.exp(s_n - m)
    l = jnp.sum(p_p, axis=-1, keepdims=True) + jnp.sum(p_n, axis=-1, keepdims=True)
    return p_p.astype(BF16), p_n.astype(BF16), l


def _joint_pv(p_p, p_n, l, vt_p, v_n):
    return (_mm_nt(p_p, vt_p) + _mm(p_n, v_n)) / l


def _fox_sample_kernel(q_ref, kt_ref, vt_ref, kn_ref, vn_ref, lfp_ref, lfn_ref, z_ref, o_ref, *, nq):
    r_new, total_new = _suffix_sums(lfn_ref[0])
    r_past, _ = _suffix_sums(lfp_ref[0])
    r_past = r_past + total_new
    lane = lax.broadcasted_iota(jnp.int32, (nq, LANES), 1)
    lane2 = lax.broadcasted_iota(jnp.int32, (2 * nq, LANES), 1)
    row2 = lax.broadcasted_iota(jnp.int32, (2 * nq, LANES), 0)
    upper = row2 >= nq
    causal = lane2 <= jnp.where(upper, row2 - nq, row2)
    sls = [slice(blk * LANES, (blk + 1) * LANES) for blk in range(PAIRS)]
    scores = []
    for blk, sl in enumerate(sls):
        q = q_ref[:, sl]
        zero = jnp.zeros_like(q)
        q2 = jnp.concatenate([jnp.where(lane < HD, q, zero), jnp.where(lane >= HD, q, zero)], axis=0)
        kt = kt_ref[0, sl, :].astype(BF16)
        kn = _pad_rows(kn_ref[:, sl], LANES).astype(BF16)
        past_rows = lax.broadcasted_iota(jnp.int32, (2 * nq, r_past.shape[1]), 0) >= nq
        bias_p = jnp.where(past_rows, r_past[2 * blk + 1:2 * blk + 2, :], r_past[2 * blk:2 * blk + 1, :])
        bias_n = jnp.where(upper, r_new[2 * blk + 1:2 * blk + 2, :], r_new[2 * blk:2 * blk + 1, :])
        scores.append((_mm(q2, kt) + bias_p, jnp.where(causal, _mm_nt(q2, kn) + bias_n, NEG)))
    weights = [_joint_softmax(s_p, s_n) for s_p, s_n in scores]
    for blk, sl in enumerate(sls):
        vt = vt_ref[0, sl, :].astype(BF16)
        vn = _pad_rows(vn_ref[:, sl], LANES).astype(BF16)
        o2 = _joint_pv(*weights[blk], vt, vn)
        o = jnp.where(lane < HD, o2[0:nq, :], o2[nq:2 * nq, :])
        o_ref[:, sl] = (o * _silu(z_ref[:, sl].astype(F32))).astype(BF16)


def _diff_sample_kernel(q_ref, kt_ref, v_ref, kn_ref, vn_ref, bias_ref, z_ref,
                        lq1_ref, lk1_ref, lq2_ref, lk2_ref, subg_ref, o_ref, *, nq, past):
    lane = lax.broadcasted_iota(jnp.int32, (nq, LANES), 1)
    lam = _lambda(lq1_ref, lk1_ref, lq2_ref, lk2_ref)
    sls = [slice(hd * LANES, (hd + 1) * LANES) for hd in range(DIFF_HEADS)]
    scores = []
    for hd, sl in enumerate(sls):
        q = q_ref[:, sl]
        kt = kt_ref[0, sl, :].astype(BF16)
        kn = _pad_rows(kn_ref[:, sl], LANES).astype(BF16)
        for mp in range(2):
            qm = jnp.where((lane < HD) if mp == 0 else (lane >= HD), q, jnp.zeros_like(q))
            scores.append((_mm(qm, kt) + bias_ref[hd, :, 0:past],
                           _mm_nt(qm, kn) + bias_ref[hd, :, past:past + LANES]))
    weights = [_joint_softmax(s_p, s_n) for s_p, s_n in scores]
    for hd, sl in enumerate(sls):
        vt = v_ref[0, pl.ds(hd, past, stride=DIFF_HEADS), :].T.astype(BF16)
        vn = _pad_rows(vn_ref[:, sl], LANES).astype(BF16)
        outs = [_joint_pv(*weights[2 * hd + mp], vt, vn) for mp in range(2)]
        o_ref[:, sl] = _diff_finish(outs[0] - lam * outs[1], subg_ref[...], z_ref[:, sl].astype(F32))


def _out_kernel(yf_ref, yd_ref, ga_ref, gb_ref, x_ref, wuf_ref, wud_ref, wo_ref, g_ref, o_ref):
    merged = (ga_ref[...].astype(F32) * _mm(yf_ref[...], wuf_ref[...])
              + gb_ref[...].astype(F32) * _mm(yd_ref[...], wud_ref[...]))
    out = x_ref[...] + _mm(merged.astype(BF16), wo_ref[...])
    ms = jnp.mean(out * out, axis=-1, keepdims=True)
    o_ref[...] = out * lax.rsqrt(ms + NORM_EPS) * g_ref[...]


def _output(yf, yd, ga, gb, x2d, wuf, wud, wo, g, *, tm):
    rows = x2d.shape[0]
    assert rows % tm == 0
    row_spec = lambda n: pl.BlockSpec((tm, n), lambda i: (i, 0))
    const = lambda shape: pl.BlockSpec(shape, lambda i: (0, 0))
    return pl.pallas_call(
        _out_kernel,
        grid=(rows // tm,),
        in_specs=[row_spec(WIDTH), row_spec(WIDTH), row_spec(D_MODEL), row_spec(D_MODEL), row_spec(D_MODEL),
                  const((WIDTH, D_MODEL)), const((WIDTH, D_MODEL)), const((D_MODEL, D_MODEL)),
                  const((1, D_MODEL))],
        out_specs=row_spec(D_MODEL),
        out_shape=jax.ShapeDtypeStruct((rows, D_MODEL), F32),
        compiler_params=pltpu.CompilerParams(
            dimension_semantics=("parallel",), vmem_limit_bytes=VMEM_LIMIT),
        name="out_proj",
    )(yf, yd, ga, gb, x2d, wuf, wud, wo, g)


def _forget_lane_heads():
    lanes = np.arange(LANES)
    heads = (lanes % HD) // FL_SLOT
    heads[:FOX_HEADS] = np.arange(FOX_HEADS)
    return heads


def _pack_w_in(w_in, b_forget):
    sizes = (WIDTH, WIDTH, WIDTH, WIDTH, FOX_HEADS, WIDTH, WIDTH, WIDTH, WIDTH, D_MODEL, D_MODEL)
    offs = np.cumsum((0,) + sizes)
    fq, fk, fv, fz, fl, dq, dk, dv, dz, ga, gb = (w_in[:, offs[i]:offs[i + 1]] for i in range(len(sizes)))
    heads = _forget_lane_heads()
    w_all = jnp.concatenate([fq, fk, fv, fz, dq, dk, dv, dz, ga, gb, fl[:, heads]], axis=1).astype(BF16)
    return w_all, b_forget[heads][None, :]


def kernel(x_prompt, x_sample, cache_fox_k, cache_fox_v, cache_fox_logf, cache_diff_k, cache_diff_v,
           norm_in_g, w_in, b_forget, lambda_q1, lambda_k1, lambda_q2, lambda_k2, subln_g,
           w_up_fox, w_up_diff, w_o, rel_bias, final_norm_g):
    bp, tp, _ = x_prompt.shape
    bs, ts, _ = x_sample.shape
    past = cache_fox_k.shape[2]
    blk, tm = ATTN_BLOCK, ROW_TILE
    assert tp % (QUERY_BLOCKS_PER_STEP * blk) == 0 and blk > _T5_THR[-1] and blk % CHUNK == 0

    w_all, bfl = _pack_w_in(w_in[0], b_forget[0])
    g_in = norm_in_g[0][None, :]
    g_out = final_norm_g[None, :]
    wuf, wud, wo = w_up_fox[0].astype(BF16), w_up_diff[0].astype(BF16), w_o[0].astype(BF16)
    lq1, lk1, lq2, lk2 = (a[0][None, :] for a in (lambda_q1, lambda_k1, lambda_q2, lambda_k2))
    subg = subln_g[0][None, :]

    xs = x_sample.reshape(bs * ts, D_MODEL)
    (fk_s, fv_s, lf_s, dk_s, dv_s, fq_s, fz_s, dq_s, dz_s, ga_s, gb_s) = _project(
        xs, g_in, w_all, bfl, seq_len=ts, aug=False, tm=tm)
    xp = x_prompt.reshape(bp * tp, D_MODEL)
    (fkt, fvt, lft, dkt, dv4, qa, ka, vtf, fz, dq, dkb, vtd, dz, ga, gb) = _project(
        xp, g_in, w_all, bfl, seq_len=tp, aug=True, tm=tm)
    fox_k_p = jnp.transpose(fkt.reshape(bp, FOX_HEADS, HD, tp), (0, 3, 1, 2))[None]
    fox_v_p = jnp.transpose(fvt.reshape(bp, FOX_HEADS, HD, tp), (0, 3, 1, 2))[None]
    fox_lf_p = jnp.transpose(lft, (0, 2, 1))[None]
    diff_k_p = jnp.transpose(dkt.reshape(bp, DIFF_HEADS, 2, HD, tp), (0, 4, 1, 2, 3))[None]
    diff_v_p = dv4.reshape(1, bp, tp, DIFF_HEADS, 2 * HD)
    fox_kt = jnp.transpose(cache_fox_k[0], (0, 2, 3, 1)).reshape(bs, WIDTH, past)
    fox_vt = jnp.transpose(cache_fox_v[0], (0, 2, 3, 1)).reshape(bs, WIDTH, past)
    diff_kt = jnp.transpose(cache_diff_k[0], (0, 2, 3, 4, 1)).reshape(bs, WIDTH, past)
    diff_v = cache_diff_v[0].reshape(bs, past * DIFF_HEADS, 2 * HD)
    lf_past_t = jnp.swapaxes(cache_fox_logf[0], 1, 2)
    lf_new_t = jnp.pad(jnp.swapaxes(lf_s.reshape(bs, ts, FOX_HEADS), 1, 2), ((0, 0), (0, 0), (0, LANES - ts)))

    r3 = lambda a: a.reshape(bp, tp, a.shape[-1])
    yf, yf_s = _fox_attention(r3(qa), r3(ka), vtf, r3(fz),
                              (fq_s, fox_kt, fox_vt, fk_s, fv_s, lf_past_t, lf_new_t, fz_s),
                              blk=blk, nh=FOX_STEP_HEADS, sub=QUERY_BLOCKS_PER_STEP, nq=ts)
    bias_p = _bias_tiles(rel_bias, q_start=blk, nq=blk, segments=((0, blk, blk), (blk, blk, blk)),
                         transposed=True, scale=LOG2E)
    bias_s = _bias_tiles(rel_bias, q_start=past, nq=ts, segments=((0, past, past), (past, LANES, ts)),
                         transposed=False, scale=1.0)
    yd, yd_s = _diff_attention(r3(dq), r3(dkb), vtd, bias_p, r3(dz), lq1, lk1, lq2, lk2, subg,
                               (dq_s, diff_kt, diff_v, dk_s, dv_s, bias_s, dz_s),
                               blk=blk, nh=DIFF_STEP_HEADS, sub=QUERY_BLOCKS_PER_STEP, nq=ts)

    y_p = _output(yf.reshape(bp * tp, WIDTH), yd.reshape(bp * tp, WIDTH), ga, gb, xp, wuf, wud, wo, g_out,
                  tm=OUT_ROW_TILE)
    y_s = _output(yf_s, yd_s, ga_s, gb_s, xs, wuf, wud, wo, g_out, tm=tm)

    return (y_p.reshape(bp, tp, D_MODEL), y_s.reshape(bs, ts, D_MODEL),
            fox_k_p, fox_v_p, fox_lf_p, diff_k_p, diff_v_p,
            fk_s.reshape(1, bs, ts, FOX_HEADS, HD), fv_s.reshape(1, bs, ts, FOX_HEADS, HD),
            lf_s.reshape(1, bs, ts, FOX_HEADS),
            dk_s.reshape(1, bs, ts, DIFF_HEADS, 2, HD), dv_s.reshape(1, bs, ts, DIFF_HEADS, 2 * HD))
```

```python
import functools
import math

import numpy as np
import jax
import jax.numpy as jnp
from jax import lax
from jax.experimental import pallas as pl
from jax.experimental.pallas import tpu as pltpu

F32 = jnp.float32
BF16 = jnp.bfloat16

D_MODEL = 1024
HD = 64
FOX_HEADS = 8
DIFF_HEADS = 4
WIDTH = 512
CHUNK = 64
CHUNK_SHIFT = CHUNK.bit_length() - 1
N_BUCKETS = 32
MAX_DISTANCE = 128
NORM_EPS = 1e-6
SCALE = HD ** -0.5
NEG = -1e30
LAMBDA_INIT = 0.8 - 0.6 * math.exp(-0.3 * 0)
LOG2E = math.log2(math.e)

LANES = 128
PAIRS = WIDTH // LANES
VT_PAD = 16
VT_ROWS = LANES + VT_PAD
FVT_ROWS = HD + VT_PAD

_O_FQ, _O_FK, _O_FV, _O_FZ = 0, 512, 1024, 1536
_O_DQ, _O_DK, _O_DV, _O_DZ = 2048, 2560, 3072, 3584
_O_GA, _O_GB = 4096, 5120
_O_FL = 6144
_N_ALL = 6272
FL_SLOT = 8

VMEM_LIMIT = 58 * 1024 * 1024
ATTN_BLOCK = 256
ROW_TILE = 512
OUT_ROW_TILE = 1024

FOX_STEP_HEADS = 8
DIFF_STEP_HEADS = 4
HEADROOM = 64.0
FIRST_STEP_GROUP = 2
QUERY_BLOCKS_PER_STEP = 4
STEP_BLOCKS = (4, 2, 1)


def _t5_thresholds():
    nb = N_BUCKETS // 2
    max_exact = nb // 2
    n = np.arange(0, 4 * MAX_DISTANCE)
    large = max_exact + (np.log(np.maximum(n, 1).astype(np.float32) / max_exact)
                         / math.log(MAX_DISTANCE / max_exact) * (nb - max_exact)).astype(np.int32)
    large = np.minimum(large, nb - 1)
    thr = [int(np.argmax(large >= b)) for b in range(max_exact + 1, nb)]
    return max_exact, tuple(thr)


_MAX_EXACT, _T5_THR = _t5_thresholds()


def _mm(a, b):
    return jnp.dot(a, b, preferred_element_type=F32)


def _mm_nt(a, b):
    return lax.dot_general(a, b, (((1,), (1,)), ((), ())), preferred_element_type=F32)


def _log_sigmoid(x):
    return -(jnp.maximum(-x, 0.0) + jnp.log1p(jnp.exp(-jnp.abs(x))))


def _sigmoid(x):
    return 0.5 * jnp.tanh(0.5 * x) + 0.5


def _silu(x):
    return x * _sigmoid(x)


def _split3(x):
    p1 = x.astype(BF16).astype(F32)
    r1 = x - p1
    p2 = r1.astype(BF16).astype(F32)
    p3 = r1 - p2
    return p1, p2, p3


def _proj_kernel(x_ref, g_ref, w_ref, bfl_ref, *refs, tm, seq_len, aug):
    if aug:
        (fkt_ref, fvt_ref, lft_ref, dkt_ref, dv4_ref, qa_ref, ka_ref, vtf_ref,
         fz_ref, dq_ref, dkb_ref, vtd_ref, dz_ref, ga_ref, gb_ref, carry_ref) = refs
    else:
        (fk_ref, fv_ref, lf_ref, dk_ref, dv_ref, fq_ref,
         fz_ref, dq_ref, dz_ref, ga_ref, gb_ref) = refs

    x = x_ref[...]
    ms = jnp.mean(x * x, axis=-1, keepdims=True)
    h = (x * lax.rsqrt(ms + NORM_EPS) * g_ref[...]).astype(BF16)

    def grp(off, n=WIDTH):
        return _mm(h, w_ref[:, off:off + n])

    if not aug:
        fq_ref[...] = (grp(_O_FQ) * SCALE).astype(BF16)
        fk_ref[...] = grp(_O_FK)
        fv_ref[...] = grp(_O_FV)
        fz_ref[...] = grp(_O_FZ).astype(BF16)
        dq_ref[...] = (grp(_O_DQ) * SCALE).astype(BF16)
        dk_ref[...] = grp(_O_DK)
        dv_ref[...] = grp(_O_DV)
        dz_ref[...] = grp(_O_DZ).astype(BF16)
        ga_ref[...] = _sigmoid(grp(_O_GA, D_MODEL)).astype(BF16)
        gb_ref[...] = _sigmoid(grp(_O_GB, D_MODEL)).astype(BF16)
        lf_ref[...] = _log_sigmoid(grp(_O_FL, LANES) + bfl_ref[...])[:, :FOX_HEADS]
        return

    lf = _log_sigmoid(grp(_O_FL, LANES) + bfl_ref[...])
    lft_ref[0] = lf.T[:FOX_HEADS, :]
    row = lax.broadcasted_iota(jnp.int32, (tm, LANES), 0)
    c = lf
    shift = 1
    while shift < tm:
        c = c + jnp.where(row >= shift, pltpu.roll(c, shift, 0), 0.0)
        shift *= 2
    tiles_per_seq = seq_len // tm

    @pl.when(pl.program_id(0) % tiles_per_seq == 0)
    def _():
        carry_ref[...] = jnp.zeros_like(carry_ref)

    c = c + carry_ref[...]
    carry_ref[...] = c[tm - 1:tm, :]
    c_hi, c_mid, c_lo = _split3(c * LOG2E)

    fq = grp(_O_FQ) * (SCALE * LOG2E)
    fk = grp(_O_FK)
    fv = grp(_O_FV)
    fvt = fv.T
    fkt_ref[0] = fk.T
    fvt_ref[0] = fvt
    tail = jnp.where(lax.broadcasted_iota(jnp.int32, (VT_PAD, tm), 0) == 0, 1.0, 0.0).astype(BF16)
    lane = lax.broadcasted_iota(jnp.int32, (tm, LANES), 1)
    pos = lane & (FL_SLOT - 1)
    q_slots = jnp.where(pos == 0, c_hi, jnp.where(pos == 1, c_mid, jnp.where(
        pos == 2, c_lo, jnp.where(pos < 6, 1.0, 0.0))))
    k_slots = jnp.where(pos < 3, 1.0, jnp.where(pos == 3, -c_hi, jnp.where(
        pos == 4, -c_mid, jnp.where(pos == 5, -c_lo, 0.0))))
    for hd in range(FOX_HEADS):
        blk, par = divmod(hd, 2)
        sl = slice(blk * LANES, (blk + 1) * LANES)
        data = (lane < HD) if par == 0 else (lane >= HD)
        slot0 = (1 - par) * HD + hd * FL_SLOT
        own = (lane >= slot0) & (lane < slot0 + FL_SLOT)
        q_aug = jnp.where(own, q_slots, 0.0)
        k_aug = jnp.where(own, k_slots, 0.0)
        osl = slice(hd * LANES, (hd + 1) * LANES)
        qa_ref[:, osl] = jnp.where(data, fq[:, sl], q_aug).astype(BF16)
        ka_ref[:, osl] = jnp.where(data, fk[:, sl], k_aug).astype(BF16)
        base = hd * FVT_ROWS
        vtf_ref[0, base:base + HD, :] = fvt[hd * HD:(hd + 1) * HD, :].astype(BF16)
        vtf_ref[0, base + HD:base + FVT_ROWS, :] = tail

    ga_ref[...] = _sigmoid(grp(_O_GA, D_MODEL)).astype(BF16)
    gb_ref[...] = _sigmoid(grp(_O_GB, D_MODEL)).astype(BF16)

    dk = grp(_O_DK)
    dkt_ref[0] = dk.T
    dkb_ref[...] = dk.astype(BF16)
    dv = grp(_O_DV)
    dv4_ref[...] = dv.reshape(tm, DIFF_HEADS, 2 * HD)
    for hd in range(DIFF_HEADS):
        base = hd * VT_ROWS
        vtd_ref[0, base:base + LANES, :] = dv[:, hd * LANES:(hd + 1) * LANES].T.astype(BF16)
        vtd_ref[0, base + LANES:base + VT_ROWS, :] = tail

    dq_ref[...] = (grp(_O_DQ) * (SCALE * LOG2E)).astype(BF16)
    fz_ref[...] = grp(_O_FZ).astype(BF16)
    dz_ref[...] = grp(_O_DZ).astype(BF16)


def _project(x2d, g, w_all, bfl, *, seq_len, aug, tm):
    rows = x2d.shape[0]
    assert rows % tm == 0
    row_spec = lambda n: pl.BlockSpec((tm, n), lambda i: (i, 0))
    const = lambda shape: pl.BlockSpec(shape, lambda i: (0, 0))
    f32o = lambda n: jax.ShapeDtypeStruct((rows, n), F32)
    b16o = lambda n: jax.ShapeDtypeStruct((rows, n), BF16)
    if aug:
        assert seq_len % tm == 0
        wide = FOX_HEADS * LANES
        tps = seq_len // tm
        t_shape = lambda n, dt: jax.ShapeDtypeStruct((rows // seq_len, n, seq_len), dt)
        t_spec = lambda n: pl.BlockSpec((1, n, tm), lambda i: (i // tps, 0, i % tps))
        out_shape = [t_shape(WIDTH, F32), t_shape(WIDTH, F32), t_shape(FOX_HEADS, F32), t_shape(WIDTH, F32),
                     jax.ShapeDtypeStruct((rows, DIFF_HEADS, 2 * HD), F32)]
        out_specs = [t_spec(WIDTH), t_spec(WIDTH), t_spec(FOX_HEADS), t_spec(WIDTH),
                     pl.BlockSpec((tm, DIFF_HEADS, 2 * HD), lambda i: (i, 0, 0))]
        out_shape += ([b16o(wide)] * 2 + [t_shape(FOX_HEADS * FVT_ROWS, BF16)] + [b16o(WIDTH)] * 3
                      + [t_shape(DIFF_HEADS * VT_ROWS, BF16)] + [b16o(WIDTH)] + [b16o(D_MODEL)] * 2)
        out_specs += ([row_spec(wide)] * 2 + [t_spec(FOX_HEADS * FVT_ROWS)] + [row_spec(WIDTH)] * 3
                      + [t_spec(DIFF_HEADS * VT_ROWS)] + [row_spec(WIDTH)] + [row_spec(D_MODEL)] * 2)
        scratch = [pltpu.VMEM((1, LANES), F32)]
    else:
        out_shape = [f32o(WIDTH), f32o(WIDTH), f32o(FOX_HEADS), f32o(WIDTH), f32o(WIDTH)]
        out_specs = [row_spec(WIDTH), row_spec(WIDTH), row_spec(FOX_HEADS), row_spec(WIDTH), row_spec(WIDTH)]
        out_shape += [b16o(WIDTH)] * 4 + [b16o(D_MODEL)] * 2
        out_specs += [row_spec(WIDTH)] * 4 + [row_spec(D_MODEL)] * 2
        scratch = []
    return pl.pallas_call(
        functools.partial(_proj_kernel, tm=tm, seq_len=seq_len, aug=aug),
        grid=(rows // tm,),
        in_specs=[row_spec(D_MODEL), const((1, D_MODEL)),
                  pl.BlockSpec((D_MODEL, _N_ALL), lambda i: (0, 0), pipeline_mode=pl.Buffered(1)),
                  const((1, LANES))],
        out_specs=out_specs,
        out_shape=out_shape,
        scratch_shapes=scratch,
        compiler_params=pltpu.CompilerParams(
            dimension_semantics=("arbitrary",), vmem_limit_bytes=VMEM_LIMIT),
        name="proj_aug" if aug else "proj_plain",
    )(x2d, g, w_all, bfl)


def _bias_kernel(tbl_ref, o_ref, *, q_start, nq, segments, transposed, scale):
    hd = pl.program_id(0)
    far = tbl_ref[N_BUCKETS // 2 - 1, hd]
    q_axis, k_axis = (1, 0) if transposed else (0, 1)
    off = 0
    for k_start, nk, valid in segments:
        shape = (nk, nq) if transposed else (nq, nk)
        qpos = q_start + lax.broadcasted_iota(jnp.int32, shape, q_axis)
        kidx = lax.broadcasted_iota(jnp.int32, shape, k_axis)
        kpos = k_start + kidx
        rel = kpos - qpos
        n = jnp.abs(rel)
        large = jnp.full(shape, _MAX_EXACT, jnp.int32)
        for thr in _T5_THR:
            large = large + jnp.where(n >= thr, 1, 0)
        bucket = jnp.where(rel > 0, N_BUCKETS // 2, 0) + jnp.where(n < _MAX_EXACT, n, large)
        val = jnp.zeros(shape, F32)
        for b in range(N_BUCKETS):
            val = jnp.where(bucket == b, tbl_ref[b, hd], val)
        val = (val - far) * scale
        visible = ((kpos >> CHUNK_SHIFT) <= (qpos >> CHUNK_SHIFT)) & (kidx < valid)
        val = jnp.where(visible, val, NEG)
        if transposed:
            o_ref[0, off:off + nk, :] = val
        else:
            o_ref[0, :, off:off + nk] = val
        off += nk


def _bias_tiles(rel_bias, *, q_start, nq, segments, transposed, scale):
    total = sum(s[1] for s in segments)
    shape = (total, nq) if transposed else (nq, total)
    return pl.pallas_call(
        functools.partial(_bias_kernel, q_start=q_start, nq=nq, segments=segments, transposed=transposed,
                          scale=scale),
        grid=(DIFF_HEADS,),
        in_specs=[pl.BlockSpec(memory_space=pltpu.SMEM)],
        out_specs=pl.BlockSpec((1,) + shape, lambda h: (h, 0, 0)),
        out_shape=jax.ShapeDtypeStruct((DIFF_HEADS,) + shape, F32),
        name="t5_bias",
    )(rel_bias)


def _flash_t(k_ref, vt_ref, acc_ref, bad_ref, streams, qi, blk, prev_bias, diag_bias, diag_mask):
    n = len(streams)

    def scores(first_blk, width):
        start = pl.multiple_of(first_blk * blk, blk)
        return [_mm_nt(k_ref[0, pl.ds(start, width), ksl], q) for q, ksl, _ in streams]

    def step(first_blk, width, ms, biases, mask):
        start = pl.multiple_of(first_blk * blk, blk)
        sts = scores(first_blk, width)
        out, ps, alphas = [], [], []
        for s, st in enumerate(sts):
            if biases is not None:
                st = st + biases[s]()
            if mask is not None:
                st = jnp.where(mask, st, NEG)
            m_new = jnp.maximum(ms[s], jnp.max(st, axis=0, keepdims=True))
            ps.append(jnp.exp2(st - m_new).astype(BF16))
            alphas.append(jnp.exp2(ms[s] - m_new))
            out.append(m_new)
        for s, (_, _, vsl) in enumerate(streams):
            acc_ref[s] = alphas[s] * acc_ref[s] + _mm(vt_ref[0, vsl, pl.ds(start, width)], ps[s])
        return tuple(out)

    def first_step():
        start = pl.multiple_of(qi * blk, blk)
        out, ps = [], []
        for s, st in enumerate(scores(qi, blk)):
            if diag_bias is not None:
                st = st + diag_bias[s]()
            if diag_mask is not None:
                st = jnp.where(diag_mask, st, NEG)
            m = jnp.max(st, axis=0, keepdims=True)
            ps.append(jnp.exp2(st - m).astype(BF16))
            out.append(m)
        for s, (_, _, vsl) in enumerate(streams):
            acc_ref[s] = _mm(vt_ref[0, vsl, pl.ds(start, blk)], ps[s])
        return tuple(out)

    ms = first_step()
    bad_ref[...] = jnp.zeros_like(bad_ref)

    def consume(sts, first_blk, width):
        start = pl.multiple_of(first_blk * blk, blk)
        ps = [jnp.exp2(st - ms[s]).astype(BF16) for s, st in enumerate(sts)]
        worst = None
        for p in ps:
            d = jnp.max(p, axis=0, keepdims=True)
            worst = d if worst is None else jnp.maximum(worst, d)
        bad_ref[...] = jnp.maximum(bad_ref[...], worst.astype(F32))
        for s, (_, _, vsl) in enumerate(streams):
            acc_ref[s] = acc_ref[s] + _mm(vt_ref[0, vsl, pl.ds(start, width)], ps[s])

    if prev_bias is not None:
        @pl.when(qi >= 1)
        def _():
            consume([st + prev_bias[s]() for s, st in enumerate(scores(qi - 1, blk))], qi - 1, blk)

    n_plain = qi if prev_bias is None else jnp.maximum(qi - 1, 0)
    done = 0
    for w in STEP_BLOCKS:
        count = (n_plain - done) // w

        def fast_step(j, carry, w=w, done=done):
            consume(scores(done + j * w, w * blk), done + j * w, w * blk)
            return carry

        lax.fori_loop(0, count, fast_step, 0)
        done = done + count * w

    @pl.when(jnp.max(bad_ref[...]) > 2.0 ** HEADROOM)
    def _():
        c = first_step()
        if prev_bias is not None:
            c = lax.cond(qi >= 1, lambda c: step(qi - 1, blk, c, prev_bias, None), lambda c: c, c)
        lax.fori_loop(0, n_plain, lambda j, c: step(j, blk, c, None, None), c)


def _fox_kernel(qa_ref, ka_ref, vt_ref, z_ref, *rest, blk, nh, sub, nq):
    sample_in, (o_ref, so_ref, acc_ref, bad_ref) = rest[:8], rest[8:]
    key = lax.broadcasted_iota(jnp.int32, (blk, blk), 0)
    qry = lax.broadcasted_iota(jnp.int32, (blk, blk), 1)
    hsl = [slice(hd * LANES, (hd + 1) * LANES) for hd in range(nh)]

    def query_block(i, carry):
        qi = pl.program_id(2) * sub + i
        rows = pl.ds(pl.multiple_of(i * blk, blk), blk)
        streams = [(qa_ref[0, rows, sl], sl, slice(hd * FVT_ROWS, (hd + 1) * FVT_ROWS))
                   for hd, sl in enumerate(hsl)]
        _flash_t(ka_ref, vt_ref, acc_ref, bad_ref, streams, qi, blk, None, None, key <= qry)
        for pair in range(nh // 2):
            halves = []
            for par in range(2):
                acc = acc_ref[2 * pair + par]
                halves.append(acc[0:HD, :] / acc[HD:HD + 1, :])
            o = jnp.concatenate(halves, axis=0).T
            o_ref[0, rows, hsl[pair]] = (o * _silu(z_ref[0, rows, hsl[pair]].astype(F32))).astype(BF16)
        return carry

    lax.fori_loop(0, sub, query_block, 0)
    _fox_sample_kernel(*sample_in, so_ref, nq=nq)


def _fox_attention(qa, ka, vt, fz, sample_args, *, blk, nh, sub, nq):
    b, t, _ = qa.shape
    step = sub * blk
    per_seq = t // step
    nb, _, past = sample_args[1].shape
    assert nh == FOX_HEADS and nb == b * per_seq
    row = lambda b, p, i: b * per_seq + i
    new = lambda: pl.BlockSpec((nq, WIDTH), lambda b, p, i: (row(b, p, i), 0))
    cache = lambda r, c: pl.BlockSpec((1, r, c), lambda b, p, i: (row(b, p, i), 0, 0))
    return pl.pallas_call(
        functools.partial(_fox_kernel, blk=blk, nh=nh, sub=sub, nq=nq),
        grid=(b, FOX_HEADS // nh, per_seq),
        in_specs=[pl.BlockSpec((1, step, nh * LANES), lambda b, p, i: (b, i, p)),
                  pl.BlockSpec((1, t, nh * LANES), lambda b, p, i: (b, 0, p)),
                  pl.BlockSpec((1, nh * FVT_ROWS, t), lambda b, p, i: (b, p, 0)),
                  pl.BlockSpec((1, step, nh * HD), lambda b, p, i: (b, i, p)),
                  new(), cache(WIDTH, past), cache(WIDTH, past), new(), new(),
                  cache(FOX_HEADS, past), cache(FOX_HEADS, LANES), new()],
        out_specs=[pl.BlockSpec((1, step, nh * HD), lambda b, p, i: (b, i, p)), new()],
        out_shape=[jax.ShapeDtypeStruct((b, t, WIDTH), BF16), jax.ShapeDtypeStruct((nb * nq, WIDTH), BF16)],
        scratch_shapes=[pltpu.VMEM((nh, FVT_ROWS, blk), F32), pltpu.VMEM((1, blk), F32)],
        compiler_params=pltpu.CompilerParams(
            dimension_semantics=("parallel", "parallel", "arbitrary"), vmem_limit_bytes=VMEM_LIMIT),
        name="fox_prompt",
    )(qa, ka, vt, fz, *sample_args)


def _lambda(lq1_ref, lk1_ref, lq2_ref, lk2_ref):
    s1 = jnp.sum(lq1_ref[...] * lk1_ref[...], axis=-1, keepdims=True)
    s2 = jnp.sum(lq2_ref[...] * lk2_ref[...], axis=-1, keepdims=True)
    return jnp.exp(s1) - jnp.exp(s2) + LAMBDA_INIT


def _diff_finish(o, subg, z):
    ms = jnp.mean(o * o, axis=-1, keepdims=True)
    od = o * lax.rsqrt(ms + NORM_EPS) * subg * (1.0 - LAMBDA_INIT)
    return (od * _silu(z)).astype(BF16)


def _diff_kernel(q_ref, k_ref, vt_ref, bias_ref, z_ref, lq1_ref, lk1_ref, lq2_ref, lk2_ref, subg_ref,
                 *rest, blk, nh, sub, nq, past):
    sample_in, (o_ref, so_ref, acc_ref, ms_ref, bad_ref) = rest[:7], rest[7:]
    lane = lax.broadcasted_iota(jnp.int32, (blk, LANES), 1)
    lam = _lambda(lq1_ref, lk1_ref, lq2_ref, lk2_ref)
    n = 2 * nh
    qi0 = pl.program_id(2) * sub
    ksl = [slice((s // 2) * LANES, (s // 2 + 1) * LANES) for s in range(n)]
    vsl = [slice((s // 2) * VT_ROWS, (s // 2 + 1) * VT_ROWS) for s in range(n)]

    def queries(rows):
        out = []
        for s in range(n):
            q = q_ref[0, rows, ksl[s]]
            out.append(jnp.where((lane < HD) if s % 2 == 0 else (lane >= HD), q, jnp.zeros_like(q)))
        return out

    def scores(qs, blk0, width):
        start = pl.multiple_of(blk0 * blk, blk)
        return [_mm_nt(k_ref[0, pl.ds(start, width), ksl[s]], qs[s]) for s in range(n)]

    def values(s, blk0, width):
        return vt_ref[0, vsl[s], pl.ds(pl.multiple_of(blk0 * blk, blk), width)]

    def prev_bias(s):
        return bias_ref[s // 2, 0:blk, :]

    def diag_bias(s):
        return bias_ref[s // 2, blk:2 * blk, :]

    qs_all = [queries(slice(j * blk, (j + 1) * blk)) for j in range(sub)]
    for group in range(0, sub, FIRST_STEP_GROUP):
        js = range(group, min(group + FIRST_STEP_GROUP, sub))
        diag = {j: scores(qs_all[j], qi0 + j, blk) for j in js}
        ms_all, p_diag = {}, {}
        for j in js:
            ms_all[j], p_diag[j] = [], []
            for s in range(n):
                st = diag[j][s] + diag_bias(s)
                m = jnp.max(st, axis=0, keepdims=True)
                p_diag[j].append(jnp.exp2(st - m).astype(BF16))
                ms_all[j].append(m)
                ms_ref[j, s] = m
        prev = {j: scores(qs_all[j], qi0 + j - 1, blk) for j in js if j >= 1}
        p_prev = {j: [jnp.exp2(prev[j][s] + prev_bias(s) - ms_all[j][s]).astype(BF16) for s in range(n)]
                  for j in prev}
        for j in js:
            worst = None
            for s in range(n):
                acc = _mm(values(s, qi0 + j, blk), p_diag[j][s])
                if j >= 1:
                    acc = acc + _mm(values(s, qi0 + j - 1, blk), p_prev[j][s])
                    d = jnp.max(p_prev[j][s], axis=0, keepdims=True)
                    worst = d if worst is None else jnp.maximum(worst, d)
                acc_ref[j, s] = acc
            bad_ref[j] = jnp.zeros((1, blk), F32) if worst is None else worst.astype(F32)

    def consume(j, ms, sts, blk0, width, biased):
        ps = [jnp.exp2((sts[s] + prev_bias(s) if biased else sts[s]) - ms[s]).astype(BF16) for s in range(n)]
        worst = None
        for p in ps:
            d = jnp.max(p, axis=0, keepdims=True)
            worst = d if worst is None else jnp.maximum(worst, d)
        bad_ref[j] = jnp.maximum(bad_ref[j], worst.astype(F32))
        for s in range(n):
            acc_ref[j, s] = acc_ref[j, s] + _mm(values(s, blk0, width), ps[s])

    @pl.when(qi0 >= 1)
    def _():
        ms0 = [ms_ref[0, s] for s in range(n)]
        consume(0, ms0, scores(qs_all[0], qi0 - 1, blk), qi0 - 1, blk, True)

    def query_block(j, carry):
        qi = qi0 + j
        qs = queries(pl.ds(pl.multiple_of(j * blk, blk), blk))
        ms = [ms_ref[j, s] for s in range(n)]
        n_plain = jnp.maximum(qi - 1, 0)
        done = 0
        for w in STEP_BLOCKS:
            count = (n_plain - done) // w

            def fast_step(i, c, w=w, done=done):
                consume(j, ms, scores(qs, done + i * w, w * blk), done + i * w, w * blk, False)
                return c

            lax.fori_loop(0, count, fast_step, 0)
            done = done + count * w

        @pl.when(jnp.max(bad_ref[j]) > 2.0 ** HEADROOM)
        def _():
            def exact_step(blk0, ms_run, bias):
                out, ps, alphas = [], [], []
                for s, st in enumerate(scores(qs, blk0, blk)):
                    if bias is not None:
                        st = st + bias(s)
                    m_new = jnp.maximum(ms_run[s], jnp.max(st, axis=0, keepdims=True))
                    ps.append(jnp.exp2(st - m_new).astype(BF16))
                    alphas.append(jnp.exp2(ms_run[s] - m_new))
                    out.append(m_new)
                for s in range(n):
                    acc_ref[j, s] = alphas[s] * acc_ref[j, s] + _mm(values(s, blk0, blk), ps[s])
                return tuple(out)

            acc_ref[j] = jnp.zeros(acc_ref.shape[1:], F32)
            c = exact_step(qi, tuple(jnp.full((1, blk), NEG, F32) for _ in range(n)), diag_bias)
            c = lax.cond(qi >= 1, lambda c: exact_step(qi - 1, c, prev_bias), lambda c: c, c)
            lax.fori_loop(0, n_plain, lambda i, c: exact_step(i, c, None), c)

        return carry

    lax.fori_loop(0, sub, query_block, 0)

    for j in range(sub):
        rows = slice(j * blk, (j + 1) * blk)
        for hd in range(nh):
            a0, a1 = acc_ref[j, 2 * hd], acc_ref[j, 2 * hd + 1]
            o_t = a0[0:LANES, :] / a0[LANES:LANES + 1, :] - lam * (a1[0:LANES, :] / a1[LANES:LANES + 1, :])
            sl = slice(hd * LANES, (hd + 1) * LANES)
            o_ref[0, rows, sl] = _diff_finish(o_t.T, subg_ref[...], z_ref[0, rows, sl].astype(F32))

    _diff_sample_kernel(*sample_in, lq1_ref, lk1_ref, lq2_ref, lk2_ref, subg_ref, so_ref, nq=nq, past=past)


def _diff_attention(dq, dk, dvt, bias, dz, lq1, lk1, lq2, lk2, subg, sample_args, *, blk, nh, sub, nq):
    b, t, _ = dq.shape
    step = sub * blk
    per_seq = t // step
    nb, _, past = sample_args[1].shape
    assert nh == DIFF_HEADS and nb == b * per_seq
    small = lambda n: pl.BlockSpec((1, n), lambda b, h, i: (0, 0))
    row = lambda b, h, i: b * per_seq + i
    new = lambda: pl.BlockSpec((nq, WIDTH), lambda b, h, i: (row(b, h, i), 0))
    cache = lambda r, c: pl.BlockSpec((1, r, c), lambda b, h, i: (row(b, h, i), 0, 0))
    return pl.pallas_call(
        functools.partial(_diff_kernel, blk=blk, nh=nh, sub=sub, nq=nq, past=past),
        grid=(b, DIFF_HEADS // nh, per_seq),
        in_specs=[pl.BlockSpec((1, step, nh * LANES), lambda b, h, i: (b, i, h)),
                  pl.BlockSpec((1, t, nh * LANES), lambda b, h, i: (b, 0, h)),
                  pl.BlockSpec((1, nh * VT_ROWS, t), lambda b, h, i: (b, h, 0)),
                  pl.BlockSpec((nh, 2 * blk, blk), lambda b, h, i: (h, 0, 0)),
                  pl.BlockSpec((1, step, nh * LANES), lambda b, h, i: (b, i, h)),
                  small(HD), small(HD), small(HD), small(HD), small(2 * HD),
                  new(), cache(WIDTH, past), cache(past * DIFF_HEADS, 2 * HD), new(), new(),
                  pl.BlockSpec((DIFF_HEADS, nq, past + LANES), lambda b, h, i: (0, 0, 0)), new()],
        out_specs=[pl.BlockSpec((1, step, nh * LANES), lambda b, h, i: (b, i, h)), new()],
        out_shape=[jax.ShapeDtypeStruct((b, t, WIDTH), BF16), jax.ShapeDtypeStruct((nb * nq, WIDTH), BF16)],
        scratch_shapes=[pltpu.VMEM((sub, 2 * nh, VT_ROWS, blk), F32), pltpu.VMEM((sub, 2 * nh, 1, blk), F32),
                        pltpu.VMEM((sub, 1, blk), F32)],
        compiler_params=pltpu.CompilerParams(
            dimension_semantics=("parallel", "parallel", "arbitrary"), vmem_limit_bytes=VMEM_LIMIT),
        name="diff_prompt",
    )(dq, dk, dvt, bias, dz, lq1, lk1, lq2, lk2, subg, *sample_args)


def _suffix_sums(x):
    n = x.shape[1] // LANES
    lane = lax.broadcasted_iota(jnp.int32, (x.shape[0], LANES), 1)
    after = jnp.zeros((x.shape[0], 1), F32)
    pieces = [None] * n
    for blk in reversed(range(n)):
        piece = x[:, blk * LANES:(blk + 1) * LANES]
        y = piece
        shift = 1
        while shift < LANES:
            y = y + jnp.where(lane + shift < LANES, pltpu.roll(y, LANES - shift, 1), 0.0)
            shift *= 2
        pieces[blk] = y - piece + after
        after = after + y[:, 0:1]
    return jnp.concatenate(pieces, axis=1), after


def _pad_rows(x, rows):
    return jnp.concatenate([x, jnp.zeros((rows - x.shape[0], x.shape[1]), x.dtype)], axis=0)


def _joint_softmax(s_p, s_n):
    m = jnp.maximum(jnp.max(s_p, axis=-1, keepdims=True), jnp.max(s_n, axis=-1, keepdims=True))
    p_p = jnp.exp(s_p - m)
    p_n = jnp.exp(s_n - m)
    l = jnp.sum(p_p, axis=-1, keepdims=True) + jnp.sum(p_n, axis=-1, keepdims=True)
    return p_p.astype(BF16), p_n.astype(BF16), l


def _joint_pv(p_p, p_n, l, vt_p, v_n):
    return (_mm_nt(p_p, vt_p) + _mm(p_n, v_n)) / l


def _fox_sample_kernel(q_ref, kt_ref, vt_ref, kn_ref, vn_ref, lfp_ref, lfn_ref, z_ref, o_ref, *, nq):
    r_new, total_new = _suffix_sums(lfn_ref[0])
    r_past, _ = _suffix_sums(lfp_ref[0])
    r_past = r_past + total_new
    lane = lax.broadcasted_iota(jnp.int32, (nq, LANES), 1)
    lane2 = lax.broadcasted_iota(jnp.int32, (2 * nq, LANES), 1)
    row2 = lax.broadcasted_iota(jnp.int32, (2 * nq, LANES), 0)
    upper = row2 >= nq
    causal = lane2 <= jnp.where(upper, row2 - nq, row2)
    sls = [slice(blk * LANES, (blk + 1) * LANES) for blk in range(PAIRS)]
    scores = []
    for blk, sl in enumerate(sls):
        q = q_ref[:, sl]
        zero = jnp.zeros_like(q)
        q2 = jnp.concatenate([jnp.where(lane < HD, q, zero), jnp.where(lane >= HD, q, zero)], axis=0)
        kt = kt_ref[0, sl, :].astype(BF16)
        kn = _pad_rows(kn_ref[:, sl], LANES).astype(BF16)
        past_rows = lax.broadcasted_iota(jnp.int32, (2 * nq, r_past.shape[1]), 0) >= nq
        bias_p = jnp.where(past_rows, r_past[2 * blk + 1:2 * blk + 2, :], r_past[2 * blk:2 * blk + 1, :])
        bias_n = jnp.where(upper, r_new[2 * blk + 1:2 * blk + 2, :], r_new[2 * blk:2 * blk + 1, :])
        scores.append((_mm(q2, kt) + bias_p, jnp.where(causal, _mm_nt(q2, kn) + bias_n, NEG)))
    weights = [_joint_softmax(s_p, s_n) for s_p, s_n in scores]
    for blk, sl in enumerate(sls):
        vt = vt_ref[0, sl, :].astype(BF16)
        vn = _pad_rows(vn_ref[:, sl], LANES).astype(BF16)
        o2 = _joint_pv(*weights[blk], vt, vn)
        o = jnp.where(lane < HD, o2[0:nq, :], o2[nq:2 * nq, :])
        o_ref[:, sl] = (o * _silu(z_ref[:, sl].astype(F32))).astype(BF16)


def _diff_sample_kernel(q_ref, kt_ref, v_ref, kn_ref, vn_ref, bias_ref, z_ref,
                        lq1_ref, lk1_ref, lq2_ref, lk2_ref, subg_ref, o_ref, *, nq, past):
    lane = lax.broadcasted_iota(jnp.int32, (nq, LANES), 1)
    lam = _lambda(lq1_ref, lk1_ref, lq2_ref, lk2_ref)
    sls = [slice(hd * LANES, (hd + 1) * LANES) for hd in range(DIFF_HEADS)]
    scores = []
    for hd, sl in enumerate(sls):
        q = q_ref[:, sl]
        kt = kt_ref[0, sl, :].astype(BF16)
        kn = _pad_rows(kn_ref[:, sl], LANES).astype(BF16)
        for mp in range(2):
            qm = jnp.where((lane < HD) if mp == 0 else (lane >= HD), q, jnp.zeros_like(q))
            scores.append((_mm(qm, kt) + bias_ref[hd, :, 0:past],
                           _mm_nt(qm, kn) + bias_ref[hd, :, past:past + LANES]))
    weights = [_joint_softmax(s_p, s_n) for s_p, s_n in scores]
    for hd, sl in enumerate(sls):
        vt = v_ref[0, pl.ds(hd, past, stride=DIFF_HEADS), :].T.astype(BF16)
        vn = _pad_rows(vn_ref[:, sl], LANES).astype(BF16)
        outs = [_joint_pv(*weights[2 * hd + mp], vt, vn) for mp in range(2)]
        o_ref[:, sl] = _diff_finish(outs[0] - lam * outs[1], subg_ref[...], z_ref[:, sl].astype(F32))


def _out_kernel(yf_ref, yd_ref, ga_ref, gb_ref, x_ref, wuf_ref, wud_ref, wo_ref, g_ref, o_ref):
    merged = (ga_ref[...].astype(F32) * _mm(yf_ref[...], wuf_ref[...])
              + gb_ref[...].astype(F32) * _mm(yd_ref[...], wud_ref[...]))
    out = x_ref[...] + _mm(merged.astype(BF16), wo_ref[...])
    ms = jnp.mean(out * out, axis=-1, keepdims=True)
    o_ref[...] = out * lax.rsqrt(ms + NORM_EPS) * g_ref[...]


def _output(yf, yd, ga, gb, x2d, wuf, wud, wo, g, *, tm):
    rows = x2d.shape[0]
    assert rows % tm == 0
    row_spec = lambda n: pl.BlockSpec((tm, n), lambda i: (i, 0))
    const = lambda shape: pl.BlockSpec(shape, lambda i: (0, 0))
    return pl.pallas_call(
        _out_kernel,
        grid=(rows // tm,),
        in_specs=[row_spec(WIDTH), row_spec(WIDTH), row_spec(D_MODEL), row_spec(D_MODEL), row_spec(D_MODEL),
                  const((WIDTH, D_MODEL)), const((WIDTH, D_MODEL)), const((D_MODEL, D_MODEL)),
                  const((1, D_MODEL))],
        out_specs=row_spec(D_MODEL),
        out_shape=jax.ShapeDtypeStruct((rows, D_MODEL), F32),
        compiler_params=pltpu.CompilerParams(
            dimension_semantics=("parallel",), vmem_limit_bytes=VMEM_LIMIT),
        name="out_proj",
    )(yf, yd, ga, gb, x2d, wuf, wud, wo, g)


def _forget_lane_heads():
    lanes = np.arange(LANES)
    heads = (lanes % HD) // FL_SLOT
    heads[:FOX_HEADS] = np.arange(FOX_HEADS)
    return heads


def _pack_w_in(w_in, b_forget):
    sizes = (WIDTH, WIDTH, WIDTH, WIDTH, FOX_HEADS, WIDTH, WIDTH, WIDTH, WIDTH, D_MODEL, D_MODEL)
    offs = np.cumsum((0,) + sizes)
    fq, fk, fv, fz, fl, dq, dk, dv, dz, ga, gb = (w_in[:, offs[i]:offs[i + 1]] for i in range(len(sizes)))
    heads = _forget_lane_heads()
    w_all = jnp.concatenate([fq, fk, fv, fz, dq, dk, dv, dz, ga, gb, fl[:, heads]], axis=1).astype(BF16)
    return w_all, b_forget[heads][None, :]


def kernel(x_prompt, x_sample, cache_fox_k, cache_fox_v, cache_fox_logf, cache_diff_k, cache_diff_v,
           norm_in_g, w_in, b_forget, lambda_q1, lambda_k1, lambda_q2, lambda_k2, subln_g,
           w_up_fox, w_up_diff, w_o, rel_bias, final_norm_g):
    bp, tp, _ = x_prompt.shape
    bs, ts, _ = x_sample.shape
    past = cache_fox_k.shape[2]
    blk, tm = ATTN_BLOCK, ROW_TILE
    assert tp % (QUERY_BLOCKS_PER_STEP * blk) == 0 and blk > _T5_THR[-1] and blk % CHUNK == 0

    w_all, bfl = _pack_w_in(w_in[0], b_forget[0])
    g_in = norm_in_g[0][None, :]
    g_out = final_norm_g[None, :]
    wuf, wud, wo = w_up_fox[0].astype(BF16), w_up_diff[0].astype(BF16), w_o[0].astype(BF16)
    lq1, lk1, lq2, lk2 = (a[0][None, :] for a in (lambda_q1, lambda_k1, lambda_q2, lambda_k2))
    subg = subln_g[0][None, :]

    xs = x_sample.reshape(bs * ts, D_MODEL)
    (fk_s, fv_s, lf_s, dk_s, dv_s, fq_s, fz_s, dq_s, dz_s, ga_s, gb_s) = _project(
        xs, g_in, w_all, bfl, seq_len=ts, aug=False, tm=tm)
    xp = x_prompt.reshape(bp * tp, D_MODEL)
    (fkt, fvt, lft, dkt, dv4, qa, ka, vtf, fz, dq, dkb, vtd, dz, ga, gb) = _project(
        xp, g_in, w_all, bfl, seq_len=tp, aug=True, tm=tm)
    fox_k_p = jnp.transpose(fkt.reshape(bp, FOX_HEADS, HD, tp), (0, 3, 1, 2))[None]
    fox_v_p = jnp.transpose(fvt.reshape(bp, FOX_HEADS, HD, tp), (0, 3, 1, 2))[None]
    fox_lf_p = jnp.transpose(lft, (0, 2, 1))[None]
    diff_k_p = jnp.transpose(dkt.reshape(bp, DIFF_HEADS, 2, HD, tp), (0, 4, 1, 2, 3))[None]
    diff_v_p = dv4.reshape(1, bp, tp, DIFF_HEADS, 2 * HD)
    fox_kt = jnp.transpose(cache_fox_k[0], (0, 2, 3, 1)).reshape(bs, WIDTH, past)
    fox_vt = jnp.transpose(cache_fox_v[0], (0, 2, 3, 1)).reshape(bs, WIDTH, past)
    diff_kt = jnp.transpose(cache_diff_k[0], (0, 2, 3, 4, 1)).reshape(bs, WIDTH, past)
    diff_v = cache_diff_v[0].reshape(bs, past * DIFF_HEADS, 2 * HD)
    lf_past_t = jnp.swapaxes(cache_fox_logf[0], 1, 2)
    lf_new_t = jnp.pad(jnp.swapaxes(lf_s.reshape(bs, ts, FOX_HEADS), 1, 2), ((0, 0), (0, 0), (0, LANES - ts)))

    r3 = lambda a: a.reshape(bp, tp, a.shape[-1])
    yf, yf_s = _fox_attention(r3(qa), r3(ka), vtf, r3(fz),
                              (fq_s, fox_kt, fox_vt, fk_s, fv_s, lf_past_t, lf_new_t, fz_s),
                              blk=blk, nh=FOX_STEP_HEADS, sub=QUERY_BLOCKS_PER_STEP, nq=ts)
    bias_p = _bias_tiles(rel_bias, q_start=blk, nq=blk, segments=((0, blk, blk), (blk, blk, blk)),
                         transposed=True, scale=LOG2E)
    bias_s = _bias_tiles(rel_bias, q_start=past, nq=ts, segments=((0, past, past), (past, LANES, ts)),
                         transposed=False, scale=1.0)
    yd, yd_s = _diff_attention(r3(dq), r3(dkb), vtd, bias_p, r3(dz), lq1, lk1, lq2, lk2, subg,
                               (dq_s, diff_kt, diff_v, dk_s, dv_s, bias_s, dz_s),
                               blk=blk, nh=DIFF_STEP_HEADS, sub=QUERY_BLOCKS_PER_STEP, nq=ts)

    y_p = _output(yf.reshape(bp * tp, WIDTH), yd.reshape(bp * tp, WIDTH), ga, gb, xp, wuf, wud, wo, g_out,
                  tm=OUT_ROW_TILE)
    y_s = _output(yf_s, yd_s, ga_s, gb_s, xs, wuf, wud, wo, g_out, tm=tm)

    return (y_p.reshape(bp, tp, D_MODEL), y_s.reshape(bs, ts, D_MODEL),
            fox_k_p, fox_v_p, fox_lf_p, diff_k_p, diff_v_p,
            fk_s.reshape(1, bs, ts, FOX_HEADS, HD), fv_s.reshape(1, bs, ts, FOX_HEADS, HD),
            lf_s.reshape(1, bs, ts, FOX_HEADS),
            dk_s.reshape(1, bs, ts, DIFF_HEADS, 2, HD), dv_s.reshape(1, bs, ts, DIFF_HEADS, 2 * HD))
```
